```python
import math
import jax
import jax.numpy as jnp
from jax import lax
import numpy as np

D_MODEL = 2048
BATCH = 2
SEQ = 8192
DEPTH = 2

CTX_LEN = 256
GRID_W = 64
ROPE_BASE = 10000.0
NORM_EPS = 1e-6
BLOCK_Q = 128

DIFF_HEADS = 8
DIFF_HD = 64
DIFF_VD = 2 * DIFF_HD
MLSTM_HEADS = 8
MLSTM_QK = 64
MLSTM_V = 128
MLSTM_CHUNK = 128
MLA_HEADS = 8
MLA_NOPE = 128
MLA_ROPE = 64
MLA_V = 128
Q_LORA = 448
KV_LORA = 512
MLA_SCALE = (MLA_NOPE + MLA_ROPE) ** -0.5
SWA_HEADS = 8
SWA_KV_HEADS = 2
SWA_HD = 128
WINDOW = 128
N_EXPERTS = 64
TOP_K = 6
N_GROUPS = 8
TOPK_GROUPS = 4
EXPERT_FF = 512
SHARED_FF = 512
ROUTED_SCALE = 2.5
MOE_BLOCK = 128

AB_IN = DIFF_HEADS * (4 * DIFF_HD + DIFF_VD) + MLSTM_HEADS * (2 * MLSTM_QK + 2 * MLSTM_V) + 4 * MLSTM_HEADS
AB_OUT = DIFF_HEADS * DIFF_VD + MLSTM_HEADS * MLSTM_V
CD_IN = Q_LORA + KV_LORA + MLA_ROPE + (SWA_HEADS + 2 * SWA_KV_HEADS) * SWA_HD
CD_OUT = MLA_HEADS * MLA_V + SWA_HEADS * SWA_HD
N_EVEN = (DEPTH + 1) // 2
N_ODD = DEPTH // 2
F32 = jnp.float32

kernel_name = 'hybrid_diffusion_block'


def rms_norm(x, g):
    xf = x.astype(F32)
    y = xf * lax.rsqrt(jnp.mean(xf * xf, axis=-1, keepdims=True) + NORM_EPS)
    return (y * g.astype(F32)).astype(x.dtype)


def modulation(c, w, b):
    return jnp.split(jax.nn.silu(c) @ w + b, 6, axis=-1)


def modulate(h, shift, scale):
    return h * (1.0 + scale) + shift


def diff_lambda_init(layer):
    return 0.8 - 0.6 * math.exp(-0.3 * layer)


def axial_rope(row, col, dim):
    quarter = dim // 4
    inv = ROPE_BASE ** (-jnp.arange(quarter, dtype=F32) / quarter)
    ang = jnp.concatenate([row.astype(F32)[:, None] * inv, col.astype(F32)[:, None] * inv], axis=-1)
    return jnp.cos(ang), jnp.sin(ang)


def apply_rope(x, cs):
    cos, sin = cs
    shp = (1, x.shape[1]) + (1,) * (x.ndim - 3) + (cos.shape[-1],)
    cos, sin = cos.reshape(shp), sin.reshape(shp)
    xf = x.astype(F32).reshape(x.shape[:-1] + (-1, 2))
    x1, x2 = xf[..., 0], xf[..., 1]
    return jnp.stack([x1 * cos - x2 * sin, x1 * sin + x2 * cos], axis=-1).reshape(x.shape).astype(x.dtype)


def project(h, w_in, layout, names):
    offs, o = {}, 0
    for n, size in layout:
        offs[n] = (o, o + size)
        o += size
    if len(names) < len(layout):
        w_in = w_in[:, np.concatenate([np.arange(*offs[n]) for n in names])]
    p = h @ w_in
    cuts = [int(v) for v in np.cumsum([offs[n][1] - offs[n][0] for n in names])[:-1]]
    return dict(zip(names, jnp.split(p, cuts, axis=-1)))


def ab_layout():
    return [('dq', DIFF_HEADS * 2 * DIFF_HD), ('dk', DIFF_HEADS * 2 * DIFF_HD), ('dv', DIFF_HEADS * DIFF_VD),
            ('mq', MLSTM_HEADS * MLSTM_QK), ('mk', MLSTM_HEADS * MLSTM_QK), ('mv', MLSTM_HEADS * MLSTM_V),
            ('mo', MLSTM_HEADS * MLSTM_V), ('gates', 4 * MLSTM_HEADS)]


def cd_layout():
    return [('cq', Q_LORA), ('ckv', KV_LORA), ('kr', MLA_ROPE), ('sq', SWA_HEADS * SWA_HD),
            ('sk', SWA_KV_HEADS * SWA_HD), ('sv', SWA_KV_HEADS * SWA_HD)]


def sweep_blocks(f, q):
    b, s = q.shape[:2]
    nb = s // BLOCK_Q
    qb = jnp.moveaxis(q.reshape((b, nb, BLOCK_Q) + q.shape[2:]), 1, 0)
    out = lax.map(f, qb)
    return jnp.moveaxis(out, 0, 1).reshape((b, s) + out.shape[3:])


def diff_attend(q, k, v, lam):
    s = jnp.einsum('bqhmd,bkhmd->bhmqk', q, k, preferred_element_type=F32) * DIFF_HD ** -0.5
    p = jax.nn.softmax(s, axis=-1)
    a = p[:, :, 0] - lam * p[:, :, 1]
    return jnp.einsum('bhqk,bkhd->bqhd', a, v.astype(F32))


def softmax_attend(q, k, v, scale):
    s = jnp.einsum('bqhd,bkhd->bhqk', q, k, preferred_element_type=F32) * scale
    return jnp.einsum('bhqk,bkhd->bqhd', jax.nn.softmax(s, axis=-1), v.astype(F32))


def mlstm_chunked(q, k, v, i_pre, f_pre, state):
    b, t, nh, dk = k.shape
    nc = t // MLSTM_CHUNK
    causal = jnp.tril(jnp.ones((MLSTM_CHUNK, MLSTM_CHUNK), bool))[None, :, :, None]

    def chunks(a):
        return jnp.moveaxis(a.reshape((b, nc, MLSTM_CHUNK) + a.shape[2:]), 1, 0)

    def step(carry, xs):
        c_st, n_st, m_st = carry
        *qx, kb, vb, ib, fb = xs
        bcum = jnp.cumsum(jax.nn.log_sigmoid(fb), axis=1)
        btot = bcum[:, -1]
        w_end = btot[:, None] - bcum + ib
        m_new = jnp.maximum(btot + m_st, jnp.max(w_end, axis=1))
        decay = jnp.exp(btot + m_st - m_new)
        w_k = jnp.exp(w_end - m_new[:, None])
        c_new = decay[..., None, None] * c_st + jnp.einsum('blh,blhv,blhk->bhvk', w_k, vb, kb)
        n_new = decay[..., None] * n_st + jnp.einsum('blh,blhk->bhk', w_k, kb)
        if not qx:
            return (c_new, n_new, m_new), None
        qb = qx[0]
        log_d = bcum[:, :, None] - bcum[:, None] + ib[:, None]
        log_d = jnp.where(causal, log_d, -jnp.inf)
        log_inter = bcum + m_st[:, None]
        m_row = jnp.maximum(log_inter, jnp.max(log_d, axis=2))
        w_intra = jnp.exp(log_d - m_row[:, :, None])
        w_inter = jnp.exp(log_inter - m_row)
        qk = jnp.einsum('bqhd,bshd->bqsh', qb, kb) * w_intra
        num = w_inter[..., None] * jnp.einsum('bqhk,bhvk->bqhv', qb, c_st) + jnp.einsum('bqsh,bshv->bqhv', qk, vb)
        den = w_inter * jnp.einsum('bqhk,bhk->bqh', qb, n_st) + jnp.sum(qk, axis=2)
        h_out = num / jnp.maximum(jnp.abs(den), jnp.exp(-m_row))[..., None]
        return (c_new, n_new, m_new), h_out

    xs = tuple(chunks(a) for a in ((() if q is None else (q,)) + (k, v, i_pre, f_pre)))
    state, hs = lax.scan(step, state, xs)
    if q is None:
        return None, state
    return jnp.moveaxis(hs, 0, 1).reshape(b, t, nh, v.shape[-1]), state


def flip(a):
    return None if a is None else jnp.flip(a, axis=1)


def ab_mixer(hl, hc, w_in, w_out, lam_vec, d_norm, gate_b, m_norm, lam_init, rope, with_ctx):
    b, s, _ = hl.shape
    L = hc.shape[1]
    layout = ab_layout()
    names = [n for n, _ in layout]
    pl = project(hl, w_in, layout, names)
    pc = project(hc, w_in, layout, names if with_ctx else ['dk', 'dv', 'mk', 'mv', 'gates'])

    lv = lam_vec.astype(F32)
    lam = jnp.exp(jnp.sum(lv[0] * lv[1])) - jnp.exp(jnp.sum(lv[2] * lv[3])) + lam_init

    def qk(p, t, rot):
        a = p.reshape(b, t, DIFF_HEADS * 2, DIFF_HD)
        if rot:
            a = apply_rope(a, rope)
        return a.reshape(b, t, DIFF_HEADS, 2, DIFF_HD)

    k_all = jnp.concatenate([qk(pc['dk'], L, False), qk(pl['dk'], s, True)], axis=1)
    v_all = jnp.concatenate([pc['dv'].reshape(b, L, DIFF_HEADS, DIFF_VD),
                             pl['dv'].reshape(b, s, DIFF_HEADS, DIFF_VD)], axis=1)

    def diff_out(a):
        return (rms_norm(a, d_norm) * (1.0 - lam_init)).reshape(a.shape[0], a.shape[1], -1)

    a_l = diff_out(sweep_blocks(lambda qb: diff_attend(qb, k_all, v_all, lam), qk(pl['dq'], s, True)))

    def m_in(p, t):
        k = p['mk'].reshape(b, t, MLSTM_HEADS, MLSTM_QK).astype(F32) * MLSTM_QK ** -0.5
        v = p['mv'].reshape(b, t, MLSTM_HEADS, MLSTM_V).astype(F32)
        g = p['gates'].reshape(b, t, 4, MLSTM_HEADS).astype(F32) + gate_b.astype(F32)
        return k, v, g

    def m_q(p, t):
        return p['mq'].reshape(b, t, MLSTM_HEADS, MLSTM_QK).astype(F32)

    zero = (jnp.zeros((b, MLSTM_HEADS, MLSTM_V, MLSTM_QK), F32),
            jnp.zeros((b, MLSTM_HEADS, MLSTM_QK), F32), jnp.zeros((b, MLSTM_HEADS), F32))
    kc, vc, gc = m_in(pc, L)
    kl, vl, gl = m_in(pl, s)
    ql = m_q(pl, s)
    qc = m_q(pc, L) if with_ctx else None
    hcf, st_f = mlstm_chunked(qc, kc, vc, gc[:, :, 0], gc[:, :, 1], zero)
    hlf, _ = mlstm_chunked(ql, kl, vl, gl[:, :, 0], gl[:, :, 1], st_f)
    hcb, st_b = mlstm_chunked(flip(qc), flip(kc), flip(vc), flip(gc[:, :, 2]), flip(gc[:, :, 3]), zero)
    hlb, _ = mlstm_chunked(flip(ql), flip(kl), flip(vl), flip(gl[:, :, 2]), flip(gl[:, :, 3]), st_b)

    def m_out(hf, hb_rev, p, t):
        o = jax.nn.sigmoid(p['mo'].reshape(b, t, MLSTM_HEADS, MLSTM_V).astype(F32))
        return (rms_norm(hf + jnp.flip(hb_rev, axis=1), m_norm) * o).reshape(b, t, -1)

    dt = hl.dtype
    y_l = jnp.concatenate([a_l, m_out(hlf, hlb, pl, s)], axis=-1).astype(dt) @ w_out
    if not with_ctx:
        return y_l, None
    a_c = diff_out(diff_attend(qk(pc['dq'], L, False), k_all[:, :L], v_all[:, :L], lam))
    y_c = jnp.concatenate([a_c, m_out(hcf, hcb, pc, L)], axis=-1).astype(dt) @ w_out
    return y_l, y_c


def swa_latent(q, k, v, k_ctx, v_ctx, sink):
    b, s = q.shape[:2]
    nb = s // BLOCK_Q
    r = SWA_HEADS // SWA_KV_HEADS

    def band(a):
        ap = jnp.pad(a, ((0, 0), (BLOCK_Q, BLOCK_Q), (0, 0), (0, 0))).reshape(b, nb + 2, BLOCK_Q, SWA_KV_HEADS, SWA_HD)
        return jnp.concatenate([ap[:, :-2], ap[:, 1:-1], ap[:, 2:]], axis=2)

    kb, vb = band(k), band(v)
    qb = q.reshape(b, nb, BLOCK_Q, SWA_KV_HEADS, r, SWA_HD)
    scale = SWA_HD ** -0.5
    s_loc = jnp.einsum('bnqgrd,bnkgd->bngrqk', qb, kb, preferred_element_type=F32) * scale
    q_pos = jnp.arange(nb)[:, None, None] * BLOCK_Q + jnp.arange(BLOCK_Q)[None, :, None]
    k_pos = jnp.arange(nb)[:, None, None] * BLOCK_Q - BLOCK_Q + jnp.arange(3 * BLOCK_Q)[None, None, :]
    valid = (jnp.abs(k_pos - q_pos) <= WINDOW) & (k_pos >= 0) & (k_pos < s)
    s_loc = jnp.where(valid[None, :, None, None], s_loc, -jnp.inf)
    s_ctx = jnp.einsum('bnqgrd,bkgd->bngrqk', qb, k_ctx, preferred_element_type=F32) * scale
    sink_col = jnp.broadcast_to(sink.astype(F32).reshape(1, 1, SWA_KV_HEADS, r, 1, 1), s_loc.shape[:-1] + (1,))
    p = jax.nn.softmax(jnp.concatenate([s_loc, s_ctx, sink_col], axis=-1), axis=-1)
    n_loc = 3 * BLOCK_Q
    out = (jnp.einsum('bngrqk,bnkgd->bnqgrd', p[..., :n_loc], vb.astype(F32))
           + jnp.einsum('bngrqk,bkgd->bnqgrd', p[..., n_loc:-1], v_ctx.astype(F32)))
    return out.reshape(b, s, SWA_HEADS, SWA_HD)


def swa_context(q, k, v, sink):
    b, t = q.shape[:2]
    r = SWA_HEADS // SWA_KV_HEADS
    qg = q.reshape(b, t, SWA_KV_HEADS, r, SWA_HD)
    s = jnp.einsum('bqgrd,bkgd->bgrqk', qg, k, preferred_element_type=F32) * SWA_HD ** -0.5
    sink_col = jnp.broadcast_to(sink.astype(F32).reshape(1, SWA_KV_HEADS, r, 1, 1), s.shape[:-1] + (1,))
    p = jax.nn.softmax(jnp.concatenate([s, sink_col], axis=-1), axis=-1)[..., :-1]
    return jnp.einsum('bgrqk,bkgd->bqgrd', p, v.astype(F32)).reshape(b, t, SWA_HEADS, SWA_HD)


def cd_mixer(hl, hc, w_in, w_out, q_norm, w_uq, kv_norm, w_ukv, sink, rope64, rope128, with_ctx):
    b, s, _ = hl.shape
    L = hc.shape[1]
    layout = cd_layout()
    names = [n for n, _ in layout]
    pl = project(hl, w_in, layout, names)
    pc = project(hc, w_in, layout, names if with_ctx else ['ckv', 'kr', 'sk', 'sv'])

    def mla_q(p, t, rot):
        q = (rms_norm(p['cq'], q_norm) @ w_uq).reshape(b, t, MLA_HEADS, MLA_NOPE + MLA_ROPE)
        q_rope = q[..., MLA_NOPE:]
        if rot:
            q_rope = apply_rope(q_rope, rope64)
        return jnp.concatenate([q[..., :MLA_NOPE], q_rope], axis=-1)

    def mla_kv(p, t, rot):
        kv = (rms_norm(p['ckv'], kv_norm) @ w_ukv).reshape(b, t, MLA_HEADS, MLA_NOPE + MLA_V)
        k_rope = p['kr'].reshape(b, t, 1, MLA_ROPE)
        if rot:
            k_rope = apply_rope(k_rope, rope64)
        k = jnp.concatenate([kv[..., :MLA_NOPE], jnp.broadcast_to(k_rope, (b, t, MLA_HEADS, MLA_ROPE))], axis=-1)
        return k, kv[..., MLA_NOPE:]

    kc, vc = mla_kv(pc, L, False)
    kl, vl = mla_kv(pl, s, True)
    k_all = jnp.concatenate([kc, kl], axis=1)
    v_all = jnp.concatenate([vc, vl], axis=1)
    mla_l = sweep_blocks(lambda qb: softmax_attend(qb, k_all, v_all, MLA_SCALE), mla_q(pl, s, True))

    skc = pc['sk'].reshape(b, L, SWA_KV_HEADS, SWA_HD)
    svc = pc['sv'].reshape(b, L, SWA_KV_HEADS, SWA_HD)
    skl = apply_rope(pl['sk'].reshape(b, s, SWA_KV_HEADS, SWA_HD), rope128)
    svl = pl['sv'].reshape(b, s, SWA_KV_HEADS, SWA_HD)
    sql = apply_rope(pl['sq'].reshape(b, s, SWA_HEADS, SWA_HD), rope128)
    swa_l = swa_latent(sql, skl, svl, skc, svc, sink)

    dt = hl.dtype
    y_l = jnp.concatenate([mla_l.reshape(b, s, -1), swa_l.reshape(b, s, -1)], axis=-1).astype(dt) @ w_out
    if not with_ctx:
        return y_l, None
    mla_c = softmax_attend(mla_q(pc, L, False), kc, vc, MLA_SCALE)
    swa_c = swa_context(pc['sq'].reshape(b, L, SWA_HEADS, SWA_HD), skc, svc, sink)
    y_c = jnp.concatenate([mla_c.reshape(b, L, -1), swa_c.reshape(b, L, -1)], axis=-1).astype(dt) @ w_out
    return y_l, y_c


def swiglu(x, wg, wu, wd):
    return (jax.nn.silu(x @ wg) * (x @ wu)) @ wd


def moe_ffn(h, router_w, router_b, w_gate, w_up, w_down, s_gate, s_up, s_down):
    n, d = h.shape
    scores = jax.nn.sigmoid(jnp.dot(h, router_w, preferred_element_type=F32))
    choice = scores + router_b.astype(F32)
    per_group = N_EXPERTS // N_GROUPS
    group_score = lax.top_k(choice.reshape(n, N_GROUPS, per_group), 2)[0].sum(-1)
    top_groups = lax.top_k(group_score, TOPK_GROUPS)[1]
    group_ok = jnp.any(top_groups[:, :, None] == jnp.arange(N_GROUPS)[None, None, :], axis=1)
    expert_ok = jnp.repeat(group_ok, per_group, axis=1)
    top_e = lax.top_k(jnp.where(expert_ok, choice, -jnp.inf), TOP_K)[1]
    gate = jnp.take_along_axis(scores, top_e, axis=1)
    gate = gate / jnp.sum(gate, axis=-1, keepdims=True) * ROUTED_SCALE

    n_slots = n * TOP_K
    flat_e = top_e.reshape(-1)
    order = jnp.argsort(flat_e)
    e_sorted = flat_e[order]
    counts = jnp.bincount(flat_e, length=N_EXPERTS)
    blocks_per_e = (counts + MOE_BLOCK - 1) // MOE_BLOCK
    blocks_end = jnp.cumsum(blocks_per_e)
    rank = jnp.arange(n_slots) - (jnp.cumsum(counts) - counts)[e_sorted]
    dest = (blocks_end - blocks_per_e)[e_sorted] * MOE_BLOCK + rank
    n_blocks = n_slots // MOE_BLOCK + N_EXPERTS
    n_rows = n_blocks * MOE_BLOCK
    row_tok = jnp.zeros((n_rows,), jnp.int32).at[dest].set((order // TOP_K).astype(jnp.int32))
    row_gate = jnp.zeros((n_rows,), F32).at[dest].set(gate.reshape(-1)[order])
    block_e = jnp.minimum(jnp.searchsorted(blocks_end, jnp.arange(n_blocks), side='right'), N_EXPERTS - 1)

    def run_block(acc, xs):
        tok, g, e = xs
        y = swiglu(h[tok], w_gate[e], w_up[e], w_down[e])
        return acc.at[tok].add(y.astype(F32) * g[:, None]), None

    routed, _ = lax.scan(run_block, jnp.zeros((n, d), F32),
                         (row_tok.reshape(n_blocks, MOE_BLOCK), row_gate.reshape(n_blocks, MOE_BLOCK), block_e))
    shared = swiglu(h, s_gate, s_up, s_down).astype(F32)
    return (routed + shared).astype(h.dtype)


def setup_inputs(seed: int = 0) -> dict:
    key = jax.random.key(seed)
    ks = iter(jax.random.split(key, 40))
    D = D_MODEL

    def nrm(shape, scale):
        return jax.random.normal(next(ks), shape, F32) * scale

    def gain(shape):
        return 1.0 + nrm(shape, 0.02)

    gate_offset = jnp.array([0.0, 3.0, 0.0, 3.0], F32)[None, :, None]
    return {
        'x': nrm((BATCH, SEQ, D), 1.0),
        'c': nrm((BATCH, D), 1.0),
        'ctx': nrm((BATCH, CTX_LEN, D), 1.0),
        'c_ctx': nrm((D,), 1.0),
        'mod_w': nrm((DEPTH, D, 6 * D), 0.5 * D ** -0.5),
        'mod_b': nrm((DEPTH, 6 * D), 0.01),
        'norm_mix_pre': gain((DEPTH, D)),
        'norm_mix_post': gain((DEPTH, D)),
        'norm_ffn_pre': gain((DEPTH, D)),
        'norm_ffn_post': gain((DEPTH, D)),
        'ab_w_in': nrm((N_EVEN, D, AB_IN), D ** -0.5),
        'ab_w_out': nrm((N_EVEN, AB_OUT, D), AB_OUT ** -0.5),
        'diff_lambda': nrm((N_EVEN, 4, DIFF_HD), 0.1),
        'diff_norm': gain((N_EVEN, DIFF_VD)),
        'mlstm_gate_b': gate_offset + nrm((N_EVEN, 4, MLSTM_HEADS), 0.1),
        'mlstm_norm': gain((N_EVEN, MLSTM_V)),
        'cd_w_in': nrm((N_ODD, D, CD_IN), D ** -0.5),
        'cd_w_out': nrm((N_ODD, CD_OUT, D), CD_OUT ** -0.5),
        'mla_q_norm': gain((N_ODD, Q_LORA)),
        'mla_w_uq': nrm((N_ODD, Q_LORA, MLA_HEADS * (MLA_NOPE + MLA_ROPE)), Q_LORA ** -0.5),
        'mla_kv_norm': gain((N_ODD, KV_LORA)),
        'mla_w_ukv': nrm((N_ODD, KV_LORA, MLA_HEADS * (MLA_NOPE + MLA_V)), KV_LORA ** -0.5),
        'swa_sink': nrm((N_ODD, SWA_HEADS), 0.5),
        'router_w': nrm((DEPTH, D, N_EXPERTS), D ** -0.5),
        'router_b': nrm((DEPTH, N_EXPERTS), 0.01),
        'exp_w_gate': nrm((DEPTH, N_EXPERTS, D, EXPERT_FF), D ** -0.5),
        'exp_w_up': nrm((DEPTH, N_EXPERTS, D, EXPERT_FF), D ** -0.5),
        'exp_w_down': nrm((DEPTH, N_EXPERTS, EXPERT_FF, D), EXPERT_FF ** -0.5),
        'sh_w_gate': nrm((DEPTH, D, SHARED_FF), D ** -0.5),
        'sh_w_up': nrm((DEPTH, D, SHARED_FF), D ** -0.5),
        'sh_w_down': nrm((DEPTH, SHARED_FF, D), SHARED_FF ** -0.5),
    }


def reference(x, c, ctx, c_ctx, mod_w, mod_b, norm_mix_pre, norm_mix_post, norm_ffn_pre, norm_ffn_post,
              ab_w_in, ab_w_out, diff_lambda, diff_norm, mlstm_gate_b, mlstm_norm,
              cd_w_in, cd_w_out, mla_q_norm, mla_w_uq, mla_kv_norm, mla_w_ukv, swa_sink,
              router_w, router_b, exp_w_gate, exp_w_up, exp_w_down, sh_w_gate, sh_w_up, sh_w_down):
    b, s, d = x.shape
    ctx_len = ctx.shape[1]
    rows = s // GRID_W
    row = jnp.repeat(jnp.arange(rows), GRID_W)
    col = jnp.tile(jnp.arange(GRID_W), rows)
    rope64 = axial_rope(row, col, DIFF_HD)
    rope128 = axial_rope(row, col, SWA_HD)

    xl, xc = x, ctx
    for layer in range(DEPTH):
        with_ctx = layer < DEPTH - 1
        sh1, sc1, g1, sh2, sc2, g2 = [m[:, None, :] for m in modulation(c, mod_w[layer], mod_b[layer])]
        csh1, csc1, cg1, csh2, csc2, cg2 = modulation(c_ctx, mod_w[layer], mod_b[layer])
        hl = modulate(rms_norm(xl, norm_mix_pre[layer]), sh1, sc1)
        hc = modulate(rms_norm(xc, norm_mix_pre[layer]), csh1, csc1)
        j = layer // 2
        if layer % 2 == 0:
            y_l, y_c = ab_mixer(hl, hc, ab_w_in[j], ab_w_out[j], diff_lambda[j], diff_norm[j],
                                mlstm_gate_b[j], mlstm_norm[j], diff_lambda_init(layer), rope64, with_ctx)
        else:
            y_l, y_c = cd_mixer(hl, hc, cd_w_in[j], cd_w_out[j], mla_q_norm[j], mla_w_uq[j], mla_kv_norm[j],
                                mla_w_ukv[j], swa_sink[j], rope64, rope128, with_ctx)
        xl = xl + g1 * rms_norm(y_l, norm_mix_post[layer])
        f_in_l = modulate(rms_norm(xl, norm_ffn_pre[layer]), sh2, sc2)
        if with_ctx:
            xc = xc + cg1 * rms_norm(y_c, norm_mix_post[layer])
            f_in_c = modulate(rms_norm(xc, norm_ffn_pre[layer]), csh2, csc2)
            tokens = jnp.concatenate([f_in_c.reshape(-1, d), f_in_l.reshape(-1, d)], axis=0)
        else:
            tokens = f_in_l.reshape(-1, d)
        f = moe_ffn(tokens, router_w[layer], router_b[layer], exp_w_gate[layer], exp_w_up[layer],
                    exp_w_down[layer], sh_w_gate[layer], sh_w_up[layer], sh_w_down[layer])
        xl = xl + g2 * rms_norm(f[f.shape[0] - b * s:].reshape(b, s, d), norm_ffn_post[layer])
        if with_ctx:
            xc = xc + cg2 * rms_norm(f[:b * ctx_len].reshape(b, ctx_len, d), norm_ffn_post[layer])
    return xl
```

```python
import functools
import math

import jax
import jax.numpy as jnp
import numpy as np
from jax import lax
from jax.experimental import pallas as pl
from jax.experimental.pallas import tpu as pltpu

F32 = jnp.float32
BF16 = jnp.bfloat16

GRID_W = 64
ROPE_BASE = 10000.0
NORM_EPS = 1e-6

DIFF_HEADS = 8
DIFF_HD = 64
DIFF_VD = 2 * DIFF_HD
MLSTM_HEADS = 8
MLSTM_QK = 64
MLSTM_V = 128
MLSTM_CHUNK = 128
MLA_HEADS = 8
MLA_NOPE = 128
MLA_ROPE = 64
MLA_V = 128
Q_LORA = 448
KV_LORA = 512
MLA_SCALE = (MLA_NOPE + MLA_ROPE) ** -0.5
SWA_HEADS = 8
SWA_KV_HEADS = 2
SWA_HD = 128
WINDOW = 128
BLOCK_Q = 128
N_EXPERTS = 64
TOP_K = 6
N_GROUPS = 8
TOPK_GROUPS = 4
ROUTED_SCALE = 2.5

LANES = 128
LOG2E = 1.4426950408889634
VMEM_LIMIT = 56 * 1024 * 1024

ROW_TILE = 256
EXPERT_ROWS = 256


def _cparams(sem):
    return pltpu.CompilerParams(dimension_semantics=sem, vmem_limit_bytes=VMEM_LIMIT)


def _pair_swap(a):
    lane = lax.broadcasted_iota(jnp.int32, a.shape, 1)
    return jnp.where(lane % 2 == 0, pltpu.roll(a, LANES - 1, 1), pltpu.roll(a, 1, 1))


def _norm_proj_kernel(x_ref, g_ref, sh_ref, sc_ref, w_ref, cos_ref, sin_ref, o_ref, h_scr, *,
                      n_valid, pattern):
    j = pl.program_id(2)

    @pl.when(j == 0)
    def _():
        x = x_ref[0].astype(F32)
        ms = jnp.sum(x * x, axis=-1, keepdims=True) * (1.0 / n_valid)
        y = x * lax.rsqrt(ms + NORM_EPS) * g_ref[...]
        h_scr[...] = (y * (1.0 + sc_ref[0]) + sh_ref[0]).astype(BF16)

    acc = jnp.dot(h_scr[...], w_ref[...], preferred_element_type=F32)
    if pattern is None:
        o_ref[0] = acc.astype(o_ref.dtype)
    else:
        for c, tbl in enumerate(pattern):
            a = acc[:, c * LANES:(c + 1) * LANES]
            if tbl >= 0:
                a = a * cos_ref[tbl] + _pair_swap(a) * sin_ref[tbl]
            o_ref[0, :, c * LANES:(c + 1) * LANES] = a.astype(o_ref.dtype)


def _norm_proj(x, g, shift, scale, w, *, seg_tiles, tn, out_dtype=BF16, n_valid=None,
               rope=None, pattern=None):
    b, t, k = x.shape
    n = w.shape[1]
    tm = ROW_TILE
    assert t % tm == 0 and n % tn == 0 and tn % LANES == 0
    if rope is None:
        cos = sin = jnp.zeros((1, tm, LANES), F32)
        tbl_map = lambda bi, i, j: (0, 0, 0)
    else:
        cos, sin = rope
        tbl_map = lambda bi, i, j: (0, i, 0)
    ntab = cos.shape[0]
    kern = functools.partial(_norm_proj_kernel, n_valid=float(n_valid or k), pattern=pattern)
    mod_map = lambda bi, i, j: (bi * 2 + jnp.where(i >= seg_tiles, 1, 0), 0, 0)
    return pl.pallas_call(
        kern,
        grid=(b, t // tm, n // tn),
        in_specs=[
            pl.BlockSpec((1, tm, k), lambda bi, i, j: (bi, i, 0)),
            pl.BlockSpec((1, k), lambda bi, i, j: (0, 0)),
            pl.BlockSpec((1, 1, k), mod_map),
            pl.BlockSpec((1, 1, k), mod_map),
            pl.BlockSpec((k, tn), lambda bi, i, j: (0, j)),
            pl.BlockSpec((ntab, tm, LANES), tbl_map),
            pl.BlockSpec((ntab, tm, LANES), tbl_map),
        ],
        out_specs=pl.BlockSpec((1, tm, tn), lambda bi, i, j: (bi, i, j)),
        out_shape=jax.ShapeDtypeStruct((b, t, n), out_dtype),
        scratch_shapes=[pltpu.VMEM((tm, k), BF16)],
        compiler_params=_cparams(("parallel", "parallel", "arbitrary")),
    )(x, g.reshape(1, k).astype(F32), shift, scale, w, cos, sin)


def _matmul_kernel(a_ref, w_ref, o_ref):
    o_ref[...] = jnp.dot(a_ref[...].astype(BF16), w_ref[...],
                         preferred_element_type=F32).astype(o_ref.dtype)


def _matmul(a, w, *, tm, tn, out_dtype=F32):
    m, k = a.shape
    n = w.shape[1]
    tn = min(tn, n)
    assert m % tm == 0 and n % tn == 0
    return pl.pallas_call(
        _matmul_kernel,
        grid=(m // tm, n // tn),
        in_specs=[pl.BlockSpec((tm, k), lambda i, j: (i, 0)),
                  pl.BlockSpec((k, tn), lambda i, j: (0, j))],
        out_specs=pl.BlockSpec((tm, tn), lambda i, j: (i, j)),
        out_shape=jax.ShapeDtypeStruct((m, n), out_dtype),
        compiler_params=_cparams(("parallel", "arbitrary")),
    )(a, w)


def _flash_kernel(q_ref, k_ref, v_ref, lam_ref, dn_ref, o_ref, q_scr, m_scr, l_scr, acc_scr, *,
                  scale, diff, lam_init, ctx_tiles, ctx_len, tq, tk, nk):
    i = pl.program_id(2)
    j = pl.program_id(3)

    @pl.when(j == 0)
    def _():
        q = q_ref[0].astype(F32) * (scale * LOG2E)
        if diff:
            lane = lax.broadcasted_iota(jnp.int32, q.shape, 1)
            half = q.shape[1] // 2
            q_scr[0:tq] = jnp.where(lane < half, q, 0.0).astype(BF16)
            q_scr[tq:2 * tq] = jnp.where(lane >= half, q, 0.0).astype(BF16)
        else:
            q_scr[...] = q.astype(BF16)
        m_scr[...] = jnp.full(m_scr.shape, -jnp.inf, F32)
        l_scr[...] = jnp.zeros(l_scr.shape, F32)
        acc_scr[...] = jnp.zeros(acc_scr.shape, F32)

    def step(masked):
        s = lax.dot_general(q_scr[...], k_ref[0], (((1,), (1,)), ((), ())),
                            preferred_element_type=F32)
        if masked:
            col = lax.broadcasted_iota(jnp.int32, s.shape, 1)
            s = jnp.where(col < ctx_len, s, -jnp.inf)
        m_old = m_scr[...]
        m_new = jnp.maximum(m_old, jnp.max(s, axis=-1, keepdims=True))
        p = jnp.exp2(s - m_new)
        alpha = jnp.exp2(m_old - m_new)
        l_scr[...] = alpha * l_scr[...] + jnp.sum(p, axis=-1, keepdims=True)
        acc_scr[...] = alpha * acc_scr[...] + jnp.dot(p.astype(BF16), v_ref[0],
                                                      preferred_element_type=F32)
        m_scr[...] = m_new

    if ctx_tiles:
        pl.when((i < ctx_tiles) & (j == 0))(lambda: step(True))
        pl.when(i >= ctx_tiles)(lambda: step(False))
    else:
        step(False)

    @pl.when(j == nk - 1)
    def _():
        o = acc_scr[...] / l_scr[...]
        if diff:
            lv = lam_ref[...]
            lam = (jnp.exp(jnp.sum(lv[0:1] * lv[1:2], axis=-1, keepdims=True))
                   - jnp.exp(jnp.sum(lv[2:3] * lv[3:4], axis=-1, keepdims=True)) + lam_init)
            a = o[0:tq] - lam * o[tq:2 * tq]
            ms = jnp.mean(a * a, axis=-1, keepdims=True)
            a = a * lax.rsqrt(ms + NORM_EPS) * dn_ref[...] * (1.0 - lam_init)
            o_ref[0] = a.astype(o_ref.dtype)
        else:
            o_ref[0] = o.astype(o_ref.dtype)


def _flash(q, k, v, *, heads, dq, dv, q_col0, k_col0, v_col0, q_row_tiles0, n_q_tiles, scale,
           tk, diff=False, lam_vec=None, d_norm=None, lam_init=0.0, ctx_tiles=0, ctx_len=0):
    b, t, _ = k.shape
    tq = ROW_TILE
    assert t % tk == 0
    nk = t // tk
    assert not ctx_tiles or ctx_len <= tk
    rows = 2 * tq if diff else tq
    if lam_vec is None:
        lam_vec = jnp.zeros((4, DIFF_HD), F32)
        d_norm = jnp.zeros((dv,), F32)
    kern = functools.partial(_flash_kernel, scale=scale, diff=diff, lam_init=lam_init,
                             ctx_tiles=ctx_tiles, ctx_len=ctx_len, tq=tq, tk=tk, nk=nk)

    def kv_row(i, j):
        return jnp.where(i < ctx_tiles, 0, j) if ctx_tiles else j

    return pl.pallas_call(
        kern,
        grid=(b, heads, n_q_tiles, nk),
        in_specs=[
            pl.BlockSpec((1, tq, dq), lambda bi, h, i, j: (bi, i + q_row_tiles0, q_col0 + h)),
            pl.BlockSpec((1, tk, dq), lambda bi, h, i, j: (bi, kv_row(i, j), k_col0 + h)),
            pl.BlockSpec((1, tk, dv), lambda bi, h, i, j: (bi, kv_row(i, j), v_col0 + h)),
            pl.BlockSpec((4, DIFF_HD), lambda bi, h, i, j: (0, 0)),
            pl.BlockSpec((1, dv), lambda bi, h, i, j: (0, 0)),
        ],
        out_specs=pl.BlockSpec((1, tq, dv), lambda bi, h, i, j: (bi, i, h)),
        out_shape=jax.ShapeDtypeStruct((b, n_q_tiles * tq, heads * dv), BF16),
        scratch_shapes=[pltpu.VMEM((rows, dq), BF16), pltpu.VMEM((rows, 1), F32),
                        pltpu.VMEM((rows, 1), F32), pltpu.VMEM((rows, dv), F32)],
        compiler_params=_cparams(("parallel", "parallel", "parallel", "arbitrary")),
    )(q, k, v, lam_vec.astype(F32), d_norm.reshape(1, dv).astype(F32))


def _expert_kernel(be_ref, x_ref, wg_ref, wu_ref, wd_ref, o_ref):
    del be_ref
    x = x_ref[...].astype(BF16)
    g = jnp.dot(x, wg_ref[0], preferred_element_type=F32)
    u = jnp.dot(x, wu_ref[0], preferred_element_type=F32)
    a = (g * jax.nn.sigmoid(g) * u).astype(BF16)
    o_ref[...] = jnp.dot(a, wd_ref[0], preferred_element_type=F32)


def _expert_ffn(x, block_e, wg, wu, wd, *, tm):
    n, d = x.shape
    ff = wg.shape[2]
    assert n % tm == 0
    grid_spec = pltpu.PrefetchScalarGridSpec(
        num_scalar_prefetch=1,
        grid=(n // tm,),
        in_specs=[
            pl.BlockSpec((tm, d), lambda i, be: (i, 0)),
            pl.BlockSpec((1, d, ff), lambda i, be: (be[i], 0, 0)),
            pl.BlockSpec((1, d, ff), lambda i, be: (be[i], 0, 0)),
            pl.BlockSpec((1, ff, d), lambda i, be: (be[i], 0, 0)),
        ],
        out_specs=pl.BlockSpec((tm, d), lambda i, be: (i, 0)),
    )
    return pl.pallas_call(
        _expert_kernel,
        grid_spec=grid_spec,
        out_shape=jax.ShapeDtypeStruct((n, d), F32),
        compiler_params=_cparams(("arbitrary",)),
    )(block_e, x, wg, wu, wd)


def _rms_norm(x, g):
    xf = x.astype(F32)
    y = xf * lax.rsqrt(jnp.mean(xf * xf, axis=-1, keepdims=True) + NORM_EPS)
    return y * g.astype(F32)


def _rope_tables(s, ctx_len, dim):
    rows = s // GRID_W
    row = jnp.repeat(jnp.arange(rows), GRID_W)
    col = jnp.tile(jnp.arange(GRID_W), rows)
    quarter = dim // 4
    inv = ROPE_BASE ** (-jnp.arange(quarter, dtype=F32) / quarter)
    ang = jnp.concatenate([row.astype(F32)[:, None] * inv, col.astype(F32)[:, None] * inv], axis=-1)
    cos = jnp.repeat(jnp.cos(ang), 2, axis=-1)
    sin = jnp.repeat(jnp.sin(ang), 2, axis=-1) * jnp.tile(jnp.array([-1.0, 1.0], F32), dim // 2)
    cos = jnp.concatenate([jnp.ones((ctx_len, dim), F32), cos], axis=0)
    sin = jnp.concatenate([jnp.zeros((ctx_len, dim), F32), sin], axis=0)
    return cos, sin


def _mlstm_chunked(q, k, v, i_pre, f_pre, state):
    b, t, nh, dk = k.shape
    nc = t // MLSTM_CHUNK
    causal = jnp.tril(jnp.ones((MLSTM_CHUNK, MLSTM_CHUNK), bool))[None, :, :, None]

    def chunks(a):
        return jnp.moveaxis(a.reshape((b, nc, MLSTM_CHUNK) + a.shape[2:]), 1, 0)

    def step(carry, xs):
        c_st, n_st, m_st = carry
        *qx, kb, vb, ib, fb = xs
        bcum = jnp.cumsum(jax.nn.log_sigmoid(fb), axis=1)
        btot = bcum[:, -1]
        w_end = btot[:, None] - bcum + ib
        m_new = jnp.maximum(btot + m_st, jnp.max(w_end, axis=1))
        decay = jnp.exp(btot + m_st - m_new)
        w_k = jnp.exp(w_end - m_new[:, None])
        c_new = decay[..., None, None] * c_st + jnp.einsum('blh,blhv,blhk->bhvk', w_k, vb, kb)
        n_new = decay[..., None] * n_st + jnp.einsum('blh,blhk->bhk', w_k, kb)
        if not qx:
            return (c_new, n_new, m_new), None
        qb = qx[0]
        log_d = bcum[:, :, None] - bcum[:, None] + ib[:, None]
        log_d = jnp.where(causal, log_d, -jnp.inf)
        log_inter = bcum + m_st[:, None]
        m_row = jnp.maximum(log_inter, jnp.max(log_d, axis=2))
        w_intra = jnp.exp(log_d - m_row[:, :, None])
        w_inter = jnp.exp(log_inter - m_row)
        qk = jnp.einsum('bqhd,bshd->bqsh', qb, kb) * w_intra
        num = w_inter[..., None] * jnp.einsum('bqhk,bhvk->bqhv', qb, c_st) + jnp.einsum('bqsh,bshv->bqhv', qk, vb)
        den = w_inter * jnp.einsum('bqhk,bhk->bqh', qb, n_st) + jnp.sum(qk, axis=2)
        h_out = num / jnp.maximum(jnp.abs(den), jnp.exp(-m_row))[..., None]
        return (c_new, n_new, m_new), h_out

    xs = tuple(chunks(a) for a in ((() if q is None else (q,)) + (k, v, i_pre, f_pre)))
    state, hs = lax.scan(step, state, xs)
    if q is None:
        return None, state
    return jnp.moveaxis(hs, 0, 1).reshape(b, t, nh, v.shape[-1]), state


def _flip(a):
    return None if a is None else jnp.flip(a, axis=1)


def _mlstm_mixer(mq, mk, mv, mo, gates, gate_b, m_norm, ctx_len, with_ctx):
    b, t, _ = mq.shape
    L = ctx_len

    def heads(a, d):
        return a.reshape(b, a.shape[1], MLSTM_HEADS, d).astype(F32)

    q = heads(mq, MLSTM_QK)
    k = heads(mk, MLSTM_QK) * MLSTM_QK ** -0.5
    v = heads(mv, MLSTM_V)
    g = gates[..., :4 * MLSTM_HEADS].reshape(b, t, 4, MLSTM_HEADS).astype(F32) + gate_b.astype(F32)
    zero = (jnp.zeros((b, MLSTM_HEADS, MLSTM_V, MLSTM_QK), F32),
            jnp.zeros((b, MLSTM_HEADS, MLSTM_QK), F32), jnp.zeros((b, MLSTM_HEADS), F32))
    qc, kc, vc, gc = (a[:, :L] for a in (q, k, v, g))
    ql, kl, vl, gl = (a[:, L:] for a in (q, k, v, g))
    if not with_ctx:
        qc = None
    hcf, st_f = _mlstm_chunked(qc, kc, vc, gc[:, :, 0], gc[:, :, 1], zero)
    hlf, _ = _mlstm_chunked(ql, kl, vl, gl[:, :, 0], gl[:, :, 1], st_f)
    hcb, st_b = _mlstm_chunked(_flip(qc), _flip(kc), _flip(vc), _flip(gc[:, :, 2]), _flip(gc[:, :, 3]), zero)
    hlb, _ = _mlstm_chunked(_flip(ql), _flip(kl), _flip(vl), _flip(gl[:, :, 2]), _flip(gl[:, :, 3]), st_b)

    def m_out(hf, hb_rev, o_pre):
        o = jax.nn.sigmoid(heads(o_pre, MLSTM_V))
        return (_rms_norm(hf + jnp.flip(hb_rev, axis=1), m_norm) * o).reshape(b, hf.shape[1], -1)

    out_l = m_out(hlf, hlb, mo[:, L:])
    if not with_ctx:
        return out_l
    return jnp.concatenate([m_out(hcf, hcb, mo[:, :L]), out_l], axis=1)


def _swa_latent(q, k, v, k_ctx, v_ctx, sink):
    b, s = q.shape[:2]
    nb = s // BLOCK_Q
    r = SWA_HEADS // SWA_KV_HEADS

    def band(a):
        ap = jnp.pad(a, ((0, 0), (BLOCK_Q, BLOCK_Q), (0, 0), (0, 0))).reshape(b, nb + 2, BLOCK_Q, SWA_KV_HEADS, SWA_HD)
        return jnp.concatenate([ap[:, :-2], ap[:, 1:-1], ap[:, 2:]], axis=2)

    kb, vb = band(k), band(v)
    qb = q.reshape(b, nb, BLOCK_Q, SWA_KV_HEADS, r, SWA_HD)
    scale = SWA_HD ** -0.5
    s_loc = jnp.einsum('bnqgrd,bnkgd->bngrqk', qb, kb, preferred_element_type=F32) * scale
    q_pos = jnp.arange(nb)[:, None, None] * BLOCK_Q + jnp.arange(BLOCK_Q)[None, :, None]
    k_pos = jnp.arange(nb)[:, None, None] * BLOCK_Q - BLOCK_Q + jnp.arange(3 * BLOCK_Q)[None, None, :]
    valid = (jnp.abs(k_pos - q_pos) <= WINDOW) & (k_pos >= 0) & (k_pos < s)
    s_loc = jnp.where(valid[None, :, None, None], s_loc, -jnp.inf)
    s_ctx = jnp.einsum('bnqgrd,bkgd->bngrqk', qb, k_ctx, preferred_element_type=F32) * scale
    sink_col = jnp.broadcast_to(sink.astype(F32).reshape(1, 1, SWA_KV_HEADS, r, 1, 1), s_loc.shape[:-1] + (1,))
    p = jax.nn.softmax(jnp.concatenate([s_loc, s_ctx, sink_col], axis=-1), axis=-1)
    n_loc = 3 * BLOCK_Q
    out = (jnp.einsum('bngrqk,bnkgd->bnqgrd', p[..., :n_loc], vb.astype(F32))
           + jnp.einsum('bngrqk,bkgd->bnqgrd', p[..., n_loc:-1], v_ctx.astype(F32)))
    return out.reshape(b, s, SWA_HEADS * SWA_HD)


def _moe(tokens, router_w, router_b, wg, wu, wd, sg, su, sd):
    n, d = tokens.shape
    scores = jax.nn.sigmoid(jnp.dot(tokens, router_w, preferred_element_type=F32,
                                    precision=lax.Precision.HIGHEST))
    choice = scores + router_b.astype(F32)
    per_group = N_EXPERTS // N_GROUPS
    group_score = lax.top_k(choice.reshape(n, N_GROUPS, per_group), 2)[0].sum(-1)
    top_groups = lax.top_k(group_score, TOPK_GROUPS)[1]
    group_ok = jnp.any(top_groups[:, :, None] == jnp.arange(N_GROUPS)[None, None, :], axis=1)
    expert_ok = jnp.repeat(group_ok, per_group, axis=1)
    top_e = lax.top_k(jnp.where(expert_ok, choice, -jnp.inf), TOP_K)[1]
    gate = jnp.take_along_axis(scores, top_e, axis=1)
    gate = gate / jnp.sum(gate, axis=-1, keepdims=True) * ROUTED_SCALE

    blk = EXPERT_ROWS
    n_slots = n * TOP_K
    flat_e = top_e.reshape(-1)
    order = jnp.argsort(flat_e)
    e_sorted = flat_e[order]
    counts = jnp.bincount(flat_e, length=N_EXPERTS)
    blocks_per_e = (counts + blk - 1) // blk
    blocks_end = jnp.cumsum(blocks_per_e)
    rank = jnp.arange(n_slots) - (jnp.cumsum(counts) - counts)[e_sorted]
    dest_sorted = (blocks_end - blocks_per_e)[e_sorted] * blk + rank
    n_blocks = -(-n_slots // blk) + N_EXPERTS
    n_rows = n_blocks * blk
    row_tok = jnp.zeros((n_rows,), jnp.int32).at[dest_sorted].set((order // TOP_K).astype(jnp.int32))
    dest = jnp.zeros((n_slots,), jnp.int32).at[order].set(dest_sorted.astype(jnp.int32)).reshape(n, TOP_K)
    block_e = jnp.minimum(jnp.searchsorted(blocks_end, jnp.arange(n_blocks), side='right'),
                          N_EXPERTS - 1).astype(jnp.int32)

    tok_bf = tokens.astype(BF16)
    y = _expert_ffn(tok_bf[row_tok], block_e, wg, wu, wd, tm=blk)
    routed = jnp.sum(y[dest] * gate[:, :, None], axis=1)
    shared = _expert_ffn(tok_bf, jnp.zeros((n // ROW_TILE,), jnp.int32), sg[None], su[None], sd[None],
                         tm=ROW_TILE)
    return routed + shared


def kernel(x, c, ctx, c_ctx, mod_w, mod_b, norm_mix_pre, norm_mix_post, norm_ffn_pre, norm_ffn_post, ab_w_in, ab_w_out, diff_lambda, diff_norm, mlstm_gate_b, mlstm_norm, cd_w_in, cd_w_out, mla_q_norm, mla_w_uq, mla_kv_norm, mla_w_ukv, swa_sink, router_w, router_b, exp_w_gate, exp_w_up, exp_w_down, sh_w_gate, sh_w_up, sh_w_down):
    b, s, d = x.shape
    L = ctx.shape[1]
    t = L + s
    assert L == ROW_TILE and s % ROW_TILE == 0
    depth = mod_w.shape[0]
    ctx_tiles = L // ROW_TILE
    n_q = s // ROW_TILE
    tk = 768 if t % 768 == 0 else ROW_TILE

    cos64, sin64 = _rope_tables(s, L, DIFF_HD)
    cos64 = jnp.tile(cos64, (1, LANES // DIFF_HD))
    sin64 = jnp.tile(sin64, (1, LANES // DIFF_HD))
    cos128, sin128 = _rope_tables(s, L, SWA_HD)
    cos_kr = jnp.concatenate([cos64[:, :MLA_ROPE], jnp.ones((t, LANES - MLA_ROPE), F32)], axis=1)
    sin_kr = jnp.concatenate([sin64[:, :MLA_ROPE], jnp.zeros((t, LANES - MLA_ROPE), F32)], axis=1)

    xs = jnp.concatenate([ctx, x], axis=1)
    zeros_mod = jnp.zeros((b * 2, 1, 1), F32)

    for layer in range(depth):
        with_ctx = layer < depth - 1
        j = layer // 2
        mod_l = jax.nn.silu(c) @ mod_w[layer] + mod_b[layer]
        mod_c = jax.nn.silu(c_ctx) @ mod_w[layer] + mod_b[layer]
        mods = jnp.stack([jnp.broadcast_to(mod_c, (b, 6 * d)), mod_l], axis=1).reshape(b * 2, 1, 6, d)
        sh1, sc1, g1, sh2, sc2, g2 = (mods[:, :, m] for m in range(6))

        proj = functools.partial(_norm_proj, xs, norm_mix_pre[layer], sh1, sc1, seg_tiles=ctx_tiles)
        if layer % 2 == 0:
            w_in = ab_w_in[j].astype(BF16)
            n_qk = 2 * DIFF_HEADS * 2 * DIFF_HD
            n_plain = AB_PLAIN
            qk = proj(w_in[:, :n_qk], tn=1024, rope=(cos64[None], sin64[None]), pattern=(0,) * 8)
            pv = proj(w_in[:, n_qk:n_qk + n_plain], tn=1024)
            w_gates = jnp.pad(w_in[:, n_qk + n_plain:], ((0, 0), (0, LANES - 4 * MLSTM_HEADS)))
            gates = proj(w_gates, tn=LANES, out_dtype=F32)

            lam_init = 0.8 - 0.6 * math.exp(-0.3 * layer)
            a = _flash(qk, qk, pv, heads=DIFF_HEADS, dq=2 * DIFF_HD, dv=DIFF_VD,
                       q_col0=0, k_col0=DIFF_HEADS, v_col0=0,
                       q_row_tiles0=0 if with_ctx else ctx_tiles,
                       n_q_tiles=n_q + (ctx_tiles if with_ctx else 0),
                       scale=DIFF_HD ** -0.5, tk=tk, diff=True, lam_vec=diff_lambda[j],
                       d_norm=diff_norm[j], lam_init=lam_init,
                       ctx_tiles=ctx_tiles if with_ctx else 0, ctx_len=L)
            o0 = DIFF_HEADS * DIFF_VD
            o1 = o0 + MLSTM_HEADS * MLSTM_QK
            o2 = o1 + MLSTM_HEADS * MLSTM_QK
            o3 = o2 + MLSTM_HEADS * MLSTM_V
            m = _mlstm_mixer(pv[..., o0:o1], pv[..., o1:o2], pv[..., o2:o3], pv[..., o3:], gates,
                             mlstm_gate_b[j], mlstm_norm[j], L, with_ctx)
            mix = jnp.concatenate([a, m.astype(BF16)], axis=-1)
            w_out = ab_w_out[j]
        else:
            w_in = cd_w_in[j].astype(BF16)
            c0 = Q_LORA
            c1 = c0 + KV_LORA
            c2 = c1 + MLA_ROPE
            c3 = c2 + SWA_HEADS * SWA_HD
            c4 = c3 + SWA_KV_HEADS * SWA_HD
            w_rope = jnp.concatenate([w_in[:, c2:c4], w_in[:, c1:c2],
                                      jnp.zeros((d, LANES - MLA_ROPE), BF16)], axis=1)
            n_rope = w_rope.shape[1]
            rp = proj(w_rope, tn=n_rope, rope=(jnp.stack([cos128, cos_kr]), jnp.stack([sin128, sin_kr])),
                      pattern=(0,) * (n_rope // LANES - 1) + (1,))
            q_pad = 512 - Q_LORA
            w_plain = jnp.concatenate([w_in[:, c4:], w_in[:, :c0], jnp.zeros((d, q_pad), BF16),
                                       w_in[:, c0:c1]], axis=1)
            pp = proj(w_plain, tn=w_plain.shape[1])
            n_sv = SWA_KV_HEADS * SWA_HD
            cq = pp[..., n_sv:n_sv + 512]
            ckv = pp[..., n_sv + 512:]

            hq = MLA_NOPE + MLA_ROPE
            w_uq = mla_w_uq[j].astype(BF16).reshape(Q_LORA, MLA_HEADS, hq)
            w_uq = jnp.pad(w_uq, ((0, q_pad), (0, 0), (0, 256 - hq))).reshape(512, MLA_HEADS * 256)
            no_mod = jnp.zeros((b * 2, 1, 512), F32)
            qn = jnp.pad(mla_q_norm[j], (0, q_pad))
            q_mla = _norm_proj(cq, qn, no_mod, no_mod, w_uq, seg_tiles=ctx_tiles, tn=1024, n_valid=Q_LORA,
                               rope=(cos_kr[None], sin_kr[None]), pattern=(-1, 0) * 4)
            kv = _norm_proj(ckv, mla_kv_norm[j], no_mod, no_mod, mla_w_ukv[j].astype(BF16),
                            seg_tiles=ctx_tiles, tn=1024).reshape(b, t, MLA_HEADS, MLA_NOPE + MLA_V)
            kr = rp[..., n_rope - LANES:]
            k_mla = jnp.concatenate([kv[..., :MLA_NOPE],
                                     jnp.broadcast_to(kr[:, :, None, :], (b, t, MLA_HEADS, LANES))],
                                    axis=-1).reshape(b, t, MLA_HEADS * 256)
            v_mla = kv[..., MLA_NOPE:].reshape(b, t, MLA_HEADS * MLA_V)
            a = _flash(q_mla, k_mla, v_mla, heads=MLA_HEADS, dq=256, dv=MLA_V, q_col0=0, k_col0=0,
                       v_col0=0, q_row_tiles0=ctx_tiles, n_q_tiles=n_q, scale=MLA_SCALE, tk=tk)

            n_sq = SWA_HEADS * SWA_HD
            sq = rp[:, L:, :n_sq].reshape(b, s, SWA_HEADS, SWA_HD)
            sk = rp[..., n_sq:n_sq + n_sv].reshape(b, t, SWA_KV_HEADS, SWA_HD)
            sv = pp[..., :n_sv].reshape(b, t, SWA_KV_HEADS, SWA_HD)
            w = _swa_latent(sq, sk[:, L:], sv[:, L:], sk[:, :L], sv[:, :L], swa_sink[j])
            mix = jnp.concatenate([a, w.astype(BF16)], axis=-1)
            w_out = cd_w_out[j]
            assert not with_ctx

        rows = mix.shape[1]
        y = _matmul(mix.reshape(b * rows, -1), w_out.astype(BF16), tm=512, tn=1024).reshape(b, rows, d)
        g1r = jnp.concatenate([jnp.broadcast_to(g1[0::2], (b, L, d)), jnp.broadcast_to(g1[1::2], (b, s, d))], axis=1)
        sh2r = jnp.concatenate([jnp.broadcast_to(sh2[0::2], (b, L, d)), jnp.broadcast_to(sh2[1::2], (b, s, d))], axis=1)
        sc2r = jnp.concatenate([jnp.broadcast_to(sc2[0::2], (b, L, d)), jnp.broadcast_to(sc2[1::2], (b, s, d))], axis=1)
        g2r = jnp.concatenate([jnp.broadcast_to(g2[0::2], (b, L, d)), jnp.broadcast_to(g2[1::2], (b, s, d))], axis=1)
        r0 = 0 if with_ctx else L
        xa = xs[:, r0:] + g1r[:, r0:] * _rms_norm(y, norm_mix_post[layer])
        f_in = _rms_norm(xa, norm_ffn_pre[layer]) * (1.0 + sc2r[:, r0:]) + sh2r[:, r0:]
        f = _moe(f_in.reshape(-1, d), router_w[layer], router_b[layer], exp_w_gate[layer].astype(BF16),
                 exp_w_up[layer].astype(BF16), exp_w_down[layer].astype(BF16), sh_w_gate[layer].astype(BF16),
                 sh_w_up[layer].astype(BF16), sh_w_down[layer].astype(BF16)).reshape(xa.shape)
        xa = xa + g2r[:, r0:] * _rms_norm(f, norm_ffn_post[layer])
        xs = xa if with_ctx else jnp.concatenate([xs[:, :L], xa], axis=1)
    return xs[:, L:]


AB_PLAIN = (DIFF_HEADS * DIFF_VD + 2 * MLSTM_HEADS * MLSTM_QK + 2 * MLSTM_HEADS * MLSTM_V)
```

```python
import functools
import math

import jax
import jax.numpy as jnp
import numpy as np
from jax import lax
from jax.experimental import pallas as pl
from jax.experimental.pallas import tpu as pltpu

F32 = jnp.float32
BF16 = jnp.bfloat16

GRID_W = 64
ROPE_BASE = 10000.0
NORM_EPS = 1e-6

DIFF_HEADS = 8
DIFF_HD = 64
DIFF_VD = 2 * DIFF_HD
MLSTM_HEADS = 8
MLSTM_QK = 64
MLSTM_V = 128
MLSTM_CHUNK = 128
MLA_HEADS = 8
MLA_NOPE = 128
MLA_ROPE = 64
MLA_V = 128
Q_LORA = 448
KV_LORA = 512
MLA_SCALE = (MLA_NOPE + MLA_ROPE) ** -0.5
SWA_HEADS = 8
SWA_KV_HEADS = 2
SWA_HD = 128
WINDOW = 128
BLOCK_Q = 128
N_EXPERTS = 64
TOP_K = 6
N_GROUPS = 8
TOPK_GROUPS = 4
ROUTED_SCALE = 2.5
AB_PLAIN = DIFF_HEADS * DIFF_VD + 2 * MLSTM_HEADS * MLSTM_QK + 2 * MLSTM_HEADS * MLSTM_V

LANES = 128
LOG2E = 1.4426950408889634
VMEM_LIMIT = 56 * 1024 * 1024

ROW_TILE = 256
EXPERT_ROWS = 256


def _cparams(sem):
    return pltpu.CompilerParams(dimension_semantics=sem, vmem_limit_bytes=VMEM_LIMIT)


def _pair_swap(a):
    lane = lax.broadcasted_iota(jnp.int32, a.shape, 1)
    return jnp.where(lane % 2 == 0, pltpu.roll(a, LANES - 1, 1), pltpu.roll(a, 1, 1))


def _norm_proj_kernel(x_ref, g_ref, sh_ref, sc_ref, w_ref, cos_ref, sin_ref, o_ref, h_scr, *,
                      n_valid, pattern):
    j = pl.program_id(2)

    @pl.when(j == 0)
    def _():
        x = x_ref[0].astype(F32)
        ms = jnp.sum(x * x, axis=-1, keepdims=True) * (1.0 / n_valid)
        y = x * lax.rsqrt(ms + NORM_EPS) * g_ref[...]
        h_scr[...] = (y * (1.0 + sc_ref[0]) + sh_ref[0]).astype(BF16)

    acc = jnp.dot(h_scr[...], w_ref[...], preferred_element_type=F32)
    if pattern is None:
        o_ref[0] = acc.astype(o_ref.dtype)
    else:
        for c, tbl in enumerate(pattern):
            a = acc[:, c * LANES:(c + 1) * LANES]
            if tbl >= 0:
                a = a * cos_ref[tbl] + _pair_swap(a) * sin_ref[tbl]
            o_ref[0, :, c * LANES:(c + 1) * LANES] = a.astype(o_ref.dtype)


def _norm_proj(x, g, shift, scale, w, *, seg_tiles, tn, out_dtype=BF16, n_valid=None,
               rope=None, pattern=None):
    b, t, k = x.shape
    n = w.shape[1]
    tm = ROW_TILE
    assert t % tm == 0 and n % tn == 0 and tn % LANES == 0
    if rope is None:
        cos = sin = jnp.zeros((1, tm, LANES), F32)
        tbl_map = lambda bi, i, j: (0, 0, 0)
    else:
        cos, sin = rope
        tbl_map = lambda bi, i, j: (0, i, 0)
    ntab = cos.shape[0]
    kern = functools.partial(_norm_proj_kernel, n_valid=float(n_valid or k), pattern=pattern)
    mod_map = lambda bi, i, j: (bi * 2 + jnp.where(i >= seg_tiles, 1, 0), 0, 0)
    return pl.pallas_call(
        kern,
        grid=(b, t // tm, n // tn),
        in_specs=[
            pl.BlockSpec((1, tm, k), lambda bi, i, j: (bi, i, 0)),
            pl.BlockSpec((1, k), lambda bi, i, j: (0, 0)),
            pl.BlockSpec((1, 1, k), mod_map),
            pl.BlockSpec((1, 1, k), mod_map),
            pl.BlockSpec((k, tn), lambda bi, i, j: (0, j)),
            pl.BlockSpec((ntab, tm, LANES), tbl_map),
            pl.BlockSpec((ntab, tm, LANES), tbl_map),
        ],
        out_specs=pl.BlockSpec((1, tm, tn), lambda bi, i, j: (bi, i, j)),
        out_shape=jax.ShapeDtypeStruct((b, t, n), out_dtype),
        scratch_shapes=[pltpu.VMEM((tm, k), BF16)],
        compiler_params=_cparams(("parallel", "parallel", "arbitrary")),
    )(x, g.reshape(1, k).astype(F32), shift, scale, w, cos, sin)


def _matmul_kernel(a_ref, w_ref, o_ref):
    o_ref[...] = jnp.dot(a_ref[...].astype(BF16), w_ref[...],
                         preferred_element_type=F32).astype(o_ref.dtype)


def _matmul(a, w, *, tm, tn, out_dtype=F32):
    m, k = a.shape
    n = w.shape[1]
    tn = min(tn, n)
    assert m % tm == 0 and n % tn == 0
    return pl.pallas_call(
        _matmul_kernel,
        grid=(m // tm, n // tn),
        in_specs=[pl.BlockSpec((tm, k), lambda i, j: (i, 0)),
                  pl.BlockSpec((k, tn), lambda i, j: (0, j))],
        out_specs=pl.BlockSpec((tm, tn), lambda i, j: (i, j)),
        out_shape=jax.ShapeDtypeStruct((m, n), out_dtype),
        compiler_params=_cparams(("parallel", "arbitrary")),
    )(a, w)


def _attn_kernel(q_ref, k_ref, v_ref, lam_ref, dn_ref, o_ref, q_scr, s_scr, m_scr, l_scr, acc_scr, *,
                 scale, diff, lam_init, ctx_tiles, ctx_len, tq, tk, t):
    i = pl.program_id(2)
    q = q_ref[0].astype(F32) * (scale * LOG2E)
    if diff:
        lane = lax.broadcasted_iota(jnp.int32, q.shape, 1)
        half = q.shape[1] // 2
        q_scr[0:tq] = jnp.where(lane < half, q, 0.0).astype(BF16)
        q_scr[tq:2 * tq] = jnp.where(lane >= half, q, 0.0).astype(BF16)
    else:
        q_scr[...] = q.astype(BF16)

    def attend(kv_len, chunk):
        n_chunks = kv_len // chunk
        groups = chunk // LANES

        def pass1(c, carry):
            k = k_ref[0, pl.ds(pl.multiple_of(c * chunk, chunk), chunk), :]
            s = lax.dot_general(q_scr[...], k, (((1,), (1,)), ((), ())), preferred_element_type=F32)
            s_scr[c, :, 0:chunk] = s
            m = m_scr[...]
            for g in range(groups):
                m = jnp.maximum(m, s[:, g * LANES:(g + 1) * LANES])
            m_scr[...] = m
            return carry

        def pass2(c, carry):
            s = s_scr[c, :, 0:chunk]
            p = jnp.exp2(s - pltpu.repeat(m_scr[...], groups, axis=1))
            l = l_scr[...]
            for g in range(groups):
                l = l + p[:, g * LANES:(g + 1) * LANES]
            l_scr[...] = l
            v = v_ref[0, pl.ds(pl.multiple_of(c * chunk, chunk), chunk), :]
            acc_scr[...] += jnp.dot(p.astype(BF16), v, preferred_element_type=F32)
            return carry

        m_scr[...] = jnp.full(m_scr.shape, -jnp.inf, F32)
        if n_chunks == 1:
            pass1(0, 0)
        else:
            lax.fori_loop(0, n_chunks, pass1, 0)
        m_scr[...] = jnp.broadcast_to(jnp.max(m_scr[...], axis=-1, keepdims=True), m_scr.shape)
        l_scr[...] = jnp.zeros(l_scr.shape, F32)
        acc_scr[...] = jnp.zeros(acc_scr.shape, F32)
        if n_chunks == 1:
            pass2(0, 0)
        else:
            lax.fori_loop(0, n_chunks, pass2, 0)

        o = acc_scr[...] / jnp.sum(l_scr[...], axis=-1, keepdims=True)
        if diff:
            lv = lam_ref[...]
            lam = (jnp.exp(jnp.sum(lv[0:1] * lv[1:2], axis=-1, keepdims=True))
                   - jnp.exp(jnp.sum(lv[2:3] * lv[3:4], axis=-1, keepdims=True)) + lam_init)
            a = o[0:tq] - lam * o[tq:2 * tq]
            ms = jnp.mean(a * a, axis=-1, keepdims=True)
            a = a * lax.rsqrt(ms + NORM_EPS) * dn_ref[...] * (1.0 - lam_init)
            o_ref[0] = a.astype(o_ref.dtype)
        else:
            o_ref[0] = o.astype(o_ref.dtype)

    if ctx_tiles:
        pl.when(i < ctx_tiles)(lambda: attend(ctx_len, ctx_len))
        pl.when(i >= ctx_tiles)(lambda: attend(t, tk))
    else:
        attend(t, tk)


def _attention(q, k, v, *, heads, dq, dv, q_col0, k_col0, v_col0, tq, scale, tk, diff=False,
               lam_vec=None, d_norm=None, lam_init=0.0, ctx_tiles=0, ctx_len=0):
    b, t, _ = k.shape
    sq = q.shape[1]
    assert t % tk == 0 and sq % tq == 0 and tk % LANES == 0 and ctx_len % LANES == 0
    assert ctx_len <= tk
    rows = 2 * tq if diff else tq
    if lam_vec is None:
        lam_vec = jnp.zeros((4, DIFF_HD), F32)
        d_norm = jnp.zeros((dv,), F32)
    kern = functools.partial(_attn_kernel, scale=scale, diff=diff, lam_init=lam_init,
                             ctx_tiles=ctx_tiles, ctx_len=ctx_len, tq=tq, tk=tk, t=t)
    return pl.pallas_call(
        kern,
        grid=(b, heads, sq // tq),
        in_specs=[
            pl.BlockSpec((1, tq, dq), lambda bi, h, i: (bi, i, q_col0 + h)),
            pl.BlockSpec((1, t, dq), lambda bi, h, i: (bi, 0, k_col0 + h)),
            pl.BlockSpec((1, t, dv), lambda bi, h, i: (bi, 0, v_col0 + h)),
            pl.BlockSpec((4, DIFF_HD), lambda bi, h, i: (0, 0)),
            pl.BlockSpec((1, dv), lambda bi, h, i: (0, 0)),
        ],
        out_specs=pl.BlockSpec((1, tq, dv), lambda bi, h, i: (bi, i, h)),
        out_shape=jax.ShapeDtypeStruct((b, sq, heads * dv), BF16),
        scratch_shapes=[pltpu.VMEM((rows, dq), BF16), pltpu.VMEM((t // tk, rows, tk), F32),
                        pltpu.VMEM((rows, LANES), F32), pltpu.VMEM((rows, LANES), F32),
                        pltpu.VMEM((rows, dv), F32)],
        compiler_params=_cparams(("parallel", "parallel", "arbitrary")),
    )(q, k, v, lam_vec.astype(F32), d_norm.reshape(1, dv).astype(F32))


def _expert_kernel(be_ref, x_ref, wg_ref, wu_ref, wd_ref, o_ref, wg_s, wu_s, wd_s):
    i = pl.program_id(0)

    @pl.when((i == 0) | (be_ref[i] != be_ref[jnp.maximum(i - 1, 0)]))
    def _():
        wg_s[...] = wg_ref[0].astype(BF16)
        wu_s[...] = wu_ref[0].astype(BF16)
        wd_s[...] = wd_ref[0].astype(BF16)

    x = x_ref[...].astype(BF16)
    g = jnp.dot(x, wg_s[...], preferred_element_type=F32)
    u = jnp.dot(x, wu_s[...], preferred_element_type=F32)
    a = (g * jax.nn.sigmoid(g) * u).astype(BF16)
    o_ref[...] = jnp.dot(a, wd_s[...], preferred_element_type=F32)


def _expert_ffn(x, block_e, wg, wu, wd, *, tm):
    n, d = x.shape
    ff = wg.shape[2]
    assert n % tm == 0
    grid_spec = pltpu.PrefetchScalarGridSpec(
        num_scalar_prefetch=1,
        grid=(n // tm,),
        in_specs=[
            pl.BlockSpec((tm, d), lambda i, be: (i, 0)),
            pl.BlockSpec((1, d, ff), lambda i, be: (be[i], 0, 0)),
            pl.BlockSpec((1, d, ff), lambda i, be: (be[i], 0, 0)),
            pl.BlockSpec((1, ff, d), lambda i, be: (be[i], 0, 0)),
        ],
        out_specs=pl.BlockSpec((tm, d), lambda i, be: (i, 0)),
        scratch_shapes=[pltpu.VMEM((d, ff), BF16), pltpu.VMEM((d, ff), BF16), pltpu.VMEM((ff, d), BF16)],
    )
    return pl.pallas_call(
        _expert_kernel,
        grid_spec=grid_spec,
        out_shape=jax.ShapeDtypeStruct((n, d), F32),
        compiler_params=_cparams(("arbitrary",)),
    )(block_e, x, wg, wu, wd)


def _rms_norm(x, g):
    xf = x.astype(F32)
    y = xf * lax.rsqrt(jnp.mean(xf * xf, axis=-1, keepdims=True) + NORM_EPS)
    return y * g.astype(F32)


def _rope_tables(s, ctx_len, dim):
    rows = s // GRID_W
    row = jnp.repeat(jnp.arange(rows), GRID_W)
    col = jnp.tile(jnp.arange(GRID_W), rows)
    quarter = dim // 4
    inv = ROPE_BASE ** (-jnp.arange(quarter, dtype=F32) / quarter)
    ang = jnp.concatenate([row.astype(F32)[:, None] * inv, col.astype(F32)[:, None] * inv], axis=-1)
    cos = jnp.repeat(jnp.cos(ang), 2, axis=-1)
    sin = jnp.repeat(jnp.sin(ang), 2, axis=-1) * jnp.tile(jnp.array([-1.0, 1.0], F32), dim // 2)
    cos = jnp.concatenate([jnp.ones((ctx_len, dim), F32), cos], axis=0)
    sin = jnp.concatenate([jnp.zeros((ctx_len, dim), F32), sin], axis=0)
    return cos, sin


def _mlstm_chunked(q, k, v, i_pre, f_pre, state):
    b, t, nh, dk = k.shape
    nc = t // MLSTM_CHUNK
    causal = jnp.tril(jnp.ones((MLSTM_CHUNK, MLSTM_CHUNK), bool))[None, :, :, None]

    def chunks(a):
        return jnp.moveaxis(a.reshape((b, nc, MLSTM_CHUNK) + a.shape[2:]), 1, 0)

    def step(carry, xs):
        c_st, n_st, m_st = carry
        *qx, kb, vb, ib, fb = xs
        bcum = jnp.cumsum(jax.nn.log_sigmoid(fb), axis=1)
        btot = bcum[:, -1]
        w_end = btot[:, None] - bcum + ib
        m_new = jnp.maximum(btot + m_st, jnp.max(w_end, axis=1))
        decay = jnp.exp(btot + m_st - m_new)
        w_k = jnp.exp(w_end - m_new[:, None])
        c_new = decay[..., None, None] * c_st + jnp.einsum('blh,blhv,blhk->bhvk', w_k, vb, kb)
        n_new = decay[..., None] * n_st + jnp.einsum('blh,blhk->bhk', w_k, kb)
        if not qx:
            return (c_new, n_new, m_new), None
        qb = qx[0]
        log_d = bcum[:, :, None] - bcum[:, None] + ib[:, None]
        log_d = jnp.where(causal, log_d, -jnp.inf)
        log_inter = bcum + m_st[:, None]
        m_row = jnp.maximum(log_inter, jnp.max(log_d, axis=2))
        w_intra = jnp.exp(log_d - m_row[:, :, None])
        w_inter = jnp.exp(log_inter - m_row)
        qk = jnp.einsum('bqhd,bshd->bqsh', qb, kb) * w_intra
        num = w_inter[..., None] * jnp.einsum('bqhk,bhvk->bqhv', qb, c_st) + jnp.einsum('bqsh,bshv->bqhv', qk, vb)
        den = w_inter * jnp.einsum('bqhk,bhk->bqh', qb, n_st) + jnp.sum(qk, axis=2)
        h_out = num / jnp.maximum(jnp.abs(den), jnp.exp(-m_row))[..., None]
        return (c_new, n_new, m_new), h_out

    xs = tuple(chunks(a) for a in ((() if q is None else (q,)) + (k, v, i_pre, f_pre)))
    state, hs = lax.scan(step, state, xs)
    if q is None:
        return None, state
    return jnp.moveaxis(hs, 0, 1).reshape(b, t, nh, v.shape[-1]), state


def _flip(a):
    return None if a is None else jnp.flip(a, axis=1)


def _mlstm_mixer(mq, mk, mv, mo, gates, gate_b, m_norm, ctx_len, with_ctx):
    b, t, _ = mq.shape
    L = ctx_len

    def heads(a, d):
        return a.reshape(b, a.shape[1], MLSTM_HEADS, d).astype(F32)

    q = heads(mq, MLSTM_QK)
    k = heads(mk, MLSTM_QK) * MLSTM_QK ** -0.5
    v = heads(mv, MLSTM_V)
    g = gates[..., :4 * MLSTM_HEADS].reshape(b, t, 4, MLSTM_HEADS).astype(F32) + gate_b.astype(F32)
    zero = (jnp.zeros((b, MLSTM_HEADS, MLSTM_V, MLSTM_QK), F32),
            jnp.zeros((b, MLSTM_HEADS, MLSTM_QK), F32), jnp.zeros((b, MLSTM_HEADS), F32))
    qc, kc, vc, gc = (a[:, :L] for a in (q, k, v, g))
    ql, kl, vl, gl = (a[:, L:] for a in (q, k, v, g))
    if not with_ctx:
        qc = None
    hcf, st_f = _mlstm_chunked(qc, kc, vc, gc[:, :, 0], gc[:, :, 1], zero)
    hlf, _ = _mlstm_chunked(ql, kl, vl, gl[:, :, 0], gl[:, :, 1], st_f)
    hcb, st_b = _mlstm_chunked(_flip(qc), _flip(kc), _flip(vc), _flip(gc[:, :, 2]), _flip(gc[:, :, 3]), zero)
    hlb, _ = _mlstm_chunked(_flip(ql), _flip(kl), _flip(vl), _flip(gl[:, :, 2]), _flip(gl[:, :, 3]), st_b)

    def m_out(hf, hb_rev, o_pre):
        o = jax.nn.sigmoid(heads(o_pre, MLSTM_V))
        return (_rms_norm(hf + jnp.flip(hb_rev, axis=1), m_norm) * o).reshape(b, hf.shape[1], -1)

    out_l = m_out(hlf, hlb, mo[:, L:])
    if not with_ctx:
        return out_l
    return jnp.concatenate([m_out(hcf, hcb, mo[:, :L]), out_l], axis=1)


def _swa_latent(q, k, v, k_ctx, v_ctx, sink):
    b, s = q.shape[:2]
    nb = s // BLOCK_Q
    r = SWA_HEADS // SWA_KV_HEADS

    def band(a):
        ap = jnp.pad(a, ((0, 0), (BLOCK_Q, BLOCK_Q), (0, 0), (0, 0))).reshape(b, nb + 2, BLOCK_Q, SWA_KV_HEADS, SWA_HD)
        return jnp.concatenate([ap[:, :-2], ap[:, 1:-1], ap[:, 2:]], axis=2)

    kb, vb = band(k), band(v)
    qb = q.reshape(b, nb, BLOCK_Q, SWA_KV_HEADS, r, SWA_HD)
    scale = SWA_HD ** -0.5
    s_loc = jnp.einsum('bnqgrd,bnkgd->bngrqk', qb, kb, preferred_element_type=F32) * scale
    q_pos = jnp.arange(nb)[:, None, None] * BLOCK_Q + jnp.arange(BLOCK_Q)[None, :, None]
    k_pos = jnp.arange(nb)[:, None, None] * BLOCK_Q - BLOCK_Q + jnp.arange(3 * BLOCK_Q)[None, None, :]
    valid = (jnp.abs(k_pos - q_pos) <= WINDOW) & (k_pos >= 0) & (k_pos < s)
    s_loc = jnp.where(valid[None, :, None, None], s_loc, -jnp.inf)
    s_ctx = jnp.einsum('bnqgrd,bkgd->bngrqk', qb, k_ctx, preferred_element_type=F32) * scale
    sink_col = jnp.broadcast_to(sink.astype(F32).reshape(1, 1, SWA_KV_HEADS, r, 1, 1), s_loc.shape[:-1] + (1,))
    p = jax.nn.softmax(jnp.concatenate([s_loc, s_ctx, sink_col], axis=-1), axis=-1)
    n_loc = 3 * BLOCK_Q
    out = (jnp.einsum('bngrqk,bnkgd->bnqgrd', p[..., :n_loc], vb.astype(F32))
           + jnp.einsum('bngrqk,bkgd->bnqgrd', p[..., n_loc:-1], v_ctx.astype(F32)))
    return out.reshape(b, s, SWA_HEADS * SWA_HD)


def _moe(tokens, router_w, router_b, wg, wu, wd, sg, su, sd):
    n, d = tokens.shape
    scores = jax.nn.sigmoid(jnp.dot(tokens, router_w, preferred_element_type=F32,
                                    precision=lax.Precision.HIGHEST))
    choice = scores + router_b.astype(F32)
    per_group = N_EXPERTS // N_GROUPS
    group_score = lax.top_k(choice.reshape(n, N_GROUPS, per_group), 2)[0].sum(-1)
    top_groups = lax.top_k(group_score, TOPK_GROUPS)[1]
    group_ok = jnp.any(top_groups[:, :, None] == jnp.arange(N_GROUPS)[None, None, :], axis=1)
    expert_ok = jnp.repeat(group_ok, per_group, axis=1)
    top_e = lax.top_k(jnp.where(expert_ok, choice, -jnp.inf), TOP_K)[1]
    gate = jnp.take_along_axis(scores, top_e, axis=1)
    gate = gate / jnp.sum(gate, axis=-1, keepdims=True) * ROUTED_SCALE

    blk = EXPERT_ROWS
    n_slots = n * TOP_K
    flat_e = top_e.reshape(-1)
    order = jnp.argsort(flat_e)
    e_sorted = flat_e[order]
    counts = jnp.bincount(flat_e, length=N_EXPERTS)
    blocks_per_e = (counts + blk - 1) // blk
    blocks_end = jnp.cumsum(blocks_per_e)
    rank = jnp.arange(n_slots) - (jnp.cumsum(counts) - counts)[e_sorted]
    dest_sorted = (blocks_end - blocks_per_e)[e_sorted] * blk + rank
    n_blocks = -(-n_slots // blk) + N_EXPERTS
    n_rows = n_blocks * blk
    row_tok = jnp.zeros((n_rows,), jnp.int32).at[dest_sorted].set((order // TOP_K).astype(jnp.int32))
    dest = jnp.zeros((n_slots,), jnp.int32).at[order].set(dest_sorted.astype(jnp.int32)).reshape(n, TOP_K)
    block_e = jnp.minimum(jnp.searchsorted(blocks_end, jnp.arange(n_blocks), side='right'),
                          N_EXPERTS - 1).astype(jnp.int32)

    tok_bf = tokens.astype(BF16)
    y = _expert_ffn(tok_bf[row_tok], block_e, wg, wu, wd, tm=blk)
    routed = jnp.sum(y[dest] * gate[:, :, None], axis=1)
    shared = _expert_ffn(tok_bf, jnp.zeros((n // ROW_TILE,), jnp.int32), sg[None], su[None], sd[None],
                         tm=ROW_TILE)
    return routed + shared


def kernel(x, c, ctx, c_ctx, mod_w, mod_b, norm_mix_pre, norm_mix_post, norm_ffn_pre, norm_ffn_post, ab_w_in, ab_w_out, diff_lambda, diff_norm, mlstm_gate_b, mlstm_norm, cd_w_in, cd_w_out, mla_q_norm, mla_w_uq, mla_kv_norm, mla_w_ukv, swa_sink, router_w, router_b, exp_w_gate, exp_w_up, exp_w_down, sh_w_gate, sh_w_up, sh_w_down):
    b, s, d = x.shape
    L = ctx.shape[1]
    t = L + s
    assert L == ROW_TILE and s % ROW_TILE == 0
    depth = mod_w.shape[0]
    ctx_tiles = L // ROW_TILE
    tk = 768 if t % 768 == 0 else ROW_TILE

    cos64, sin64 = _rope_tables(s, L, DIFF_HD)
    cos64 = jnp.tile(cos64, (1, LANES // DIFF_HD))
    sin64 = jnp.tile(sin64, (1, LANES // DIFF_HD))
    cos128, sin128 = _rope_tables(s, L, SWA_HD)
    cos_kr = jnp.concatenate([cos64[:, :MLA_ROPE], jnp.ones((t, LANES - MLA_ROPE), F32)], axis=1)
    sin_kr = jnp.concatenate([sin64[:, :MLA_ROPE], jnp.zeros((t, LANES - MLA_ROPE), F32)], axis=1)

    xs = jnp.concatenate([ctx, x], axis=1)

    for layer in range(depth):
        with_ctx = layer < depth - 1
        j = layer // 2
        mod_l = jax.nn.silu(c) @ mod_w[layer] + mod_b[layer]
        mod_c = jax.nn.silu(c_ctx) @ mod_w[layer] + mod_b[layer]
        mods = jnp.stack([jnp.broadcast_to(mod_c, (b, 6 * d)), mod_l], axis=1).reshape(b * 2, 1, 6, d)
        sh1, sc1, g1, sh2, sc2, g2 = (mods[:, :, m] for m in range(6))

        proj = functools.partial(_norm_proj, xs, norm_mix_pre[layer], sh1, sc1, seg_tiles=ctx_tiles)
        if layer % 2 == 0:
            w_in = ab_w_in[j].astype(BF16)
            n_qk = 2 * DIFF_HEADS * 2 * DIFF_HD
            qk = proj(w_in[:, :n_qk], tn=1024, rope=(cos64[None], sin64[None]), pattern=(0,) * 8)
            pv = proj(w_in[:, n_qk:n_qk + AB_PLAIN], tn=1024)
            w_gates = jnp.pad(w_in[:, n_qk + AB_PLAIN:], ((0, 0), (0, LANES - 4 * MLSTM_HEADS)))
            gates = proj(w_gates, tn=LANES, out_dtype=F32)

            lam_init = 0.8 - 0.6 * math.exp(-0.3 * layer)
            a = _attention(qk if with_ctx else qk[:, L:], qk, pv, heads=DIFF_HEADS, dq=2 * DIFF_HD,
                           dv=DIFF_VD, q_col0=0, k_col0=DIFF_HEADS, v_col0=0, tq=ROW_TILE,
                           scale=DIFF_HD ** -0.5, tk=tk, diff=True, lam_vec=diff_lambda[j],
                           d_norm=diff_norm[j], lam_init=lam_init,
                           ctx_tiles=ctx_tiles if with_ctx else 0, ctx_len=L)
            o0 = DIFF_HEADS * DIFF_VD
            o1 = o0 + MLSTM_HEADS * MLSTM_QK
            o2 = o1 + MLSTM_HEADS * MLSTM_QK
            o3 = o2 + MLSTM_HEADS * MLSTM_V
            m = _mlstm_mixer(pv[..., o0:o1], pv[..., o1:o2], pv[..., o2:o3], pv[..., o3:], gates,
                             mlstm_gate_b[j], mlstm_norm[j], L, with_ctx)
            mix = jnp.concatenate([a, m.astype(BF16)], axis=-1)
            w_out = ab_w_out[j]
        else:
            assert not with_ctx
            w_in = cd_w_in[j].astype(BF16)
            c0 = Q_LORA
            c1 = c0 + KV_LORA
            c2 = c1 + MLA_ROPE
            c3 = c2 + SWA_HEADS * SWA_HD
            c4 = c3 + SWA_KV_HEADS * SWA_HD
            w_rope = jnp.concatenate([w_in[:, c2:c4], w_in[:, c1:c2],
                                      jnp.zeros((d, LANES - MLA_ROPE), BF16)], axis=1)
            n_rope = w_rope.shape[1]
            rp = proj(w_rope, tn=n_rope, rope=(jnp.stack([cos128, cos_kr]), jnp.stack([sin128, sin_kr])),
                      pattern=(0,) * (n_rope // LANES - 1) + (1,))
            q_pad = 512 - Q_LORA
            w_plain = jnp.concatenate([w_in[:, c4:], w_in[:, :c0], jnp.zeros((d, q_pad), BF16),
                                       w_in[:, c0:c1]], axis=1)
            pp = proj(w_plain, tn=w_plain.shape[1])
            n_sv = SWA_KV_HEADS * SWA_HD
            cq = pp[..., n_sv:n_sv + 512]
            ckv = pp[..., n_sv + 512:]

            hq = MLA_NOPE + MLA_ROPE
            w_uq = mla_w_uq[j].astype(BF16).reshape(Q_LORA, MLA_HEADS, hq)
            w_uq = jnp.pad(w_uq, ((0, q_pad), (0, 0), (0, 256 - hq))).reshape(512, MLA_HEADS * 256)
            no_mod = jnp.zeros((b * 2, 1, 512), F32)
            qn = jnp.pad(mla_q_norm[j], (0, q_pad))
            q_mla = _norm_proj(cq[:, L:], qn, no_mod, no_mod, w_uq, seg_tiles=0, tn=1024, n_valid=Q_LORA,
                               rope=(cos_kr[None, L:], sin_kr[None, L:]), pattern=(-1, 0) * 4)
            kv = _norm_proj(ckv, mla_kv_norm[j], no_mod, no_mod, mla_w_ukv[j].astype(BF16),
                            seg_tiles=ctx_tiles, tn=1024).reshape(b, t, MLA_HEADS, MLA_NOPE + MLA_V)
            kr = rp[..., n_rope - LANES:]
            k_mla = jnp.concatenate([kv[..., :MLA_NOPE],
                                     jnp.broadcast_to(kr[:, :, None, :], (b, t, MLA_HEADS, LANES))],
                                    axis=-1).reshape(b, t, MLA_HEADS * 256)
            v_mla = kv[..., MLA_NOPE:].reshape(b, t, MLA_HEADS * MLA_V)
            a = _attention(q_mla, k_mla, v_mla, heads=MLA_HEADS, dq=256, dv=MLA_V, q_col0=0, k_col0=0,
                           v_col0=0, tq=512, scale=MLA_SCALE, tk=tk)

            n_sq = SWA_HEADS * SWA_HD
            sq = rp[:, L:, :n_sq].reshape(b, s, SWA_HEADS, SWA_HD)
            sk = rp[..., n_sq:n_sq + n_sv].reshape(b, t, SWA_KV_HEADS, SWA_HD)
            sv = pp[..., :n_sv].reshape(b, t, SWA_KV_HEADS, SWA_HD)
            w = _swa_latent(sq, sk[:, L:], sv[:, L:], sk[:, :L], sv[:, :L], swa_sink[j])
            mix = jnp.concatenate([a, w.astype(BF16)], axis=-1)
            w_out = cd_w_out[j]

        rows = mix.shape[1]
        y = _matmul(mix.reshape(b * rows, -1), w_out.astype(BF16), tm=512, tn=1024).reshape(b, rows, d)
        g1r = jnp.concatenate([jnp.broadcast_to(g1[0::2], (b, L, d)), jnp.broadcast_to(g1[1::2], (b, s, d))], axis=1)
        sh2r = jnp.concatenate([jnp.broadcast_to(sh2[0::2], (b, L, d)), jnp.broadcast_to(sh2[1::2], (b, s, d))], axis=1)
        sc2r = jnp.concatenate([jnp.broadcast_to(sc2[0::2], (b, L, d)), jnp.broadcast_to(sc2[1::2], (b, s, d))], axis=1)
        g2r = jnp.concatenate([jnp.broadcast_to(g2[0::2], (b, L, d)), jnp.broadcast_to(g2[1::2], (b, s, d))], axis=1)
        r0 = 0 if with_ctx else L
        xa = xs[:, r0:] + g1r[:, r0:] * _rms_norm(y, norm_mix_post[layer])
        f_in = _rms_norm(xa, norm_ffn_pre[layer]) * (1.0 + sc2r[:, r0:]) + sh2r[:, r0:]
        f = _moe(f_in.reshape(-1, d), router_w[layer], router_b[layer], exp_w_gate[layer], exp_w_up[layer],
                 exp_w_down[layer], sh_w_gate[layer], sh_w_up[layer], sh_w_down[layer]).reshape(xa.shape)
        xa = xa + g2r[:, r0:] * _rms_norm(f, norm_ffn_post[layer])
        xs = xa if with_ctx else jnp.concatenate([xs[:, :L], xa], axis=1)
    return xs[:, L:]
```

```python
import functools
import math

import jax
import jax.numpy as jnp
import numpy as np
from jax import lax
from jax.experimental import pallas as pl
from jax.experimental.pallas import tpu as pltpu

F32 = jnp.float32
BF16 = jnp.bfloat16

GRID_W = 64
ROPE_BASE = 10000.0
NORM_EPS = 1e-6

DIFF_HEADS = 8
DIFF_HD = 64
DIFF_VD = 2 * DIFF_HD
MLSTM_HEADS = 8
MLSTM_QK = 64
MLSTM_V = 128
MLSTM_CHUNK = 128
MLA_HEADS = 8
MLA_NOPE = 128
MLA_ROPE = 64
MLA_V = 128
Q_LORA = 448
KV_LORA = 512
MLA_SCALE = (MLA_NOPE + MLA_ROPE) ** -0.5
SWA_HEADS = 8
SWA_KV_HEADS = 2
SWA_HD = 128
WINDOW = 128
BLOCK_Q = 128
N_EXPERTS = 64
TOP_K = 6
N_GROUPS = 8
TOPK_GROUPS = 4
ROUTED_SCALE = 2.5
AB_PLAIN = DIFF_HEADS * DIFF_VD + 2 * MLSTM_HEADS * MLSTM_QK + 2 * MLSTM_HEADS * MLSTM_V

LANES = 128
LOG2E = 1.4426950408889634
VMEM_LIMIT = 56 * 1024 * 1024

ROW_TILE = 256
EXPERT_ROWS = 256


def _cparams(sem):
    return pltpu.CompilerParams(dimension_semantics=sem, vmem_limit_bytes=VMEM_LIMIT)


def _pair_swap(a):
    lane = lax.broadcasted_iota(jnp.int32, a.shape, 1)
    return jnp.where(lane % 2 == 0, pltpu.roll(a, LANES - 1, 1), pltpu.roll(a, 1, 1))


def _norm_proj_kernel(x_ref, g_ref, sh_ref, sc_ref, w_ref, cos_ref, sin_ref, o_ref, h_scr, *,
                      n_valid, pattern):
    j = pl.program_id(2)

    @pl.when(j == 0)
    def _():
        x = x_ref[0].astype(F32)
        ms = jnp.sum(x * x, axis=-1, keepdims=True) * (1.0 / n_valid)
        y = x * lax.rsqrt(ms + NORM_EPS) * g_ref[...]
        h_scr[...] = (y * (1.0 + sc_ref[0]) + sh_ref[0]).astype(BF16)

    acc = jnp.dot(h_scr[...], w_ref[...], preferred_element_type=F32)
    if pattern is None:
        o_ref[0] = acc.astype(o_ref.dtype)
    else:
        for c, tbl in enumerate(pattern):
            a = acc[:, c * LANES:(c + 1) * LANES]
            if tbl >= 0:
                a = a * cos_ref[tbl] + _pair_swap(a) * sin_ref[tbl]
            o_ref[0, :, c * LANES:(c + 1) * LANES] = a.astype(o_ref.dtype)


def _norm_proj(x, g, shift, scale, w, *, seg_tiles, tn, out_dtype=BF16, n_valid=None,
               rope=None, pattern=None):
    b, t, k = x.shape
    n = w.shape[1]
    tm = ROW_TILE
    assert t % tm == 0 and n % tn == 0 and tn % LANES == 0
    if rope is None:
        cos = sin = jnp.zeros((1, tm, LANES), F32)
        tbl_map = lambda bi, i, j: (0, 0, 0)
    else:
        cos, sin = rope
        tbl_map = lambda bi, i, j: (0, i, 0)
    ntab = cos.shape[0]
    kern = functools.partial(_norm_proj_kernel, n_valid=float(n_valid or k), pattern=pattern)
    mod_map = lambda bi, i, j: (bi * 2 + jnp.where(i >= seg_tiles, 1, 0), 0, 0)
    return pl.pallas_call(
        kern,
        grid=(b, t // tm, n // tn),
        in_specs=[
            pl.BlockSpec((1, tm, k), lambda bi, i, j: (bi, i, 0)),
            pl.BlockSpec((1, k), lambda bi, i, j: (0, 0)),
            pl.BlockSpec((1, 1, k), mod_map),
            pl.BlockSpec((1, 1, k), mod_map),
            pl.BlockSpec((k, tn), lambda bi, i, j: (0, j)),
            pl.BlockSpec((ntab, tm, LANES), tbl_map),
            pl.BlockSpec((ntab, tm, LANES), tbl_map),
        ],
        out_specs=pl.BlockSpec((1, tm, tn), lambda bi, i, j: (bi, i, j)),
        out_shape=jax.ShapeDtypeStruct((b, t, n), out_dtype),
        scratch_shapes=[pltpu.VMEM((tm, k), BF16)],
        compiler_params=_cparams(("parallel", "parallel", "arbitrary")),
    )(x, g.reshape(1, k).astype(F32), shift, scale, w, cos, sin)


def _matmul_kernel(a_ref, w_ref, o_ref):
    o_ref[...] = jnp.dot(a_ref[...].astype(BF16), w_ref[...],
                         preferred_element_type=F32).astype(o_ref.dtype)


def _matmul(a, w, *, tm, tn, out_dtype=F32):
    m, k = a.shape
    n = w.shape[1]
    tn = min(tn, n)
    assert m % tm == 0 and n % tn == 0
    return pl.pallas_call(
        _matmul_kernel,
        grid=(m // tm, n // tn),
        in_specs=[pl.BlockSpec((tm, k), lambda i, j: (i, 0)),
                  pl.BlockSpec((k, tn), lambda i, j: (0, j))],
        out_specs=pl.BlockSpec((tm, tn), lambda i, j: (i, j)),
        out_shape=jax.ShapeDtypeStruct((m, n), out_dtype),
        compiler_params=_cparams(("parallel", "arbitrary")),
    )(a, w)


def _attn_kernel(q_ref, k_ref, v_ref, lam_ref, dn_ref, o_ref, q_scr, s_scr, m_scr, l_scr, acc_scr, *,
                 scale, diff, lam_init, ctx_tiles, ctx_len, tq, tk, t):
    i = pl.program_id(2)
    q = q_ref[0].astype(F32) * (scale * LOG2E)
    if diff:
        lane = lax.broadcasted_iota(jnp.int32, q.shape, 1)
        half = q.shape[1] // 2
        q_scr[0:tq] = jnp.where(lane < half, q, 0.0).astype(BF16)
        q_scr[tq:2 * tq] = jnp.where(lane >= half, q, 0.0).astype(BF16)
    else:
        q_scr[...] = q.astype(BF16)

    def attend(kv_len, chunk):
        n_chunks = kv_len // chunk
        groups = chunk // LANES

        def pass1(c, carry):
            k = k_ref[0, pl.ds(pl.multiple_of(c * chunk, chunk), chunk), :]
            s = lax.dot_general(q_scr[...], k, (((1,), (1,)), ((), ())), preferred_element_type=F32)
            s_scr[c, :, 0:chunk] = s
            m = m_scr[...]
            for g in range(groups):
                m = jnp.maximum(m, s[:, g * LANES:(g + 1) * LANES])
            m_scr[...] = m
            return carry

        def pass2(c, carry):
            s = s_scr[c, :, 0:chunk]
            p = jnp.exp2(s - jnp.concatenate([m_scr[...]] * groups, axis=1))
            l = l_scr[...]
            for g in range(groups):
                l = l + p[:, g * LANES:(g + 1) * LANES]
            l_scr[...] = l
            v = v_ref[0, pl.ds(pl.multiple_of(c * chunk, chunk), chunk), :]
            acc_scr[...] += jnp.dot(p.astype(BF16), v, preferred_element_type=F32)
            return carry

        m_scr[...] = jnp.full(m_scr.shape, -jnp.inf, F32)
        if n_chunks == 1:
            pass1(0, 0)
        else:
            lax.fori_loop(0, n_chunks, pass1, 0)
        m_scr[...] = jnp.broadcast_to(jnp.max(m_scr[...], axis=-1, keepdims=True), m_scr.shape)
        l_scr[...] = jnp.zeros(l_scr.shape, F32)
        acc_scr[...] = jnp.zeros(acc_scr.shape, F32)
        if n_chunks == 1:
            pass2(0, 0)
        else:
            lax.fori_loop(0, n_chunks, pass2, 0)

        o = acc_scr[...] / jnp.sum(l_scr[...], axis=-1, keepdims=True)
        if diff:
            lv = lam_ref[...]
            lam = (jnp.exp(jnp.sum(lv[0:1] * lv[1:2], axis=-1, keepdims=True))
                   - jnp.exp(jnp.sum(lv[2:3] * lv[3:4], axis=-1, keepdims=True)) + lam_init)
            a = o[0:tq] - lam * o[tq:2 * tq]
            ms = jnp.mean(a * a, axis=-1, keepdims=True)
            a = a * lax.rsqrt(ms + NORM_EPS) * dn_ref[...] * (1.0 - lam_init)
            o_ref[0] = a.astype(o_ref.dtype)
        else:
            o_ref[0] = o.astype(o_ref.dtype)

    if ctx_tiles:
        pl.when(i < ctx_tiles)(lambda: attend(ctx_len, ctx_len))
        pl.when(i >= ctx_tiles)(lambda: attend(t, tk))
    else:
        attend(t, tk)


def _attention(q, k, v, *, heads, dq, dv, q_col0, k_col0, v_col0, tq, scale, tk, diff=False,
               lam_vec=None, d_norm=None, lam_init=0.0, ctx_tiles=0, ctx_len=0):
    b, t, _ = k.shape
    sq = q.shape[1]
    assert t % tk == 0 and sq % tq == 0 and tk % LANES == 0 and ctx_len % LANES == 0
    assert ctx_len <= tk
    rows = 2 * tq if diff else tq
    if lam_vec is None:
        lam_vec = jnp.zeros((4, DIFF_HD), F32)
        d_norm = jnp.zeros((dv,), F32)
    kern = functools.partial(_attn_kernel, scale=scale, diff=diff, lam_init=lam_init,
                             ctx_tiles=ctx_tiles, ctx_len=ctx_len, tq=tq, tk=tk, t=t)
    return pl.pallas_call(
        kern,
        grid=(b, heads, sq // tq),
        in_specs=[
            pl.BlockSpec((1, tq, dq), lambda bi, h, i: (bi, i, q_col0 + h)),
            pl.BlockSpec((1, t, dq), lambda bi, h, i: (bi, 0, k_col0 + h)),
            pl.BlockSpec((1, t, dv), lambda bi, h, i: (bi, 0, v_col0 + h)),
            pl.BlockSpec((4, DIFF_HD), lambda bi, h, i: (0, 0)),
            pl.BlockSpec((1, dv), lambda bi, h, i: (0, 0)),
        ],
        out_specs=pl.BlockSpec((1, tq, dv), lambda bi, h, i: (bi, i, h)),
        out_shape=jax.ShapeDtypeStruct((b, sq, heads * dv), BF16),
        scratch_shapes=[pltpu.VMEM((rows, dq), BF16), pltpu.VMEM((t // tk, rows, tk), F32),
                        pltpu.VMEM((rows, LANES), F32), pltpu.VMEM((rows, LANES), F32),
                        pltpu.VMEM((rows, dv), F32)],
        compiler_params=_cparams(("parallel", "parallel", "arbitrary")),
    )(q, k, v, lam_vec.astype(F32), d_norm.reshape(1, dv).astype(F32))


def _pack_bf16_pairs(x):
    h = x.shape[1] // 2
    lo = lax.bitcast_convert_type(x[:, :h].astype(BF16).astype(F32), jnp.uint32) >> 16
    hi = lax.bitcast_convert_type(x[:, h:].astype(BF16).astype(F32), jnp.uint32) & jnp.uint32(0xFFFF0000)
    return hi | lo


def _unpack_bf16_pairs(w):
    lo = lax.bitcast_convert_type(w << 16, F32)
    hi = lax.bitcast_convert_type(w & jnp.uint32(0xFFFF0000), F32)
    return jnp.concatenate([lo.astype(BF16), hi.astype(BF16)], axis=1)


def _swiglu(x, wg, wu, wd):
    g = jnp.dot(x, wg, preferred_element_type=F32)
    u = jnp.dot(x, wu, preferred_element_type=F32)
    a = (g * jax.nn.sigmoid(g) * u).astype(BF16)
    return jnp.dot(a, wd, preferred_element_type=F32)


def _expert_kernel(be_ref, nu_ref, x_ref, wg_ref, wu_ref, wd_ref, o_ref, wg_s, wu_s, wd_s):
    i = pl.program_id(0)

    @pl.when(i < nu_ref[0])
    def _():
        @pl.when((i == 0) | (be_ref[i] != be_ref[jnp.maximum(i - 1, 0)]))
        def _():
            wg_s[...] = wg_ref[0].astype(BF16)
            wu_s[...] = wu_ref[0].astype(BF16)
            wd_s[...] = wd_ref[0].astype(BF16)

        o_ref[...] = _swiglu(_unpack_bf16_pairs(x_ref[...]), wg_s[...], wu_s[...], wd_s[...])

    @pl.when(i >= nu_ref[0])
    def _():
        o_ref[...] = jnp.zeros(o_ref.shape, o_ref.dtype)


def _expert_ffn(x, block_e, n_used, wg, wu, wd, *, tm):
    n, dh = x.shape
    d = 2 * dh
    ff = wg.shape[2]
    assert n % tm == 0

    def blk(i, be, nu):
        return jnp.minimum(i, nu[0] - 1)

    grid_spec = pltpu.PrefetchScalarGridSpec(
        num_scalar_prefetch=2,
        grid=(n // tm,),
        in_specs=[
            pl.BlockSpec((tm, dh), lambda i, be, nu: (blk(i, be, nu), 0)),
            pl.BlockSpec((1, d, ff), lambda i, be, nu: (be[blk(i, be, nu)], 0, 0)),
            pl.BlockSpec((1, d, ff), lambda i, be, nu: (be[blk(i, be, nu)], 0, 0)),
            pl.BlockSpec((1, ff, d), lambda i, be, nu: (be[blk(i, be, nu)], 0, 0)),
        ],
        out_specs=pl.BlockSpec((tm, d), lambda i, be, nu: (i, 0)),
        scratch_shapes=[pltpu.VMEM((d, ff), BF16), pltpu.VMEM((d, ff), BF16), pltpu.VMEM((ff, d), BF16)],
    )
    return pl.pallas_call(
        _expert_kernel,
        grid_spec=grid_spec,
        out_shape=jax.ShapeDtypeStruct((n, d), F32),
        compiler_params=_cparams(("arbitrary",)),
    )(block_e, n_used, x, wg, wu, wd)


def _post_mix_kernel(mix_ref, x_ref, w_ref, gp_ref, gf_ref, g1_ref, sh_ref, sc_ref, rw_ref,
                     xa_ref, f_ref, lg_ref):
    y = jnp.dot(mix_ref[0], w_ref[...], preferred_element_type=F32)
    yn = y * lax.rsqrt(jnp.mean(y * y, axis=-1, keepdims=True) + NORM_EPS) * gp_ref[...]
    xa = x_ref[0] + g1_ref[0] * yn
    xa_ref[0] = xa
    fn = xa * lax.rsqrt(jnp.mean(xa * xa, axis=-1, keepdims=True) + NORM_EPS) * gf_ref[...]
    f = fn * (1.0 + sc_ref[0]) + sh_ref[0]
    f_ref[0] = _pack_bf16_pairs(f)
    lg_ref[...] = lax.dot_general(rw_ref[...], f, (((1,), (1,)), ((), ())),
                                  precision=lax.Precision.HIGHEST, preferred_element_type=F32)


def _post_mix(mix, xs, w_out, g_post, g_ffn, g1, sh2, sc2, router_wt, *, row_tile0, seg_tiles):
    b, rows, k = mix.shape
    d = w_out.shape[1]
    e = router_wt.shape[0]
    tm = ROW_TILE
    nt = rows // tm
    mod_map = lambda bi, i: (bi * 2 + jnp.where(i + row_tile0 >= seg_tiles, 1, 0), 0, 0)
    vec = lambda: pl.BlockSpec((1, d), lambda bi, i: (0, 0))
    return pl.pallas_call(
        _post_mix_kernel,
        grid=(b, nt),
        in_specs=[
            pl.BlockSpec((1, tm, k), lambda bi, i: (bi, i, 0)),
            pl.BlockSpec((1, tm, d), lambda bi, i: (bi, i + row_tile0, 0)),
            pl.BlockSpec((k, d), lambda bi, i: (0, 0)),
            vec(), vec(),
            pl.BlockSpec((1, 1, d), mod_map), pl.BlockSpec((1, 1, d), mod_map),
            pl.BlockSpec((1, 1, d), mod_map),
            pl.BlockSpec((e, d), lambda bi, i: (0, 0)),
        ],
        out_specs=[
            pl.BlockSpec((1, tm, d), lambda bi, i: (bi, i, 0)),
            pl.BlockSpec((1, tm, d // 2), lambda bi, i: (bi, i, 0)),
            pl.BlockSpec((e, tm), lambda bi, i: (0, bi * nt + i)),
        ],
        out_shape=[jax.ShapeDtypeStruct((b, rows, d), F32),
                   jax.ShapeDtypeStruct((b, rows, d // 2), jnp.uint32),
                   jax.ShapeDtypeStruct((e, b * rows), F32)],
        compiler_params=_cparams(("parallel", "arbitrary")),
    )(mix, xs, w_out, g_post.reshape(1, d).astype(F32), g_ffn.reshape(1, d).astype(F32),
      g1, sh2, sc2, router_wt)


def _route_kernel(lg_ref, rb_ref, e_ref, g_ref):
    per = N_EXPERTS // N_GROUPS
    tn = lg_ref.shape[1]
    neg = -jnp.inf
    r_io = lax.broadcasted_iota(jnp.int32, (per, tn), 0)
    scores, choice, gs = [], [], []
    for g in range(N_GROUPS):
        sg = jax.nn.sigmoid(lg_ref[g * per:(g + 1) * per, :])
        cg = sg + rb_ref[g * per:(g + 1) * per, :]
        m1 = jnp.max(cg, axis=0, keepdims=True)
        i1 = jnp.min(jnp.where(cg == m1, r_io, per), axis=0, keepdims=True)
        m2 = jnp.max(jnp.where(r_io == i1, neg, cg), axis=0, keepdims=True)
        scores.append(sg)
        choice.append(cg)
        gs.append(m1 + m2)
    masked = []
    for g in range(N_GROUPS):
        ahead = jnp.zeros((1, tn), jnp.int32)
        for o in range(N_GROUPS):
            if o < g:
                ahead = ahead + jnp.where(gs[o] >= gs[g], 1, 0)
            elif o > g:
                ahead = ahead + jnp.where(gs[o] > gs[g], 1, 0)
        masked.append(jnp.where(ahead < TOPK_GROUPS, choice[g], neg))
    ids, gates = [], []
    for _ in range(TOP_K):
        best = masked[0]
        for g in range(1, N_GROUPS):
            best = jnp.maximum(best, masked[g])
        best = jnp.max(best, axis=0, keepdims=True)
        cand = jnp.where(masked[0] == best, r_io, N_EXPERTS)
        for g in range(1, N_GROUPS):
            cand = jnp.minimum(cand, jnp.where(masked[g] == best, r_io + g * per, N_EXPERTS))
        idx = jnp.min(cand, axis=0, keepdims=True)
        gk = jnp.zeros((per, tn), F32)
        for g in range(N_GROUPS):
            hit = (r_io + g * per) == idx
            gk = gk + jnp.where(hit, scores[g], 0.0)
            masked[g] = jnp.where(hit, neg, masked[g])
        ids.append(idx)
        gates.append(jnp.sum(gk, axis=0, keepdims=True))
    total = gates[0]
    for k in range(1, TOP_K):
        total = total + gates[k]
    pad = 8 - TOP_K
    e_ref[...] = jnp.concatenate(ids + [jnp.zeros((pad, tn), jnp.int32)], axis=0)
    g_ref[...] = jnp.concatenate([gk / total * ROUTED_SCALE for gk in gates]
                                 + [jnp.zeros((pad, tn), F32)], axis=0)


def _route(logits_t, router_b, *, tn):
    e, n = logits_t.shape
    assert n % tn == 0
    return pl.pallas_call(
        _route_kernel,
        grid=(n // tn,),
        in_specs=[pl.BlockSpec((e, tn), lambda i: (0, i)),
                  pl.BlockSpec((e, 1), lambda i: (0, 0))],
        out_specs=[pl.BlockSpec((8, tn), lambda i: (0, i)), pl.BlockSpec((8, tn), lambda i: (0, i))],
        out_shape=[jax.ShapeDtypeStruct((8, n), jnp.int32), jax.ShapeDtypeStruct((8, n), F32)],
        compiler_params=_cparams(("parallel",)),
    )(logits_t, router_b.reshape(e, 1).astype(F32))


def _dispatch_kernel(e_ref, dest_ref, be_ref, nu_ref, cnt_scr, start_scr, run_scr, *, blk):
    ph = pl.program_id(0)
    i = pl.program_id(1)
    tn = e_ref.shape[1]
    e_io = lax.broadcasted_iota(jnp.int32, (N_EXPERTS, tn), 0)
    hot = jnp.zeros((N_EXPERTS, tn), F32)
    for k in range(TOP_K):
        hot = hot + jnp.where(e_io == e_ref[k:k + 1, :], 1.0, 0.0)
    tile_cnt = jnp.sum(hot, axis=1, keepdims=True)

    @pl.when((ph == 0) & (i == 0))
    def _():
        cnt_scr[...] = jnp.zeros(cnt_scr.shape, F32)

    @pl.when(ph == 0)
    def _():
        cnt_scr[...] += jnp.broadcast_to(tile_cnt, cnt_scr.shape)

    @pl.when((ph == 1) & (i == 0))
    def _():
        bpe = jnp.floor((cnt_scr[...] + (blk - 1.0)) * (1.0 / blk))
        r = lax.broadcasted_iota(jnp.int32, (N_EXPERTS, N_EXPERTS), 0)
        c = lax.broadcasted_iota(jnp.int32, (N_EXPERTS, N_EXPERTS), 1)
        lower = jnp.where(c < r, 1.0, 0.0)
        before = jnp.dot(lower, bpe, precision=lax.Precision.HIGHEST, preferred_element_type=F32)
        start_scr[...] = before * blk
        run_scr[...] = jnp.zeros(run_scr.shape, F32)
        ends = (before + bpe)[:, 0:1]
        nb = be_ref.shape[1]
        bid = lax.broadcasted_iota(jnp.int32, (N_EXPERTS, nb), 1).astype(F32)
        be = jnp.sum(jnp.where(ends <= bid, 1, 0), axis=0, keepdims=True)
        be_ref[...] = jnp.minimum(be, N_EXPERTS - 1).astype(jnp.int32)
        nu_ref[...] = jnp.broadcast_to(jnp.max(ends, axis=0, keepdims=True), nu_ref.shape).astype(jnp.int32)

    @pl.when(ph == 1)
    def _():
        rr = lax.broadcasted_iota(jnp.int32, (tn, tn), 0)
        cc = lax.broadcasted_iota(jnp.int32, (tn, tn), 1)
        upper = jnp.where(rr < cc, 1.0, 0.0).astype(BF16)
        prior = jnp.dot(hot.astype(BF16), upper, preferred_element_type=F32)
        pos = prior + jnp.concatenate([start_scr[...] + run_scr[...]] * (tn // LANES), axis=1)
        rows = []
        for k in range(TOP_K):
            rows.append(jnp.sum(jnp.where(e_io == e_ref[k:k + 1, :], pos, 0.0), axis=0, keepdims=True))
        rows.append(jnp.zeros((8 - TOP_K, tn), F32))
        dest_ref[0] = jnp.concatenate(rows, axis=0).astype(jnp.int32)
        run_scr[...] += jnp.broadcast_to(tile_cnt, run_scr.shape)


def _dispatch(top_e, *, blk, n_blocks):
    n = top_e.shape[1]
    tn = LANES
    nbp = -(-n_blocks // LANES) * LANES
    kern = functools.partial(_dispatch_kernel, blk=blk)
    return pl.pallas_call(
        kern,
        grid=(2, n // tn),
        in_specs=[pl.BlockSpec((8, tn), lambda ph, i: (0, i))],
        out_specs=[pl.BlockSpec((1, 8, tn), lambda ph, i: (i * ph, 0, 0)),
                   pl.BlockSpec((1, nbp), lambda ph, i: (0, 0)),
                   pl.BlockSpec((8, LANES), lambda ph, i: (0, 0))],
        out_shape=[jax.ShapeDtypeStruct((n // tn, 8, tn), jnp.int32),
                   jax.ShapeDtypeStruct((1, nbp), jnp.int32),
                   jax.ShapeDtypeStruct((8, LANES), jnp.int32)],
        scratch_shapes=[pltpu.VMEM((N_EXPERTS, LANES), F32)] * 3,
        compiler_params=_cparams(("arbitrary", "arbitrary")),
    )(top_e)


def _scatter_rows_kernel(dest_ref, x_ref, zero_ref, o_ref, sem):
    del zero_ref
    tm = x_ref.shape[0]

    def row_copy(t, k):
        return pltpu.make_async_copy(x_ref.at[pl.ds(t, 1)], o_ref.at[pl.ds(dest_ref[0, k, t], 1)], sem)

    def issue(t, carry):
        for k in range(TOP_K):
            row_copy(t, k).start()
        return carry

    def drain(t, carry):
        for k in range(TOP_K):
            row_copy(t, k).wait()
        return carry

    lax.fori_loop(0, tm, issue, 0)
    lax.fori_loop(0, tm, drain, 0)


def _scatter_rows(dest, x, n_rows):
    n, dh = x.shape
    tm = LANES
    zeros = jnp.zeros((n_rows, dh), x.dtype)
    return pl.pallas_call(
        _scatter_rows_kernel,
        grid=(n // tm,),
        in_specs=[pl.BlockSpec((1, 8, tm), lambda i: (i, 0, 0), memory_space=pltpu.SMEM),
                  pl.BlockSpec((tm, dh), lambda i: (i, 0)),
                  pl.BlockSpec(memory_space=pl.ANY)],
        out_specs=pl.BlockSpec(memory_space=pl.ANY),
        out_shape=jax.ShapeDtypeStruct((n_rows, dh), x.dtype),
        scratch_shapes=[pltpu.SemaphoreType.DMA(())],
        input_output_aliases={2: 0},
        compiler_params=_cparams(("arbitrary",)),
    )(dest, x, zeros)


def _combine_kernel(dest_ref, y_ref, gate_ref, f_ref, xa_ref, sg_ref, su_ref, sd_ref, gp_ref, g2_ref,
                    o_ref, buf, sem):
    tm = f_ref.shape[0]

    def row_copy(t, k):
        return pltpu.make_async_copy(y_ref.at[pl.ds(dest_ref[0, k, t], 1)], buf.at[k, pl.ds(t, 1)], sem)

    def issue(t, carry):
        for k in range(TOP_K):
            row_copy(t, k).start()
        return carry

    def drain(t, carry):
        for k in range(TOP_K):
            row_copy(t, k).wait()
        return carry

    lax.fori_loop(0, tm, issue, 0)
    f = _swiglu(_unpack_bf16_pairs(f_ref[...]), sg_ref[...], su_ref[...], sd_ref[...])
    lax.fori_loop(0, tm, drain, 0)
    gate = gate_ref[...]
    for k in range(TOP_K):
        f = f + gate[:, k:k + 1] * buf[k]
    fn = f * lax.rsqrt(jnp.mean(f * f, axis=-1, keepdims=True) + NORM_EPS) * gp_ref[...]
    o_ref[0] = xa_ref[0] + g2_ref[0] * fn


def _combine(dest, y, gate, f_pk, xa, sg, su, sd, g_post, g2, *, row_tile0, seg_tiles):
    b, rows, d = xa.shape
    tm = LANES
    nt = rows // tm
    per = ROW_TILE // tm
    mod_map = lambda bi, i: (bi * 2 + jnp.where(i // per + row_tile0 >= seg_tiles, 1, 0), 0, 0)
    const = lambda shape: pl.BlockSpec(shape, lambda bi, i: (0,) * len(shape))
    return pl.pallas_call(
        _combine_kernel,
        grid=(b, nt),
        in_specs=[
            pl.BlockSpec((1, 8, tm), lambda bi, i: (bi * nt + i, 0, 0), memory_space=pltpu.SMEM),
            pl.BlockSpec(memory_space=pl.ANY),
            pl.BlockSpec((tm, 8), lambda bi, i: (bi * nt + i, 0)),
            pl.BlockSpec((tm, d // 2), lambda bi, i: (bi * nt + i, 0)),
            pl.BlockSpec((1, tm, d), lambda bi, i: (bi, i, 0)),
            const(sg.shape), const(su.shape), const(sd.shape), const((1, d)),
            pl.BlockSpec((1, 1, d), mod_map),
        ],
        out_specs=pl.BlockSpec((1, tm, d), lambda bi, i: (bi, i, 0)),
        out_shape=jax.ShapeDtypeStruct((b, rows, d), F32),
        scratch_shapes=[pltpu.VMEM((TOP_K, tm, d), F32), pltpu.SemaphoreType.DMA(())],
        compiler_params=_cparams(("arbitrary", "arbitrary")),
    )(dest, y, gate, f_pk, xa, sg, su, sd, g_post.reshape(1, d).astype(F32), g2)


def _rms_norm(x, g):
    xf = x.astype(F32)
    y = xf * lax.rsqrt(jnp.mean(xf * xf, axis=-1, keepdims=True) + NORM_EPS)
    return y * g.astype(F32)


def _rope_tables(s, ctx_len, dim):
    rows = s // GRID_W
    row = jnp.repeat(jnp.arange(rows), GRID_W)
    col = jnp.tile(jnp.arange(GRID_W), rows)
    quarter = dim // 4
    inv = ROPE_BASE ** (-jnp.arange(quarter, dtype=F32) / quarter)
    ang = jnp.concatenate([row.astype(F32)[:, None] * inv, col.astype(F32)[:, None] * inv], axis=-1)
    cos = jnp.repeat(jnp.cos(ang), 2, axis=-1)
    sin = jnp.repeat(jnp.sin(ang), 2, axis=-1) * jnp.tile(jnp.array([-1.0, 1.0], F32), dim // 2)
    cos = jnp.concatenate([jnp.ones((ctx_len, dim), F32), cos], axis=0)
    sin = jnp.concatenate([jnp.zeros((ctx_len, dim), F32), sin], axis=0)
    return cos, sin


def _mlstm_chunked(q, k, v, i_pre, f_pre, state):
    b, t, nh, dk = k.shape
    nc = t // MLSTM_CHUNK
    causal = jnp.tril(jnp.ones((MLSTM_CHUNK, MLSTM_CHUNK), bool))[None, :, :, None]

    def chunks(a):
        return jnp.moveaxis(a.reshape((b, nc, MLSTM_CHUNK) + a.shape[2:]), 1, 0)

    def step(carry, xs):
        c_st, n_st, m_st = carry
        *qx, kb, vb, ib, fb = xs
        bcum = jnp.cumsum(jax.nn.log_sigmoid(fb), axis=1)
        btot = bcum[:, -1]
        w_end = btot[:, None] - bcum + ib
        m_new = jnp.maximum(btot + m_st, jnp.max(w_end, axis=1))
        decay = jnp.exp(btot + m_st - m_new)
        w_k = jnp.exp(w_end - m_new[:, None])
        c_new = decay[..., None, None] * c_st + jnp.einsum('blh,blhv,blhk->bhvk', w_k, vb, kb)
        n_new = decay[..., None] * n_st + jnp.einsum('blh,blhk->bhk', w_k, kb)
        if not qx:
            return (c_new, n_new, m_new), None
        qb = qx[0]
        log_d = bcum[:, :, None] - bcum[:, None] + ib[:, None]
        log_d = jnp.where(causal, log_d, -jnp.inf)
        log_inter = bcum + m_st[:, None]
        m_row = jnp.maximum(log_inter, jnp.max(log_d, axis=2))
        w_intra = jnp.exp(log_d - m_row[:, :, None])
        w_inter = jnp.exp(log_inter - m_row)
        qk = jnp.einsum('bqhd,bshd->bqsh', qb, kb) * w_intra
        num = w_inter[..., None] * jnp.einsum('bqhk,bhvk->bqhv', qb, c_st) + jnp.einsum('bqsh,bshv->bqhv', qk, vb)
        den = w_inter * jnp.einsum('bqhk,bhk->bqh', qb, n_st) + jnp.sum(qk, axis=2)
        h_out = num / jnp.maximum(jnp.abs(den), jnp.exp(-m_row))[..., None]
        return (c_new, n_new, m_new), h_out

    xs = tuple(chunks(a) for a in ((() if q is None else (q,)) + (k, v, i_pre, f_pre)))
    state, hs = lax.scan(step, state, xs)
    if q is None:
        return None, state
    return jnp.moveaxis(hs, 0, 1).reshape(b, t, nh, v.shape[-1]), state


def _flip(a):
    return None if a is None else jnp.flip(a, axis=1)


def _mlstm_mixer(mq, mk, mv, mo, gates, gate_b, m_norm, ctx_len, with_ctx):
    b, t, _ = mq.shape
    L = ctx_len

    def heads(a, d):
        return a.reshape(b, a.shape[1], MLSTM_HEADS, d).astype(F32)

    q = heads(mq, MLSTM_QK)
    k = heads(mk, MLSTM_QK) * MLSTM_QK ** -0.5
    v = heads(mv, MLSTM_V)
    g = gates[..., :4 * MLSTM_HEADS].reshape(b, t, 4, MLSTM_HEADS).astype(F32) + gate_b.astype(F32)
    zero = (jnp.zeros((b, MLSTM_HEADS, MLSTM_V, MLSTM_QK), F32),
            jnp.zeros((b, MLSTM_HEADS, MLSTM_QK), F32), jnp.zeros((b, MLSTM_HEADS), F32))
    qc, kc, vc, gc = (a[:, :L] for a in (q, k, v, g))
    ql, kl, vl, gl = (a[:, L:] for a in (q, k, v, g))
    if not with_ctx:
        qc = None
    hcf, st_f = _mlstm_chunked(qc, kc, vc, gc[:, :, 0], gc[:, :, 1], zero)
    hlf, _ = _mlstm_chunked(ql, kl, vl, gl[:, :, 0], gl[:, :, 1], st_f)
    hcb, st_b = _mlstm_chunked(_flip(qc), _flip(kc), _flip(vc), _flip(gc[:, :, 2]), _flip(gc[:, :, 3]), zero)
    hlb, _ = _mlstm_chunked(_flip(ql), _flip(kl), _flip(vl), _flip(gl[:, :, 2]), _flip(gl[:, :, 3]), st_b)

    def m_out(hf, hb_rev, o_pre):
        o = jax.nn.sigmoid(heads(o_pre, MLSTM_V))
        return (_rms_norm(hf + jnp.flip(hb_rev, axis=1), m_norm) * o).reshape(b, hf.shape[1], -1)

    out_l = m_out(hlf, hlb, mo[:, L:])
    if not with_ctx:
        return out_l
    return jnp.concatenate([m_out(hcf, hcb, mo[:, :L]), out_l], axis=1)


def _swa_latent(q, k, v, k_ctx, v_ctx, sink):
    b, s = q.shape[:2]
    nb = s // BLOCK_Q
    r = SWA_HEADS // SWA_KV_HEADS

    def band(a):
        ap = jnp.pad(a, ((0, 0), (BLOCK_Q, BLOCK_Q), (0, 0), (0, 0))).reshape(b, nb + 2, BLOCK_Q, SWA_KV_HEADS, SWA_HD)
        return jnp.concatenate([ap[:, :-2], ap[:, 1:-1], ap[:, 2:]], axis=2)

    kb, vb = band(k), band(v)
    qb = q.reshape(b, nb, BLOCK_Q, SWA_KV_HEADS, r, SWA_HD)
    scale = SWA_HD ** -0.5
    s_loc = jnp.einsum('bnqgrd,bnkgd->bngrqk', qb, kb, preferred_element_type=F32) * scale
    q_pos = jnp.arange(nb)[:, None, None] * BLOCK_Q + jnp.arange(BLOCK_Q)[None, :, None]
    k_pos = jnp.arange(nb)[:, None, None] * BLOCK_Q - BLOCK_Q + jnp.arange(3 * BLOCK_Q)[None, None, :]
    valid = (jnp.abs(k_pos - q_pos) <= WINDOW) & (k_pos >= 0) & (k_pos < s)
    s_loc = jnp.where(valid[None, :, None, None], s_loc, -jnp.inf)
    s_ctx = jnp.einsum('bnqgrd,bkgd->bngrqk', qb, k_ctx, preferred_element_type=F32) * scale
    sink_col = jnp.broadcast_to(sink.astype(F32).reshape(1, 1, SWA_KV_HEADS, r, 1, 1), s_loc.shape[:-1] + (1,))
    p = jax.nn.softmax(jnp.concatenate([s_loc, s_ctx, sink_col], axis=-1), axis=-1)
    n_loc = 3 * BLOCK_Q
    out = (jnp.einsum('bngrqk,bnkgd->bnqgrd', p[..., :n_loc], vb.astype(F32))
           + jnp.einsum('bngrqk,bkgd->bnqgrd', p[..., n_loc:-1], v_ctx.astype(F32)))
    return out.reshape(b, s, SWA_HEADS * SWA_HD)


def _mix_out_and_moe(mix, xs, w_out, g_post, g_ffn_pre, g_ffn_post, mods, router_w, router_b,
                     wg, wu, wd, sg, su, sd, *, row_tile0, seg_tiles):
    sh1, sc1, g1, sh2, sc2, g2 = mods
    b, rows, _ = mix.shape
    n = b * rows
    seg = dict(row_tile0=row_tile0, seg_tiles=seg_tiles)
    xa, f_pk, logits_t = _post_mix(mix, xs, w_out.astype(BF16), g_post, g_ffn_pre, g1, sh2, sc2,
                                   router_w.T.astype(F32), **seg)
    top_e, gate = _route(logits_t, router_b, tn=512)
    blk = EXPERT_ROWS
    n_blocks = -(-n * TOP_K // blk) + N_EXPERTS
    dest, block_e, n_used = _dispatch(top_e, blk=blk, n_blocks=n_blocks)
    f_pk = f_pk.reshape(n, -1)
    x_sorted = _scatter_rows(dest, f_pk, n_blocks * blk)
    y = _expert_ffn(x_sorted, block_e[0, :n_blocks], n_used[0, :1], wg, wu, wd, tm=blk)
    return _combine(dest, y, gate.T, f_pk, xa, sg.astype(BF16), su.astype(BF16), sd.astype(BF16),
                    g_ffn_post, g2, **seg)


def kernel(x, c, ctx, c_ctx, mod_w, mod_b, norm_mix_pre, norm_mix_post, norm_ffn_pre, norm_ffn_post, ab_w_in, ab_w_out, diff_lambda, diff_norm, mlstm_gate_b, mlstm_norm, cd_w_in, cd_w_out, mla_q_norm, mla_w_uq, mla_kv_norm, mla_w_ukv, swa_sink, router_w, router_b, exp_w_gate, exp_w_up, exp_w_down, sh_w_gate, sh_w_up, sh_w_down):
    b, s, d = x.shape
    L = ctx.shape[1]
    t = L + s
    assert L == ROW_TILE and s % ROW_TILE == 0
    depth = mod_w.shape[0]
    ctx_tiles = L // ROW_TILE
    tk = next(c for c in (2816, 768, ROW_TILE) if t % c == 0)

    cos64, sin64 = _rope_tables(s, L, DIFF_HD)
    cos64 = jnp.tile(cos64, (1, LANES // DIFF_HD))
    sin64 = jnp.tile(sin64, (1, LANES // DIFF_HD))
    cos128, sin128 = _rope_tables(s, L, SWA_HD)
    cos_kr = jnp.concatenate([cos64[:, :MLA_ROPE], jnp.ones((t, LANES - MLA_ROPE), F32)], axis=1)
    sin_kr = jnp.concatenate([sin64[:, :MLA_ROPE], jnp.zeros((t, LANES - MLA_ROPE), F32)], axis=1)

    xs = jnp.concatenate([ctx, x], axis=1)

    for layer in range(depth):
        with_ctx = layer < depth - 1
        j = layer // 2
        mod_l = jax.nn.silu(c) @ mod_w[layer] + mod_b[layer]
        mod_c = jax.nn.silu(c_ctx) @ mod_w[layer] + mod_b[layer]
        mods = jnp.stack([jnp.broadcast_to(mod_c, (b, 6 * d)), mod_l], axis=1).reshape(b * 2, 1, 6, d)
        sh1, sc1, g1, sh2, sc2, g2 = (mods[:, :, m] for m in range(6))

        proj = functools.partial(_norm_proj, xs, norm_mix_pre[layer], sh1, sc1, seg_tiles=ctx_tiles)
        if layer % 2 == 0:
            w_in = ab_w_in[j].astype(BF16)
            n_qk = 2 * DIFF_HEADS * 2 * DIFF_HD
            qk = proj(w_in[:, :n_qk], tn=1024, rope=(cos64[None], sin64[None]), pattern=(0,) * 8)
            pv = proj(w_in[:, n_qk:n_qk + AB_PLAIN], tn=1024)
            w_gates = jnp.pad(w_in[:, n_qk + AB_PLAIN:], ((0, 0), (0, LANES - 4 * MLSTM_HEADS)))
            gates = proj(w_gates, tn=LANES, out_dtype=F32)

            lam_init = 0.8 - 0.6 * math.exp(-0.3 * layer)
            a = _attention(qk if with_ctx else qk[:, L:], qk, pv, heads=DIFF_HEADS, dq=2 * DIFF_HD,
                           dv=DIFF_VD, q_col0=0, k_col0=DIFF_HEADS, v_col0=0, tq=ROW_TILE,
                           scale=DIFF_HD ** -0.5, tk=tk, diff=True, lam_vec=diff_lambda[j],
                           d_norm=diff_norm[j], lam_init=lam_init,
                           ctx_tiles=ctx_tiles if with_ctx else 0, ctx_len=L)
            o0 = DIFF_HEADS * DIFF_VD
            o1 = o0 + MLSTM_HEADS * MLSTM_QK
            o2 = o1 + MLSTM_HEADS * MLSTM_QK
            o3 = o2 + MLSTM_HEADS * MLSTM_V
            m = _mlstm_mixer(pv[..., o0:o1], pv[..., o1:o2], pv[..., o2:o3], pv[..., o3:], gates,
                             mlstm_gate_b[j], mlstm_norm[j], L, with_ctx)
            mix = jnp.concatenate([a, m.astype(BF16)], axis=-1)
            w_out = ab_w_out[j]
        else:
            assert not with_ctx
            w_in = cd_w_in[j].astype(BF16)
            c0 = Q_LORA
            c1 = c0 + KV_LORA
            c2 = c1 + MLA_ROPE
            c3 = c2 + SWA_HEADS * SWA_HD
            c4 = c3 + SWA_KV_HEADS * SWA_HD
            w_rope = jnp.concatenate([w_in[:, c2:c4], w_in[:, c1:c2],
                                      jnp.zeros((d, LANES - MLA_ROPE), BF16)], axis=1)
            n_rope = w_rope.shape[1]
            rp = proj(w_rope, tn=n_rope, rope=(jnp.stack([cos128, cos_kr]), jnp.stack([sin128, sin_kr])),
                      pattern=(0,) * (n_rope // LANES - 1) + (1,))
            q_pad = 512 - Q_LORA
            w_plain = jnp.concatenate([w_in[:, c4:], w_in[:, :c0], jnp.zeros((d, q_pad), BF16),
                                       w_in[:, c0:c1]], axis=1)
            pp = proj(w_plain, tn=w_plain.shape[1])
            n_sv = SWA_KV_HEADS * SWA_HD
            cq = pp[..., n_sv:n_sv + 512]
            ckv = pp[..., n_sv + 512:]

            hq = MLA_NOPE + MLA_ROPE
            w_uq = mla_w_uq[j].astype(BF16).reshape(Q_LORA, MLA_HEADS, hq)
            w_uq = jnp.pad(w_uq, ((0, q_pad), (0, 0), (0, 256 - hq))).reshape(512, MLA_HEADS * 256)
            no_mod = jnp.zeros((b * 2, 1, 512), F32)
            qn = jnp.pad(mla_q_norm[j], (0, q_pad))
            q_mla = _norm_proj(cq[:, L:], qn, no_mod, no_mod, w_uq, seg_tiles=0, tn=1024, n_valid=Q_LORA,
                               rope=(cos_kr[None, L:], sin_kr[None, L:]), pattern=(-1, 0) * 4)
            kv = _norm_proj(ckv, mla_kv_norm[j], no_mod, no_mod, mla_w_ukv[j].astype(BF16),
                            seg_tiles=ctx_tiles, tn=1024).reshape(b, t, MLA_HEADS, MLA_NOPE + MLA_V)
            kr = rp[..., n_rope - LANES:]
            k_mla = jnp.concatenate([kv[..., :MLA_NOPE],
                                     jnp.broadcast_to(kr[:, :, None, :], (b, t, MLA_HEADS, LANES))],
                                    axis=-1).reshape(b, t, MLA_HEADS * 256)
            v_mla = kv[..., MLA_NOPE:].reshape(b, t, MLA_HEADS * MLA_V)
            a = _attention(q_mla, k_mla, v_mla, heads=MLA_HEADS, dq=256, dv=MLA_V, q_col0=0, k_col0=0,
                           v_col0=0, tq=512, scale=MLA_SCALE, tk=tk)

            n_sq = SWA_HEADS * SWA_HD
            sq = rp[:, L:, :n_sq].reshape(b, s, SWA_HEADS, SWA_HD)
            sk = rp[..., n_sq:n_sq + n_sv].reshape(b, t, SWA_KV_HEADS, SWA_HD)
            sv = pp[..., :n_sv].reshape(b, t, SWA_KV_HEADS, SWA_HD)
            w = _swa_latent(sq, sk[:, L:], sv[:, L:], sk[:, :L], sv[:, :L], swa_sink[j])
            mix = jnp.concatenate([a, w.astype(BF16)], axis=-1)
            w_out = cd_w_out[j]

        xa = _mix_out_and_moe(mix, xs, w_out, norm_mix_post[layer], norm_ffn_pre[layer], norm_ffn_post[layer],
                              (sh1, sc1, g1, sh2, sc2, g2), router_w[layer], router_b[layer],
                              exp_w_gate[layer], exp_w_up[layer], exp_w_down[layer],
                              sh_w_gate[layer], sh_w_up[layer], sh_w_down[layer],
                              row_tile0=0 if with_ctx else ctx_tiles, seg_tiles=ctx_tiles)
        xs = xa if with_ctx else jnp.concatenate([xs[:, :L], xa], axis=1)
    return xs[:, L:]
```

```python
import functools
import math

import jax
import jax.numpy as jnp
import numpy as np
from jax import lax
from jax.experimental import pallas as pl
from jax.experimental.pallas import tpu as pltpu

F32 = jnp.float32
BF16 = jnp.bfloat16

GRID_W = 64
ROPE_BASE = 10000.0
NORM_EPS = 1e-6

DIFF_HEADS = 8
DIFF_HD = 64
DIFF_VD = 2 * DIFF_HD
MLSTM_HEADS = 8
MLSTM_QK = 64
MLSTM_V = 128
MLSTM_CHUNK = 128
MLA_HEADS = 8
MLA_NOPE = 128
MLA_ROPE = 64
MLA_V = 128
Q_LORA = 448
KV_LORA = 512
MLA_SCALE = (MLA_NOPE + MLA_ROPE) ** -0.5
SWA_HEADS = 8
SWA_KV_HEADS = 2
SWA_HD = 128
WINDOW = 128
BLOCK_Q = 128
N_EXPERTS = 64
TOP_K = 6
N_GROUPS = 8
TOPK_GROUPS = 4
ROUTED_SCALE = 2.5
AB_PLAIN = DIFF_HEADS * DIFF_VD + 2 * MLSTM_HEADS * MLSTM_QK + 2 * MLSTM_HEADS * MLSTM_V

LANES = 128
LOG2E = 1.4426950408889634
VMEM_LIMIT = 56 * 1024 * 1024

ROW_TILE = 256
EXPERT_ROWS = 256


def _cparams(sem):
    return pltpu.CompilerParams(dimension_semantics=sem, vmem_limit_bytes=VMEM_LIMIT)


def _pair_swap(a):
    lane = lax.broadcasted_iota(jnp.int32, a.shape, 1)
    return jnp.where(lane % 2 == 0, pltpu.roll(a, LANES - 1, 1), pltpu.roll(a, 1, 1))


def _norm_proj_kernel(x_ref, g_ref, sh_ref, sc_ref, w_ref, cos_ref, sin_ref, o_ref, h_scr, *,
                      n_valid, pattern):
    j = pl.program_id(2)

    @pl.when(j == 0)
    def _():
        x = x_ref[0].astype(F32)
        ms = jnp.sum(x * x, axis=-1, keepdims=True) * (1.0 / n_valid)
        y = x * lax.rsqrt(ms + NORM_EPS) * g_ref[...]
        h_scr[...] = (y * (1.0 + sc_ref[0]) + sh_ref[0]).astype(BF16)

    acc = jnp.dot(h_scr[...], w_ref[...], preferred_element_type=F32)
    if pattern is None:
        o_ref[0] = acc.astype(o_ref.dtype)
    else:
        for c, tbl in enumerate(pattern):
            a = acc[:, c * LANES:(c + 1) * LANES]
            if tbl >= 0:
                a = a * cos_ref[tbl] + _pair_swap(a) * sin_ref[tbl]
            o_ref[0, :, c * LANES:(c + 1) * LANES] = a.astype(o_ref.dtype)


def _norm_proj(x, g, shift, scale, w, *, seg_tiles, tn, out_dtype=BF16, n_valid=None,
               rope=None, pattern=None):
    b, t, k = x.shape
    n = w.shape[1]
    tm = ROW_TILE
    assert t % tm == 0 and n % tn == 0 and tn % LANES == 0
    if rope is None:
        cos = sin = jnp.zeros((1, tm, LANES), F32)
        tbl_map = lambda bi, i, j: (0, 0, 0)
    else:
        cos, sin = rope
        tbl_map = lambda bi, i, j: (0, i, 0)
    ntab = cos.shape[0]
    kern = functools.partial(_norm_proj_kernel, n_valid=float(n_valid or k), pattern=pattern)
    mod_map = lambda bi, i, j: (bi * 2 + jnp.where(i >= seg_tiles, 1, 0), 0, 0)
    return pl.pallas_call(
        kern,
        grid=(b, t // tm, n // tn),
        in_specs=[
            pl.BlockSpec((1, tm, k), lambda bi, i, j: (bi, i, 0)),
            pl.BlockSpec((1, k), lambda bi, i, j: (0, 0)),
            pl.BlockSpec((1, 1, k), mod_map),
            pl.BlockSpec((1, 1, k), mod_map),
            pl.BlockSpec((k, tn), lambda bi, i, j: (0, j)),
            pl.BlockSpec((ntab, tm, LANES), tbl_map),
            pl.BlockSpec((ntab, tm, LANES), tbl_map),
        ],
        out_specs=pl.BlockSpec((1, tm, tn), lambda bi, i, j: (bi, i, j)),
        out_shape=jax.ShapeDtypeStruct((b, t, n), out_dtype),
        scratch_shapes=[pltpu.VMEM((tm, k), BF16)],
        compiler_params=_cparams(("parallel", "parallel", "arbitrary")),
    )(x, g.reshape(1, k).astype(F32), shift, scale, w, cos, sin)


def _attn_kernel(q_ref, k_ref, v_ref, lam_ref, dn_ref, o_ref, q_scr, s_scr, m_scr, l_scr, acc_scr, *,
                 scale, diff, lam_init, ctx_tiles, ctx_len, tq, tk, t):
    i = pl.program_id(2)
    q = q_ref[0].astype(F32) * (scale * LOG2E)
    if diff:
        lane = lax.broadcasted_iota(jnp.int32, q.shape, 1)
        half = q.shape[1] // 2
        q_scr[0:tq] = jnp.where(lane < half, q, 0.0).astype(BF16)
        q_scr[tq:2 * tq] = jnp.where(lane >= half, q, 0.0).astype(BF16)
    else:
        q_scr[...] = q.astype(BF16)

    def attend(kv_len, chunk):
        n_chunks = kv_len // chunk
        groups = chunk // LANES

        def pass1(c, carry):
            k = k_ref[0, pl.ds(pl.multiple_of(c * chunk, chunk), chunk), :]
            s = lax.dot_general(q_scr[...], k, (((1,), (1,)), ((), ())), preferred_element_type=F32)
            s_scr[c, :, 0:chunk] = s
            m = m_scr[...]
            for g in range(groups):
                m = jnp.maximum(m, s[:, g * LANES:(g + 1) * LANES])
            m_scr[...] = m
            return carry

        def pass2(c, carry):
            s = s_scr[c, :, 0:chunk]
            p = jnp.exp2(s - jnp.concatenate([m_scr[...]] * groups, axis=1))
            l = l_scr[...]
            for g in range(groups):
                l = l + p[:, g * LANES:(g + 1) * LANES]
            l_scr[...] = l
            v = v_ref[0, pl.ds(pl.multiple_of(c * chunk, chunk), chunk), :]
            acc_scr[...] += jnp.dot(p.astype(BF16), v, preferred_element_type=F32)
            return carry

        m_scr[...] = jnp.full(m_scr.shape, -jnp.inf, F32)
        if n_chunks == 1:
            pass1(0, 0)
        else:
            lax.fori_loop(0, n_chunks, pass1, 0)
        m_scr[...] = jnp.broadcast_to(jnp.max(m_scr[...], axis=-1, keepdims=True), m_scr.shape)
        l_scr[...] = jnp.zeros(l_scr.shape, F32)
        acc_scr[...] = jnp.zeros(acc_scr.shape, F32)
        if n_chunks == 1:
            pass2(0, 0)
        else:
            lax.fori_loop(0, n_chunks, pass2, 0)

        o = acc_scr[...] / jnp.sum(l_scr[...], axis=-1, keepdims=True)
        if diff:
            lv = lam_ref[...]
            lam = (jnp.exp(jnp.sum(lv[0:1] * lv[1:2], axis=-1, keepdims=True))
                   - jnp.exp(jnp.sum(lv[2:3] * lv[3:4], axis=-1, keepdims=True)) + lam_init)
            a = o[0:tq] - lam * o[tq:2 * tq]
            ms = jnp.mean(a * a, axis=-1, keepdims=True)
            a = a * lax.rsqrt(ms + NORM_EPS) * dn_ref[...] * (1.0 - lam_init)
            o_ref[0] = a.astype(o_ref.dtype)
        else:
            o_ref[0] = o.astype(o_ref.dtype)

    if ctx_tiles:
        pl.when(i < ctx_tiles)(lambda: attend(ctx_len, ctx_len))
        pl.when(i >= ctx_tiles)(lambda: attend(t, tk))
    else:
        attend(t, tk)


def _attention(q, k, v, *, heads, dq, dv, q_col0, k_col0, v_col0, tq, scale, tk, diff=False,
               lam_vec=None, d_norm=None, lam_init=0.0, ctx_tiles=0, ctx_len=0):
    b, t, _ = k.shape
    sq = q.shape[1]
    assert t % tk == 0 and sq % tq == 0 and tk % LANES == 0 and ctx_len % LANES == 0
    assert ctx_len <= tk
    rows = 2 * tq if diff else tq
    if lam_vec is None:
        lam_vec = jnp.zeros((4, DIFF_HD), F32)
        d_norm = jnp.zeros((dv,), F32)
    kern = functools.partial(_attn_kernel, scale=scale, diff=diff, lam_init=lam_init,
                             ctx_tiles=ctx_tiles, ctx_len=ctx_len, tq=tq, tk=tk, t=t)
    return pl.pallas_call(
        kern,
        grid=(b, heads, sq // tq),
        in_specs=[
            pl.BlockSpec((1, tq, dq), lambda bi, h, i: (bi, i, q_col0 + h)),
            pl.BlockSpec((1, t, dq), lambda bi, h, i: (bi, 0, k_col0 + h)),
            pl.BlockSpec((1, t, dv), lambda bi, h, i: (bi, 0, v_col0 + h)),
            pl.BlockSpec((4, DIFF_HD), lambda bi, h, i: (0, 0)),
            pl.BlockSpec((1, dv), lambda bi, h, i: (0, 0)),
        ],
        out_specs=pl.BlockSpec((1, tq, dv), lambda bi, h, i: (bi, i, h)),
        out_shape=jax.ShapeDtypeStruct((b, sq, heads * dv), BF16),
        scratch_shapes=[pltpu.VMEM((rows, dq), BF16), pltpu.VMEM((t // tk, rows, tk), F32),
                        pltpu.VMEM((rows, LANES), F32), pltpu.VMEM((rows, LANES), F32),
                        pltpu.VMEM((rows, dv), F32)],
        compiler_params=_cparams(("parallel", "parallel", "arbitrary")),
    )(q, k, v, lam_vec.astype(F32), d_norm.reshape(1, dv).astype(F32))


def _pack_bf16_pairs(x):
    h = x.shape[1] // 2
    lo = lax.bitcast_convert_type(x[:, :h].astype(BF16).astype(F32), jnp.uint32) >> 16
    hi = lax.bitcast_convert_type(x[:, h:].astype(BF16).astype(F32), jnp.uint32) & jnp.uint32(0xFFFF0000)
    return hi | lo


def _unpack_bf16_pairs(w):
    lo = lax.bitcast_convert_type(w << 16, F32)
    hi = lax.bitcast_convert_type(w & jnp.uint32(0xFFFF0000), F32)
    return jnp.concatenate([lo.astype(BF16), hi.astype(BF16)], axis=1)


def _swiglu(x, wg, wu, wd):
    g = jnp.dot(x, wg, preferred_element_type=F32)
    u = jnp.dot(x, wu, preferred_element_type=F32)
    a = (g * jax.nn.sigmoid(g) * u).astype(BF16)
    return jnp.dot(a, wd, preferred_element_type=F32)


def _expert_kernel(be_ref, nu_ref, x_ref, wg_ref, wu_ref, wd_ref, o_ref, wg_s, wu_s, wd_s):
    i = pl.program_id(0)

    @pl.when(i < nu_ref[0])
    def _():
        @pl.when((i == 0) | (be_ref[i] != be_ref[jnp.maximum(i - 1, 0)]))
        def _():
            wg_s[...] = wg_ref[0].astype(BF16)
            wu_s[...] = wu_ref[0].astype(BF16)
            wd_s[...] = wd_ref[0].astype(BF16)

        o_ref[...] = _swiglu(_unpack_bf16_pairs(x_ref[...]), wg_s[...], wu_s[...], wd_s[...])

    @pl.when(i >= nu_ref[0])
    def _():
        o_ref[...] = jnp.zeros(o_ref.shape, o_ref.dtype)


def _expert_ffn(x, block_e, n_used, wg, wu, wd, *, tm):
    n, dh = x.shape
    d = 2 * dh
    ff = wg.shape[2]
    assert n % tm == 0

    def blk(i, be, nu):
        return jnp.minimum(i, nu[0] - 1)

    grid_spec = pltpu.PrefetchScalarGridSpec(
        num_scalar_prefetch=2,
        grid=(n // tm,),
        in_specs=[
            pl.BlockSpec((tm, dh), lambda i, be, nu: (blk(i, be, nu), 0)),
            pl.BlockSpec((1, d, ff), lambda i, be, nu: (be[blk(i, be, nu)], 0, 0)),
            pl.BlockSpec((1, d, ff), lambda i, be, nu: (be[blk(i, be, nu)], 0, 0)),
            pl.BlockSpec((1, ff, d), lambda i, be, nu: (be[blk(i, be, nu)], 0, 0)),
        ],
        out_specs=pl.BlockSpec((tm, d), lambda i, be, nu: (i, 0)),
        scratch_shapes=[pltpu.VMEM((d, ff), BF16), pltpu.VMEM((d, ff), BF16), pltpu.VMEM((ff, d), BF16)],
    )
    return pl.pallas_call(
        _expert_kernel,
        grid_spec=grid_spec,
        out_shape=jax.ShapeDtypeStruct((n, d), F32),
        compiler_params=_cparams(("arbitrary",)),
    )(block_e, n_used, x, wg, wu, wd)


def _post_mix_kernel(mix_ref, x_ref, w_ref, gp_ref, gf_ref, g1_ref, sh_ref, sc_ref, rw_ref,
                     xa_ref, f_ref, lg_ref):
    y = jnp.dot(mix_ref[0], w_ref[...], preferred_element_type=F32)
    yn = y * lax.rsqrt(jnp.mean(y * y, axis=-1, keepdims=True) + NORM_EPS) * gp_ref[...]
    xa = x_ref[0] + g1_ref[0] * yn
    xa_ref[0] = xa
    fn = xa * lax.rsqrt(jnp.mean(xa * xa, axis=-1, keepdims=True) + NORM_EPS) * gf_ref[...]
    f = fn * (1.0 + sc_ref[0]) + sh_ref[0]
    f_ref[0] = _pack_bf16_pairs(f)
    lg_ref[...] = lax.dot_general(rw_ref[...], f, (((1,), (1,)), ((), ())),
                                  precision=lax.Precision.HIGHEST, preferred_element_type=F32)


def _post_mix(mix, xs, w_out, g_post, g_ffn, g1, sh2, sc2, router_wt, *, row_tile0, seg_tiles):
    b, rows, k = mix.shape
    d = w_out.shape[1]
    e = router_wt.shape[0]
    tm = ROW_TILE
    nt = rows // tm
    mod_map = lambda bi, i: (bi * 2 + jnp.where(i + row_tile0 >= seg_tiles, 1, 0), 0, 0)
    vec = lambda: pl.BlockSpec((1, d), lambda bi, i: (0, 0))
    return pl.pallas_call(
        _post_mix_kernel,
        grid=(b, nt),
        in_specs=[
            pl.BlockSpec((1, tm, k), lambda bi, i: (bi, i, 0)),
            pl.BlockSpec((1, tm, d), lambda bi, i: (bi, i + row_tile0, 0)),
            pl.BlockSpec((k, d), lambda bi, i: (0, 0)),
            vec(), vec(),
            pl.BlockSpec((1, 1, d), mod_map), pl.BlockSpec((1, 1, d), mod_map),
            pl.BlockSpec((1, 1, d), mod_map),
            pl.BlockSpec((e, d), lambda bi, i: (0, 0)),
        ],
        out_specs=[
            pl.BlockSpec((1, tm, d), lambda bi, i: (bi, i, 0)),
            pl.BlockSpec((1, tm, d // 2), lambda bi, i: (bi, i, 0)),
            pl.BlockSpec((e, tm), lambda bi, i: (0, bi * nt + i)),
        ],
        out_shape=[jax.ShapeDtypeStruct((b, rows, d), F32),
                   jax.ShapeDtypeStruct((b, rows, d // 2), jnp.uint32),
                   jax.ShapeDtypeStruct((e, b * rows), F32)],
        compiler_params=_cparams(("parallel", "arbitrary")),
    )(mix, xs, w_out, g_post.reshape(1, d).astype(F32), g_ffn.reshape(1, d).astype(F32),
      g1, sh2, sc2, router_wt)


def _route_kernel(lg_ref, rb_ref, e_ref, g_ref):
    per = N_EXPERTS // N_GROUPS
    tn = lg_ref.shape[1]
    neg = -jnp.inf
    r_io = lax.broadcasted_iota(jnp.int32, (per, tn), 0)
    scores, choice, gs = [], [], []
    for g in range(N_GROUPS):
        sg = jax.nn.sigmoid(lg_ref[g * per:(g + 1) * per, :])
        cg = sg + rb_ref[g * per:(g + 1) * per, :]
        m1 = jnp.max(cg, axis=0, keepdims=True)
        i1 = jnp.min(jnp.where(cg == m1, r_io, per), axis=0, keepdims=True)
        m2 = jnp.max(jnp.where(r_io == i1, neg, cg), axis=0, keepdims=True)
        scores.append(sg)
        choice.append(cg)
        gs.append(m1 + m2)
    masked = []
    for g in range(N_GROUPS):
        ahead = jnp.zeros((1, tn), jnp.int32)
        for o in range(N_GROUPS):
            if o < g:
                ahead = ahead + jnp.where(gs[o] >= gs[g], 1, 0)
            elif o > g:
                ahead = ahead + jnp.where(gs[o] > gs[g], 1, 0)
        masked.append(jnp.where(ahead < TOPK_GROUPS, choice[g], neg))
    ids, gates = [], []
    for _ in range(TOP_K):
        best = masked[0]
        for g in range(1, N_GROUPS):
            best = jnp.maximum(best, masked[g])
        best = jnp.max(best, axis=0, keepdims=True)
        cand = jnp.where(masked[0] == best, r_io, N_EXPERTS)
        for g in range(1, N_GROUPS):
            cand = jnp.minimum(cand, jnp.where(masked[g] == best, r_io + g * per, N_EXPERTS))
        idx = jnp.min(cand, axis=0, keepdims=True)
        gk = jnp.zeros((per, tn), F32)
        for g in range(N_GROUPS):
            hit = (r_io + g * per) == idx
            gk = gk + jnp.where(hit, scores[g], 0.0)
            masked[g] = jnp.where(hit, neg, masked[g])
        ids.append(idx)
        gates.append(jnp.sum(gk, axis=0, keepdims=True))
    total = gates[0]
    for k in range(1, TOP_K):
        total = total + gates[k]
    pad = 8 - TOP_K
    e_ref[...] = jnp.concatenate(ids + [jnp.zeros((pad, tn), jnp.int32)], axis=0)
    g_ref[...] = jnp.concatenate([gk / total * ROUTED_SCALE for gk in gates]
                                 + [jnp.zeros((pad, tn), F32)], axis=0)


def _route(logits_t, router_b, *, tn):
    e, n = logits_t.shape
    assert n % tn == 0
    return pl.pallas_call(
        _route_kernel,
        grid=(n // tn,),
        in_specs=[pl.BlockSpec((e, tn), lambda i: (0, i)),
                  pl.BlockSpec((e, 1), lambda i: (0, 0))],
        out_specs=[pl.BlockSpec((8, tn), lambda i: (0, i)), pl.BlockSpec((8, tn), lambda i: (0, i))],
        out_shape=[jax.ShapeDtypeStruct((8, n), jnp.int32), jax.ShapeDtypeStruct((8, n), F32)],
        compiler_params=_cparams(("parallel",)),
    )(logits_t, router_b.reshape(e, 1).astype(F32))


def _dispatch_kernel(e_ref, dest_ref, be_ref, nu_ref, cnt_scr, start_scr, run_scr, *, blk):
    ph = pl.program_id(0)
    i = pl.program_id(1)
    tn = e_ref.shape[1]
    e_io = lax.broadcasted_iota(jnp.int32, (N_EXPERTS, tn), 0)
    hot = jnp.zeros((N_EXPERTS, tn), F32)
    for k in range(TOP_K):
        hot = hot + jnp.where(e_io == e_ref[k:k + 1, :], 1.0, 0.0)
    tile_cnt = jnp.sum(hot, axis=1, keepdims=True)

    @pl.when((ph == 0) & (i == 0))
    def _():
        cnt_scr[...] = jnp.zeros(cnt_scr.shape, F32)

    @pl.when(ph == 0)
    def _():
        cnt_scr[...] += jnp.broadcast_to(tile_cnt, cnt_scr.shape)

    @pl.when((ph == 1) & (i == 0))
    def _():
        bpe = jnp.floor((cnt_scr[...] + (blk - 1.0)) * (1.0 / blk))
        r = lax.broadcasted_iota(jnp.int32, (N_EXPERTS, N_EXPERTS), 0)
        c = lax.broadcasted_iota(jnp.int32, (N_EXPERTS, N_EXPERTS), 1)
        lower = jnp.where(c < r, 1.0, 0.0)
        before = jnp.dot(lower, bpe, precision=lax.Precision.HIGHEST, preferred_element_type=F32)
        start_scr[...] = before * blk
        run_scr[...] = jnp.zeros(run_scr.shape, F32)
        ends = (before + bpe)[:, 0:1]
        nb = be_ref.shape[1]
        bid = lax.broadcasted_iota(jnp.int32, (N_EXPERTS, nb), 1).astype(F32)
        be = jnp.sum(jnp.where(ends <= bid, 1, 0), axis=0, keepdims=True)
        be_ref[...] = jnp.minimum(be, N_EXPERTS - 1).astype(jnp.int32)
        nu_ref[...] = jnp.broadcast_to(jnp.max(ends, axis=0, keepdims=True), nu_ref.shape).astype(jnp.int32)

    @pl.when(ph == 1)
    def _():
        rr = lax.broadcasted_iota(jnp.int32, (tn, tn), 0)
        cc = lax.broadcasted_iota(jnp.int32, (tn, tn), 1)
        upper = jnp.where(rr < cc, 1.0, 0.0).astype(BF16)
        prior = jnp.dot(hot.astype(BF16), upper, preferred_element_type=F32)
        pos = prior + jnp.concatenate([start_scr[...] + run_scr[...]] * (tn // LANES), axis=1)
        rows = []
        for k in range(TOP_K):
            rows.append(jnp.sum(jnp.where(e_io == e_ref[k:k + 1, :], pos, 0.0), axis=0, keepdims=True))
        rows.append(jnp.zeros((8 - TOP_K, tn), F32))
        dest_ref[0] = jnp.concatenate(rows, axis=0).astype(jnp.int32)
        run_scr[...] += jnp.broadcast_to(tile_cnt, run_scr.shape)


def _dispatch(top_e, *, blk, n_blocks):
    n = top_e.shape[1]
    tn = LANES
    nbp = -(-n_blocks // LANES) * LANES
    kern = functools.partial(_dispatch_kernel, blk=blk)
    return pl.pallas_call(
        kern,
        grid=(2, n // tn),
        in_specs=[pl.BlockSpec((8, tn), lambda ph, i: (0, i))],
        out_specs=[pl.BlockSpec((1, 8, tn), lambda ph, i: (i * ph, 0, 0)),
                   pl.BlockSpec((1, nbp), lambda ph, i: (0, 0)),
                   pl.BlockSpec((8, LANES), lambda ph, i: (0, 0))],
        out_shape=[jax.ShapeDtypeStruct((n // tn, 8, tn), jnp.int32),
                   jax.ShapeDtypeStruct((1, nbp), jnp.int32),
                   jax.ShapeDtypeStruct((8, LANES), jnp.int32)],
        scratch_shapes=[pltpu.VMEM((N_EXPERTS, LANES), F32)] * 3,
        compiler_params=_cparams(("arbitrary", "arbitrary")),
    )(top_e)


def _scatter_rows_kernel(dest_ref, x_ref, zero_ref, o_ref, sem):
    del zero_ref
    tm = x_ref.shape[0]

    def row_copy(t, k):
        return pltpu.make_async_copy(x_ref.at[pl.ds(t, 1)], o_ref.at[pl.ds(dest_ref[0, k, t], 1)], sem)

    def issue(t, carry):
        for k in range(TOP_K):
            row_copy(t, k).start()
        return carry

    def drain(t, carry):
        for k in range(TOP_K):
            row_copy(t, k).wait()
        return carry

    lax.fori_loop(0, tm, issue, 0)
    lax.fori_loop(0, tm, drain, 0)


def _scatter_rows(dest, x, n_rows):
    n, dh = x.shape
    tm = LANES
    zeros = jnp.zeros((n_rows, dh), x.dtype)
    return pl.pallas_call(
        _scatter_rows_kernel,
        grid=(n // tm,),
        in_specs=[pl.BlockSpec((1, 8, tm), lambda i: (i, 0, 0), memory_space=pltpu.SMEM),
                  pl.BlockSpec((tm, dh), lambda i: (i, 0)),
                  pl.BlockSpec(memory_space=pl.ANY)],
        out_specs=pl.BlockSpec(memory_space=pl.ANY),
        out_shape=jax.ShapeDtypeStruct((n_rows, dh), x.dtype),
        scratch_shapes=[pltpu.SemaphoreType.DMA(())],
        input_output_aliases={2: 0},
        compiler_params=_cparams(("arbitrary",)),
    )(dest, x, zeros)


def _combine_kernel(dest_ref, y_ref, gate_ref, f_ref, xa_ref, sg_ref, su_ref, sd_ref, gp_ref, g2_ref,
                    o_ref, buf, sem):
    tm = f_ref.shape[0]

    def row_copy(t, k):
        return pltpu.make_async_copy(y_ref.at[pl.ds(dest_ref[0, k, t], 1)], buf.at[k, pl.ds(t, 1)], sem)

    def issue(t, carry):
        for k in range(TOP_K):
            row_copy(t, k).start()
        return carry

    def drain(t, carry):
        for k in range(TOP_K):
            row_copy(t, k).wait()
        return carry

    lax.fori_loop(0, tm, issue, 0)
    f = _swiglu(_unpack_bf16_pairs(f_ref[...]), sg_ref[...], su_ref[...], sd_ref[...])
    lax.fori_loop(0, tm, drain, 0)
    gate = gate_ref[...]
    for k in range(TOP_K):
        f = f + gate[:, k:k + 1] * buf[k]
    fn = f * lax.rsqrt(jnp.mean(f * f, axis=-1, keepdims=True) + NORM_EPS) * gp_ref[...]
    o_ref[0] = xa_ref[0] + g2_ref[0] * fn


def _combine(dest, y, gate, f_pk, xa, sg, su, sd, g_post, g2, *, row_tile0, seg_tiles):
    b, rows, d = xa.shape
    tm = LANES
    nt = rows // tm
    per = ROW_TILE // tm
    mod_map = lambda bi, i: (bi * 2 + jnp.where(i // per + row_tile0 >= seg_tiles, 1, 0), 0, 0)
    const = lambda shape: pl.BlockSpec(shape, lambda bi, i: (0,) * len(shape))
    return pl.pallas_call(
        _combine_kernel,
        grid=(b, nt),
        in_specs=[
            pl.BlockSpec((1, 8, tm), lambda bi, i: (bi * nt + i, 0, 0), memory_space=pltpu.SMEM),
            pl.BlockSpec(memory_space=pl.ANY),
            pl.BlockSpec((tm, 8), lambda bi, i: (bi * nt + i, 0)),
            pl.BlockSpec((tm, d // 2), lambda bi, i: (bi * nt + i, 0)),
            pl.BlockSpec((1, tm, d), lambda bi, i: (bi, i, 0)),
            const(sg.shape), const(su.shape), const(sd.shape), const((1, d)),
            pl.BlockSpec((1, 1, d), mod_map),
        ],
        out_specs=pl.BlockSpec((1, tm, d), lambda bi, i: (bi, i, 0)),
        out_shape=jax.ShapeDtypeStruct((b, rows, d), F32),
        scratch_shapes=[pltpu.VMEM((TOP_K, tm, d), F32), pltpu.SemaphoreType.DMA(())],
        compiler_params=_cparams(("arbitrary", "arbitrary")),
    )(dest, y, gate, f_pk, xa, sg, su, sd, g_post.reshape(1, d).astype(F32), g2)


def _rope_tables(s, ctx_len, dim):
    rows = s // GRID_W
    row = jnp.repeat(jnp.arange(rows), GRID_W)
    col = jnp.tile(jnp.arange(GRID_W), rows)
    quarter = dim // 4
    inv = ROPE_BASE ** (-jnp.arange(quarter, dtype=F32) / quarter)
    ang = jnp.concatenate([row.astype(F32)[:, None] * inv, col.astype(F32)[:, None] * inv], axis=-1)
    cos = jnp.repeat(jnp.cos(ang), 2, axis=-1)
    sin = jnp.repeat(jnp.sin(ang), 2, axis=-1) * jnp.tile(jnp.array([-1.0, 1.0], F32), dim // 2)
    cos = jnp.concatenate([jnp.ones((ctx_len, dim), F32), cos], axis=0)
    sin = jnp.concatenate([jnp.zeros((ctx_len, dim), F32), sin], axis=0)
    return cos, sin


def _log_sigmoid(x):
    return jnp.minimum(x, 0.0) - jnp.log1p(jnp.exp(-jnp.abs(x)))


def _mlstm_kernel(q_ref, k_ref, kt_ref, v_ref, g_ref, gt_ref, gb_ref, gbt_ref, hf_ref, mo_ref, mn_ref,
                  o_ref, s_scr, m_scr, *, reverse, final):
    L = MLSTM_CHUNK
    hv = MLSTM_V
    io, fo = (2 * MLSTM_HEADS, 3 * MLSTM_HEADS) if reverse else (0, MLSTM_HEADS)

    @pl.when(pl.program_id(1) == 0)
    def _():
        s_scr[...] = jnp.zeros(s_scr.shape, F32)
        m_scr[...] = jnp.zeros(m_scr.shape, F32)

    g = g_ref[0] + gb_ref[...]
    gt = gt_ref[0] + gbt_ref[...]
    r_io = lax.broadcasted_iota(jnp.int32, (L, L), 0)
    c_io = lax.broadcasted_iota(jnp.int32, (L, L), 1)
    seen = (c_io >= r_io) if reverse else (c_io <= r_io)
    tri = jnp.where(seen, 1.0, 0.0)
    hi = lax.Precision.HIGHEST
    bc_col = jnp.dot(tri, _log_sigmoid(g), precision=hi, preferred_element_type=F32)
    lf_row = _log_sigmoid(gt)
    bc_row = lax.dot_general(lf_row, tri, (((1,), (1,)), ((), ())), precision=hi,
                             preferred_element_type=F32)
    lane = lax.broadcasted_iota(jnp.int32, (L, LANES), 1)
    sub = lax.broadcasted_iota(jnp.int32, (LANES, L), 0)

    for h in range(MLSTM_HEADS):
        pair, odd = h // 2, h % 2
        lo = odd * MLSTM_QK
        a_col = bc_col[:, fo + h:fo + h + 1]
        i_col = g[:, io + h:io + h + 1]
        b_row = bc_row[fo + h:fo + h + 1, :]
        i_row = gt[io + h:io + h + 1, :]
        btot = jnp.sum(lf_row[fo + h:fo + h + 1, :], axis=1, keepdims=True)
        m_st = m_scr[h:h + 1, 0:1]

        w_end = btot - a_col + i_col
        m_new = jnp.maximum(btot + m_st, jnp.max(w_end, axis=0, keepdims=True))
        decay = jnp.exp(btot + m_st - m_new)
        w_k = jnp.exp(w_end - m_new)

        log_d = jnp.where(seen, a_col - b_row + i_row, -jnp.inf)
        log_inter = a_col + m_st
        m_row = jnp.maximum(log_inter, jnp.max(log_d, axis=1, keepdims=True))
        w_intra = jnp.exp(log_d - m_row)
        w_inter = jnp.exp(log_inter - m_row)

        in_head = (lane >= lo) & (lane < lo + MLSTM_QK)
        qm = jnp.where(in_head, q_ref[0, :, pair * LANES:(pair + 1) * LANES], 0).astype(BF16)
        kp = k_ref[0, :, pair * LANES:(pair + 1) * LANES]
        v = v_ref[0, :, h * hv:(h + 1) * hv]
        state = s_scr[h]

        qk = lax.dot_general(qm, kp, (((1,), (1,)), ((), ())), preferred_element_type=F32)
        qk = qk * (MLSTM_QK ** -0.5) * w_intra
        inter = jnp.dot(qm, state.astype(BF16), preferred_element_type=F32)
        num = w_inter * inter[:, :hv] + jnp.dot(qk.astype(BF16), v, preferred_element_type=F32)
        den = w_inter * inter[:, hv:hv + 1] + jnp.sum(qk, axis=1, keepdims=True)
        out = num / jnp.maximum(jnp.abs(den), jnp.exp(-m_row))

        wv = jnp.concatenate([(w_k * v.astype(F32)).astype(BF16),
                              jnp.where(lane == 0, w_k, 0.0).astype(BF16)], axis=1)
        kt = kt_ref[0, pair * LANES:(pair + 1) * LANES, :]
        in_rows = (sub >= lo) & (sub < lo + MLSTM_QK)
        ktm = (jnp.where(in_rows, kt, 0).astype(F32) * (MLSTM_QK ** -0.5)).astype(BF16)
        s_scr[h] = decay * state + jnp.dot(ktm, wv, preferred_element_type=F32)
        m_scr[h:h + 1, :] = jnp.broadcast_to(m_new, (1, LANES))

        if final:
            tot = out + hf_ref[0, :, h * hv:(h + 1) * hv]
            nrm = tot * lax.rsqrt(jnp.mean(tot * tot, axis=-1, keepdims=True) + NORM_EPS) * mn_ref[...]
            gate = jax.nn.sigmoid(mo_ref[0, :, h * hv:(h + 1) * hv].astype(F32))
            o_ref[0, :, h * hv:(h + 1) * hv] = (nrm * gate).astype(o_ref.dtype)
        else:
            o_ref[0, :, h * hv:(h + 1) * hv] = out.astype(o_ref.dtype)


def _mlstm(pv, kt, gates, gates_t, gate_b, m_norm, hf, *, ctx_chunks, reverse):
    b, t, _ = pv.shape
    L = MLSTM_CHUNK
    nc = t // L
    final = hf is not None
    nq = MLSTM_HEADS * MLSTM_QK
    nv = MLSTM_HEADS * MLSTM_V

    def chunk(j):
        if not reverse:
            return j
        return jnp.where(j < ctx_chunks, ctx_chunks - 1 - j, nc - 1 - (j - ctx_chunks))

    gb = jnp.pad(gate_b.astype(F32).reshape(-1), (0, LANES - gate_b.size))
    if hf is None:
        hf = jnp.zeros((1, L, nv), F32)
        hf_spec = pl.BlockSpec((1, L, nv), lambda bi, j: (0, 0, 0))
    else:
        hf_spec = pl.BlockSpec((1, L, nv), lambda bi, j: (bi, chunk(j), 0))
    kern = functools.partial(_mlstm_kernel, reverse=reverse, final=final)
    return pl.pallas_call(
        kern,
        grid=(b, nc),
        in_specs=[
            pl.BlockSpec((1, L, nq), lambda bi, j: (bi, chunk(j), nv // nq)),
            pl.BlockSpec((1, L, nq), lambda bi, j: (bi, chunk(j), nv // nq + 1)),
            pl.BlockSpec((1, nq, L), lambda bi, j: (bi, 0, chunk(j))),
            pl.BlockSpec((1, L, nv), lambda bi, j: (bi, chunk(j), 2)),
            pl.BlockSpec((1, L, LANES), lambda bi, j: (bi, chunk(j), 0)),
            pl.BlockSpec((1, LANES, L), lambda bi, j: (bi, 0, chunk(j))),
            pl.BlockSpec((1, LANES), lambda bi, j: (0, 0)),
            pl.BlockSpec((LANES, 1), lambda bi, j: (0, 0)),
            hf_spec,
            pl.BlockSpec((1, L, nv), lambda bi, j: (bi, chunk(j), 3)),
            pl.BlockSpec((1, MLSTM_V), lambda bi, j: (0, 0)),
        ],
        out_specs=pl.BlockSpec((1, L, nv), lambda bi, j: (bi, chunk(j), 0)),
        out_shape=jax.ShapeDtypeStruct((b, t, nv), BF16 if final else F32),
        scratch_shapes=[pltpu.VMEM((MLSTM_HEADS, LANES, 2 * LANES), F32), pltpu.VMEM((MLSTM_HEADS, LANES), F32)],
        compiler_params=_cparams(("parallel", "arbitrary")),
    )(pv, pv, kt, pv, gates, gates_t, gb.reshape(1, LANES), gb.reshape(LANES, 1), hf, pv,
      m_norm.reshape(1, MLSTM_V).astype(F32))


def _swa_kernel(sink_ref, q_ref, kp_ref, kc_ref, kn_ref, kx_ref, vp_ref, vc_ref, vn_ref, vx_ref, o_ref, *,
                n_blocks):
    i = pl.program_id(1)
    bq = BLOCK_Q
    hd = SWA_HD
    rep = SWA_HEADS // SWA_KV_HEADS
    n_ctx = kx_ref.shape[1]
    rows = rep * bq
    scale = hd ** -0.5 * LOG2E
    row = lax.broadcasted_iota(jnp.int32, (rows, 3 * bq), 0)
    col = lax.broadcasted_iota(jnp.int32, (rows, 3 * bq), 1)
    dt = (col - bq) - (row % bq)
    blk = col // bq
    ok = (jnp.abs(dt) <= WINDOW) & ((blk != 0) | (i > 0)) & ((blk != 2) | (i < n_blocks - 1))
    head_of_row = lax.broadcasted_iota(jnp.int32, (rows, 1), 0) // bq
    for g in range(SWA_KV_HEADS):
        q = jnp.concatenate([q_ref[0, :, (g * rep + r) * hd:(g * rep + r + 1) * hd] for r in range(rep)], axis=0)
        q = (q.astype(F32) * scale).astype(BF16)
        cs = slice(g * hd, (g + 1) * hd)
        k_loc = jnp.concatenate([kp_ref[0, :, cs], kc_ref[0, :, cs], kn_ref[0, :, cs]], axis=0)
        v_loc = jnp.concatenate([vp_ref[0, :, cs], vc_ref[0, :, cs], vn_ref[0, :, cs]], axis=0)
        nt = (((1,), (1,)), ((), ()))
        s_loc = jnp.where(ok, lax.dot_general(q, k_loc, nt, preferred_element_type=F32), -jnp.inf)
        s_ctx = lax.dot_general(q, kx_ref[0, :, cs], nt, preferred_element_type=F32)
        sink = jnp.zeros((rows, 1), F32)
        for r in range(rep):
            sink = jnp.where(head_of_row == r, sink_ref[g * rep + r] * LOG2E, sink)
        m = jnp.maximum(jnp.maximum(jnp.max(s_loc, axis=-1, keepdims=True),
                                    jnp.max(s_ctx, axis=-1, keepdims=True)), sink)
        p_loc = jnp.exp2(s_loc - m)
        p_ctx = jnp.exp2(s_ctx - m)
        den = (jnp.sum(p_loc, axis=-1, keepdims=True) + jnp.sum(p_ctx, axis=-1, keepdims=True)
               + jnp.exp2(sink - m))
        out = (jnp.dot(p_loc.astype(BF16), v_loc, preferred_element_type=F32)
               + jnp.dot(p_ctx.astype(BF16), vx_ref[0, :, cs], preferred_element_type=F32)) / den
        for r in range(rep):
            o_ref[0, :, (g * rep + r) * hd:(g * rep + r + 1) * hd] = out[r * bq:(r + 1) * bq].astype(o_ref.dtype)


def _swa(qk, v, sink, *, n_ctx, k_col0):
    b, t, _ = qk.shape
    bq = BLOCK_Q
    cb = n_ctx // bq
    nb = (t - n_ctx) // bq
    nq = SWA_HEADS * SWA_HD
    nk = SWA_KV_HEADS * SWA_HD
    kcol = k_col0
    prev = lambda bi, i, c: (bi, i + cb - 1, c)
    cur = lambda bi, i, c: (bi, i + cb, c)
    nxt = lambda bi, i, c: (bi, jnp.minimum(i + cb + 1, nb + cb - 1), c)
    kern = functools.partial(_swa_kernel, n_blocks=nb)

    def spec(rows, fn, c):
        return pl.BlockSpec((1, rows, nk), lambda bi, i: fn(bi, i, c))

    ctx = lambda bi, i, c: (bi, 0, c)
    return pl.pallas_call(
        kern,
        grid=(b, nb),
        in_specs=[
            pl.BlockSpec(memory_space=pltpu.SMEM),
            pl.BlockSpec((1, bq, nq), lambda bi, i: (bi, i + cb, 0)),
            spec(bq, prev, kcol), spec(bq, cur, kcol), spec(bq, nxt, kcol), spec(n_ctx, ctx, kcol),
            spec(bq, prev, 0), spec(bq, cur, 0), spec(bq, nxt, 0), spec(n_ctx, ctx, 0),
        ],
        out_specs=pl.BlockSpec((1, bq, nq), lambda bi, i: (bi, i, 0)),
        out_shape=jax.ShapeDtypeStruct((b, nb * bq, nq), BF16),
        compiler_params=_cparams(("parallel", "arbitrary")),
    )(sink.astype(F32), qk, qk, qk, qk, qk, v, v, v, v)


def _mix_out_and_moe(mix, xs, w_out, g_post, g_ffn_pre, g_ffn_post, mods, router_w, router_b,
                     wg, wu, wd, sg, su, sd, *, row_tile0, seg_tiles):
    sh1, sc1, g1, sh2, sc2, g2 = mods
    b, rows, _ = mix.shape
    n = b * rows
    seg = dict(row_tile0=row_tile0, seg_tiles=seg_tiles)
    xa, f_pk, logits_t = _post_mix(mix, xs, w_out.astype(BF16), g_post, g_ffn_pre, g1, sh2, sc2,
                                   router_w.T.astype(F32), **seg)
    top_e, gate = _route(logits_t, router_b, tn=512)
    blk = EXPERT_ROWS
    n_blocks = -(-n * TOP_K // blk) + N_EXPERTS
    dest, block_e, n_used = _dispatch(top_e, blk=blk, n_blocks=n_blocks)
    f_pk = f_pk.reshape(n, -1)
    x_sorted = _scatter_rows(dest, f_pk, n_blocks * blk)
    y = _expert_ffn(x_sorted, block_e[0, :n_blocks], n_used[0, :1], wg, wu, wd, tm=blk)
    return _combine(dest, y, gate.T, f_pk, xa, sg.astype(BF16), su.astype(BF16), sd.astype(BF16),
                    g_ffn_post, g2, **seg)


def kernel(x, c, ctx, c_ctx, mod_w, mod_b, norm_mix_pre, norm_mix_post, norm_ffn_pre, norm_ffn_post, ab_w_in, ab_w_out, diff_lambda, diff_norm, mlstm_gate_b, mlstm_norm, cd_w_in, cd_w_out, mla_q_norm, mla_w_uq, mla_kv_norm, mla_w_ukv, swa_sink, router_w, router_b, exp_w_gate, exp_w_up, exp_w_down, sh_w_gate, sh_w_up, sh_w_down):
    b, s, d = x.shape
    L = ctx.shape[1]
    t = L + s
    assert L == ROW_TILE and s % ROW_TILE == 0
    depth = mod_w.shape[0]
    ctx_tiles = L // ROW_TILE
    tk = next(c for c in (2816, 768, ROW_TILE) if t % c == 0)

    cos64, sin64 = _rope_tables(s, L, DIFF_HD)
    cos64 = jnp.tile(cos64, (1, LANES // DIFF_HD))
    sin64 = jnp.tile(sin64, (1, LANES // DIFF_HD))
    cos128, sin128 = _rope_tables(s, L, SWA_HD)
    cos_kr = jnp.concatenate([cos64[:, :MLA_ROPE], jnp.ones((t, LANES - MLA_ROPE), F32)], axis=1)
    sin_kr = jnp.concatenate([sin64[:, :MLA_ROPE], jnp.zeros((t, LANES - MLA_ROPE), F32)], axis=1)

    xs = jnp.concatenate([ctx, x], axis=1)

    for layer in range(depth):
        with_ctx = layer < depth - 1
        j = layer // 2
        mod_l = jax.nn.silu(c) @ mod_w[layer] + mod_b[layer]
        mod_c = jax.nn.silu(c_ctx) @ mod_w[layer] + mod_b[layer]
        mods = jnp.stack([jnp.broadcast_to(mod_c, (b, 6 * d)), mod_l], axis=1).reshape(b * 2, 1, 6, d)
        sh1, sc1, g1, sh2, sc2, g2 = (mods[:, :, m] for m in range(6))

        proj = functools.partial(_norm_proj, xs, norm_mix_pre[layer], sh1, sc1, seg_tiles=ctx_tiles)
        if layer % 2 == 0:
            w_in = ab_w_in[j].astype(BF16)
            n_qk = 2 * DIFF_HEADS * 2 * DIFF_HD
            qk = proj(w_in[:, :n_qk], tn=1024, rope=(cos64[None], sin64[None]), pattern=(0,) * 8)
            pv = proj(w_in[:, n_qk:n_qk + AB_PLAIN], tn=1024)
            w_gates = jnp.pad(w_in[:, n_qk + AB_PLAIN:], ((0, 0), (0, LANES - 4 * MLSTM_HEADS)))
            gates = proj(w_gates, tn=LANES, out_dtype=F32)

            lam_init = 0.8 - 0.6 * math.exp(-0.3 * layer)
            a = _attention(qk if with_ctx else qk[:, L:], qk, pv, heads=DIFF_HEADS, dq=2 * DIFF_HD,
                           dv=DIFF_VD, q_col0=0, k_col0=DIFF_HEADS, v_col0=0, tq=ROW_TILE,
                           scale=DIFF_HD ** -0.5, tk=tk, diff=True, lam_vec=diff_lambda[j],
                           d_norm=diff_norm[j], lam_init=lam_init,
                           ctx_tiles=ctx_tiles if with_ctx else 0, ctx_len=L)
            o_mk = DIFF_HEADS * DIFF_VD + MLSTM_HEADS * MLSTM_QK
            kt = jnp.swapaxes(pv[..., o_mk:o_mk + MLSTM_HEADS * MLSTM_QK], 1, 2)
            scan = functools.partial(_mlstm, pv, kt, gates, jnp.swapaxes(gates, 1, 2), mlstm_gate_b[j],
                                     mlstm_norm[j], ctx_chunks=L // MLSTM_CHUNK)
            m = scan(scan(None, reverse=False), reverse=True)
            mix = jnp.concatenate([a, m if with_ctx else m[:, L:]], axis=-1)
            w_out = ab_w_out[j]
        else:
            assert not with_ctx
            w_in = cd_w_in[j].astype(BF16)
            c0 = Q_LORA
            c1 = c0 + KV_LORA
            c2 = c1 + MLA_ROPE
            c3 = c2 + SWA_HEADS * SWA_HD
            c4 = c3 + SWA_KV_HEADS * SWA_HD
            w_rope = jnp.concatenate([w_in[:, c2:c4], w_in[:, c1:c2],
                                      jnp.zeros((d, LANES - MLA_ROPE), BF16)], axis=1)
            n_rope = w_rope.shape[1]
            rp = proj(w_rope, tn=n_rope, rope=(jnp.stack([cos128, cos_kr]), jnp.stack([sin128, sin_kr])),
                      pattern=(0,) * (n_rope // LANES - 1) + (1,))
            q_pad = 512 - Q_LORA
            w_plain = jnp.concatenate([w_in[:, c4:], w_in[:, :c0], jnp.zeros((d, q_pad), BF16),
                                       w_in[:, c0:c1]], axis=1)
            pp = proj(w_plain, tn=w_plain.shape[1])
            n_sv = SWA_KV_HEADS * SWA_HD
            cq = pp[..., n_sv:n_sv + 512]
            ckv = pp[..., n_sv + 512:]

            hq = MLA_NOPE + MLA_ROPE
            w_uq = mla_w_uq[j].astype(BF16).reshape(Q_LORA, MLA_HEADS, hq)
            w_uq = jnp.pad(w_uq, ((0, q_pad), (0, 0), (0, 256 - hq))).reshape(512, MLA_HEADS * 256)
            no_mod = jnp.zeros((b * 2, 1, 512), F32)
            qn = jnp.pad(mla_q_norm[j], (0, q_pad))
            q_mla = _norm_proj(cq[:, L:], qn, no_mod, no_mod, w_uq, seg_tiles=0, tn=1024, n_valid=Q_LORA,
                               rope=(cos_kr[None, L:], sin_kr[None, L:]), pattern=(-1, 0) * 4)
            kv = _norm_proj(ckv, mla_kv_norm[j], no_mod, no_mod, mla_w_ukv[j].astype(BF16),
                            seg_tiles=ctx_tiles, tn=1024).reshape(b, t, MLA_HEADS, MLA_NOPE + MLA_V)
            kr = rp[..., n_rope - LANES:]
            k_mla = jnp.concatenate([kv[..., :MLA_NOPE],
                                     jnp.broadcast_to(kr[:, :, None, :], (b, t, MLA_HEADS, LANES))],
                                    axis=-1).reshape(b, t, MLA_HEADS * 256)
            v_mla = kv[..., MLA_NOPE:].reshape(b, t, MLA_HEADS * MLA_V)
            a = _attention(q_mla, k_mla, v_mla, heads=MLA_HEADS, dq=256, dv=MLA_V, q_col0=0, k_col0=0,
                           v_col0=0, tq=512, scale=MLA_SCALE, tk=tk)

            w = _swa(rp, pp, swa_sink[j], n_ctx=L, k_col0=SWA_HEADS * SWA_HD // n_sv)
            mix = jnp.concatenate([a, w], axis=-1)
            w_out = cd_w_out[j]

        xa = _mix_out_and_moe(mix, xs, w_out, norm_mix_post[layer], norm_ffn_pre[layer], norm_ffn_post[layer],
                              (sh1, sc1, g1, sh2, sc2, g2), router_w[layer], router_b[layer],
                              exp_w_gate[layer], exp_w_up[layer], exp_w_down[layer],
                              sh_w_gate[layer], sh_w_up[layer], sh_w_down[layer],
                              row_tile0=0 if with_ctx else ctx_tiles, seg_tiles=ctx_tiles)
        xs = xa if with_ctx else jnp.concatenate([xs[:, :L], xa], axis=1)
    return xs[:, L:]
```

```python
import functools
import math

import jax
import jax.numpy as jnp
import numpy as np
from jax import lax
from jax.experimental import pallas as pl
from jax.experimental.pallas import tpu as pltpu

F32 = jnp.float32
BF16 = jnp.bfloat16

GRID_W = 64
ROPE_BASE = 10000.0
NORM_EPS = 1e-6

DIFF_HEADS = 8
DIFF_HD = 64
DIFF_VD = 2 * DIFF_HD
MLSTM_HEADS = 8
MLSTM_QK = 64
MLSTM_V = 128
MLSTM_CHUNK = 128
MLA_HEADS = 8
MLA_NOPE = 128
MLA_ROPE = 64
MLA_V = 128
Q_LORA = 448
KV_LORA = 512
MLA_SCALE = (MLA_NOPE + MLA_ROPE) ** -0.5
SWA_HEADS = 8
SWA_KV_HEADS = 2
SWA_HD = 128
WINDOW = 128
BLOCK_Q = 128
N_EXPERTS = 64
TOP_K = 6
N_GROUPS = 8
TOPK_GROUPS = 4
ROUTED_SCALE = 2.5
AB_PLAIN = DIFF_HEADS * DIFF_VD + 2 * MLSTM_HEADS * MLSTM_QK + 2 * MLSTM_HEADS * MLSTM_V

LANES = 128
LOG2E = 1.4426950408889634
VMEM_LIMIT = 56 * 1024 * 1024

ROW_TILE = 256
EXPERT_ROWS = 256


def _cparams(sem):
    return pltpu.CompilerParams(dimension_semantics=sem, vmem_limit_bytes=VMEM_LIMIT)


def _pair_swap(a):
    lane = lax.broadcasted_iota(jnp.int32, a.shape, 1)
    return jnp.where(lane % 2 == 0, pltpu.roll(a, LANES - 1, 1), pltpu.roll(a, 1, 1))


def _norm_proj_kernel(x_ref, g_ref, sh_ref, sc_ref, w_ref, cos_ref, sin_ref, o_ref, h_scr, *,
                      n_valid, pattern):
    j = pl.program_id(2)

    @pl.when(j == 0)
    def _():
        x = x_ref[0].astype(F32)
        ms = jnp.sum(x * x, axis=-1, keepdims=True) * (1.0 / n_valid)
        y = x * lax.rsqrt(ms + NORM_EPS) * g_ref[...]
        h_scr[...] = (y * (1.0 + sc_ref[0]) + sh_ref[0]).astype(BF16)

    acc = jnp.dot(h_scr[...], w_ref[...], preferred_element_type=F32)
    if pattern is None:
        o_ref[0] = acc.astype(o_ref.dtype)
    else:
        for c, tbl in enumerate(pattern):
            a = acc[:, c * LANES:(c + 1) * LANES]
            if tbl >= 0:
                a = a * cos_ref[tbl] + _pair_swap(a) * sin_ref[tbl]
            o_ref[0, :, c * LANES:(c + 1) * LANES] = a.astype(o_ref.dtype)


def _norm_proj(x, g, shift, scale, w, *, seg_tiles, tn, out_dtype=BF16, n_valid=None,
               rope=None, pattern=None):
    b, t, k = x.shape
    n = w.shape[1]
    tm = ROW_TILE
    assert t % tm == 0 and n % tn == 0 and tn % LANES == 0
    if rope is None:
        cos = sin = jnp.zeros((1, tm, LANES), F32)
        tbl_map = lambda bi, i, j: (0, 0, 0)
    else:
        cos, sin = rope
        tbl_map = lambda bi, i, j: (0, i, 0)
    ntab = cos.shape[0]
    kern = functools.partial(_norm_proj_kernel, n_valid=float(n_valid or k), pattern=pattern)
    mod_map = lambda bi, i, j: (bi * 2 + jnp.where(i >= seg_tiles, 1, 0), 0, 0)
    return pl.pallas_call(
        kern,
        grid=(b, t // tm, n // tn),
        in_specs=[
            pl.BlockSpec((1, tm, k), lambda bi, i, j: (bi, i, 0)),
            pl.BlockSpec((1, k), lambda bi, i, j: (0, 0)),
            pl.BlockSpec((1, 1, k), mod_map),
            pl.BlockSpec((1, 1, k), mod_map),
            pl.BlockSpec((k, tn), lambda bi, i, j: (0, j)),
            pl.BlockSpec((ntab, tm, LANES), tbl_map),
            pl.BlockSpec((ntab, tm, LANES), tbl_map),
        ],
        out_specs=pl.BlockSpec((1, tm, tn), lambda bi, i, j: (bi, i, j)),
        out_shape=jax.ShapeDtypeStruct((b, t, n), out_dtype),
        scratch_shapes=[pltpu.VMEM((tm, k), BF16)],
        compiler_params=_cparams(("parallel", "parallel", "arbitrary")),
    )(x, g.reshape(1, k).astype(F32), shift, scale, w, cos, sin)


def _attn_kernel(q_ref, k_ref, v_ref, lam_ref, dn_ref, o_ref, q_scr, s_scr, m_scr, l_scr, acc_scr, *,
                 scale, diff, lam_init, ctx_tiles, ctx_len, tq, tk, t):
    i = pl.program_id(2)
    q = q_ref[0].astype(F32) * (scale * LOG2E)
    if diff:
        lane = lax.broadcasted_iota(jnp.int32, q.shape, 1)
        half = q.shape[1] // 2
        q_scr[0:tq] = jnp.where(lane < half, q, 0.0).astype(BF16)
        q_scr[tq:2 * tq] = jnp.where(lane >= half, q, 0.0).astype(BF16)
    else:
        q_scr[...] = q.astype(BF16)

    def attend(kv_len, chunk):
        n_chunks = kv_len // chunk
        groups = chunk // LANES

        def pass1(c, carry):
            k = k_ref[0, pl.ds(pl.multiple_of(c * chunk, chunk), chunk), :]
            s = lax.dot_general(q_scr[...], k, (((1,), (1,)), ((), ())), preferred_element_type=F32)
            s_scr[c, :, 0:chunk] = s
            m = m_scr[...]
            for g in range(groups):
                m = jnp.maximum(m, s[:, g * LANES:(g + 1) * LANES])
            m_scr[...] = m
            return carry

        def pass2(c, carry):
            s = s_scr[c, :, 0:chunk]
            p = jnp.exp2(s - jnp.concatenate([m_scr[...]] * groups, axis=1))
            l = l_scr[...]
            for g in range(groups):
                l = l + p[:, g * LANES:(g + 1) * LANES]
            l_scr[...] = l
            v = v_ref[0, pl.ds(pl.multiple_of(c * chunk, chunk), chunk), :]
            acc_scr[...] += jnp.dot(p.astype(BF16), v, preferred_element_type=F32)
            return carry

        m_scr[...] = jnp.full(m_scr.shape, -jnp.inf, F32)
        if n_chunks == 1:
            pass1(0, 0)
        else:
            lax.fori_loop(0, n_chunks, pass1, 0)
        m_scr[...] = jnp.broadcast_to(jnp.max(m_scr[...], axis=-1, keepdims=True), m_scr.shape)
        l_scr[...] = jnp.zeros(l_scr.shape, F32)
        acc_scr[...] = jnp.zeros(acc_scr.shape, F32)
        if n_chunks == 1:
            pass2(0, 0)
        else:
            lax.fori_loop(0, n_chunks, pass2, 0)

        o = acc_scr[...] / jnp.sum(l_scr[...], axis=-1, keepdims=True)
        if diff:
            lv = lam_ref[...]
            lam = (jnp.exp(jnp.sum(lv[0:1] * lv[1:2], axis=-1, keepdims=True))
                   - jnp.exp(jnp.sum(lv[2:3] * lv[3:4], axis=-1, keepdims=True)) + lam_init)
            a = o[0:tq] - lam * o[tq:2 * tq]
            ms = jnp.mean(a * a, axis=-1, keepdims=True)
            a = a * lax.rsqrt(ms + NORM_EPS) * dn_ref[...] * (1.0 - lam_init)
            o_ref[0] = a.astype(o_ref.dtype)
        else:
            o_ref[0] = o.astype(o_ref.dtype)

    if ctx_tiles:
        pl.when(i < ctx_tiles)(lambda: attend(ctx_len, ctx_len))
        pl.when(i >= ctx_tiles)(lambda: attend(t, tk))
    else:
        attend(t, tk)


def _attention(q, k, v, *, heads, dq, dv, q_col0, k_col0, v_col0, tq, scale, tk, diff=False,
               lam_vec=None, d_norm=None, lam_init=0.0, ctx_tiles=0, ctx_len=0):
    b, t, _ = k.shape
    sq = q.shape[1]
    assert t % tk == 0 and sq % tq == 0 and tk % LANES == 0 and ctx_len % LANES == 0
    assert ctx_len <= tk
    rows = 2 * tq if diff else tq
    if lam_vec is None:
        lam_vec = jnp.zeros((4, DIFF_HD), F32)
        d_norm = jnp.zeros((dv,), F32)
    kern = functools.partial(_attn_kernel, scale=scale, diff=diff, lam_init=lam_init,
                             ctx_tiles=ctx_tiles, ctx_len=ctx_len, tq=tq, tk=tk, t=t)
    return pl.pallas_call(
        kern,
        grid=(b, heads, sq // tq),
        in_specs=[
            pl.BlockSpec((1, tq, dq), lambda bi, h, i: (bi, i, q_col0 + h)),
            pl.BlockSpec((1, t, dq), lambda bi, h, i: (bi, 0, k_col0 + h)),
            pl.BlockSpec((1, t, dv), lambda bi, h, i: (bi, 0, v_col0 + h)),
            pl.BlockSpec((4, DIFF_HD), lambda bi, h, i: (0, 0)),
            pl.BlockSpec((1, dv), lambda bi, h, i: (0, 0)),
        ],
        out_specs=pl.BlockSpec((1, tq, dv), lambda bi, h, i: (bi, i, h)),
        out_shape=jax.ShapeDtypeStruct((b, sq, heads * dv), BF16),
        scratch_shapes=[pltpu.VMEM((rows, dq), BF16), pltpu.VMEM((t // tk, rows, tk), F32),
                        pltpu.VMEM((rows, LANES), F32), pltpu.VMEM((rows, LANES), F32),
                        pltpu.VMEM((rows, dv), F32)],
        compiler_params=_cparams(("parallel", "parallel", "arbitrary")),
    )(q, k, v, lam_vec.astype(F32), d_norm.reshape(1, dv).astype(F32))


def _pack_bf16_pairs(x):
    h = x.shape[1] // 2
    lo = lax.bitcast_convert_type(x[:, :h].astype(F32), jnp.uint32) >> 16
    hi = lax.bitcast_convert_type(x[:, h:].astype(F32), jnp.uint32) & jnp.uint32(0xFFFF0000)
    return hi | lo


def _unpack_bf16_pairs(w):
    lo = lax.bitcast_convert_type(w << 16, F32)
    hi = lax.bitcast_convert_type(w & jnp.uint32(0xFFFF0000), F32)
    return jnp.concatenate([lo.astype(BF16), hi.astype(BF16)], axis=1)


def _swiglu(x, wg, wu, wd):
    g = jnp.dot(x, wg, preferred_element_type=F32)
    u = jnp.dot(x, wu, preferred_element_type=F32)
    a = (g * jax.nn.sigmoid(g) * u).astype(BF16)
    return jnp.dot(a, wd, preferred_element_type=F32)


def _expert_kernel(be_ref, nu_ref, x_ref, wg_ref, wu_ref, wd_ref, o_ref, wg_s, wu_s, wd_s):
    i = pl.program_id(0)

    @pl.when(i < nu_ref[0])
    def _():
        @pl.when((i == 0) | (be_ref[i] != be_ref[jnp.maximum(i - 1, 0)]))
        def _():
            wg_s[...] = wg_ref[0, 0].astype(BF16)
            wu_s[...] = wu_ref[0, 0].astype(BF16)
            wd_s[...] = wd_ref[0, 0].astype(BF16)

        o_ref[...] = _swiglu(_unpack_bf16_pairs(x_ref[...]), wg_s[...], wu_s[...], wd_s[...])

    @pl.when(i >= nu_ref[0])
    def _():
        o_ref[...] = jnp.zeros(o_ref.shape, o_ref.dtype)


def _expert_ffn(x, block_e, n_used, wg, wu, wd, *, layer, tm):
    n, dh = x.shape
    d = 2 * dh
    ff = wg.shape[3]
    assert n % tm == 0

    def blk(i, be, nu):
        return jnp.minimum(i, nu[0] - 1)

    grid_spec = pltpu.PrefetchScalarGridSpec(
        num_scalar_prefetch=2,
        grid=(n // tm,),
        in_specs=[
            pl.BlockSpec((tm, dh), lambda i, be, nu: (blk(i, be, nu), 0)),
            pl.BlockSpec((1, 1, d, ff), lambda i, be, nu: (layer, be[blk(i, be, nu)], 0, 0)),
            pl.BlockSpec((1, 1, d, ff), lambda i, be, nu: (layer, be[blk(i, be, nu)], 0, 0)),
            pl.BlockSpec((1, 1, ff, d), lambda i, be, nu: (layer, be[blk(i, be, nu)], 0, 0)),
        ],
        out_specs=pl.BlockSpec((tm, d), lambda i, be, nu: (i, 0)),
        scratch_shapes=[pltpu.VMEM((d, ff), BF16), pltpu.VMEM((d, ff), BF16), pltpu.VMEM((ff, d), BF16)],
    )
    return pl.pallas_call(
        _expert_kernel,
        grid_spec=grid_spec,
        out_shape=jax.ShapeDtypeStruct((n, d), F32),
        compiler_params=_cparams(("arbitrary",)),
    )(block_e, n_used, x, wg, wu, wd)


def _post_mix_kernel(mix_ref, x_ref, w_ref, gp_ref, gf_ref, g1_ref, sh_ref, sc_ref, rw_ref,
                     xa_ref, f_ref, lg_ref):
    y = jnp.dot(mix_ref[0], w_ref[...], preferred_element_type=F32)
    yn = y * lax.rsqrt(jnp.mean(y * y, axis=-1, keepdims=True) + NORM_EPS) * gp_ref[...]
    xa = x_ref[0] + g1_ref[0] * yn
    xa_ref[0] = xa
    fn = xa * lax.rsqrt(jnp.mean(xa * xa, axis=-1, keepdims=True) + NORM_EPS) * gf_ref[...]
    f = (fn * (1.0 + sc_ref[0]) + sh_ref[0]).astype(BF16)
    f_ref[0] = _pack_bf16_pairs(f)
    lg_ref[...] = lax.dot_general(rw_ref[...], f, (((1,), (1,)), ((), ())), preferred_element_type=F32)


def _post_mix(mix, xs, w_out, g_post, g_ffn, g1, sh2, sc2, router_wt, *, row_tile0, seg_tiles):
    b, rows, k = mix.shape
    d = w_out.shape[1]
    e = router_wt.shape[0]
    tm = ROW_TILE
    nt = rows // tm
    mod_map = lambda bi, i: (bi * 2 + jnp.where(i + row_tile0 >= seg_tiles, 1, 0), 0, 0)
    vec = lambda: pl.BlockSpec((1, d), lambda bi, i: (0, 0))
    return pl.pallas_call(
        _post_mix_kernel,
        grid=(b, nt),
        in_specs=[
            pl.BlockSpec((1, tm, k), lambda bi, i: (bi, i, 0)),
            pl.BlockSpec((1, tm, d), lambda bi, i: (bi, i + row_tile0, 0)),
            pl.BlockSpec((k, d), lambda bi, i: (0, 0)),
            vec(), vec(),
            pl.BlockSpec((1, 1, d), mod_map), pl.BlockSpec((1, 1, d), mod_map),
            pl.BlockSpec((1, 1, d), mod_map),
            pl.BlockSpec((e, d), lambda bi, i: (0, 0)),
        ],
        out_specs=[
            pl.BlockSpec((1, tm, d), lambda bi, i: (bi, i, 0)),
            pl.BlockSpec((1, tm, d // 2), lambda bi, i: (bi, i, 0)),
            pl.BlockSpec((e, tm), lambda bi, i: (0, bi * nt + i)),
        ],
        out_shape=[jax.ShapeDtypeStruct((b, rows, d), F32),
                   jax.ShapeDtypeStruct((b, rows, d // 2), jnp.uint32),
                   jax.ShapeDtypeStruct((e, b * rows), F32)],
        compiler_params=_cparams(("parallel", "arbitrary")),
    )(mix, xs, w_out, g_post.reshape(1, d).astype(F32), g_ffn.reshape(1, d).astype(F32),
      g1, sh2, sc2, router_wt)


def _route_kernel(lg_ref, rb_ref, e_ref, g_ref):
    per = N_EXPERTS // N_GROUPS
    tn = lg_ref.shape[1]
    neg = -jnp.inf
    r_io = lax.broadcasted_iota(jnp.int32, (per, tn), 0)
    scores, choice, gs = [], [], []
    for g in range(N_GROUPS):
        sg = jax.nn.sigmoid(lg_ref[g * per:(g + 1) * per, :])
        cg = sg + rb_ref[g * per:(g + 1) * per, :]
        m1 = jnp.max(cg, axis=0, keepdims=True)
        i1 = jnp.min(jnp.where(cg == m1, r_io, per), axis=0, keepdims=True)
        m2 = jnp.max(jnp.where(r_io == i1, neg, cg), axis=0, keepdims=True)
        scores.append(sg)
        choice.append(cg)
        gs.append(m1 + m2)
    masked = []
    for g in range(N_GROUPS):
        ahead = jnp.zeros((1, tn), jnp.int32)
        for o in range(N_GROUPS):
            if o < g:
                ahead = ahead + jnp.where(gs[o] >= gs[g], 1, 0)
            elif o > g:
                ahead = ahead + jnp.where(gs[o] > gs[g], 1, 0)
        masked.append(jnp.where(ahead < TOPK_GROUPS, choice[g], neg))
    ids, gates = [], []
    for _ in range(TOP_K):
        best = masked[0]
        for g in range(1, N_GROUPS):
            best = jnp.maximum(best, masked[g])
        best = jnp.max(best, axis=0, keepdims=True)
        cand = jnp.where(masked[0] == best, r_io, N_EXPERTS)
        for g in range(1, N_GROUPS):
            cand = jnp.minimum(cand, jnp.where(masked[g] == best, r_io + g * per, N_EXPERTS))
        idx = jnp.min(cand, axis=0, keepdims=True)
        gk = jnp.zeros((per, tn), F32)
        for g in range(N_GROUPS):
            hit = (r_io + g * per) == idx
            gk = gk + jnp.where(hit, scores[g], 0.0)
            masked[g] = jnp.where(hit, neg, masked[g])
        ids.append(idx)
        gates.append(jnp.sum(gk, axis=0, keepdims=True))
    total = gates[0]
    for k in range(1, TOP_K):
        total = total + gates[k]
    pad = 8 - TOP_K
    e_ref[...] = jnp.concatenate(ids + [jnp.zeros((pad, tn), jnp.int32)], axis=0)
    g_ref[...] = jnp.concatenate([gk / total * ROUTED_SCALE for gk in gates]
                                 + [jnp.zeros((pad, tn), F32)], axis=0)


def _route(logits_t, router_b, *, tn):
    e, n = logits_t.shape
    assert n % tn == 0
    return pl.pallas_call(
        _route_kernel,
        grid=(n // tn,),
        in_specs=[pl.BlockSpec((e, tn), lambda i: (0, i)),
                  pl.BlockSpec((e, 1), lambda i: (0, 0))],
        out_specs=[pl.BlockSpec((8, tn), lambda i: (0, i)), pl.BlockSpec((8, tn), lambda i: (0, i))],
        out_shape=[jax.ShapeDtypeStruct((8, n), jnp.int32), jax.ShapeDtypeStruct((8, n), F32)],
        compiler_params=_cparams(("parallel",)),
    )(logits_t, router_b.reshape(e, 1).astype(F32))


def _dispatch_kernel(e_ref, dest_ref, be_ref, nu_ref, cnt_scr, start_scr, run_scr, *, blk):
    ph = pl.program_id(0)
    i = pl.program_id(1)
    tn = e_ref.shape[1]
    e_io = lax.broadcasted_iota(jnp.int32, (N_EXPERTS, tn), 0)
    hot = jnp.zeros((N_EXPERTS, tn), F32)
    for k in range(TOP_K):
        hot = hot + jnp.where(e_io == e_ref[k:k + 1, :], 1.0, 0.0)
    tile_cnt = jnp.sum(hot, axis=1, keepdims=True)

    @pl.when((ph == 0) & (i == 0))
    def _():
        cnt_scr[...] = jnp.zeros(cnt_scr.shape, F32)

    @pl.when(ph == 0)
    def _():
        cnt_scr[...] += jnp.broadcast_to(tile_cnt, cnt_scr.shape)

    @pl.when((ph == 1) & (i == 0))
    def _():
        bpe = jnp.floor((cnt_scr[...] + (blk - 1.0)) * (1.0 / blk))
        r = lax.broadcasted_iota(jnp.int32, (N_EXPERTS, N_EXPERTS), 0)
        c = lax.broadcasted_iota(jnp.int32, (N_EXPERTS, N_EXPERTS), 1)
        lower = jnp.where(c < r, 1.0, 0.0)
        before = jnp.dot(lower, bpe, precision=lax.Precision.HIGHEST, preferred_element_type=F32)
        start_scr[...] = before * blk
        run_scr[...] = jnp.zeros(run_scr.shape, F32)
        ends = (before + bpe)[:, 0:1]
        nb = be_ref.shape[1]
        bid = lax.broadcasted_iota(jnp.int32, (N_EXPERTS, nb), 1).astype(F32)
        be = jnp.sum(jnp.where(ends <= bid, 1, 0), axis=0, keepdims=True)
        be_ref[...] = jnp.minimum(be, N_EXPERTS - 1).astype(jnp.int32)
        nu_ref[...] = jnp.broadcast_to(jnp.max(ends, axis=0, keepdims=True), nu_ref.shape).astype(jnp.int32)

    @pl.when(ph == 1)
    def _():
        rr = lax.broadcasted_iota(jnp.int32, (tn, tn), 0)
        cc = lax.broadcasted_iota(jnp.int32, (tn, tn), 1)
        upper = jnp.where(rr < cc, 1.0, 0.0).astype(BF16)
        prior = jnp.dot(hot.astype(BF16), upper, preferred_element_type=F32)
        pos = prior + jnp.concatenate([start_scr[...] + run_scr[...]] * (tn // LANES), axis=1)
        rows = []
        for k in range(TOP_K):
            rows.append(jnp.sum(jnp.where(e_io == e_ref[k:k + 1, :], pos, 0.0), axis=0, keepdims=True))
        rows.append(jnp.zeros((8 - TOP_K, tn), F32))
        dest_ref[0] = jnp.concatenate(rows, axis=0).astype(jnp.int32)
        run_scr[...] += jnp.broadcast_to(tile_cnt, run_scr.shape)


def _dispatch(top_e, *, blk, n_blocks):
    n = top_e.shape[1]
    tn = LANES
    nbp = -(-n_blocks // LANES) * LANES
    kern = functools.partial(_dispatch_kernel, blk=blk)
    return pl.pallas_call(
        kern,
        grid=(2, n // tn),
        in_specs=[pl.BlockSpec((8, tn), lambda ph, i: (0, i))],
        out_specs=[pl.BlockSpec((1, 8, tn), lambda ph, i: (i * ph, 0, 0)),
                   pl.BlockSpec((1, nbp), lambda ph, i: (0, 0)),
                   pl.BlockSpec((8, LANES), lambda ph, i: (0, 0))],
        out_shape=[jax.ShapeDtypeStruct((n // tn, 8, tn), jnp.int32),
                   jax.ShapeDtypeStruct((1, nbp), jnp.int32),
                   jax.ShapeDtypeStruct((8, LANES), jnp.int32)],
        scratch_shapes=[pltpu.VMEM((N_EXPERTS, LANES), F32)] * 3,
        compiler_params=_cparams(("arbitrary", "arbitrary")),
    )(top_e)


def _scatter_rows_kernel(dest_ref, x_ref, zero_ref, o_ref, sem):
    del zero_ref
    tm = x_ref.shape[0]

    def row_copy(t, k):
        return pltpu.make_async_copy(x_ref.at[pl.ds(t, 1)], o_ref.at[pl.ds(dest_ref[0, k, t], 1)], sem)

    def issue(t, carry):
        for k in range(TOP_K):
            row_copy(t, k).start()
        return carry

    def drain(t, carry):
        for k in range(TOP_K):
            row_copy(t, k).wait()
        return carry

    lax.fori_loop(0, tm, issue, 0)
    lax.fori_loop(0, tm, drain, 0)


def _scatter_rows(dest, x, n_rows):
    n, dh = x.shape
    tm = LANES
    zeros = jnp.zeros((n_rows, dh), x.dtype)
    return pl.pallas_call(
        _scatter_rows_kernel,
        grid=(n // tm,),
        in_specs=[pl.BlockSpec((1, 8, tm), lambda i: (i, 0, 0), memory_space=pltpu.SMEM),
                  pl.BlockSpec((tm, dh), lambda i: (i, 0)),
                  pl.BlockSpec(memory_space=pl.ANY)],
        out_specs=pl.BlockSpec(memory_space=pl.ANY),
        out_shape=jax.ShapeDtypeStruct((n_rows, dh), x.dtype),
        scratch_shapes=[pltpu.SemaphoreType.DMA(())],
        input_output_aliases={2: 0},
        compiler_params=_cparams(("arbitrary",)),
    )(dest, x, zeros)


def _combine_kernel(dest_ref, y_ref, gate_ref, f_ref, xa_ref, sg_ref, su_ref, sd_ref, gp_ref, g2_ref,
                    o_ref, buf, sem):
    tm = f_ref.shape[0]

    def row_copy(t, k):
        return pltpu.make_async_copy(y_ref.at[pl.ds(dest_ref[0, k, t], 1)], buf.at[k, pl.ds(t, 1)], sem)

    def issue(t, carry):
        for k in range(TOP_K):
            row_copy(t, k).start()
        return carry

    def drain(t, carry):
        for k in range(TOP_K):
            row_copy(t, k).wait()
        return carry

    lax.fori_loop(0, tm, issue, 0)
    f = _swiglu(_unpack_bf16_pairs(f_ref[...]), sg_ref[...], su_ref[...], sd_ref[...])
    lax.fori_loop(0, tm, drain, 0)
    gate = gate_ref[...]
    for k in range(TOP_K):
        f = f + gate[:, k:k + 1] * buf[k]
    fn = f * lax.rsqrt(jnp.mean(f * f, axis=-1, keepdims=True) + NORM_EPS) * gp_ref[...]
    o_ref[0] = xa_ref[0] + g2_ref[0] * fn


def _combine(dest, y, gate, f_pk, xa, sg, su, sd, g_post, g2, *, row_tile0, seg_tiles):
    b, rows, d = xa.shape
    tm = LANES
    nt = rows // tm
    per = ROW_TILE // tm
    mod_map = lambda bi, i: (bi * 2 + jnp.where(i // per + row_tile0 >= seg_tiles, 1, 0), 0, 0)
    const = lambda shape: pl.BlockSpec(shape, lambda bi, i: (0,) * len(shape))
    return pl.pallas_call(
        _combine_kernel,
        grid=(b, nt),
        in_specs=[
            pl.BlockSpec((1, 8, tm), lambda bi, i: (bi * nt + i, 0, 0), memory_space=pltpu.SMEM),
            pl.BlockSpec(memory_space=pl.ANY),
            pl.BlockSpec((tm, 8), lambda bi, i: (bi * nt + i, 0)),
            pl.BlockSpec((tm, d // 2), lambda bi, i: (bi * nt + i, 0)),
            pl.BlockSpec((1, tm, d), lambda bi, i: (bi, i, 0)),
            const(sg.shape), const(su.shape), const(sd.shape), const((1, d)),
            pl.BlockSpec((1, 1, d), mod_map),
        ],
        out_specs=pl.BlockSpec((1, tm, d), lambda bi, i: (bi, i, 0)),
        out_shape=jax.ShapeDtypeStruct((b, rows, d), F32),
        scratch_shapes=[pltpu.VMEM((TOP_K, tm, d), F32), pltpu.SemaphoreType.DMA(())],
        compiler_params=_cparams(("arbitrary", "arbitrary")),
    )(dest, y, gate, f_pk, xa, sg, su, sd, g_post.reshape(1, d).astype(F32), g2)


def _rope_tables(s, ctx_len, dim):
    rows = s // GRID_W
    row = jnp.repeat(jnp.arange(rows), GRID_W)
    col = jnp.tile(jnp.arange(GRID_W), rows)
    quarter = dim // 4
    inv = ROPE_BASE ** (-jnp.arange(quarter, dtype=F32) / quarter)
    ang = jnp.concatenate([row.astype(F32)[:, None] * inv, col.astype(F32)[:, None] * inv], axis=-1)
    cos = jnp.repeat(jnp.cos(ang), 2, axis=-1)
    sin = jnp.repeat(jnp.sin(ang), 2, axis=-1) * jnp.tile(jnp.array([-1.0, 1.0], F32), dim // 2)
    cos = jnp.concatenate([jnp.ones((ctx_len, dim), F32), cos], axis=0)
    sin = jnp.concatenate([jnp.zeros((ctx_len, dim), F32), sin], axis=0)
    return cos, sin


def _log_sigmoid(x):
    return jnp.minimum(x, 0.0) - jnp.log1p(jnp.exp(-jnp.abs(x)))


def _mlstm_kernel(q_ref, k_ref, kt_ref, v_ref, g_ref, gt_ref, gb_ref, gbt_ref, hf_ref, mo_ref, mn_ref,
                  o_ref, s_scr, m_scr, *, reverse, final):
    L = MLSTM_CHUNK
    hv = MLSTM_V
    io, fo = (2 * MLSTM_HEADS, 3 * MLSTM_HEADS) if reverse else (0, MLSTM_HEADS)

    @pl.when(pl.program_id(1) == 0)
    def _():
        s_scr[...] = jnp.zeros(s_scr.shape, F32)
        m_scr[...] = jnp.zeros(m_scr.shape, F32)

    g = g_ref[0] + gb_ref[...]
    gt = gt_ref[0] + gbt_ref[...]
    r_io = lax.broadcasted_iota(jnp.int32, (L, L), 0)
    c_io = lax.broadcasted_iota(jnp.int32, (L, L), 1)
    seen = (c_io >= r_io) if reverse else (c_io <= r_io)
    tri = jnp.where(seen, 1.0, 0.0)
    hi = lax.Precision.HIGHEST
    bc_col = jnp.dot(tri, _log_sigmoid(g), precision=hi, preferred_element_type=F32)
    lf_row = _log_sigmoid(gt)
    bc_row = lax.dot_general(lf_row, tri, (((1,), (1,)), ((), ())), precision=hi,
                             preferred_element_type=F32)
    lane = lax.broadcasted_iota(jnp.int32, (L, LANES), 1)
    sub = lax.broadcasted_iota(jnp.int32, (LANES, L), 0)

    for h in range(MLSTM_HEADS):
        pair, odd = h // 2, h % 2
        lo = odd * MLSTM_QK
        a_col = bc_col[:, fo + h:fo + h + 1]
        i_col = g[:, io + h:io + h + 1]
        b_row = bc_row[fo + h:fo + h + 1, :]
        i_row = gt[io + h:io + h + 1, :]
        btot = jnp.sum(lf_row[fo + h:fo + h + 1, :], axis=1, keepdims=True)
        m_st = m_scr[h:h + 1, 0:1]

        w_end = btot - a_col + i_col
        m_new = jnp.maximum(btot + m_st, jnp.max(w_end, axis=0, keepdims=True))
        decay = jnp.exp(btot + m_st - m_new)
        w_k = jnp.exp(w_end - m_new)

        log_d = jnp.where(seen, a_col - b_row + i_row, -jnp.inf)
        log_inter = a_col + m_st
        m_row = jnp.maximum(log_inter, jnp.max(log_d, axis=1, keepdims=True))
        w_intra = jnp.exp(log_d - m_row)
        w_inter = jnp.exp(log_inter - m_row)

        in_head = (lane >= lo) & (lane < lo + MLSTM_QK)
        qm = jnp.where(in_head, q_ref[0, :, pair * LANES:(pair + 1) * LANES], 0).astype(BF16)
        kp = k_ref[0, :, pair * LANES:(pair + 1) * LANES]
        v = v_ref[0, :, h * hv:(h + 1) * hv]
        state = s_scr[h]

        qk = lax.dot_general(qm, kp, (((1,), (1,)), ((), ())), preferred_element_type=F32)
        qk = qk * (MLSTM_QK ** -0.5) * w_intra
        inter = jnp.dot(qm, state.astype(BF16), preferred_element_type=F32)
        num = w_inter * inter[:, :hv] + jnp.dot(qk.astype(BF16), v, preferred_element_type=F32)
        den = w_inter * inter[:, hv:hv + 1] + jnp.sum(qk, axis=1, keepdims=True)
        out = num / jnp.maximum(jnp.abs(den), jnp.exp(-m_row))

        wv = jnp.concatenate([(w_k * v.astype(F32)).astype(BF16),
                              jnp.where(lane == 0, w_k, 0.0).astype(BF16)], axis=1)
        kt = kt_ref[0, pair * LANES:(pair + 1) * LANES, :]
        in_rows = (sub >= lo) & (sub < lo + MLSTM_QK)
        ktm = (jnp.where(in_rows, kt, 0).astype(F32) * (MLSTM_QK ** -0.5)).astype(BF16)
        s_scr[h] = decay * state + jnp.dot(ktm, wv, preferred_element_type=F32)
        m_scr[h:h + 1, :] = jnp.broadcast_to(m_new, (1, LANES))

        if final:
            tot = out + hf_ref[0, :, h * hv:(h + 1) * hv]
            nrm = tot * lax.rsqrt(jnp.mean(tot * tot, axis=-1, keepdims=True) + NORM_EPS) * mn_ref[...]
            gate = jax.nn.sigmoid(mo_ref[0, :, h * hv:(h + 1) * hv].astype(F32))
            o_ref[0, :, h * hv:(h + 1) * hv] = (nrm * gate).astype(o_ref.dtype)
        else:
            o_ref[0, :, h * hv:(h + 1) * hv] = out.astype(o_ref.dtype)


def _mlstm(pv, kt, gates, gates_t, gate_b, m_norm, hf, *, ctx_chunks, reverse):
    b, t, _ = pv.shape
    L = MLSTM_CHUNK
    nc = t // L
    final = hf is not None
    nq = MLSTM_HEADS * MLSTM_QK
    nv = MLSTM_HEADS * MLSTM_V

    def chunk(j):
        if not reverse:
            return j
        return jnp.where(j < ctx_chunks, ctx_chunks - 1 - j, nc - 1 - (j - ctx_chunks))

    gb = jnp.pad(gate_b.astype(F32).reshape(-1), (0, LANES - gate_b.size))
    if hf is None:
        hf = jnp.zeros((1, L, nv), F32)
        hf_spec = pl.BlockSpec((1, L, nv), lambda bi, j: (0, 0, 0))
    else:
        hf_spec = pl.BlockSpec((1, L, nv), lambda bi, j: (bi, chunk(j), 0))
    kern = functools.partial(_mlstm_kernel, reverse=reverse, final=final)
    return pl.pallas_call(
        kern,
        grid=(b, nc),
        in_specs=[
            pl.BlockSpec((1, L, nq), lambda bi, j: (bi, chunk(j), nv // nq)),
            pl.BlockSpec((1, L, nq), lambda bi, j: (bi, chunk(j), nv // nq + 1)),
            pl.BlockSpec((1, nq, L), lambda bi, j: (bi, 0, chunk(j))),
            pl.BlockSpec((1, L, nv), lambda bi, j: (bi, chunk(j), 2)),
            pl.BlockSpec((1, L, LANES), lambda bi, j: (bi, chunk(j), 0)),
            pl.BlockSpec((1, LANES, L), lambda bi, j: (bi, 0, chunk(j))),
            pl.BlockSpec((1, LANES), lambda bi, j: (0, 0)),
            pl.BlockSpec((LANES, 1), lambda bi, j: (0, 0)),
            hf_spec,
            pl.BlockSpec((1, L, nv), lambda bi, j: (bi, chunk(j), 3)),
            pl.BlockSpec((1, MLSTM_V), lambda bi, j: (0, 0)),
        ],
        out_specs=pl.BlockSpec((1, L, nv), lambda bi, j: (bi, chunk(j), 0)),
        out_shape=jax.ShapeDtypeStruct((b, t, nv), BF16 if final else F32),
        scratch_shapes=[pltpu.VMEM((MLSTM_HEADS, LANES, 2 * LANES), F32), pltpu.VMEM((MLSTM_HEADS, LANES), F32)],
        compiler_params=_cparams(("parallel", "arbitrary")),
    )(pv, pv, kt, pv, gates, gates_t, gb.reshape(1, LANES), gb.reshape(LANES, 1), hf, pv,
      m_norm.reshape(1, MLSTM_V).astype(F32))


def _swa_kernel(sink_ref, q_ref, kp_ref, kc_ref, kn_ref, kx_ref, vp_ref, vc_ref, vn_ref, vx_ref, o_ref, *,
                n_blocks):
    i = pl.program_id(1)
    bq = BLOCK_Q
    hd = SWA_HD
    rep = SWA_HEADS // SWA_KV_HEADS
    n_ctx = kx_ref.shape[1]
    rows = rep * bq
    scale = hd ** -0.5 * LOG2E
    row = lax.broadcasted_iota(jnp.int32, (rows, 3 * bq), 0)
    col = lax.broadcasted_iota(jnp.int32, (rows, 3 * bq), 1)
    dt = (col - bq) - (row % bq)
    blk = col // bq
    ok = (jnp.abs(dt) <= WINDOW) & ((blk != 0) | (i > 0)) & ((blk != 2) | (i < n_blocks - 1))
    head_of_row = lax.broadcasted_iota(jnp.int32, (rows, 1), 0) // bq
    for g in range(SWA_KV_HEADS):
        q = jnp.concatenate([q_ref[0, :, (g * rep + r) * hd:(g * rep + r + 1) * hd] for r in range(rep)], axis=0)
        q = (q.astype(F32) * scale).astype(BF16)
        cs = slice(g * hd, (g + 1) * hd)
        k_loc = jnp.concatenate([kp_ref[0, :, cs], kc_ref[0, :, cs], kn_ref[0, :, cs]], axis=0)
        v_loc = jnp.concatenate([vp_ref[0, :, cs], vc_ref[0, :, cs], vn_ref[0, :, cs]], axis=0)
        nt = (((1,), (1,)), ((), ()))
        s_loc = jnp.where(ok, lax.dot_general(q, k_loc, nt, preferred_element_type=F32), -jnp.inf)
        s_ctx = lax.dot_general(q, kx_ref[0, :, cs], nt, preferred_element_type=F32)
        sink = jnp.zeros((rows, 1), F32)
        for r in range(rep):
            sink = jnp.where(head_of_row == r, sink_ref[g * rep + r] * LOG2E, sink)
        m = jnp.maximum(jnp.maximum(jnp.max(s_loc, axis=-1, keepdims=True),
                                    jnp.max(s_ctx, axis=-1, keepdims=True)), sink)
        p_loc = jnp.exp2(s_loc - m)
        p_ctx = jnp.exp2(s_ctx - m)
        den = (jnp.sum(p_loc, axis=-1, keepdims=True) + jnp.sum(p_ctx, axis=-1, keepdims=True)
               + jnp.exp2(sink - m))
        out = (jnp.dot(p_loc.astype(BF16), v_loc, preferred_element_type=F32)
               + jnp.dot(p_ctx.astype(BF16), vx_ref[0, :, cs], preferred_element_type=F32)) / den
        for r in range(rep):
            o_ref[0, :, (g * rep + r) * hd:(g * rep + r + 1) * hd] = out[r * bq:(r + 1) * bq].astype(o_ref.dtype)


def _swa(qk, v, sink, *, n_ctx, k_col0):
    b, t, _ = qk.shape
    bq = BLOCK_Q
    cb = n_ctx // bq
    nb = (t - n_ctx) // bq
    nq = SWA_HEADS * SWA_HD
    nk = SWA_KV_HEADS * SWA_HD
    kcol = k_col0
    prev = lambda bi, i, c: (bi, i + cb - 1, c)
    cur = lambda bi, i, c: (bi, i + cb, c)
    nxt = lambda bi, i, c: (bi, jnp.minimum(i + cb + 1, nb + cb - 1), c)
    kern = functools.partial(_swa_kernel, n_blocks=nb)

    def spec(rows, fn, c):
        return pl.BlockSpec((1, rows, nk), lambda bi, i: fn(bi, i, c))

    ctx = lambda bi, i, c: (bi, 0, c)
    return pl.pallas_call(
        kern,
        grid=(b, nb),
        in_specs=[
            pl.BlockSpec(memory_space=pltpu.SMEM),
            pl.BlockSpec((1, bq, nq), lambda bi, i: (bi, i + cb, 0)),
            spec(bq, prev, kcol), spec(bq, cur, kcol), spec(bq, nxt, kcol), spec(n_ctx, ctx, kcol),
            spec(bq, prev, 0), spec(bq, cur, 0), spec(bq, nxt, 0), spec(n_ctx, ctx, 0),
        ],
        out_specs=pl.BlockSpec((1, bq, nq), lambda bi, i: (bi, i, 0)),
        out_shape=jax.ShapeDtypeStruct((b, nb * bq, nq), BF16),
        compiler_params=_cparams(("parallel", "arbitrary")),
    )(sink.astype(F32), qk, qk, qk, qk, qk, v, v, v, v)


def _mix_out_and_moe(mix, xs, w_out, g_post, g_ffn_pre, g_ffn_post, mods, router_w, router_b,
                     wg, wu, wd, sg, su, sd, *, layer, row_tile0, seg_tiles):
    sh1, sc1, g1, sh2, sc2, g2 = mods
    b, rows, _ = mix.shape
    n = b * rows
    seg = dict(row_tile0=row_tile0, seg_tiles=seg_tiles)
    xa, f_pk, logits_t = _post_mix(mix, xs, w_out.astype(BF16), g_post, g_ffn_pre, g1, sh2, sc2,
                                   router_w.T.astype(BF16), **seg)
    top_e, gate = _route(logits_t, router_b, tn=512)
    blk = EXPERT_ROWS
    n_blocks = -(-n * TOP_K // blk) + N_EXPERTS
    dest, block_e, n_used = _dispatch(top_e, blk=blk, n_blocks=n_blocks)
    f_pk = f_pk.reshape(n, -1)
    x_sorted = _scatter_rows(dest, f_pk, n_blocks * blk)
    y = _expert_ffn(x_sorted, block_e[0, :n_blocks], n_used[0, :1], wg, wu, wd, layer=layer, tm=blk)
    return _combine(dest, y, gate.T, f_pk, xa, sg.astype(BF16), su.astype(BF16), sd.astype(BF16),
                    g_ffn_post, g2, **seg)


def kernel(x, c, ctx, c_ctx, mod_w, mod_b, norm_mix_pre, norm_mix_post, norm_ffn_pre, norm_ffn_post, ab_w_in, ab_w_out, diff_lambda, diff_norm, mlstm_gate_b, mlstm_norm, cd_w_in, cd_w_out, mla_q_norm, mla_w_uq, mla_kv_norm, mla_w_ukv, swa_sink, router_w, router_b, exp_w_gate, exp_w_up, exp_w_down, sh_w_gate, sh_w_up, sh_w_down):
    b, s, d = x.shape
    L = ctx.shape[1]
    t = L + s
    assert L == ROW_TILE and s % ROW_TILE == 0
    depth = mod_w.shape[0]
    ctx_tiles = L // ROW_TILE
    tk = next(c for c in (2816, 768, ROW_TILE) if t % c == 0)

    cos64, sin64 = _rope_tables(s, L, DIFF_HD)
    cos64 = jnp.tile(cos64, (1, LANES // DIFF_HD))
    sin64 = jnp.tile(sin64, (1, LANES // DIFF_HD))
    cos128, sin128 = _rope_tables(s, L, SWA_HD)
    cos_kr = jnp.concatenate([cos64[:, :MLA_ROPE], jnp.ones((t, LANES - MLA_ROPE), F32)], axis=1)
    sin_kr = jnp.concatenate([sin64[:, :MLA_ROPE], jnp.zeros((t, LANES - MLA_ROPE), F32)], axis=1)

    xs = jnp.concatenate([ctx, x], axis=1)

    for layer in range(depth):
        with_ctx = layer < depth - 1
        j = layer // 2
        mod_l = jax.nn.silu(c) @ mod_w[layer] + mod_b[layer]
        mod_c = jax.nn.silu(c_ctx) @ mod_w[layer] + mod_b[layer]
        mods = jnp.stack([jnp.broadcast_to(mod_c, (b, 6 * d)), mod_l], axis=1).reshape(b * 2, 1, 6, d)
        sh1, sc1, g1, sh2, sc2, g2 = (mods[:, :, m] for m in range(6))

        proj = functools.partial(_norm_proj, xs, norm_mix_pre[layer], sh1, sc1, seg_tiles=ctx_tiles)
        if layer % 2 == 0:
            w_in = ab_w_in[j].astype(BF16)
            n_qk = 2 * DIFF_HEADS * 2 * DIFF_HD
            qk = proj(w_in[:, :n_qk], tn=1024, rope=(cos64[None], sin64[None]), pattern=(0,) * 8)
            pv = proj(w_in[:, n_qk:n_qk + AB_PLAIN], tn=1024)
            w_gates = jnp.pad(w_in[:, n_qk + AB_PLAIN:], ((0, 0), (0, LANES - 4 * MLSTM_HEADS)))
            gates = proj(w_gates, tn=LANES, out_dtype=F32)

            lam_init = 0.8 - 0.6 * math.exp(-0.3 * layer)
            a = _attention(qk if with_ctx else qk[:, L:], qk, pv, heads=DIFF_HEADS, dq=2 * DIFF_HD,
                           dv=DIFF_VD, q_col0=0, k_col0=DIFF_HEADS, v_col0=0, tq=ROW_TILE,
                           scale=DIFF_HD ** -0.5, tk=tk, diff=True, lam_vec=diff_lambda[j],
                           d_norm=diff_norm[j], lam_init=lam_init,
                           ctx_tiles=ctx_tiles if with_ctx else 0, ctx_len=L)
            o_mk = DIFF_HEADS * DIFF_VD + MLSTM_HEADS * MLSTM_QK
            kt = jnp.swapaxes(pv[..., o_mk:o_mk + MLSTM_HEADS * MLSTM_QK], 1, 2)
            scan = functools.partial(_mlstm, pv, kt, gates, jnp.swapaxes(gates, 1, 2), mlstm_gate_b[j],
                                     mlstm_norm[j], ctx_chunks=L // MLSTM_CHUNK)
            m = scan(scan(None, reverse=False), reverse=True)
            mix = jnp.concatenate([a, m if with_ctx else m[:, L:]], axis=-1)
            w_out = ab_w_out[j]
        else:
            assert not with_ctx
            w_in = cd_w_in[j].astype(BF16)
            c0 = Q_LORA
            c1 = c0 + KV_LORA
            c2 = c1 + MLA_ROPE
            c3 = c2 + SWA_HEADS * SWA_HD
            c4 = c3 + SWA_KV_HEADS * SWA_HD
            w_rope = jnp.concatenate([w_in[:, c2:c4], w_in[:, c1:c2],
                                      jnp.zeros((d, LANES - MLA_ROPE), BF16)], axis=1)
            n_rope = w_rope.shape[1]
            rp = proj(w_rope, tn=n_rope, rope=(jnp.stack([cos128, cos_kr]), jnp.stack([sin128, sin_kr])),
                      pattern=(0,) * (n_rope // LANES - 1) + (1,))
            q_pad = 512 - Q_LORA
            w_plain = jnp.concatenate([w_in[:, c4:], w_in[:, :c0], jnp.zeros((d, q_pad), BF16),
                                       w_in[:, c0:c1]], axis=1)
            pp = proj(w_plain, tn=w_plain.shape[1])
            n_sv = SWA_KV_HEADS * SWA_HD
            cq = pp[..., n_sv:n_sv + 512]
            ckv = pp[..., n_sv + 512:]

            hq = MLA_NOPE + MLA_ROPE
            w_uq = mla_w_uq[j].astype(BF16).reshape(Q_LORA, MLA_HEADS, hq)
            w_uq = jnp.pad(w_uq, ((0, q_pad), (0, 0), (0, 256 - hq))).reshape(512, MLA_HEADS * 256)
            no_mod = jnp.zeros((b * 2, 1, 512), F32)
            qn = jnp.pad(mla_q_norm[j], (0, q_pad))
            q_mla = _norm_proj(cq[:, L:], qn, no_mod, no_mod, w_uq, seg_tiles=0, tn=1024, n_valid=Q_LORA,
                               rope=(cos_kr[None, L:], sin_kr[None, L:]), pattern=(-1, 0) * 4)
            kv = _norm_proj(ckv, mla_kv_norm[j], no_mod, no_mod, mla_w_ukv[j].astype(BF16),
                            seg_tiles=ctx_tiles, tn=1024).reshape(b, t, MLA_HEADS, MLA_NOPE + MLA_V)
            kr = rp[..., n_rope - LANES:]
            k_mla = jnp.concatenate([kv[..., :MLA_NOPE],
                                     jnp.broadcast_to(kr[:, :, None, :], (b, t, MLA_HEADS, LANES))],
                                    axis=-1).reshape(b, t, MLA_HEADS * 256)
            v_mla = kv[..., MLA_NOPE:].reshape(b, t, MLA_HEADS * MLA_V)
            a = _attention(q_mla, k_mla, v_mla, heads=MLA_HEADS, dq=256, dv=MLA_V, q_col0=0, k_col0=0,
                           v_col0=0, tq=512, scale=MLA_SCALE, tk=tk)

            w = _swa(rp, pp, swa_sink[j], n_ctx=L, k_col0=SWA_HEADS * SWA_HD // n_sv)
            mix = jnp.concatenate([a, w], axis=-1)
            w_out = cd_w_out[j]

        xa = _mix_out_and_moe(mix, xs, w_out, norm_mix_post[layer], norm_ffn_pre[layer], norm_ffn_post[layer],
                              (sh1, sc1, g1, sh2, sc2, g2), router_w[layer], router_b[layer],
                              exp_w_gate, exp_w_up, exp_w_down,
                              sh_w_gate[layer], sh_w_up[layer], sh_w_down[layer],
                              layer=layer, row_tile0=0 if with_ctx else ctx_tiles, seg_tiles=ctx_tiles)
        xs = xa if with_ctx else jnp.concatenate([xs[:, :L], xa], axis=1)
    return xs[:, L:]
```

```python
import functools
import math

import jax
import jax.numpy as jnp
import numpy as np
from jax import lax
from jax.experimental import pallas as pl
from jax.experimental.pallas import tpu as pltpu

F32 = jnp.float32
BF16 = jnp.bfloat16

GRID_W = 64
ROPE_BASE = 10000.0
NORM_EPS = 1e-6

DIFF_HEADS = 8
DIFF_HD = 64
DIFF_VD = 2 * DIFF_HD
MLSTM_HEADS = 8
MLSTM_QK = 64
MLSTM_V = 128
MLSTM_CHUNK = 128
MLA_HEADS = 8
MLA_NOPE = 128
MLA_ROPE = 64
MLA_V = 128
Q_LORA = 448
KV_LORA = 512
MLA_SCALE = (MLA_NOPE + MLA_ROPE) ** -0.5
SWA_HEADS = 8
SWA_KV_HEADS = 2
SWA_HD = 128
WINDOW = 128
BLOCK_Q = 128
N_EXPERTS = 64
TOP_K = 6
N_GROUPS = 8
TOPK_GROUPS = 4
ROUTED_SCALE = 2.5
AB_PLAIN = DIFF_HEADS * DIFF_VD + 2 * MLSTM_HEADS * MLSTM_QK + 2 * MLSTM_HEADS * MLSTM_V

LANES = 128
LOG2E = 1.4426950408889634
VMEM_LIMIT = 56 * 1024 * 1024

ROW_TILE = 256
EXPERT_ROWS = 512


def _cparams(sem):
    return pltpu.CompilerParams(dimension_semantics=sem, vmem_limit_bytes=VMEM_LIMIT)


def _pair_swap(a):
    lane = lax.broadcasted_iota(jnp.int32, a.shape, 1)
    return jnp.where(lane % 2 == 0, pltpu.roll(a, LANES - 1, 1), pltpu.roll(a, 1, 1))


def _norm_proj_kernel(x_ref, g_ref, sh_ref, sc_ref, w_ref, cos_ref, sin_ref, o_ref, h_scr, *,
                      n_valid, pattern):
    j = pl.program_id(2)

    @pl.when(j == 0)
    def _():
        x = x_ref[0].astype(F32)
        ms = jnp.sum(x * x, axis=-1, keepdims=True) * (1.0 / n_valid)
        y = x * lax.rsqrt(ms + NORM_EPS) * g_ref[...]
        h_scr[...] = (y * (1.0 + sc_ref[0]) + sh_ref[0]).astype(BF16)

    acc = jnp.dot(h_scr[...], w_ref[...], preferred_element_type=F32)
    if pattern is None:
        o_ref[0] = acc.astype(o_ref.dtype)
    else:
        for c, tbl in enumerate(pattern):
            a = acc[:, c * LANES:(c + 1) * LANES]
            if tbl >= 0:
                a = a * cos_ref[tbl] + _pair_swap(a) * sin_ref[tbl]
            o_ref[0, :, c * LANES:(c + 1) * LANES] = a.astype(o_ref.dtype)


def _norm_proj(x, g, shift, scale, w, *, seg_tiles, tn, out_dtype=BF16, n_valid=None,
               rope=None, pattern=None):
    b, t, k = x.shape
    n = w.shape[1]
    tm = ROW_TILE
    assert t % tm == 0 and n % tn == 0 and tn % LANES == 0
    if rope is None:
        cos = sin = jnp.zeros((1, tm, LANES), F32)
        tbl_map = lambda bi, i, j: (0, 0, 0)
    else:
        cos, sin = rope
        tbl_map = lambda bi, i, j: (0, i, 0)
    ntab = cos.shape[0]
    kern = functools.partial(_norm_proj_kernel, n_valid=float(n_valid or k), pattern=pattern)
    mod_map = lambda bi, i, j: (bi * 2 + jnp.where(i >= seg_tiles, 1, 0), 0, 0)
    return pl.pallas_call(
        kern,
        grid=(b, t // tm, n // tn),
        in_specs=[
            pl.BlockSpec((1, tm, k), lambda bi, i, j: (bi, i, 0)),
            pl.BlockSpec((1, k), lambda bi, i, j: (0, 0)),
            pl.BlockSpec((1, 1, k), mod_map),
            pl.BlockSpec((1, 1, k), mod_map),
            pl.BlockSpec((k, tn), lambda bi, i, j: (0, j)),
            pl.BlockSpec((ntab, tm, LANES), tbl_map),
            pl.BlockSpec((ntab, tm, LANES), tbl_map),
        ],
        out_specs=pl.BlockSpec((1, tm, tn), lambda bi, i, j: (bi, i, j)),
        out_shape=jax.ShapeDtypeStruct((b, t, n), out_dtype),
        scratch_shapes=[pltpu.VMEM((tm, k), BF16)],
        compiler_params=_cparams(("parallel", "parallel", "arbitrary")),
    )(x, g.reshape(1, k).astype(F32), shift, scale, w, cos, sin)


def _attn_kernel(q_ref, k_ref, v_ref, lam_ref, dn_ref, o_ref, q_scr, s_scr, m_scr, l_scr, acc_scr, *,
                 scale, diff, lam_init, ctx_tiles, ctx_len, tq, tk, t):
    i = pl.program_id(2)
    q = q_ref[0].astype(F32) * (scale * LOG2E)
    if diff:
        lane = lax.broadcasted_iota(jnp.int32, q.shape, 1)
        half = q.shape[1] // 2
        q_scr[0:tq] = jnp.where(lane < half, q, 0.0).astype(BF16)
        q_scr[tq:2 * tq] = jnp.where(lane >= half, q, 0.0).astype(BF16)
    else:
        q_scr[...] = q.astype(BF16)

    def attend(kv_len, chunk):
        n_chunks = kv_len // chunk
        groups = chunk // LANES

        def pass1(c, carry):
            k = k_ref[0, pl.ds(pl.multiple_of(c * chunk, chunk), chunk), :]
            s = lax.dot_general(q_scr[...], k, (((1,), (1,)), ((), ())), preferred_element_type=F32)
            s_scr[c, :, 0:chunk] = s
            m = m_scr[...]
            for g in range(groups):
                m = jnp.maximum(m, s[:, g * LANES:(g + 1) * LANES])
            m_scr[...] = m
            return carry

        def pass2(c, carry):
            s = s_scr[c, :, 0:chunk]
            p = jnp.exp2(s - jnp.concatenate([m_scr[...]] * groups, axis=1))
            l = l_scr[...]
            for g in range(groups):
                l = l + p[:, g * LANES:(g + 1) * LANES]
            l_scr[...] = l
            v = v_ref[0, pl.ds(pl.multiple_of(c * chunk, chunk), chunk), :]
            acc_scr[...] += jnp.dot(p.astype(BF16), v, preferred_element_type=F32)
            return carry

        m_scr[...] = jnp.full(m_scr.shape, -jnp.inf, F32)
        if n_chunks == 1:
            pass1(0, 0)
        else:
            lax.fori_loop(0, n_chunks, pass1, 0)
        m_scr[...] = jnp.broadcast_to(jnp.max(m_scr[...], axis=-1, keepdims=True), m_scr.shape)
        l_scr[...] = jnp.zeros(l_scr.shape, F32)
        acc_scr[...] = jnp.zeros(acc_scr.shape, F32)
        if n_chunks == 1:
            pass2(0, 0)
        else:
            lax.fori_loop(0, n_chunks, pass2, 0)

        o = acc_scr[...] / jnp.sum(l_scr[...], axis=-1, keepdims=True)
        if diff:
            lv = lam_ref[...]
            lam = (jnp.exp(jnp.sum(lv[0:1] * lv[1:2], axis=-1, keepdims=True))
                   - jnp.exp(jnp.sum(lv[2:3] * lv[3:4], axis=-1, keepdims=True)) + lam_init)
            a = o[0:tq] - lam * o[tq:2 * tq]
            ms = jnp.mean(a * a, axis=-1, keepdims=True)
            a = a * lax.rsqrt(ms + NORM_EPS) * dn_ref[...] * (1.0 - lam_init)
            o_ref[0] = a.astype(o_ref.dtype)
        else:
            o_ref[0] = o.astype(o_ref.dtype)

    if ctx_tiles:
        pl.when(i < ctx_tiles)(lambda: attend(ctx_len, ctx_len))
        pl.when(i >= ctx_tiles)(lambda: attend(t, tk))
    else:
        attend(t, tk)


def _attention(q, k, v, *, heads, dq, dv, q_col0, k_col0, v_col0, tq, scale, tk, diff=False,
               lam_vec=None, d_norm=None, lam_init=0.0, ctx_tiles=0, ctx_len=0):
    b, t, _ = k.shape
    sq = q.shape[1]
    assert t % tk == 0 and sq % tq == 0 and tk % LANES == 0 and ctx_len % LANES == 0
    assert ctx_len <= tk
    rows = 2 * tq if diff else tq
    if lam_vec is None:
        lam_vec = jnp.zeros((4, DIFF_HD), F32)
        d_norm = jnp.zeros((dv,), F32)
    kern = functools.partial(_attn_kernel, scale=scale, diff=diff, lam_init=lam_init,
                             ctx_tiles=ctx_tiles, ctx_len=ctx_len, tq=tq, tk=tk, t=t)
    return pl.pallas_call(
        kern,
        grid=(b, heads, sq // tq),
        in_specs=[
            pl.BlockSpec((1, tq, dq), lambda bi, h, i: (bi, i, q_col0 + h)),
            pl.BlockSpec((1, t, dq), lambda bi, h, i: (bi, 0, k_col0 + h)),
            pl.BlockSpec((1, t, dv), lambda bi, h, i: (bi, 0, v_col0 + h)),
            pl.BlockSpec((4, DIFF_HD), lambda bi, h, i: (0, 0)),
            pl.BlockSpec((1, dv), lambda bi, h, i: (0, 0)),
        ],
        out_specs=pl.BlockSpec((1, tq, dv), lambda bi, h, i: (bi, i, h)),
        out_shape=jax.ShapeDtypeStruct((b, sq, heads * dv), BF16),
        scratch_shapes=[pltpu.VMEM((rows, dq), BF16), pltpu.VMEM((t // tk, rows, tk), F32),
                        pltpu.VMEM((rows, LANES), F32), pltpu.VMEM((rows, LANES), F32),
                        pltpu.VMEM((rows, dv), F32)],
        compiler_params=_cparams(("parallel", "parallel", "arbitrary")),
    )(q, k, v, lam_vec.astype(F32), d_norm.reshape(1, dv).astype(F32))


def _pack_bf16_pairs(x):
    h = x.shape[1] // 2
    lo = lax.bitcast_convert_type(x[:, :h].astype(F32), jnp.uint32) >> 16
    hi = lax.bitcast_convert_type(x[:, h:].astype(F32), jnp.uint32) & jnp.uint32(0xFFFF0000)
    return hi | lo


def _unpack_bf16_pairs(w):
    lo = lax.bitcast_convert_type(w << 16, F32)
    hi = lax.bitcast_convert_type(w & jnp.uint32(0xFFFF0000), F32)
    return jnp.concatenate([lo.astype(BF16), hi.astype(BF16)], axis=1)


def _swiglu(x, wg, wu, wd):
    g = jnp.dot(x, wg, preferred_element_type=F32)
    u = jnp.dot(x, wu, preferred_element_type=F32)
    a = (g * jax.nn.sigmoid(g) * u).astype(BF16)
    return jnp.dot(a, wd, preferred_element_type=F32)


def _expert_kernel(be_ref, nu_ref, x_ref, wg_ref, wu_ref, wd_ref, o_ref, wg_s, wu_s, wd_s):
    i = pl.program_id(0)

    @pl.when(i < nu_ref[0])
    def _():
        @pl.when((i == 0) | (be_ref[i] != be_ref[jnp.maximum(i - 1, 0)]))
        def _():
            wg_s[...] = wg_ref[0, 0].astype(BF16)
            wu_s[...] = wu_ref[0, 0].astype(BF16)
            wd_s[...] = wd_ref[0, 0].astype(BF16)

        y = _swiglu(_unpack_bf16_pairs(x_ref[...]), wg_s[...], wu_s[...], wd_s[...])
        o_ref[...] = _pack_bf16_pairs(y.astype(BF16))

    @pl.when(i >= nu_ref[0])
    def _():
        o_ref[...] = jnp.zeros(o_ref.shape, o_ref.dtype)


def _expert_ffn(x, block_e, n_used, wg, wu, wd, *, layer, tm):
    n, dh = x.shape
    d = 2 * dh
    ff = wg.shape[3]
    assert n % tm == 0

    def blk(i, be, nu):
        return jnp.minimum(i, nu[0] - 1)

    grid_spec = pltpu.PrefetchScalarGridSpec(
        num_scalar_prefetch=2,
        grid=(n // tm,),
        in_specs=[
            pl.BlockSpec((tm, dh), lambda i, be, nu: (blk(i, be, nu), 0)),
            pl.BlockSpec((1, 1, d, ff), lambda i, be, nu: (layer, be[blk(i, be, nu)], 0, 0)),
            pl.BlockSpec((1, 1, d, ff), lambda i, be, nu: (layer, be[blk(i, be, nu)], 0, 0)),
            pl.BlockSpec((1, 1, ff, d), lambda i, be, nu: (layer, be[blk(i, be, nu)], 0, 0)),
        ],
        out_specs=pl.BlockSpec((tm, dh), lambda i, be, nu: (i, 0)),
        scratch_shapes=[pltpu.VMEM((d, ff), BF16), pltpu.VMEM((d, ff), BF16), pltpu.VMEM((ff, d), BF16)],
    )
    return pl.pallas_call(
        _expert_kernel,
        grid_spec=grid_spec,
        out_shape=jax.ShapeDtypeStruct((n, dh), jnp.uint32),
        compiler_params=_cparams(("arbitrary",)),
    )(block_e, n_used, x, wg, wu, wd)


def _post_mix_kernel(mix_ref, x_ref, w_ref, gp_ref, gf_ref, g1_ref, sh_ref, sc_ref, rw_ref,
                     xa_ref, f_ref, lg_ref):
    y = jnp.dot(mix_ref[0], w_ref[...], preferred_element_type=F32)
    yn = y * lax.rsqrt(jnp.mean(y * y, axis=-1, keepdims=True) + NORM_EPS) * gp_ref[...]
    xa = x_ref[0] + g1_ref[0] * yn
    xa_ref[0] = xa
    fn = xa * lax.rsqrt(jnp.mean(xa * xa, axis=-1, keepdims=True) + NORM_EPS) * gf_ref[...]
    f = (fn * (1.0 + sc_ref[0]) + sh_ref[0]).astype(BF16)
    f_ref[0] = _pack_bf16_pairs(f)
    lg_ref[...] = lax.dot_general(rw_ref[...], f, (((1,), (1,)), ((), ())), preferred_element_type=F32)


def _post_mix(mix, xs, w_out, g_post, g_ffn, g1, sh2, sc2, router_wt, *, row_tile0, seg_tiles):
    b, rows, k = mix.shape
    d = w_out.shape[1]
    e = router_wt.shape[0]
    tm = ROW_TILE
    nt = rows // tm
    mod_map = lambda bi, i: (bi * 2 + jnp.where(i + row_tile0 >= seg_tiles, 1, 0), 0, 0)
    vec = lambda: pl.BlockSpec((1, d), lambda bi, i: (0, 0))
    return pl.pallas_call(
        _post_mix_kernel,
        grid=(b, nt),
        in_specs=[
            pl.BlockSpec((1, tm, k), lambda bi, i: (bi, i, 0)),
            pl.BlockSpec((1, tm, d), lambda bi, i: (bi, i + row_tile0, 0)),
            pl.BlockSpec((k, d), lambda bi, i: (0, 0)),
            vec(), vec(),
            pl.BlockSpec((1, 1, d), mod_map), pl.BlockSpec((1, 1, d), mod_map),
            pl.BlockSpec((1, 1, d), mod_map),
            pl.BlockSpec((e, d), lambda bi, i: (0, 0)),
        ],
        out_specs=[
            pl.BlockSpec((1, tm, d), lambda bi, i: (bi, i, 0)),
            pl.BlockSpec((1, tm, d // 2), lambda bi, i: (bi, i, 0)),
            pl.BlockSpec((e, tm), lambda bi, i: (0, bi * nt + i)),
        ],
        out_shape=[jax.ShapeDtypeStruct((b, rows, d), F32),
                   jax.ShapeDtypeStruct((b, rows, d // 2), jnp.uint32),
                   jax.ShapeDtypeStruct((e, b * rows), F32)],
        compiler_params=_cparams(("parallel", "arbitrary")),
    )(mix, xs, w_out, g_post.reshape(1, d).astype(F32), g_ffn.reshape(1, d).astype(F32),
      g1, sh2, sc2, router_wt)


def _route_kernel(lg_ref, rb_ref, e_ref, g_ref):
    per = N_EXPERTS // N_GROUPS
    tn = lg_ref.shape[1]
    neg = -jnp.inf
    r_io = lax.broadcasted_iota(jnp.int32, (per, tn), 0)
    scores, choice, gs = [], [], []
    for g in range(N_GROUPS):
        sg = jax.nn.sigmoid(lg_ref[g * per:(g + 1) * per, :])
        cg = sg + rb_ref[g * per:(g + 1) * per, :]
        m1 = jnp.max(cg, axis=0, keepdims=True)
        i1 = jnp.min(jnp.where(cg == m1, r_io, per), axis=0, keepdims=True)
        m2 = jnp.max(jnp.where(r_io == i1, neg, cg), axis=0, keepdims=True)
        scores.append(sg)
        choice.append(cg)
        gs.append(m1 + m2)
    masked = []
    for g in range(N_GROUPS):
        ahead = jnp.zeros((1, tn), jnp.int32)
        for o in range(N_GROUPS):
            if o < g:
                ahead = ahead + jnp.where(gs[o] >= gs[g], 1, 0)
            elif o > g:
                ahead = ahead + jnp.where(gs[o] > gs[g], 1, 0)
        masked.append(jnp.where(ahead < TOPK_GROUPS, choice[g], neg))
    ids, gates = [], []
    for _ in range(TOP_K):
        best = masked[0]
        for g in range(1, N_GROUPS):
            best = jnp.maximum(best, masked[g])
        best = jnp.max(best, axis=0, keepdims=True)
        cand = jnp.where(masked[0] == best, r_io, N_EXPERTS)
        for g in range(1, N_GROUPS):
            cand = jnp.minimum(cand, jnp.where(masked[g] == best, r_io + g * per, N_EXPERTS))
        idx = jnp.min(cand, axis=0, keepdims=True)
        gk = jnp.zeros((per, tn), F32)
        for g in range(N_GROUPS):
            hit = (r_io + g * per) == idx
            gk = gk + jnp.where(hit, scores[g], 0.0)
            masked[g] = jnp.where(hit, neg, masked[g])
        ids.append(idx)
        gates.append(jnp.sum(gk, axis=0, keepdims=True))
    total = gates[0]
    for k in range(1, TOP_K):
        total = total + gates[k]
    pad = 8 - TOP_K
    e_ref[...] = jnp.concatenate(ids + [jnp.zeros((pad, tn), jnp.int32)], axis=0)
    g_ref[...] = jnp.concatenate([gk / total * ROUTED_SCALE for gk in gates]
                                 + [jnp.zeros((pad, tn), F32)], axis=0)


def _route(logits_t, router_b, *, tn):
    e, n = logits_t.shape
    assert n % tn == 0
    return pl.pallas_call(
        _route_kernel,
        grid=(n // tn,),
        in_specs=[pl.BlockSpec((e, tn), lambda i: (0, i)),
                  pl.BlockSpec((e, 1), lambda i: (0, 0))],
        out_specs=[pl.BlockSpec((8, tn), lambda i: (0, i)), pl.BlockSpec((8, tn), lambda i: (0, i))],
        out_shape=[jax.ShapeDtypeStruct((8, n), jnp.int32), jax.ShapeDtypeStruct((8, n), F32)],
        compiler_params=_cparams(("parallel",)),
    )(logits_t, router_b.reshape(e, 1).astype(F32))


def _dispatch_kernel(e_ref, dest_ref, be_ref, nu_ref, cnt_scr, start_scr, run_scr, *, blk):
    ph = pl.program_id(0)
    i = pl.program_id(1)
    tn = e_ref.shape[1]
    e_io = lax.broadcasted_iota(jnp.int32, (N_EXPERTS, tn), 0)
    hot = jnp.zeros((N_EXPERTS, tn), F32)
    for k in range(TOP_K):
        hot = hot + jnp.where(e_io == e_ref[k:k + 1, :], 1.0, 0.0)
    tile_cnt = jnp.sum(hot, axis=1, keepdims=True)

    @pl.when((ph == 0) & (i == 0))
    def _():
        cnt_scr[...] = jnp.zeros(cnt_scr.shape, F32)

    @pl.when(ph == 0)
    def _():
        cnt_scr[...] += jnp.broadcast_to(tile_cnt, cnt_scr.shape)

    @pl.when((ph == 1) & (i == 0))
    def _():
        bpe = jnp.floor((cnt_scr[...] + (blk - 1.0)) * (1.0 / blk))
        r = lax.broadcasted_iota(jnp.int32, (N_EXPERTS, N_EXPERTS), 0)
        c = lax.broadcasted_iota(jnp.int32, (N_EXPERTS, N_EXPERTS), 1)
        lower = jnp.where(c < r, 1.0, 0.0)
        before = jnp.dot(lower, bpe, precision=lax.Precision.HIGHEST, preferred_element_type=F32)
        start_scr[...] = before * blk
        run_scr[...] = jnp.zeros(run_scr.shape, F32)
        ends = (before + bpe)[:, 0:1]
        nb = be_ref.shape[1]
        bid = lax.broadcasted_iota(jnp.int32, (N_EXPERTS, nb), 1).astype(F32)
        be = jnp.sum(jnp.where(ends <= bid, 1, 0), axis=0, keepdims=True)
        be_ref[...] = jnp.minimum(be, N_EXPERTS - 1).astype(jnp.int32)
        nu_ref[...] = jnp.broadcast_to(jnp.max(ends, axis=0, keepdims=True), nu_ref.shape).astype(jnp.int32)

    @pl.when(ph == 1)
    def _():
        rr = lax.broadcasted_iota(jnp.int32, (tn, tn), 0)
        cc = lax.broadcasted_iota(jnp.int32, (tn, tn), 1)
        upper = jnp.where(rr < cc, 1.0, 0.0).astype(BF16)
        prior = jnp.dot(hot.astype(BF16), upper, preferred_element_type=F32)
        pos = prior + jnp.concatenate([start_scr[...] + run_scr[...]] * (tn // LANES), axis=1)
        rows = []
        for k in range(TOP_K):
            rows.append(jnp.sum(jnp.where(e_io == e_ref[k:k + 1, :], pos, 0.0), axis=0, keepdims=True))
        rows.append(jnp.zeros((8 - TOP_K, tn), F32))
        dest_ref[0] = jnp.concatenate(rows, axis=0).astype(jnp.int32)
        run_scr[...] += jnp.broadcast_to(tile_cnt, run_scr.shape)


def _dispatch(top_e, *, blk, n_blocks):
    n = top_e.shape[1]
    tn = LANES
    nbp = -(-n_blocks // LANES) * LANES
    kern = functools.partial(_dispatch_kernel, blk=blk)
    return pl.pallas_call(
        kern,
        grid=(2, n // tn),
        in_specs=[pl.BlockSpec((8, tn), lambda ph, i: (0, i))],
        out_specs=[pl.BlockSpec((1, 8, tn), lambda ph, i: (i * ph, 0, 0)),
                   pl.BlockSpec((1, nbp), lambda ph, i: (0, 0)),
                   pl.BlockSpec((8, LANES), lambda ph, i: (0, 0))],
        out_shape=[jax.ShapeDtypeStruct((n // tn, 8, tn), jnp.int32),
                   jax.ShapeDtypeStruct((1, nbp), jnp.int32),
                   jax.ShapeDtypeStruct((8, LANES), jnp.int32)],
        scratch_shapes=[pltpu.VMEM((N_EXPERTS, LANES), F32)] * 3,
        compiler_params=_cparams(("arbitrary", "arbitrary")),
    )(top_e)


def _scatter_rows_kernel(dest_ref, x_ref, zero_ref, o_ref, sem):
    del zero_ref
    tm = x_ref.shape[0]

    def row_copy(t, k):
        return pltpu.make_async_copy(x_ref.at[pl.ds(t, 1)], o_ref.at[pl.ds(dest_ref[0, k, t], 1)], sem)

    def issue(t, carry):
        for k in range(TOP_K):
            row_copy(t, k).start()
        return carry

    def drain(t, carry):
        for k in range(TOP_K):
            row_copy(t, k).wait()
        return carry

    lax.fori_loop(0, tm, issue, 0)
    lax.fori_loop(0, tm, drain, 0)


def _scatter_rows(dest, x, n_rows):
    n, dh = x.shape
    tm = LANES
    zeros = jnp.zeros((n_rows, dh), x.dtype)
    return pl.pallas_call(
        _scatter_rows_kernel,
        grid=(n // tm,),
        in_specs=[pl.BlockSpec((1, 8, tm), lambda i: (i, 0, 0), memory_space=pltpu.SMEM),
                  pl.BlockSpec((tm, dh), lambda i: (i, 0)),
                  pl.BlockSpec(memory_space=pl.ANY)],
        out_specs=pl.BlockSpec(memory_space=pl.ANY),
        out_shape=jax.ShapeDtypeStruct((n_rows, dh), x.dtype),
        scratch_shapes=[pltpu.SemaphoreType.DMA(())],
        input_output_aliases={2: 0},
        compiler_params=_cparams(("arbitrary",)),
    )(dest, x, zeros)


def _combine_kernel(dest_ref, y_ref, gate_ref, f_ref, xa_ref, sg_ref, su_ref, sd_ref, gp_ref, g2_ref,
                    o_ref, buf, sem):
    tm = f_ref.shape[0]

    def row_copy(t, k):
        return pltpu.make_async_copy(y_ref.at[pl.ds(dest_ref[0, k, t], 1)], buf.at[k, pl.ds(t, 1)], sem)

    def issue(t, carry):
        for k in range(TOP_K):
            row_copy(t, k).start()
        return carry

    def drain(t, carry):
        for k in range(TOP_K):
            row_copy(t, k).wait()
        return carry

    lax.fori_loop(0, tm, issue, 0)
    f = _swiglu(_unpack_bf16_pairs(f_ref[...]), sg_ref[...], su_ref[...], sd_ref[...])
    lax.fori_loop(0, tm, drain, 0)
    gate = gate_ref[...]
    for k in range(TOP_K):
        f = f + gate[:, k:k + 1] * _unpack_bf16_pairs(buf[k]).astype(F32)
    fn = f * lax.rsqrt(jnp.mean(f * f, axis=-1, keepdims=True) + NORM_EPS) * gp_ref[...]
    o_ref[0] = xa_ref[0] + g2_ref[0] * fn


def _combine(dest, y, gate, f_pk, xa, sg, su, sd, g_post, g2, *, row_tile0, seg_tiles):
    b, rows, d = xa.shape
    tm = LANES
    nt = rows // tm
    per = ROW_TILE // tm
    mod_map = lambda bi, i: (bi * 2 + jnp.where(i // per + row_tile0 >= seg_tiles, 1, 0), 0, 0)
    const = lambda shape: pl.BlockSpec(shape, lambda bi, i: (0,) * len(shape))
    return pl.pallas_call(
        _combine_kernel,
        grid=(b, nt),
        in_specs=[
            pl.BlockSpec((1, 8, tm), lambda bi, i: (bi * nt + i, 0, 0), memory_space=pltpu.SMEM),
            pl.BlockSpec(memory_space=pl.ANY),
            pl.BlockSpec((tm, 8), lambda bi, i: (bi * nt + i, 0)),
            pl.BlockSpec((tm, d // 2), lambda bi, i: (bi * nt + i, 0)),
            pl.BlockSpec((1, tm, d), lambda bi, i: (bi, i, 0)),
            const(sg.shape), const(su.shape), const(sd.shape), const((1, d)),
            pl.BlockSpec((1, 1, d), mod_map),
        ],
        out_specs=pl.BlockSpec((1, tm, d), lambda bi, i: (bi, i, 0)),
        out_shape=jax.ShapeDtypeStruct((b, rows, d), F32),
        scratch_shapes=[pltpu.VMEM((TOP_K, tm, d // 2), jnp.uint32), pltpu.SemaphoreType.DMA(())],
        compiler_params=_cparams(("arbitrary", "arbitrary")),
    )(dest, y, gate, f_pk, xa, sg, su, sd, g_post.reshape(1, d).astype(F32), g2)


def _rope_tables(s, ctx_len, dim):
    rows = s // GRID_W
    row = jnp.repeat(jnp.arange(rows), GRID_W)
    col = jnp.tile(jnp.arange(GRID_W), rows)
    quarter = dim // 4
    inv = ROPE_BASE ** (-jnp.arange(quarter, dtype=F32) / quarter)
    ang = jnp.concatenate([row.astype(F32)[:, None] * inv, col.astype(F32)[:, None] * inv], axis=-1)
    cos = jnp.repeat(jnp.cos(ang), 2, axis=-1)
    sin = jnp.repeat(jnp.sin(ang), 2, axis=-1) * jnp.tile(jnp.array([-1.0, 1.0], F32), dim // 2)
    cos = jnp.concatenate([jnp.ones((ctx_len, dim), F32), cos], axis=0)
    sin = jnp.concatenate([jnp.zeros((ctx_len, dim), F32), sin], axis=0)
    return cos, sin


def _log_sigmoid(x):
    return jnp.minimum(x, 0.0) - jnp.log1p(jnp.exp(-jnp.abs(x)))


def _mlstm_kernel(q_ref, k_ref, kt_ref, v_ref, g_ref, gt_ref, gb_ref, gbt_ref, hf_ref, mo_ref, mn_ref,
                  o_ref, s_scr, m_scr, *, reverse, final):
    L = MLSTM_CHUNK
    hv = MLSTM_V
    io, fo = (2 * MLSTM_HEADS, 3 * MLSTM_HEADS) if reverse else (0, MLSTM_HEADS)

    @pl.when(pl.program_id(1) == 0)
    def _():
        s_scr[...] = jnp.zeros(s_scr.shape, F32)
        m_scr[...] = jnp.zeros(m_scr.shape, F32)

    g = g_ref[0] + gb_ref[...]
    gt = gt_ref[0] + gbt_ref[...]
    r_io = lax.broadcasted_iota(jnp.int32, (L, L), 0)
    c_io = lax.broadcasted_iota(jnp.int32, (L, L), 1)
    seen = (c_io >= r_io) if reverse else (c_io <= r_io)
    tri = jnp.where(seen, 1.0, 0.0)
    hi = lax.Precision.HIGHEST
    bc_col = jnp.dot(tri, _log_sigmoid(g), precision=hi, preferred_element_type=F32)
    lf_row = _log_sigmoid(gt)
    bc_row = lax.dot_general(lf_row, tri, (((1,), (1,)), ((), ())), precision=hi,
                             preferred_element_type=F32)
    lane = lax.broadcasted_iota(jnp.int32, (L, LANES), 1)
    sub = lax.broadcasted_iota(jnp.int32, (LANES, L), 0)

    for h in range(MLSTM_HEADS):
        pair, odd = h // 2, h % 2
        lo = odd * MLSTM_QK
        a_col = bc_col[:, fo + h:fo + h + 1]
        i_col = g[:, io + h:io + h + 1]
        b_row = bc_row[fo + h:fo + h + 1, :]
        i_row = gt[io + h:io + h + 1, :]
        btot = jnp.sum(lf_row[fo + h:fo + h + 1, :], axis=1, keepdims=True)
        m_st = m_scr[h:h + 1, 0:1]

        w_end = btot - a_col + i_col
        m_new = jnp.maximum(btot + m_st, jnp.max(w_end, axis=0, keepdims=True))
        decay = jnp.exp(btot + m_st - m_new)
        w_k = jnp.exp(w_end - m_new)

        log_d = jnp.where(seen, a_col - b_row + i_row, -jnp.inf)
        log_inter = a_col + m_st
        m_row = jnp.maximum(log_inter, jnp.max(log_d, axis=1, keepdims=True))
        w_intra = jnp.exp(log_d - m_row)
        w_inter = jnp.exp(log_inter - m_row)

        in_head = (lane >= lo) & (lane < lo + MLSTM_QK)
        qm = jnp.where(in_head, q_ref[0, :, pair * LANES:(pair + 1) * LANES], 0).astype(BF16)
        kp = k_ref[0, :, pair * LANES:(pair + 1) * LANES]
        v = v_ref[0, :, h * hv:(h + 1) * hv]
        state = s_scr[h]

        qk = lax.dot_general(qm, kp, (((1,), (1,)), ((), ())), preferred_element_type=F32)
        qk = qk * (MLSTM_QK ** -0.5) * w_intra
        inter = jnp.dot(qm, state.astype(BF16), preferred_element_type=F32)
        num = w_inter * inter[:, :hv] + jnp.dot(qk.astype(BF16), v, preferred_element_type=F32)
        den = w_inter * inter[:, hv:hv + 1] + jnp.sum(qk, axis=1, keepdims=True)
        out = num / jnp.maximum(jnp.abs(den), jnp.exp(-m_row))

        wv = jnp.concatenate([(w_k * v.astype(F32)).astype(BF16),
                              jnp.where(lane == 0, w_k, 0.0).astype(BF16)], axis=1)
        kt = kt_ref[0, pair * LANES:(pair + 1) * LANES, :]
        in_rows = (sub >= lo) & (sub < lo + MLSTM_QK)
        ktm = (jnp.where(in_rows, kt, 0).astype(F32) * (MLSTM_QK ** -0.5)).astype(BF16)
        s_scr[h] = decay * state + jnp.dot(ktm, wv, preferred_element_type=F32)
        m_scr[h:h + 1, :] = jnp.broadcast_to(m_new, (1, LANES))

        if final:
            tot = out + hf_ref[0, :, h * hv:(h + 1) * hv]
            nrm = tot * lax.rsqrt(jnp.mean(tot * tot, axis=-1, keepdims=True) + NORM_EPS) * mn_ref[...]
            gate = jax.nn.sigmoid(mo_ref[0, :, h * hv:(h + 1) * hv].astype(F32))
            o_ref[0, :, h * hv:(h + 1) * hv] = (nrm * gate).astype(o_ref.dtype)
        else:
            o_ref[0, :, h * hv:(h + 1) * hv] = out.astype(o_ref.dtype)


def _mlstm(pv, kt, gates, gates_t, gate_b, m_norm, hf, *, ctx_chunks, reverse):
    b, t, _ = pv.shape
    L = MLSTM_CHUNK
    nc = t // L
    final = hf is not None
    nq = MLSTM_HEADS * MLSTM_QK
    nv = MLSTM_HEADS * MLSTM_V

    def chunk(j):
        if not reverse:
            return j
        return jnp.where(j < ctx_chunks, ctx_chunks - 1 - j, nc - 1 - (j - ctx_chunks))

    gb = jnp.pad(gate_b.astype(F32).reshape(-1), (0, LANES - gate_b.size))
    if hf is None:
        hf = jnp.zeros((1, L, nv), F32)
        hf_spec = pl.BlockSpec((1, L, nv), lambda bi, j: (0, 0, 0))
    else:
        hf_spec = pl.BlockSpec((1, L, nv), lambda bi, j: (bi, chunk(j), 0))
    kern = functools.partial(_mlstm_kernel, reverse=reverse, final=final)
    return pl.pallas_call(
        kern,
        grid=(b, nc),
        in_specs=[
            pl.BlockSpec((1, L, nq), lambda bi, j: (bi, chunk(j), nv // nq)),
            pl.BlockSpec((1, L, nq), lambda bi, j: (bi, chunk(j), nv // nq + 1)),
            pl.BlockSpec((1, nq, L), lambda bi, j: (bi, 0, chunk(j))),
            pl.BlockSpec((1, L, nv), lambda bi, j: (bi, chunk(j), 2)),
            pl.BlockSpec((1, L, LANES), lambda bi, j: (bi, chunk(j), 0)),
            pl.BlockSpec((1, LANES, L), lambda bi, j: (bi, 0, chunk(j))),
            pl.BlockSpec((1, LANES), lambda bi, j: (0, 0)),
            pl.BlockSpec((LANES, 1), lambda bi, j: (0, 0)),
            hf_spec,
            pl.BlockSpec((1, L, nv), lambda bi, j: (bi, chunk(j), 3)),
            pl.BlockSpec((1, MLSTM_V), lambda bi, j: (0, 0)),
        ],
        out_specs=pl.BlockSpec((1, L, nv), lambda bi, j: (bi, chunk(j), 0)),
        out_shape=jax.ShapeDtypeStruct((b, t, nv), BF16 if final else F32),
        scratch_shapes=[pltpu.VMEM((MLSTM_HEADS, LANES, 2 * LANES), F32), pltpu.VMEM((MLSTM_HEADS, LANES), F32)],
        compiler_params=_cparams(("parallel", "arbitrary")),
    )(pv, pv, kt, pv, gates, gates_t, gb.reshape(1, LANES), gb.reshape(LANES, 1), hf, pv,
      m_norm.reshape(1, MLSTM_V).astype(F32))


def _swa_kernel(sink_ref, q_ref, kp_ref, kc_ref, kn_ref, kx_ref, vp_ref, vc_ref, vn_ref, vx_ref, o_ref, *,
                n_blocks):
    i = pl.program_id(1)
    bq = BLOCK_Q
    hd = SWA_HD
    rep = SWA_HEADS // SWA_KV_HEADS
    n_ctx = kx_ref.shape[1]
    rows = rep * bq
    scale = hd ** -0.5 * LOG2E
    row = lax.broadcasted_iota(jnp.int32, (rows, 3 * bq), 0)
    col = lax.broadcasted_iota(jnp.int32, (rows, 3 * bq), 1)
    dt = (col - bq) - (row % bq)
    blk = col // bq
    ok = (jnp.abs(dt) <= WINDOW) & ((blk != 0) | (i > 0)) & ((blk != 2) | (i < n_blocks - 1))
    head_of_row = lax.broadcasted_iota(jnp.int32, (rows, 1), 0) // bq
    for g in range(SWA_KV_HEADS):
        q = jnp.concatenate([q_ref[0, :, (g * rep + r) * hd:(g * rep + r + 1) * hd] for r in range(rep)], axis=0)
        q = (q.astype(F32) * scale).astype(BF16)
        cs = slice(g * hd, (g + 1) * hd)
        k_loc = jnp.concatenate([kp_ref[0, :, cs], kc_ref[0, :, cs], kn_ref[0, :, cs]], axis=0)
        v_loc = jnp.concatenate([vp_ref[0, :, cs], vc_ref[0, :, cs], vn_ref[0, :, cs]], axis=0)
        nt = (((1,), (1,)), ((), ()))
        s_loc = jnp.where(ok, lax.dot_general(q, k_loc, nt, preferred_element_type=F32), -jnp.inf)
        s_ctx = lax.dot_general(q, kx_ref[0, :, cs], nt, preferred_element_type=F32)
        sink = jnp.zeros((rows, 1), F32)
        for r in range(rep):
            sink = jnp.where(head_of_row == r, sink_ref[g * rep + r] * LOG2E, sink)
        m = jnp.maximum(jnp.maximum(jnp.max(s_loc, axis=-1, keepdims=True),
                                    jnp.max(s_ctx, axis=-1, keepdims=True)), sink)
        p_loc = jnp.exp2(s_loc - m)
        p_ctx = jnp.exp2(s_ctx - m)
        den = (jnp.sum(p_loc, axis=-1, keepdims=True) + jnp.sum(p_ctx, axis=-1, keepdims=True)
               + jnp.exp2(sink - m))
        out = (jnp.dot(p_loc.astype(BF16), v_loc, preferred_element_type=F32)
               + jnp.dot(p_ctx.astype(BF16), vx_ref[0, :, cs], preferred_element_type=F32)) / den
        for r in range(rep):
            o_ref[0, :, (g * rep + r) * hd:(g * rep + r + 1) * hd] = out[r * bq:(r + 1) * bq].astype(o_ref.dtype)


def _swa(qk, v, sink, *, n_ctx, k_col0):
    b, t, _ = qk.shape
    bq = BLOCK_Q
    cb = n_ctx // bq
    nb = (t - n_ctx) // bq
    nq = SWA_HEADS * SWA_HD
    nk = SWA_KV_HEADS * SWA_HD
    kcol = k_col0
    prev = lambda bi, i, c: (bi, i + cb - 1, c)
    cur = lambda bi, i, c: (bi, i + cb, c)
    nxt = lambda bi, i, c: (bi, jnp.minimum(i + cb + 1, nb + cb - 1), c)
    kern = functools.partial(_swa_kernel, n_blocks=nb)

    def spec(rows, fn, c):
        return pl.BlockSpec((1, rows, nk), lambda bi, i: fn(bi, i, c))

    ctx = lambda bi, i, c: (bi, 0, c)
    return pl.pallas_call(
        kern,
        grid=(b, nb),
        in_specs=[
            pl.BlockSpec(memory_space=pltpu.SMEM),
            pl.BlockSpec((1, bq, nq), lambda bi, i: (bi, i + cb, 0)),
            spec(bq, prev, kcol), spec(bq, cur, kcol), spec(bq, nxt, kcol), spec(n_ctx, ctx, kcol),
            spec(bq, prev, 0), spec(bq, cur, 0), spec(bq, nxt, 0), spec(n_ctx, ctx, 0),
        ],
        out_specs=pl.BlockSpec((1, bq, nq), lambda bi, i: (bi, i, 0)),
        out_shape=jax.ShapeDtypeStruct((b, nb * bq, nq), BF16),
        compiler_params=_cparams(("parallel", "arbitrary")),
    )(sink.astype(F32), qk, qk, qk, qk, qk, v, v, v, v)


def _mix_out_and_moe(mix, xs, w_out, g_post, g_ffn_pre, g_ffn_post, mods, router_w, router_b,
                     wg, wu, wd, sg, su, sd, *, layer, row_tile0, seg_tiles):
    sh1, sc1, g1, sh2, sc2, g2 = mods
    b, rows, _ = mix.shape
    n = b * rows
    seg = dict(row_tile0=row_tile0, seg_tiles=seg_tiles)
    xa, f_pk, logits_t = _post_mix(mix, xs, w_out.astype(BF16), g_post, g_ffn_pre, g1, sh2, sc2,
                                   router_w.T.astype(BF16), **seg)
    top_e, gate = _route(logits_t, router_b, tn=512)
    blk = EXPERT_ROWS
    n_blocks = -(-n * TOP_K // blk) + N_EXPERTS
    dest, block_e, n_used = _dispatch(top_e, blk=blk, n_blocks=n_blocks)
    f_pk = f_pk.reshape(n, -1)
    x_sorted = _scatter_rows(dest, f_pk, n_blocks * blk)
    y = _expert_ffn(x_sorted, block_e[0, :n_blocks], n_used[0, :1], wg, wu, wd, layer=layer, tm=blk)
    return _combine(dest, y, gate.T, f_pk, xa, sg.astype(BF16), su.astype(BF16), sd.astype(BF16),
                    g_ffn_post, g2, **seg)


def kernel(x, c, ctx, c_ctx, mod_w, mod_b, norm_mix_pre, norm_mix_post, norm_ffn_pre, norm_ffn_post, ab_w_in, ab_w_out, diff_lambda, diff_norm, mlstm_gate_b, mlstm_norm, cd_w_in, cd_w_out, mla_q_norm, mla_w_uq, mla_kv_norm, mla_w_ukv, swa_sink, router_w, router_b, exp_w_gate, exp_w_up, exp_w_down, sh_w_gate, sh_w_up, sh_w_down):
    b, s, d = x.shape
    L = ctx.shape[1]
    t = L + s
    assert L == ROW_TILE and s % ROW_TILE == 0
    depth = mod_w.shape[0]
    ctx_tiles = L // ROW_TILE
    tk = next(c for c in (2816, 768, ROW_TILE) if t % c == 0)

    cos64, sin64 = _rope_tables(s, L, DIFF_HD)
    cos64 = jnp.tile(cos64, (1, LANES // DIFF_HD))
    sin64 = jnp.tile(sin64, (1, LANES // DIFF_HD))
    cos128, sin128 = _rope_tables(s, L, SWA_HD)
    cos_kr = jnp.concatenate([cos64[:, :MLA_ROPE], jnp.ones((t, LANES - MLA_ROPE), F32)], axis=1)
    sin_kr = jnp.concatenate([sin64[:, :MLA_ROPE], jnp.zeros((t, LANES - MLA_ROPE), F32)], axis=1)

    xs = jnp.concatenate([ctx, x], axis=1)

    for layer in range(depth):
        with_ctx = layer < depth - 1
        j = layer // 2
        mod_l = jax.nn.silu(c) @ mod_w[layer] + mod_b[layer]
        mod_c = jax.nn.silu(c_ctx) @ mod_w[layer] + mod_b[layer]
        mods = jnp.stack([jnp.broadcast_to(mod_c, (b, 6 * d)), mod_l], axis=1).reshape(b * 2, 1, 6, d)
        sh1, sc1, g1, sh2, sc2, g2 = (mods[:, :, m] for m in range(6))

        proj = functools.partial(_norm_proj, xs, norm_mix_pre[layer], sh1, sc1, seg_tiles=ctx_tiles)
        if layer % 2 == 0:
            w_in = ab_w_in[j].astype(BF16)
            n_qk = 2 * DIFF_HEADS * 2 * DIFF_HD
            qk = proj(w_in[:, :n_qk], tn=1024, rope=(cos64[None], sin64[None]), pattern=(0,) * 8)
            pv = proj(w_in[:, n_qk:n_qk + AB_PLAIN], tn=1024)
            w_gates = jnp.pad(w_in[:, n_qk + AB_PLAIN:], ((0, 0), (0, LANES - 4 * MLSTM_HEADS)))
            gates = proj(w_gates, tn=LANES, out_dtype=F32)

            lam_init = 0.8 - 0.6 * math.exp(-0.3 * layer)
            a = _attention(qk if with_ctx else qk[:, L:], qk, pv, heads=DIFF_HEADS, dq=2 * DIFF_HD,
                           dv=DIFF_VD, q_col0=0, k_col0=DIFF_HEADS, v_col0=0, tq=ROW_TILE,
                           scale=DIFF_HD ** -0.5, tk=tk, diff=True, lam_vec=diff_lambda[j],
                           d_norm=diff_norm[j], lam_init=lam_init,
                           ctx_tiles=ctx_tiles if with_ctx else 0, ctx_len=L)
            o_mk = DIFF_HEADS * DIFF_VD + MLSTM_HEADS * MLSTM_QK
            kt = jnp.swapaxes(pv[..., o_mk:o_mk + MLSTM_HEADS * MLSTM_QK], 1, 2)
            scan = functools.partial(_mlstm, pv, kt, gates, jnp.swapaxes(gates, 1, 2), mlstm_gate_b[j],
                                     mlstm_norm[j], ctx_chunks=L // MLSTM_CHUNK)
            m = scan(scan(None, reverse=False), reverse=True)
            mix = jnp.concatenate([a, m if with_ctx else m[:, L:]], axis=-1)
            w_out = ab_w_out[j]
        else:
            assert not with_ctx
            w_in = cd_w_in[j].astype(BF16)
            c0 = Q_LORA
            c1 = c0 + KV_LORA
            c2 = c1 + MLA_ROPE
            c3 = c2 + SWA_HEADS * SWA_HD
            c4 = c3 + SWA_KV_HEADS * SWA_HD
            w_rope = jnp.concatenate([w_in[:, c2:c4], w_in[:, c1:c2],
                                      jnp.zeros((d, LANES - MLA_ROPE), BF16)], axis=1)
            n_rope = w_rope.shape[1]
            rp = proj(w_rope, tn=n_rope, rope=(jnp.stack([cos128, cos_kr]), jnp.stack([sin128, sin_kr])),
                      pattern=(0,) * (n_rope // LANES - 1) + (1,))
            q_pad = 512 - Q_LORA
            w_plain = jnp.concatenate([w_in[:, c4:], w_in[:, :c0], jnp.zeros((d, q_pad), BF16),
                                       w_in[:, c0:c1]], axis=1)
            pp = proj(w_plain, tn=w_plain.shape[1])
            n_sv = SWA_KV_HEADS * SWA_HD
            cq = pp[..., n_sv:n_sv + 512]
            ckv = pp[..., n_sv + 512:]

            hq = MLA_NOPE + MLA_ROPE
            w_uq = mla_w_uq[j].astype(BF16).reshape(Q_LORA, MLA_HEADS, hq)
            w_uq = jnp.pad(w_uq, ((0, q_pad), (0, 0), (0, 256 - hq))).reshape(512, MLA_HEADS * 256)
            no_mod = jnp.zeros((b * 2, 1, 512), F32)
            qn = jnp.pad(mla_q_norm[j], (0, q_pad))
            q_mla = _norm_proj(cq[:, L:], qn, no_mod, no_mod, w_uq, seg_tiles=0, tn=1024, n_valid=Q_LORA,
                               rope=(cos_kr[None, L:], sin_kr[None, L:]), pattern=(-1, 0) * 4)
            kv = _norm_proj(ckv, mla_kv_norm[j], no_mod, no_mod, mla_w_ukv[j].astype(BF16),
                            seg_tiles=ctx_tiles, tn=1024).reshape(b, t, MLA_HEADS, MLA_NOPE + MLA_V)
            kr = rp[..., n_rope - LANES:]
            k_mla = jnp.concatenate([kv[..., :MLA_NOPE],
                                     jnp.broadcast_to(kr[:, :, None, :], (b, t, MLA_HEADS, LANES))],
                                    axis=-1).reshape(b, t, MLA_HEADS * 256)
            v_mla = kv[..., MLA_NOPE:].reshape(b, t, MLA_HEADS * MLA_V)
            a = _attention(q_mla, k_mla, v_mla, heads=MLA_HEADS, dq=256, dv=MLA_V, q_col0=0, k_col0=0,
                           v_col0=0, tq=512, scale=MLA_SCALE, tk=tk)

            w = _swa(rp, pp, swa_sink[j], n_ctx=L, k_col0=SWA_HEADS * SWA_HD // n_sv)
            mix = jnp.concatenate([a, w], axis=-1)
            w_out = cd_w_out[j]

        xa = _mix_out_and_moe(mix, xs, w_out, norm_mix_post[layer], norm_ffn_pre[layer], norm_ffn_post[layer],
                              (sh1, sc1, g1, sh2, sc2, g2), router_w[layer], router_b[layer],
                              exp_w_gate, exp_w_up, exp_w_down,
                              sh_w_gate[layer], sh_w_up[layer], sh_w_down[layer],
                              layer=layer, row_tile0=0 if with_ctx else ctx_tiles, seg_tiles=ctx_tiles)
        xs = xa if with_ctx else jnp.concatenate([xs[:, :L], xa], axis=1)
    return xs[:, L:]
```

```python
import functools
import math

import jax
import jax.numpy as jnp
import numpy as np
from jax import lax
from jax.experimental import pallas as pl
from jax.experimental.pallas import tpu as pltpu

F32 = jnp.float32
BF16 = jnp.bfloat16

GRID_W = 64
ROPE_BASE = 10000.0
NORM_EPS = 1e-6

DIFF_HEADS = 8
DIFF_HD = 64
DIFF_VD = 2 * DIFF_HD
MLSTM_HEADS = 8
MLSTM_QK = 64
MLSTM_V = 128
MLSTM_CHUNK = 128
MLA_HEADS = 8
MLA_NOPE = 128
MLA_ROPE = 64
MLA_V = 128
Q_LORA = 448
KV_LORA = 512
MLA_SCALE = (MLA_NOPE + MLA_ROPE) ** -0.5
SWA_HEADS = 8
SWA_KV_HEADS = 2
SWA_HD = 128
WINDOW = 128
BLOCK_Q = 128
N_EXPERTS = 64
TOP_K = 6
N_GROUPS = 8
TOPK_GROUPS = 4
ROUTED_SCALE = 2.5
AB_PLAIN = DIFF_HEADS * DIFF_VD + 2 * MLSTM_HEADS * MLSTM_QK + 2 * MLSTM_HEADS * MLSTM_V

LANES = 128
LOG2E = 1.4426950408889634
VMEM_LIMIT = 56 * 1024 * 1024

ROW_TILE = 256
EXPERT_ROWS = 512


def _cparams(sem):
    return pltpu.CompilerParams(dimension_semantics=sem, vmem_limit_bytes=VMEM_LIMIT)


def _pair_swap(a):
    lane = lax.broadcasted_iota(jnp.int32, a.shape, 1)
    return jnp.where(lane % 2 == 0, pltpu.roll(a, LANES - 1, 1), pltpu.roll(a, 1, 1))


def _norm_proj_kernel(x_ref, g_ref, sh_ref, sc_ref, w_ref, cos_ref, sin_ref, o_ref, *,
                      n_valid, pattern):
    x = x_ref[0].astype(F32)
    ms = jnp.sum(x * x, axis=-1, keepdims=True) * (1.0 / n_valid)
    y = x * lax.rsqrt(ms + NORM_EPS) * g_ref[...]
    h = (y * (1.0 + sc_ref[0]) + sh_ref[0]).astype(BF16)
    acc = jnp.dot(h, w_ref[...], preferred_element_type=F32)
    if pattern is None:
        o_ref[0] = acc.astype(o_ref.dtype)
    else:
        for c, tbl in enumerate(pattern):
            a = acc[:, c * LANES:(c + 1) * LANES]
            if tbl >= 0:
                a = a * cos_ref[tbl] + _pair_swap(a) * sin_ref[tbl]
            o_ref[0, :, c * LANES:(c + 1) * LANES] = a.astype(o_ref.dtype)


def _norm_proj(x, g, shift, scale, w, *, seg_tiles, tn, out_dtype=BF16, n_valid=None,
               rope=None, pattern=None):
    b, t, k = x.shape
    n = w.shape[1]
    tm = ROW_TILE
    assert t % tm == 0 and n % tn == 0 and tn % LANES == 0
    if rope is None:
        cos = sin = jnp.zeros((1, tm, LANES), F32)
        tbl_map = lambda j, bi, i: (0, 0, 0)
    else:
        cos, sin = rope
        tbl_map = lambda j, bi, i: (0, i, 0)
    ntab = cos.shape[0]
    kern = functools.partial(_norm_proj_kernel, n_valid=float(n_valid or k), pattern=pattern)
    mod_map = lambda j, bi, i: (bi * 2 + jnp.where(i >= seg_tiles, 1, 0), 0, 0)
    return pl.pallas_call(
        kern,
        grid=(n // tn, b, t // tm),
        in_specs=[
            pl.BlockSpec((1, tm, k), lambda j, bi, i: (bi, i, 0)),
            pl.BlockSpec((1, k), lambda j, bi, i: (0, 0)),
            pl.BlockSpec((1, 1, k), mod_map),
            pl.BlockSpec((1, 1, k), mod_map),
            pl.BlockSpec((k, tn), lambda j, bi, i: (0, j)),
            pl.BlockSpec((ntab, tm, LANES), tbl_map),
            pl.BlockSpec((ntab, tm, LANES), tbl_map),
        ],
        out_specs=pl.BlockSpec((1, tm, tn), lambda j, bi, i: (bi, i, j)),
        out_shape=jax.ShapeDtypeStruct((b, t, n), out_dtype),
        compiler_params=_cparams(("parallel", "parallel", "parallel")),
    )(x, g.reshape(1, k).astype(F32), shift, scale, w, cos, sin)


def _attn_kernel(q_ref, k_ref, v_ref, lam_ref, dn_ref, o_ref, q_scr, s_scr, m_scr, l_scr, acc_scr, *,
                 scale, diff, lam_init, ctx_tiles, ctx_len, tq, tk, t):
    i = pl.program_id(2)
    q = q_ref[0].astype(F32) * (scale * LOG2E)
    if diff:
        lane = lax.broadcasted_iota(jnp.int32, q.shape, 1)
        half = q.shape[1] // 2
        q_scr[0:tq] = jnp.where(lane < half, q, 0.0).astype(BF16)
        q_scr[tq:2 * tq] = jnp.where(lane >= half, q, 0.0).astype(BF16)
    else:
        q_scr[...] = q.astype(BF16)

    def attend(kv_len, chunk):
        n_chunks = kv_len // chunk
        groups = chunk // LANES

        def pass1(c, carry):
            k = k_ref[0, pl.ds(pl.multiple_of(c * chunk, chunk), chunk), :]
            s = lax.dot_general(q_scr[...], k, (((1,), (1,)), ((), ())), preferred_element_type=F32)
            s_scr[c, :, 0:chunk] = s
            m = m_scr[...]
            for g in range(groups):
                m = jnp.maximum(m, s[:, g * LANES:(g + 1) * LANES])
            m_scr[...] = m
            return carry

        def pass2(c, carry):
            s = s_scr[c, :, 0:chunk]
            p = jnp.exp2(s - jnp.concatenate([m_scr[...]] * groups, axis=1))
            l = l_scr[...]
            for g in range(groups):
                l = l + p[:, g * LANES:(g + 1) * LANES]
            l_scr[...] = l
            v = v_ref[0, pl.ds(pl.multiple_of(c * chunk, chunk), chunk), :]
            acc_scr[...] += jnp.dot(p.astype(BF16), v, preferred_element_type=F32)
            return carry

        m_scr[...] = jnp.full(m_scr.shape, -jnp.inf, F32)
        if n_chunks == 1:
            pass1(0, 0)
        else:
            lax.fori_loop(0, n_chunks, pass1, 0)
        m_scr[...] = jnp.broadcast_to(jnp.max(m_scr[...], axis=-1, keepdims=True), m_scr.shape)
        l_scr[...] = jnp.zeros(l_scr.shape, F32)
        acc_scr[...] = jnp.zeros(acc_scr.shape, F32)
        if n_chunks == 1:
            pass2(0, 0)
        else:
            lax.fori_loop(0, n_chunks, pass2, 0)

        o = acc_scr[...] / jnp.sum(l_scr[...], axis=-1, keepdims=True)
        if diff:
            lv = lam_ref[...]
            lam = (jnp.exp(jnp.sum(lv[0:1] * lv[1:2], axis=-1, keepdims=True))
                   - jnp.exp(jnp.sum(lv[2:3] * lv[3:4], axis=-1, keepdims=True)) + lam_init)
            a = o[0:tq] - lam * o[tq:2 * tq]
            ms = jnp.mean(a * a, axis=-1, keepdims=True)
            a = a * lax.rsqrt(ms + NORM_EPS) * dn_ref[...] * (1.0 - lam_init)
            o_ref[0] = a.astype(o_ref.dtype)
        else:
            o_ref[0] = o.astype(o_ref.dtype)

    if ctx_tiles:
        pl.when(i < ctx_tiles)(lambda: attend(ctx_len, ctx_len))
        pl.when(i >= ctx_tiles)(lambda: attend(t, tk))
    else:
        attend(t, tk)


def _attention(q, k, v, *, heads, dq, dv, q_col0, k_col0, v_col0, tq, scale, tk, diff=False,
               lam_vec=None, d_norm=None, lam_init=0.0, ctx_tiles=0, ctx_len=0):
    b, t, _ = k.shape
    sq = q.shape[1]
    assert t % tk == 0 and sq % tq == 0 and tk % LANES == 0 and ctx_len % LANES == 0
    assert ctx_len <= tk
    rows = 2 * tq if diff else tq
    if lam_vec is None:
        lam_vec = jnp.zeros((4, DIFF_HD), F32)
        d_norm = jnp.zeros((dv,), F32)
    kern = functools.partial(_attn_kernel, scale=scale, diff=diff, lam_init=lam_init,
                             ctx_tiles=ctx_tiles, ctx_len=ctx_len, tq=tq, tk=tk, t=t)
    return pl.pallas_call(
        kern,
        grid=(b, heads, sq // tq),
        in_specs=[
            pl.BlockSpec((1, tq, dq), lambda bi, h, i: (bi, i, q_col0 + h)),
            pl.BlockSpec((1, t, dq), lambda bi, h, i: (bi, 0, k_col0 + h)),
            pl.BlockSpec((1, t, dv), lambda bi, h, i: (bi, 0, v_col0 + h)),
            pl.BlockSpec((4, DIFF_HD), lambda bi, h, i: (0, 0)),
            pl.BlockSpec((1, dv), lambda bi, h, i: (0, 0)),
        ],
        out_specs=pl.BlockSpec((1, tq, dv), lambda bi, h, i: (bi, i, h)),
        out_shape=jax.ShapeDtypeStruct((b, sq, heads * dv), BF16),
        scratch_shapes=[pltpu.VMEM((rows, dq), BF16), pltpu.VMEM((t // tk, rows, tk), F32),
                        pltpu.VMEM((rows, LANES), F32), pltpu.VMEM((rows, LANES), F32),
                        pltpu.VMEM((rows, dv), F32)],
        compiler_params=_cparams(("parallel", "parallel", "arbitrary")),
    )(q, k, v, lam_vec.astype(F32), d_norm.reshape(1, dv).astype(F32))


def _pack_bf16_pairs(x):
    h = x.shape[1] // 2
    lo = lax.bitcast_convert_type(x[:, :h].astype(F32), jnp.uint32) >> 16
    hi = lax.bitcast_convert_type(x[:, h:].astype(F32), jnp.uint32) & jnp.uint32(0xFFFF0000)
    return hi | lo


def _unpack_bf16_pairs(w):
    lo = lax.bitcast_convert_type(w << 16, F32)
    hi = lax.bitcast_convert_type(w & jnp.uint32(0xFFFF0000), F32)
    return jnp.concatenate([lo.astype(BF16), hi.astype(BF16)], axis=1)


def _swiglu(x, wg, wu, wd):
    g = jnp.dot(x, wg, preferred_element_type=F32)
    u = jnp.dot(x, wu, preferred_element_type=F32)
    a = (g * jax.nn.sigmoid(g) * u).astype(BF16)
    return jnp.dot(a, wd, preferred_element_type=F32)


def _expert_kernel(be_ref, nu_ref, x_ref, wg_ref, wu_ref, wd_ref, o_ref, wg_s, wu_s, wd_s):
    i = pl.program_id(0)

    @pl.when(i < nu_ref[0])
    def _():
        @pl.when((i == 0) | (be_ref[i] != be_ref[jnp.maximum(i - 1, 0)]))
        def _():
            wg_s[...] = wg_ref[0, 0].astype(BF16)
            wu_s[...] = wu_ref[0, 0].astype(BF16)
            wd_s[...] = wd_ref[0, 0].astype(BF16)

        y = _swiglu(_unpack_bf16_pairs(x_ref[...]), wg_s[...], wu_s[...], wd_s[...])
        o_ref[...] = _pack_bf16_pairs(y.astype(BF16))

    @pl.when(i >= nu_ref[0])
    def _():
        o_ref[...] = jnp.zeros(o_ref.shape, o_ref.dtype)


def _expert_ffn(x, block_e, n_used, wg, wu, wd, *, layer, tm):
    n, dh = x.shape
    d = 2 * dh
    ff = wg.shape[3]
    assert n % tm == 0

    def blk(i, be, nu):
        return jnp.minimum(i, nu[0] - 1)

    grid_spec = pltpu.PrefetchScalarGridSpec(
        num_scalar_prefetch=2,
        grid=(n // tm,),
        in_specs=[
            pl.BlockSpec((tm, dh), lambda i, be, nu: (blk(i, be, nu), 0)),
            pl.BlockSpec((1, 1, d, ff), lambda i, be, nu: (layer, be[blk(i, be, nu)], 0, 0)),
            pl.BlockSpec((1, 1, d, ff), lambda i, be, nu: (layer, be[blk(i, be, nu)], 0, 0)),
            pl.BlockSpec((1, 1, ff, d), lambda i, be, nu: (layer, be[blk(i, be, nu)], 0, 0)),
        ],
        out_specs=pl.BlockSpec((tm, dh), lambda i, be, nu: (i, 0)),
        scratch_shapes=[pltpu.VMEM((d, ff), BF16), pltpu.VMEM((d, ff), BF16), pltpu.VMEM((ff, d), BF16)],
    )
    return pl.pallas_call(
        _expert_kernel,
        grid_spec=grid_spec,
        out_shape=jax.ShapeDtypeStruct((n, dh), jnp.uint32),
        compiler_params=_cparams(("arbitrary",)),
    )(block_e, n_used, x, wg, wu, wd)


def _post_mix_kernel(mix_ref, x_ref, w_ref, gp_ref, gf_ref, g1_ref, sh_ref, sc_ref, rw_ref,
                     xa_ref, f_ref, lg_ref):
    y = jnp.dot(mix_ref[0], w_ref[...], preferred_element_type=F32)
    yn = y * lax.rsqrt(jnp.mean(y * y, axis=-1, keepdims=True) + NORM_EPS) * gp_ref[...]
    xa = x_ref[0] + g1_ref[0] * yn
    xa_ref[0] = xa
    fn = xa * lax.rsqrt(jnp.mean(xa * xa, axis=-1, keepdims=True) + NORM_EPS) * gf_ref[...]
    f = (fn * (1.0 + sc_ref[0]) + sh_ref[0]).astype(BF16)
    f_ref[0] = _pack_bf16_pairs(f)
    lg_ref[...] = lax.dot_general(rw_ref[...], f, (((1,), (1,)), ((), ())), preferred_element_type=F32)


def _post_mix(mix, xs, w_out, g_post, g_ffn, g1, sh2, sc2, router_wt, *, row_tile0, seg_tiles):
    b, rows, k = mix.shape
    d = w_out.shape[1]
    e = router_wt.shape[0]
    tm = ROW_TILE
    nt = rows // tm
    mod_map = lambda bi, i: (bi * 2 + jnp.where(i + row_tile0 >= seg_tiles, 1, 0), 0, 0)
    vec = lambda: pl.BlockSpec((1, d), lambda bi, i: (0, 0))
    return pl.pallas_call(
        _post_mix_kernel,
        grid=(b, nt),
        in_specs=[
            pl.BlockSpec((1, tm, k), lambda bi, i: (bi, i, 0)),
            pl.BlockSpec((1, tm, d), lambda bi, i: (bi, i + row_tile0, 0)),
            pl.BlockSpec((k, d), lambda bi, i: (0, 0)),
            vec(), vec(),
            pl.BlockSpec((1, 1, d), mod_map), pl.BlockSpec((1, 1, d), mod_map),
            pl.BlockSpec((1, 1, d), mod_map),
            pl.BlockSpec((e, d), lambda bi, i: (0, 0)),
        ],
        out_specs=[
            pl.BlockSpec((1, tm, d), lambda bi, i: (bi, i, 0)),
            pl.BlockSpec((1, tm, d // 2), lambda bi, i: (bi, i, 0)),
            pl.BlockSpec((e, tm), lambda bi, i: (0, bi * nt + i)),
        ],
        out_shape=[jax.ShapeDtypeStruct((b, rows, d), F32),
                   jax.ShapeDtypeStruct((b, rows, d // 2), jnp.uint32),
                   jax.ShapeDtypeStruct((e, b * rows), F32)],
        compiler_params=_cparams(("parallel", "arbitrary")),
    )(mix, xs, w_out, g_post.reshape(1, d).astype(F32), g_ffn.reshape(1, d).astype(F32),
      g1, sh2, sc2, router_wt)


def _route_kernel(lg_ref, rb_ref, e_ref, g_ref):
    per = N_EXPERTS // N_GROUPS
    tn = lg_ref.shape[1]
    neg = -jnp.inf
    r_io = lax.broadcasted_iota(jnp.int32, (per, tn), 0)
    scores, choice, gs = [], [], []
    for g in range(N_GROUPS):
        sg = jax.nn.sigmoid(lg_ref[g * per:(g + 1) * per, :])
        cg = sg + rb_ref[g * per:(g + 1) * per, :]
        m1 = jnp.max(cg, axis=0, keepdims=True)
        i1 = jnp.min(jnp.where(cg == m1, r_io, per), axis=0, keepdims=True)
        m2 = jnp.max(jnp.where(r_io == i1, neg, cg), axis=0, keepdims=True)
        scores.append(sg)
        choice.append(cg)
        gs.append(m1 + m2)
    masked = []
    for g in range(N_GROUPS):
        ahead = jnp.zeros((1, tn), jnp.int32)
        for o in range(N_GROUPS):
            if o < g:
                ahead = ahead + jnp.where(gs[o] >= gs[g], 1, 0)
            elif o > g:
                ahead = ahead + jnp.where(gs[o] > gs[g], 1, 0)
        masked.append(jnp.where(ahead < TOPK_GROUPS, choice[g], neg))
    ids, gates = [], []
    for _ in range(TOP_K):
        best = masked[0]
        for g in range(1, N_GROUPS):
            best = jnp.maximum(best, masked[g])
        best = jnp.max(best, axis=0, keepdims=True)
        cand = jnp.where(masked[0] == best, r_io, N_EXPERTS)
        for g in range(1, N_GROUPS):
            cand = jnp.minimum(cand, jnp.where(masked[g] == best, r_io + g * per, N_EXPERTS))
        idx = jnp.min(cand, axis=0, keepdims=True)
        gk = jnp.zeros((per, tn), F32)
        for g in range(N_GROUPS):
            hit = (r_io + g * per) == idx
            gk = gk + jnp.where(hit, scores[g], 0.0)
            masked[g] = jnp.where(hit, neg, masked[g])
        ids.append(idx)
        gates.append(jnp.sum(gk, axis=0, keepdims=True))
    total = gates[0]
    for k in range(1, TOP_K):
        total = total + gates[k]
    pad = 8 - TOP_K
    e_ref[...] = jnp.concatenate(ids + [jnp.zeros((pad, tn), jnp.int32)], axis=0)
    g_ref[...] = jnp.concatenate([gk / total * ROUTED_SCALE for gk in gates]
                                 + [jnp.zeros((pad, tn), F32)], axis=0)


def _route(logits_t, router_b, *, tn):
    e, n = logits_t.shape
    assert n % tn == 0
    return pl.pallas_call(
        _route_kernel,
        grid=(n // tn,),
        in_specs=[pl.BlockSpec((e, tn), lambda i: (0, i)),
                  pl.BlockSpec((e, 1), lambda i: (0, 0))],
        out_specs=[pl.BlockSpec((8, tn), lambda i: (0, i)), pl.BlockSpec((8, tn), lambda i: (0, i))],
        out_shape=[jax.ShapeDtypeStruct((8, n), jnp.int32), jax.ShapeDtypeStruct((8, n), F32)],
        compiler_params=_cparams(("parallel",)),
    )(logits_t, router_b.reshape(e, 1).astype(F32))


def _dispatch_kernel(e_ref, dest_ref, be_ref, nu_ref, cnt_scr, start_scr, run_scr, *, blk):
    ph = pl.program_id(0)
    i = pl.program_id(1)
    tn = e_ref.shape[1]
    e_io = lax.broadcasted_iota(jnp.int32, (N_EXPERTS, tn), 0)
    hot = jnp.zeros((N_EXPERTS, tn), F32)
    for k in range(TOP_K):
        hot = hot + jnp.where(e_io == e_ref[k:k + 1, :], 1.0, 0.0)
    tile_cnt = jnp.sum(hot, axis=1, keepdims=True)

    @pl.when((ph == 0) & (i == 0))
    def _():
        cnt_scr[...] = jnp.zeros(cnt_scr.shape, F32)

    @pl.when(ph == 0)
    def _():
        cnt_scr[...] += jnp.broadcast_to(tile_cnt, cnt_scr.shape)

    @pl.when((ph == 1) & (i == 0))
    def _():
        bpe = jnp.floor((cnt_scr[...] + (blk - 1.0)) * (1.0 / blk))
        r = lax.broadcasted_iota(jnp.int32, (N_EXPERTS, N_EXPERTS), 0)
        c = lax.broadcasted_iota(jnp.int32, (N_EXPERTS, N_EXPERTS), 1)
        lower = jnp.where(c < r, 1.0, 0.0)
        before = jnp.dot(lower, bpe, precision=lax.Precision.HIGHEST, preferred_element_type=F32)
        start_scr[...] = before * blk
        run_scr[...] = jnp.zeros(run_scr.shape, F32)
        ends = (before + bpe)[:, 0:1]
        nb = be_ref.shape[1]
        bid = lax.broadcasted_iota(jnp.int32, (N_EXPERTS, nb), 1).astype(F32)
        be = jnp.sum(jnp.where(ends <= bid, 1, 0), axis=0, keepdims=True)
        be_ref[...] = jnp.minimum(be, N_EXPERTS - 1).astype(jnp.int32)
        nu_ref[...] = jnp.broadcast_to(jnp.max(ends, axis=0, keepdims=True), nu_ref.shape).astype(jnp.int32)

    @pl.when(ph == 1)
    def _():
        rr = lax.broadcasted_iota(jnp.int32, (tn, tn), 0)
        cc = lax.broadcasted_iota(jnp.int32, (tn, tn), 1)
        upper = jnp.where(rr < cc, 1.0, 0.0).astype(BF16)
        prior = jnp.dot(hot.astype(BF16), upper, preferred_element_type=F32)
        pos = prior + jnp.concatenate([start_scr[...] + run_scr[...]] * (tn // LANES), axis=1)
        rows = []
        for k in range(TOP_K):
            rows.append(jnp.sum(jnp.where(e_io == e_ref[k:k + 1, :], pos, 0.0), axis=0, keepdims=True))
        rows.append(jnp.zeros((8 - TOP_K, tn), F32))
        dest_ref[0] = jnp.concatenate(rows, axis=0).astype(jnp.int32)
        run_scr[...] += jnp.broadcast_to(tile_cnt, run_scr.shape)


def _dispatch(top_e, *, blk, n_blocks):
    n = top_e.shape[1]
    tn = LANES
    nbp = -(-n_blocks // LANES) * LANES
    kern = functools.partial(_dispatch_kernel, blk=blk)
    return pl.pallas_call(
        kern,
        grid=(2, n // tn),
        in_specs=[pl.BlockSpec((8, tn), lambda ph, i: (0, i))],
        out_specs=[pl.BlockSpec((1, 8, tn), lambda ph, i: (i * ph, 0, 0)),
                   pl.BlockSpec((1, nbp), lambda ph, i: (0, 0)),
                   pl.BlockSpec((8, LANES), lambda ph, i: (0, 0))],
        out_shape=[jax.ShapeDtypeStruct((n // tn, 8, tn), jnp.int32),
                   jax.ShapeDtypeStruct((1, nbp), jnp.int32),
                   jax.ShapeDtypeStruct((8, LANES), jnp.int32)],
        scratch_shapes=[pltpu.VMEM((N_EXPERTS, LANES), F32)] * 3,
        compiler_params=_cparams(("arbitrary", "arbitrary")),
    )(top_e)


def _scatter_rows_kernel(dest_ref, x_ref, zero_ref, o_ref, sem):
    del zero_ref
    tm = x_ref.shape[0]

    def row_copy(t, k):
        return pltpu.make_async_copy(x_ref.at[pl.ds(t, 1)], o_ref.at[pl.ds(dest_ref[0, k, t], 1)], sem)

    def issue(t, carry):
        for k in range(TOP_K):
            row_copy(t, k).start()
        return carry

    def drain(t, carry):
        for k in range(TOP_K):
            row_copy(t, k).wait()
        return carry

    lax.fori_loop(0, tm, issue, 0)
    lax.fori_loop(0, tm, drain, 0)


def _scatter_rows(dest, x, n_rows):
    n, dh = x.shape
    tm = LANES
    zeros = jnp.zeros((n_rows, dh), x.dtype)
    return pl.pallas_call(
        _scatter_rows_kernel,
        grid=(n // tm,),
        in_specs=[pl.BlockSpec((1, 8, tm), lambda i: (i, 0, 0), memory_space=pltpu.SMEM),
                  pl.BlockSpec((tm, dh), lambda i: (i, 0)),
                  pl.BlockSpec(memory_space=pl.ANY)],
        out_specs=pl.BlockSpec(memory_space=pl.ANY),
        out_shape=jax.ShapeDtypeStruct((n_rows, dh), x.dtype),
        scratch_shapes=[pltpu.SemaphoreType.DMA(())],
        input_output_aliases={2: 0},
        compiler_params=_cparams(("arbitrary",)),
    )(dest, x, zeros)


def _combine_kernel(dest_ref, y_ref, gate_ref, f_ref, xa_ref, sg_ref, su_ref, sd_ref, gp_ref, g2_ref,
                    o_ref, buf, sem):
    tm = f_ref.shape[0]

    def row_copy(t, k):
        return pltpu.make_async_copy(y_ref.at[pl.ds(dest_ref[0, k, t], 1)], buf.at[k, pl.ds(t, 1)], sem)

    def issue(t, carry):
        for k in range(TOP_K):
            row_copy(t, k).start()
        return carry

    def drain(t, carry):
        for k in range(TOP_K):
            row_copy(t, k).wait()
        return carry

    lax.fori_loop(0, tm, issue, 0)
    f = _swiglu(_unpack_bf16_pairs(f_ref[...]), sg_ref[...], su_ref[...], sd_ref[...])
    lax.fori_loop(0, tm, drain, 0)
    gate = gate_ref[...]
    for k in range(TOP_K):
        f = f + gate[:, k:k + 1] * _unpack_bf16_pairs(buf[k]).astype(F32)
    fn = f * lax.rsqrt(jnp.mean(f * f, axis=-1, keepdims=True) + NORM_EPS) * gp_ref[...]
    o_ref[0] = xa_ref[0] + g2_ref[0] * fn


def _combine(dest, y, gate, f_pk, xa, sg, su, sd, g_post, g2, *, row_tile0, seg_tiles):
    b, rows, d = xa.shape
    tm = LANES
    nt = rows // tm
    per = ROW_TILE // tm
    mod_map = lambda bi, i: (bi * 2 + jnp.where(i // per + row_tile0 >= seg_tiles, 1, 0), 0, 0)
    const = lambda shape: pl.BlockSpec(shape, lambda bi, i: (0,) * len(shape))
    return pl.pallas_call(
        _combine_kernel,
        grid=(b, nt),
        in_specs=[
            pl.BlockSpec((1, 8, tm), lambda bi, i: (bi * nt + i, 0, 0), memory_space=pltpu.SMEM),
            pl.BlockSpec(memory_space=pl.ANY),
            pl.BlockSpec((tm, 8), lambda bi, i: (bi * nt + i, 0)),
            pl.BlockSpec((tm, d // 2), lambda bi, i: (bi * nt + i, 0)),
            pl.BlockSpec((1, tm, d), lambda bi, i: (bi, i, 0)),
            const(sg.shape), const(su.shape), const(sd.shape), const((1, d)),
            pl.BlockSpec((1, 1, d), mod_map),
        ],
        out_specs=pl.BlockSpec((1, tm, d), lambda bi, i: (bi, i, 0)),
        out_shape=jax.ShapeDtypeStruct((b, rows, d), F32),
        scratch_shapes=[pltpu.VMEM((TOP_K, tm, d // 2), jnp.uint32), pltpu.SemaphoreType.DMA(())],
        compiler_params=_cparams(("arbitrary", "arbitrary")),
    )(dest, y, gate, f_pk, xa, sg, su, sd, g_post.reshape(1, d).astype(F32), g2)


def _rope_tables(s, ctx_len, dim):
    rows = s // GRID_W
    row = jnp.repeat(jnp.arange(rows), GRID_W)
    col = jnp.tile(jnp.arange(GRID_W), rows)
    quarter = dim // 4
    inv = ROPE_BASE ** (-jnp.arange(quarter, dtype=F32) / quarter)
    ang = jnp.concatenate([row.astype(F32)[:, None] * inv, col.astype(F32)[:, None] * inv], axis=-1)
    cos = jnp.repeat(jnp.cos(ang), 2, axis=-1)
    sin = jnp.repeat(jnp.sin(ang), 2, axis=-1) * jnp.tile(jnp.array([-1.0, 1.0], F32), dim // 2)
    cos = jnp.concatenate([jnp.ones((ctx_len, dim), F32), cos], axis=0)
    sin = jnp.concatenate([jnp.zeros((ctx_len, dim), F32), sin], axis=0)
    return cos, sin


def _log_sigmoid(x):
    return jnp.minimum(x, 0.0) - jnp.log1p(jnp.exp(-jnp.abs(x)))


def _mlstm_kernel(q_ref, k_ref, kt_ref, v_ref, g_ref, gt_ref, gb_ref, gbt_ref, hf_ref, mo_ref, mn_ref,
                  o_ref, s_scr, m_scr, *, reverse, final):
    L = MLSTM_CHUNK
    hv = MLSTM_V
    io, fo = (2 * MLSTM_HEADS, 3 * MLSTM_HEADS) if reverse else (0, MLSTM_HEADS)

    @pl.when(pl.program_id(1) == 0)
    def _():
        s_scr[...] = jnp.zeros(s_scr.shape, F32)
        m_scr[...] = jnp.zeros(m_scr.shape, F32)

    g = g_ref[0] + gb_ref[...]
    gt = gt_ref[0] + gbt_ref[...]
    r_io = lax.broadcasted_iota(jnp.int32, (L, L), 0)
    c_io = lax.broadcasted_iota(jnp.int32, (L, L), 1)
    seen = (c_io >= r_io) if reverse else (c_io <= r_io)
    tri = jnp.where(seen, 1.0, 0.0)
    hi = lax.Precision.HIGHEST
    bc_col = jnp.dot(tri, _log_sigmoid(g), precision=hi, preferred_element_type=F32)
    lf_row = _log_sigmoid(gt)
    bc_row = lax.dot_general(lf_row, tri, (((1,), (1,)), ((), ())), precision=hi,
                             preferred_element_type=F32)
    lane = lax.broadcasted_iota(jnp.int32, (L, LANES), 1)
    sub = lax.broadcasted_iota(jnp.int32, (LANES, L), 0)

    for h in range(MLSTM_HEADS):
        pair, odd = h // 2, h % 2
        lo = odd * MLSTM_QK
        a_col = bc_col[:, fo + h:fo + h + 1]
        i_col = g[:, io + h:io + h + 1]
        b_row = bc_row[fo + h:fo + h + 1, :]
        i_row = gt[io + h:io + h + 1, :]
        btot = jnp.sum(lf_row[fo + h:fo + h + 1, :], axis=1, keepdims=True)
        m_st = m_scr[h:h + 1, 0:1]

        w_end = btot - a_col + i_col
        m_new = jnp.maximum(btot + m_st, jnp.max(w_end, axis=0, keepdims=True))
        decay = jnp.exp(btot + m_st - m_new)
        w_k = jnp.exp(w_end - m_new)

        log_d = jnp.where(seen, a_col - b_row + i_row, -jnp.inf)
        log_inter = a_col + m_st
        m_row = jnp.maximum(log_inter, jnp.max(log_d, axis=1, keepdims=True))
        w_intra = jnp.exp(log_d - m_row)
        w_inter = jnp.exp(log_inter - m_row)

        in_head = (lane >= lo) & (lane < lo + MLSTM_QK)
        qm = jnp.where(in_head, q_ref[0, :, pair * LANES:(pair + 1) * LANES], 0).astype(BF16)
        kp = k_ref[0, :, pair * LANES:(pair + 1) * LANES]
        v = v_ref[0, :, h * hv:(h + 1) * hv]
        state = s_scr[h]

        qk = lax.dot_general(qm, kp, (((1,), (1,)), ((), ())), preferred_element_type=F32)
        qk = qk * (MLSTM_QK ** -0.5) * w_intra
        inter = jnp.dot(qm, state.astype(BF16), preferred_element_type=F32)
        num = w_inter * inter[:, :hv] + jnp.dot(qk.astype(BF16), v, preferred_element_type=F32)
        den = w_inter * inter[:, hv:hv + 1] + jnp.sum(qk, axis=1, keepdims=True)
        out = num / jnp.maximum(jnp.abs(den), jnp.exp(-m_row))

        wv = jnp.concatenate([(w_k * v.astype(F32)).astype(BF16),
                              jnp.where(lane == 0, w_k, 0.0).astype(BF16)], axis=1)
        kt = kt_ref[0, pair * LANES:(pair + 1) * LANES, :]
        in_rows = (sub >= lo) & (sub < lo + MLSTM_QK)
        ktm = (jnp.where(in_rows, kt, 0).astype(F32) * (MLSTM_QK ** -0.5)).astype(BF16)
        s_scr[h] = decay * state + jnp.dot(ktm, wv, preferred_element_type=F32)
        m_scr[h:h + 1, :] = jnp.broadcast_to(m_new, (1, LANES))

        if final:
            tot = out + hf_ref[0, :, h * hv:(h + 1) * hv]
            nrm = tot * lax.rsqrt(jnp.mean(tot * tot, axis=-1, keepdims=True) + NORM_EPS) * mn_ref[...]
            gate = jax.nn.sigmoid(mo_ref[0, :, h * hv:(h + 1) * hv].astype(F32))
            o_ref[0, :, h * hv:(h + 1) * hv] = (nrm * gate).astype(o_ref.dtype)
        else:
            o_ref[0, :, h * hv:(h + 1) * hv] = out.astype(o_ref.dtype)


def _mlstm(pv, kt, gates, gates_t, gate_b, m_norm, hf, *, ctx_chunks, reverse):
    b, t, _ = pv.shape
    L = MLSTM_CHUNK
    nc = t // L
    final = hf is not None
    nq = MLSTM_HEADS * MLSTM_QK
    nv = MLSTM_HEADS * MLSTM_V

    def chunk(j):
        if not reverse:
            return j
        return jnp.where(j < ctx_chunks, ctx_chunks - 1 - j, nc - 1 - (j - ctx_chunks))

    gb = jnp.pad(gate_b.astype(F32).reshape(-1), (0, LANES - gate_b.size))
    if hf is None:
        hf = jnp.zeros((1, L, nv), F32)
        hf_spec = pl.BlockSpec((1, L, nv), lambda bi, j: (0, 0, 0))
    else:
        hf_spec = pl.BlockSpec((1, L, nv), lambda bi, j: (bi, chunk(j), 0))
    kern = functools.partial(_mlstm_kernel, reverse=reverse, final=final)
    return pl.pallas_call(
        kern,
        grid=(b, nc),
        in_specs=[
            pl.BlockSpec((1, L, nq), lambda bi, j: (bi, chunk(j), nv // nq)),
            pl.BlockSpec((1, L, nq), lambda bi, j: (bi, chunk(j), nv // nq + 1)),
            pl.BlockSpec((1, nq, L), lambda bi, j: (bi, 0, chunk(j))),
            pl.BlockSpec((1, L, nv), lambda bi, j: (bi, chunk(j), 2)),
            pl.BlockSpec((1, L, LANES), lambda bi, j: (bi, chunk(j), 0)),
            pl.BlockSpec((1, LANES, L), lambda bi, j: (bi, 0, chunk(j))),
            pl.BlockSpec((1, LANES), lambda bi, j: (0, 0)),
            pl.BlockSpec((LANES, 1), lambda bi, j: (0, 0)),
            hf_spec,
            pl.BlockSpec((1, L, nv), lambda bi, j: (bi, chunk(j), 3)),
            pl.BlockSpec((1, MLSTM_V), lambda bi, j: (0, 0)),
        ],
        out_specs=pl.BlockSpec((1, L, nv), lambda bi, j: (bi, chunk(j), 0)),
        out_shape=jax.ShapeDtypeStruct((b, t, nv), BF16 if final else F32),
        scratch_shapes=[pltpu.VMEM((MLSTM_HEADS, LANES, 2 * LANES), F32), pltpu.VMEM((MLSTM_HEADS, LANES), F32)],
        compiler_params=_cparams(("parallel", "arbitrary")),
    )(pv, pv, kt, pv, gates, gates_t, gb.reshape(1, LANES), gb.reshape(LANES, 1), hf, pv,
      m_norm.reshape(1, MLSTM_V).astype(F32))


def _swa_kernel(sink_ref, q_ref, kp_ref, kc_ref, kn_ref, kx_ref, vp_ref, vc_ref, vn_ref, vx_ref, o_ref, *,
                n_blocks):
    i = pl.program_id(1)
    bq = BLOCK_Q
    hd = SWA_HD
    rep = SWA_HEADS // SWA_KV_HEADS
    n_ctx = kx_ref.shape[1]
    rows = rep * bq
    scale = hd ** -0.5 * LOG2E
    row = lax.broadcasted_iota(jnp.int32, (rows, 3 * bq), 0)
    col = lax.broadcasted_iota(jnp.int32, (rows, 3 * bq), 1)
    dt = (col - bq) - (row % bq)
    blk = col // bq
    ok = (jnp.abs(dt) <= WINDOW) & ((blk != 0) | (i > 0)) & ((blk != 2) | (i < n_blocks - 1))
    head_of_row = lax.broadcasted_iota(jnp.int32, (rows, 1), 0) // bq
    for g in range(SWA_KV_HEADS):
        q = jnp.concatenate([q_ref[0, :, (g * rep + r) * hd:(g * rep + r + 1) * hd] for r in range(rep)], axis=0)
        q = (q.astype(F32) * scale).astype(BF16)
        cs = slice(g * hd, (g + 1) * hd)
        k_loc = jnp.concatenate([kp_ref[0, :, cs], kc_ref[0, :, cs], kn_ref[0, :, cs]], axis=0)
        v_loc = jnp.concatenate([vp_ref[0, :, cs], vc_ref[0, :, cs], vn_ref[0, :, cs]], axis=0)
        nt = (((1,), (1,)), ((), ()))
        s_loc = jnp.where(ok, lax.dot_general(q, k_loc, nt, preferred_element_type=F32), -jnp.inf)
        s_ctx = lax.dot_general(q, kx_ref[0, :, cs], nt, preferred_element_type=F32)
        sink = jnp.zeros((rows, 1), F32)
        for r in range(rep):
            sink = jnp.where(head_of_row == r, sink_ref[g * rep + r] * LOG2E, sink)
        m = jnp.maximum(jnp.maximum(jnp.max(s_loc, axis=-1, keepdims=True),
                                    jnp.max(s_ctx, axis=-1, keepdims=True)), sink)
        p_loc = jnp.exp2(s_loc - m)
        p_ctx = jnp.exp2(s_ctx - m)
        den = (jnp.sum(p_loc, axis=-1, keepdims=True) + jnp.sum(p_ctx, axis=-1, keepdims=True)
               + jnp.exp2(sink - m))
        out = (jnp.dot(p_loc.astype(BF16), v_loc, preferred_element_type=F32)
               + jnp.dot(p_ctx.astype(BF16), vx_ref[0, :, cs], preferred_element_type=F32)) / den
        for r in range(rep):
            o_ref[0, :, (g * rep + r) * hd:(g * rep + r + 1) * hd] = out[r * bq:(r + 1) * bq].astype(o_ref.dtype)


def _swa(qk, v, sink, *, n_ctx, k_col0):
    b, t, _ = qk.shape
    bq = BLOCK_Q
    cb = n_ctx // bq
    nb = (t - n_ctx) // bq
    nq = SWA_HEADS * SWA_HD
    nk = SWA_KV_HEADS * SWA_HD
    kcol = k_col0
    prev = lambda bi, i, c: (bi, i + cb - 1, c)
    cur = lambda bi, i, c: (bi, i + cb, c)
    nxt = lambda bi, i, c: (bi, jnp.minimum(i + cb + 1, nb + cb - 1), c)
    kern = functools.partial(_swa_kernel, n_blocks=nb)

    def spec(rows, fn, c):
        return pl.BlockSpec((1, rows, nk), lambda bi, i: fn(bi, i, c))

    ctx = lambda bi, i, c: (bi, 0, c)
    return pl.pallas_call(
        kern,
        grid=(b, nb),
        in_specs=[
            pl.BlockSpec(memory_space=pltpu.SMEM),
            pl.BlockSpec((1, bq, nq), lambda bi, i: (bi, i + cb, 0)),
            spec(bq, prev, kcol), spec(bq, cur, kcol), spec(bq, nxt, kcol), spec(n_ctx, ctx, kcol),
            spec(bq, prev, 0), spec(bq, cur, 0), spec(bq, nxt, 0), spec(n_ctx, ctx, 0),
        ],
        out_specs=pl.BlockSpec((1, bq, nq), lambda bi, i: (bi, i, 0)),
        out_shape=jax.ShapeDtypeStruct((b, nb * bq, nq), BF16),
        compiler_params=_cparams(("parallel", "arbitrary")),
    )(sink.astype(F32), qk, qk, qk, qk, qk, v, v, v, v)


def _mix_out_and_moe(mix, xs, w_out, g_post, g_ffn_pre, g_ffn_post, mods, router_w, router_b,
                     wg, wu, wd, sg, su, sd, *, layer, row_tile0, seg_tiles):
    sh1, sc1, g1, sh2, sc2, g2 = mods
    b, rows, _ = mix.shape
    n = b * rows
    seg = dict(row_tile0=row_tile0, seg_tiles=seg_tiles)
    xa, f_pk, logits_t = _post_mix(mix, xs, w_out.astype(BF16), g_post, g_ffn_pre, g1, sh2, sc2,
                                   router_w.T.astype(BF16), **seg)
    top_e, gate = _route(logits_t, router_b, tn=512)
    blk = EXPERT_ROWS
    n_blocks = -(-n * TOP_K // blk) + N_EXPERTS
    dest, block_e, n_used = _dispatch(top_e, blk=blk, n_blocks=n_blocks)
    f_pk = f_pk.reshape(n, -1)
    x_sorted = _scatter_rows(dest, f_pk, n_blocks * blk)
    y = _expert_ffn(x_sorted, block_e[0, :n_blocks], n_used[0, :1], wg, wu, wd, layer=layer, tm=blk)
    return _combine(dest, y, gate.T, f_pk, xa, sg.astype(BF16), su.astype(BF16), sd.astype(BF16),
                    g_ffn_post, g2, **seg)


def kernel(x, c, ctx, c_ctx, mod_w, mod_b, norm_mix_pre, norm_mix_post, norm_ffn_pre, norm_ffn_post, ab_w_in, ab_w_out, diff_lambda, diff_norm, mlstm_gate_b, mlstm_norm, cd_w_in, cd_w_out, mla_q_norm, mla_w_uq, mla_kv_norm, mla_w_ukv, swa_sink, router_w, router_b, exp_w_gate, exp_w_up, exp_w_down, sh_w_gate, sh_w_up, sh_w_down):
    b, s, d = x.shape
    L = ctx.shape[1]
    t = L + s
    assert L == ROW_TILE and s % ROW_TILE == 0
    depth = mod_w.shape[0]
    ctx_tiles = L // ROW_TILE
    tk = next(c for c in (2816, 768, ROW_TILE) if t % c == 0)

    cos64, sin64 = _rope_tables(s, L, DIFF_HD)
    cos64 = jnp.tile(cos64, (1, LANES // DIFF_HD))
    sin64 = jnp.tile(sin64, (1, LANES // DIFF_HD))
    cos128, sin128 = _rope_tables(s, L, SWA_HD)
    cos_kr = jnp.concatenate([cos64[:, :MLA_ROPE], jnp.ones((t, LANES - MLA_ROPE), F32)], axis=1)
    sin_kr = jnp.concatenate([sin64[:, :MLA_ROPE], jnp.zeros((t, LANES - MLA_ROPE), F32)], axis=1)

    xs = jnp.concatenate([ctx, x], axis=1)

    for layer in range(depth):
        with_ctx = layer < depth - 1
        j = layer // 2
        mod_l = jax.nn.silu(c) @ mod_w[layer] + mod_b[layer]
        mod_c = jax.nn.silu(c_ctx) @ mod_w[layer] + mod_b[layer]
        mods = jnp.stack([jnp.broadcast_to(mod_c, (b, 6 * d)), mod_l], axis=1).reshape(b * 2, 1, 6, d)
        sh1, sc1, g1, sh2, sc2, g2 = (mods[:, :, m] for m in range(6))

        proj = functools.partial(_norm_proj, xs, norm_mix_pre[layer], sh1, sc1, seg_tiles=ctx_tiles)
        if layer % 2 == 0:
            w_in = ab_w_in[j].astype(BF16)
            n_qk = 2 * DIFF_HEADS * 2 * DIFF_HD
            qk = proj(w_in[:, :n_qk], tn=2048, rope=(cos64[None], sin64[None]), pattern=(0,) * 16)
            pv = proj(w_in[:, n_qk:n_qk + AB_PLAIN], tn=2048)
            w_gates = jnp.pad(w_in[:, n_qk + AB_PLAIN:], ((0, 0), (0, LANES - 4 * MLSTM_HEADS)))
            gates = proj(w_gates, tn=LANES, out_dtype=F32)

            lam_init = 0.8 - 0.6 * math.exp(-0.3 * layer)
            a = _attention(qk if with_ctx else qk[:, L:], qk, pv, heads=DIFF_HEADS, dq=2 * DIFF_HD,
                           dv=DIFF_VD, q_col0=0, k_col0=DIFF_HEADS, v_col0=0, tq=ROW_TILE,
                           scale=DIFF_HD ** -0.5, tk=tk, diff=True, lam_vec=diff_lambda[j],
                           d_norm=diff_norm[j], lam_init=lam_init,
                           ctx_tiles=ctx_tiles if with_ctx else 0, ctx_len=L)
            o_mk = DIFF_HEADS * DIFF_VD + MLSTM_HEADS * MLSTM_QK
            kt = jnp.swapaxes(pv[..., o_mk:o_mk + MLSTM_HEADS * MLSTM_QK], 1, 2)
            scan = functools.partial(_mlstm, pv, kt, gates, jnp.swapaxes(gates, 1, 2), mlstm_gate_b[j],
                                     mlstm_norm[j], ctx_chunks=L // MLSTM_CHUNK)
            m = scan(scan(None, reverse=False), reverse=True)
            mix = jnp.concatenate([a, m if with_ctx else m[:, L:]], axis=-1)
            w_out = ab_w_out[j]
        else:
            assert not with_ctx
            w_in = cd_w_in[j].astype(BF16)
            c0 = Q_LORA
            c1 = c0 + KV_LORA
            c2 = c1 + MLA_ROPE
            c3 = c2 + SWA_HEADS * SWA_HD
            c4 = c3 + SWA_KV_HEADS * SWA_HD
            w_rope = jnp.concatenate([w_in[:, c2:c4], w_in[:, c1:c2],
                                      jnp.zeros((d, LANES - MLA_ROPE), BF16)], axis=1)
            n_rope = w_rope.shape[1]
            rp = proj(w_rope, tn=n_rope, rope=(jnp.stack([cos128, cos_kr]), jnp.stack([sin128, sin_kr])),
                      pattern=(0,) * (n_rope // LANES - 1) + (1,))
            q_pad = 512 - Q_LORA
            w_plain = jnp.concatenate([w_in[:, c4:], w_in[:, :c0], jnp.zeros((d, q_pad), BF16),
                                       w_in[:, c0:c1]], axis=1)
            pp = proj(w_plain, tn=w_plain.shape[1])
            n_sv = SWA_KV_HEADS * SWA_HD
            cq = pp[..., n_sv:n_sv + 512]
            ckv = pp[..., n_sv + 512:]

            hq = MLA_NOPE + MLA_ROPE
            w_uq = mla_w_uq[j].astype(BF16).reshape(Q_LORA, MLA_HEADS, hq)
            w_uq = jnp.pad(w_uq, ((0, q_pad), (0, 0), (0, 256 - hq))).reshape(512, MLA_HEADS * 256)
            no_mod = jnp.zeros((b * 2, 1, 512), F32)
            qn = jnp.pad(mla_q_norm[j], (0, q_pad))
            q_mla = _norm_proj(cq[:, L:], qn, no_mod, no_mod, w_uq, seg_tiles=0, tn=2048, n_valid=Q_LORA,
                               rope=(cos_kr[None, L:], sin_kr[None, L:]), pattern=(-1, 0) * 8)
            kv = _norm_proj(ckv, mla_kv_norm[j], no_mod, no_mod, mla_w_ukv[j].astype(BF16),
                            seg_tiles=ctx_tiles, tn=2048).reshape(b, t, MLA_HEADS, MLA_NOPE + MLA_V)
            kr = rp[..., n_rope - LANES:]
            k_mla = jnp.concatenate([kv[..., :MLA_NOPE],
                                     jnp.broadcast_to(kr[:, :, None, :], (b, t, MLA_HEADS, LANES))],
                                    axis=-1).reshape(b, t, MLA_HEADS * 256)
            v_mla = kv[..., MLA_NOPE:].reshape(b, t, MLA_HEADS * MLA_V)
            a = _attention(q_mla, k_mla, v_mla, heads=MLA_HEADS, dq=256, dv=MLA_V, q_col0=0, k_col0=0,
                           v_col0=0, tq=512, scale=MLA_SCALE, tk=tk)

            w = _swa(rp, pp, swa_sink[j], n_ctx=L, k_col0=SWA_HEADS * SWA_HD // n_sv)
            mix = jnp.concatenate([a, w], axis=-1)
            w_out = cd_w_out[j]

        xa = _mix_out_and_moe(mix, xs, w_out, norm_mix_post[layer], norm_ffn_pre[layer], norm_ffn_post[layer],
                              (sh1, sc1, g1, sh2, sc2, g2), router_w[layer], router_b[layer],
                              exp_w_gate, exp_w_up, exp_w_down,
                              sh_w_gate[layer], sh_w_up[layer], sh_w_down[layer],
                              layer=layer, row_tile0=0 if with_ctx else ctx_tiles, seg_tiles=ctx_tiles)
        xs = xa if with_ctx else jnp.concatenate([xs[:, :L], xa], axis=1)
    return xs[:, L:]
```

```python
import functools
import math

import jax
import jax.numpy as jnp
import numpy as np
from jax import lax
from jax.experimental import pallas as pl
from jax.experimental.pallas import tpu as pltpu

F32 = jnp.float32
BF16 = jnp.bfloat16

GRID_W = 64
ROPE_BASE = 10000.0
NORM_EPS = 1e-6

DIFF_HEADS = 8
DIFF_HD = 64
DIFF_VD = 2 * DIFF_HD
MLSTM_HEADS = 8
MLSTM_QK = 64
MLSTM_V = 128
MLSTM_CHUNK = 128
MLA_HEADS = 8
MLA_NOPE = 128
MLA_ROPE = 64
MLA_V = 128
Q_LORA = 448
KV_LORA = 512
MLA_SCALE = (MLA_NOPE + MLA_ROPE) ** -0.5
SWA_HEADS = 8
SWA_KV_HEADS = 2
SWA_HD = 128
WINDOW = 128
BLOCK_Q = 128
N_EXPERTS = 64
TOP_K = 6
N_GROUPS = 8
TOPK_GROUPS = 4
ROUTED_SCALE = 2.5
AB_PLAIN = DIFF_HEADS * DIFF_VD + 2 * MLSTM_HEADS * MLSTM_QK + 2 * MLSTM_HEADS * MLSTM_V

LANES = 128
LOG2E = 1.4426950408889634
VMEM_LIMIT = 56 * 1024 * 1024

ROW_TILE = 256
EXPERT_ROWS = 512


def _cparams(sem):
    return pltpu.CompilerParams(dimension_semantics=sem, vmem_limit_bytes=VMEM_LIMIT)


def _pair_swap(a):
    lane = lax.broadcasted_iota(jnp.int32, a.shape, 1)
    return jnp.where(lane % 2 == 0, pltpu.roll(a, LANES - 1, 1), pltpu.roll(a, 1, 1))


def _norm_proj_kernel(x_ref, g_ref, sh_ref, sc_ref, w_ref, cos_ref, sin_ref, o_ref, *,
                      n_valid, pattern):
    x = x_ref[0].astype(F32)
    ms = jnp.sum(x * x, axis=-1, keepdims=True) * (1.0 / n_valid)
    y = x * lax.rsqrt(ms + NORM_EPS) * g_ref[...]
    h = (y * (1.0 + sc_ref[0]) + sh_ref[0]).astype(BF16)
    acc = jnp.dot(h, w_ref[...], preferred_element_type=F32)
    if pattern is None:
        o_ref[0] = acc.astype(o_ref.dtype)
    else:
        for c, tbl in enumerate(pattern):
            a = acc[:, c * LANES:(c + 1) * LANES]
            if tbl >= 0:
                a = a * cos_ref[tbl] + _pair_swap(a) * sin_ref[tbl]
            o_ref[0, :, c * LANES:(c + 1) * LANES] = a.astype(o_ref.dtype)


def _norm_proj(x, g, shift, scale, w, *, seg_tiles, tn, out_dtype=BF16, n_valid=None,
               rope=None, pattern=None):
    b, t, k = x.shape
    n = w.shape[1]
    tm = ROW_TILE
    assert t % tm == 0 and n % tn == 0 and tn % LANES == 0
    if rope is None:
        cos = sin = jnp.zeros((1, tm, LANES), F32)
        tbl_map = lambda j, bi, i: (0, 0, 0)
    else:
        cos, sin = rope
        tbl_map = lambda j, bi, i: (0, i, 0)
    ntab = cos.shape[0]
    kern = functools.partial(_norm_proj_kernel, n_valid=float(n_valid or k), pattern=pattern)
    mod_map = lambda j, bi, i: (bi * 2 + jnp.where(i >= seg_tiles, 1, 0), 0, 0)
    return pl.pallas_call(
        kern,
        grid=(n // tn, b, t // tm),
        in_specs=[
            pl.BlockSpec((1, tm, k), lambda j, bi, i: (bi, i, 0)),
            pl.BlockSpec((1, k), lambda j, bi, i: (0, 0)),
            pl.BlockSpec((1, 1, k), mod_map),
            pl.BlockSpec((1, 1, k), mod_map),
            pl.BlockSpec((k, tn), lambda j, bi, i: (0, j)),
            pl.BlockSpec((ntab, tm, LANES), tbl_map),
            pl.BlockSpec((ntab, tm, LANES), tbl_map),
        ],
        out_specs=pl.BlockSpec((1, tm, tn), lambda j, bi, i: (bi, i, j)),
        out_shape=jax.ShapeDtypeStruct((b, t, n), out_dtype),
        compiler_params=_cparams(("parallel", "parallel", "parallel")),
    )(x, g.reshape(1, k).astype(F32), shift, scale, w, cos, sin)


def _attn_kernel(q_ref, k_ref, v_ref, lam_ref, dn_ref, o_ref, q_scr, s_scr, m_scr, l_scr, acc_scr, *,
                 scale, diff, lam_init, ctx_tiles, ctx_len, tq, tk, t):
    i = pl.program_id(2)
    q = q_ref[0].astype(F32) * (scale * LOG2E)
    if diff:
        lane = lax.broadcasted_iota(jnp.int32, q.shape, 1)
        half = q.shape[1] // 2
        q_scr[0:tq] = jnp.where(lane < half, q, 0.0).astype(BF16)
        q_scr[tq:2 * tq] = jnp.where(lane >= half, q, 0.0).astype(BF16)
    else:
        q_scr[...] = q.astype(BF16)

    def attend(kv_len, chunk):
        n_chunks = kv_len // chunk
        groups = chunk // LANES

        def pass1(c, carry):
            k = k_ref[0, pl.ds(pl.multiple_of(c * chunk, chunk), chunk), :]
            s = lax.dot_general(q_scr[...], k, (((1,), (1,)), ((), ())), preferred_element_type=F32)
            s_scr[c, :, 0:chunk] = s
            m = m_scr[...]
            for g in range(groups):
                m = jnp.maximum(m, s[:, g * LANES:(g + 1) * LANES])
            m_scr[...] = m
            return carry

        def pass2(c, carry):
            s = s_scr[c, :, 0:chunk]
            p = jnp.exp2(s - jnp.concatenate([m_scr[...]] * groups, axis=1))
            l = l_scr[...]
            for g in range(groups):
                l = l + p[:, g * LANES:(g + 1) * LANES]
            l_scr[...] = l
            v = v_ref[0, pl.ds(pl.multiple_of(c * chunk, chunk), chunk), :]
            acc_scr[...] += jnp.dot(p.astype(BF16), v, preferred_element_type=F32)
            return carry

        m_scr[...] = jnp.full(m_scr.shape, -jnp.inf, F32)
        if n_chunks == 1:
            pass1(0, 0)
        else:
            lax.fori_loop(0, n_chunks, pass1, 0)
        m_scr[...] = jnp.broadcast_to(jnp.max(m_scr[...], axis=-1, keepdims=True), m_scr.shape)
        l_scr[...] = jnp.zeros(l_scr.shape, F32)
        acc_scr[...] = jnp.zeros(acc_scr.shape, F32)
        if n_chunks == 1:
            pass2(0, 0)
        else:
            lax.fori_loop(0, n_chunks, pass2, 0)

        o = acc_scr[...] / jnp.sum(l_scr[...], axis=-1, keepdims=True)
        if diff:
            lv = lam_ref[...]
            lam = (jnp.exp(jnp.sum(lv[0:1] * lv[1:2], axis=-1, keepdims=True))
                   - jnp.exp(jnp.sum(lv[2:3] * lv[3:4], axis=-1, keepdims=True)) + lam_init)
            a = o[0:tq] - lam * o[tq:2 * tq]
            ms = jnp.mean(a * a, axis=-1, keepdims=True)
            a = a * lax.rsqrt(ms + NORM_EPS) * dn_ref[...] * (1.0 - lam_init)
            o_ref[0] = a.astype(o_ref.dtype)
        else:
            o_ref[0] = o.astype(o_ref.dtype)

    if ctx_tiles:
        pl.when(i < ctx_tiles)(lambda: attend(ctx_len, ctx_len))
        pl.when(i >= ctx_tiles)(lambda: attend(t, tk))
    else:
        attend(t, tk)


def _attention(q, k, v, *, heads, dq, dv, q_col0, k_col0, v_col0, tq, scale, tk, diff=False,
               lam_vec=None, d_norm=None, lam_init=0.0, ctx_tiles=0, ctx_len=0):
    b, t, _ = k.shape
    sq = q.shape[1]
    assert t % tk == 0 and sq % tq == 0 and tk % LANES == 0 and ctx_len % LANES == 0
    assert ctx_len <= tk
    rows = 2 * tq if diff else tq
    if lam_vec is None:
        lam_vec = jnp.zeros((4, DIFF_HD), F32)
        d_norm = jnp.zeros((dv,), F32)
    kern = functools.partial(_attn_kernel, scale=scale, diff=diff, lam_init=lam_init,
                             ctx_tiles=ctx_tiles, ctx_len=ctx_len, tq=tq, tk=tk, t=t)
    return pl.pallas_call(
        kern,
        grid=(b, heads, sq // tq),
        in_specs=[
            pl.BlockSpec((1, tq, dq), lambda bi, h, i: (bi, i, q_col0 + h)),
            pl.BlockSpec((1, t, dq), lambda bi, h, i: (bi, 0, k_col0 + h)),
            pl.BlockSpec((1, t, dv), lambda bi, h, i: (bi, 0, v_col0 + h)),
            pl.BlockSpec((4, DIFF_HD), lambda bi, h, i: (0, 0)),
            pl.BlockSpec((1, dv), lambda bi, h, i: (0, 0)),
        ],
        out_specs=pl.BlockSpec((1, tq, dv), lambda bi, h, i: (bi, i, h)),
        out_shape=jax.ShapeDtypeStruct((b, sq, heads * dv), BF16),
        scratch_shapes=[pltpu.VMEM((rows, dq), BF16), pltpu.VMEM((t // tk, rows, tk), F32),
                        pltpu.VMEM((rows, LANES), F32), pltpu.VMEM((rows, LANES), F32),
                        pltpu.VMEM((rows, dv), F32)],
        compiler_params=_cparams(("parallel", "parallel", "arbitrary")),
    )(q, k, v, lam_vec.astype(F32), d_norm.reshape(1, dv).astype(F32))


def _pack_bf16_pairs(x):
    h = x.shape[1] // 2
    lo = lax.bitcast_convert_type(x[:, :h].astype(F32), jnp.uint32) >> 16
    hi = lax.bitcast_convert_type(x[:, h:].astype(F32), jnp.uint32) & jnp.uint32(0xFFFF0000)
    return hi | lo


def _unpack_bf16_pairs(w):
    lo = lax.bitcast_convert_type(w << 16, F32)
    hi = lax.bitcast_convert_type(w & jnp.uint32(0xFFFF0000), F32)
    return jnp.concatenate([lo.astype(BF16), hi.astype(BF16)], axis=1)


def _swiglu(x, wg, wu, wd):
    g = jnp.dot(x, wg, preferred_element_type=F32)
    u = jnp.dot(x, wu, preferred_element_type=F32)
    a = (g * jax.nn.sigmoid(g) * u).astype(BF16)
    return jnp.dot(a, wd, preferred_element_type=F32)


def _expert_kernel(be_ref, nu_ref, x_ref, wg_ref, wu_ref, wd_ref, o_ref, wg_s, wu_s, wd_s):
    i = pl.program_id(0)

    @pl.when(i < nu_ref[0])
    def _():
        @pl.when((i == 0) | (be_ref[i] != be_ref[jnp.maximum(i - 1, 0)]))
        def _():
            wg_s[...] = wg_ref[0, 0].astype(BF16)
            wu_s[...] = wu_ref[0, 0].astype(BF16)
            wd_s[...] = wd_ref[0, 0].astype(BF16)

        y = _swiglu(_unpack_bf16_pairs(x_ref[...]), wg_s[...], wu_s[...], wd_s[...])
        o_ref[...] = _pack_bf16_pairs(y.astype(BF16))

    @pl.when(i >= nu_ref[0])
    def _():
        o_ref[...] = jnp.zeros(o_ref.shape, o_ref.dtype)


def _expert_ffn(x, block_e, n_used, wg, wu, wd, *, layer, tm):
    n, dh = x.shape
    d = 2 * dh
    ff = wg.shape[3]
    assert n % tm == 0

    def blk(i, be, nu):
        return jnp.minimum(i, nu[0] - 1)

    grid_spec = pltpu.PrefetchScalarGridSpec(
        num_scalar_prefetch=2,
        grid=(n // tm,),
        in_specs=[
            pl.BlockSpec((tm, dh), lambda i, be, nu: (blk(i, be, nu), 0)),
            pl.BlockSpec((1, 1, d, ff), lambda i, be, nu: (layer, be[blk(i, be, nu)], 0, 0)),
            pl.BlockSpec((1, 1, d, ff), lambda i, be, nu: (layer, be[blk(i, be, nu)], 0, 0)),
            pl.BlockSpec((1, 1, ff, d), lambda i, be, nu: (layer, be[blk(i, be, nu)], 0, 0)),
        ],
        out_specs=pl.BlockSpec((tm, dh), lambda i, be, nu: (i, 0)),
        scratch_shapes=[pltpu.VMEM((d, ff), BF16), pltpu.VMEM((d, ff), BF16), pltpu.VMEM((ff, d), BF16)],
    )
    return pl.pallas_call(
        _expert_kernel,
        grid_spec=grid_spec,
        out_shape=jax.ShapeDtypeStruct((n, dh), jnp.uint32),
        compiler_params=_cparams(("arbitrary",)),
    )(block_e, n_used, x, wg, wu, wd)


def _post_mix_kernel(a_ref, m_ref, x_ref, w_ref, gp_ref, gf_ref, g1_ref, sh_ref, sc_ref, rw_ref,
                     xa_ref, f_ref, lg_ref):
    ka = a_ref.shape[2]
    y = (jnp.dot(a_ref[0], w_ref[0:ka, :], preferred_element_type=F32)
         + jnp.dot(m_ref[0], w_ref[ka:, :], preferred_element_type=F32))
    yn = y * lax.rsqrt(jnp.mean(y * y, axis=-1, keepdims=True) + NORM_EPS) * gp_ref[...]
    xa = x_ref[0] + g1_ref[0] * yn
    xa_ref[0] = xa
    fn = xa * lax.rsqrt(jnp.mean(xa * xa, axis=-1, keepdims=True) + NORM_EPS) * gf_ref[...]
    f = (fn * (1.0 + sc_ref[0]) + sh_ref[0]).astype(BF16)
    f_ref[0] = _pack_bf16_pairs(f)
    lg_ref[...] = lax.dot_general(rw_ref[...], f, (((1,), (1,)), ((), ())), preferred_element_type=F32)


def _post_mix(mix, xs, w_out, g_post, g_ffn, g1, sh2, sc2, router_wt, *, row_tile0, seg_tiles):
    mix_a, mix_m = mix
    b, rows, ka = mix_a.shape
    km = mix_m.shape[2]
    k = ka + km
    d = w_out.shape[1]
    e = router_wt.shape[0]
    tm = ROW_TILE
    nt = rows // tm
    mod_map = lambda bi, i: (bi * 2 + jnp.where(i + row_tile0 >= seg_tiles, 1, 0), 0, 0)
    vec = lambda: pl.BlockSpec((1, d), lambda bi, i: (0, 0))
    return pl.pallas_call(
        _post_mix_kernel,
        grid=(b, nt),
        in_specs=[
            pl.BlockSpec((1, tm, ka), lambda bi, i: (bi, i, 0)),
            pl.BlockSpec((1, tm, km), lambda bi, i: (bi, i, 0)),
            pl.BlockSpec((1, tm, d), lambda bi, i: (bi, i + row_tile0, 0)),
            pl.BlockSpec((k, d), lambda bi, i: (0, 0)),
            vec(), vec(),
            pl.BlockSpec((1, 1, d), mod_map), pl.BlockSpec((1, 1, d), mod_map),
            pl.BlockSpec((1, 1, d), mod_map),
            pl.BlockSpec((e, d), lambda bi, i: (0, 0)),
        ],
        out_specs=[
            pl.BlockSpec((1, tm, d), lambda bi, i: (bi, i, 0)),
            pl.BlockSpec((1, tm, d // 2), lambda bi, i: (bi, i, 0)),
            pl.BlockSpec((e, tm), lambda bi, i: (0, bi * nt + i)),
        ],
        out_shape=[jax.ShapeDtypeStruct((b, rows, d), F32),
                   jax.ShapeDtypeStruct((b, rows, d // 2), jnp.uint32),
                   jax.ShapeDtypeStruct((e, b * rows), F32)],
        compiler_params=_cparams(("parallel", "arbitrary")),
    )(mix_a, mix_m, xs, w_out, g_post.reshape(1, d).astype(F32), g_ffn.reshape(1, d).astype(F32),
      g1, sh2, sc2, router_wt)


def _route_kernel(lg_ref, rb_ref, e_ref, g_ref):
    per = N_EXPERTS // N_GROUPS
    tn = lg_ref.shape[1]
    neg = -jnp.inf
    r_io = lax.broadcasted_iota(jnp.int32, (per, tn), 0)
    scores, choice, gs = [], [], []
    for g in range(N_GROUPS):
        sg = jax.nn.sigmoid(lg_ref[g * per:(g + 1) * per, :])
        cg = sg + rb_ref[g * per:(g + 1) * per, :]
        m1 = jnp.max(cg, axis=0, keepdims=True)
        i1 = jnp.min(jnp.where(cg == m1, r_io, per), axis=0, keepdims=True)
        m2 = jnp.max(jnp.where(r_io == i1, neg, cg), axis=0, keepdims=True)
        scores.append(sg)
        choice.append(cg)
        gs.append(m1 + m2)
    masked = []
    for g in range(N_GROUPS):
        ahead = jnp.zeros((1, tn), jnp.int32)
        for o in range(N_GROUPS):
            if o < g:
                ahead = ahead + jnp.where(gs[o] >= gs[g], 1, 0)
            elif o > g:
                ahead = ahead + jnp.where(gs[o] > gs[g], 1, 0)
        masked.append(jnp.where(ahead < TOPK_GROUPS, choice[g], neg))
    ids, gates = [], []
    for _ in range(TOP_K):
        best = masked[0]
        for g in range(1, N_GROUPS):
            best = jnp.maximum(best, masked[g])
        best = jnp.max(best, axis=0, keepdims=True)
        cand = jnp.where(masked[0] == best, r_io, N_EXPERTS)
        for g in range(1, N_GROUPS):
            cand = jnp.minimum(cand, jnp.where(masked[g] == best, r_io + g * per, N_EXPERTS))
        idx = jnp.min(cand, axis=0, keepdims=True)
        gk = jnp.zeros((per, tn), F32)
        for g in range(N_GROUPS):
            hit = (r_io + g * per) == idx
            gk = gk + jnp.where(hit, scores[g], 0.0)
            masked[g] = jnp.where(hit, neg, masked[g])
        ids.append(idx)
        gates.append(jnp.sum(gk, axis=0, keepdims=True))
    total = gates[0]
    for k in range(1, TOP_K):
        total = total + gates[k]
    pad = 8 - TOP_K
    e_ref[...] = jnp.concatenate(ids + [jnp.zeros((pad, tn), jnp.int32)], axis=0)
    g_ref[...] = jnp.concatenate([gk / total * ROUTED_SCALE for gk in gates]
                                 + [jnp.zeros((pad, tn), F32)], axis=0)


def _route(logits_t, router_b, *, tn):
    e, n = logits_t.shape
    assert n % tn == 0
    return pl.pallas_call(
        _route_kernel,
        grid=(n // tn,),
        in_specs=[pl.BlockSpec((e, tn), lambda i: (0, i)),
                  pl.BlockSpec((e, 1), lambda i: (0, 0))],
        out_specs=[pl.BlockSpec((8, tn), lambda i: (0, i)), pl.BlockSpec((8, tn), lambda i: (0, i))],
        out_shape=[jax.ShapeDtypeStruct((8, n), jnp.int32), jax.ShapeDtypeStruct((8, n), F32)],
        compiler_params=_cparams(("parallel",)),
    )(logits_t, router_b.reshape(e, 1).astype(F32))


def _dispatch_kernel(e_ref, dest_ref, be_ref, nu_ref, cnt_scr, start_scr, run_scr, *, blk):
    ph = pl.program_id(0)
    i = pl.program_id(1)
    tn = e_ref.shape[1]
    e_io = lax.broadcasted_iota(jnp.int32, (N_EXPERTS, tn), 0)
    hot = jnp.zeros((N_EXPERTS, tn), F32)
    for k in range(TOP_K):
        hot = hot + jnp.where(e_io == e_ref[k:k + 1, :], 1.0, 0.0)
    tile_cnt = jnp.sum(hot, axis=1, keepdims=True)

    @pl.when((ph == 0) & (i == 0))
    def _():
        cnt_scr[...] = jnp.zeros(cnt_scr.shape, F32)

    @pl.when(ph == 0)
    def _():
        cnt_scr[...] += jnp.broadcast_to(tile_cnt, cnt_scr.shape)

    @pl.when((ph == 1) & (i == 0))
    def _():
        bpe = jnp.floor((cnt_scr[...] + (blk - 1.0)) * (1.0 / blk))
        r = lax.broadcasted_iota(jnp.int32, (N_EXPERTS, N_EXPERTS), 0)
        c = lax.broadcasted_iota(jnp.int32, (N_EXPERTS, N_EXPERTS), 1)
        lower = jnp.where(c < r, 1.0, 0.0)
        before = jnp.dot(lower, bpe, precision=lax.Precision.HIGHEST, preferred_element_type=F32)
        start_scr[...] = before * blk
        run_scr[...] = jnp.zeros(run_scr.shape, F32)
        ends = (before + bpe)[:, 0:1]
        nb = be_ref.shape[1]
        bid = lax.broadcasted_iota(jnp.int32, (N_EXPERTS, nb), 1).astype(F32)
        be = jnp.sum(jnp.where(ends <= bid, 1, 0), axis=0, keepdims=True)
        be_ref[...] = jnp.minimum(be, N_EXPERTS - 1).astype(jnp.int32)
        nu_ref[...] = jnp.broadcast_to(jnp.max(ends, axis=0, keepdims=True), nu_ref.shape).astype(jnp.int32)

    @pl.when(ph == 1)
    def _():
        rr = lax.broadcasted_iota(jnp.int32, (tn, tn), 0)
        cc = lax.broadcasted_iota(jnp.int32, (tn, tn), 1)
        upper = jnp.where(rr < cc, 1.0, 0.0).astype(BF16)
        prior = jnp.dot(hot.astype(BF16), upper, preferred_element_type=F32)
        pos = prior + jnp.concatenate([start_scr[...] + run_scr[...]] * (tn // LANES), axis=1)
        rows = []
        for k in range(TOP_K):
            rows.append(jnp.sum(jnp.where(e_io == e_ref[k:k + 1, :], pos, 0.0), axis=0, keepdims=True))
        rows.append(jnp.zeros((8 - TOP_K, tn), F32))
        dest_ref[0] = jnp.concatenate(rows, axis=0).astype(jnp.int32)
        run_scr[...] += jnp.broadcast_to(tile_cnt, run_scr.shape)


def _dispatch(top_e, *, blk, n_blocks):
    n = top_e.shape[1]
    tn = LANES
    nbp = -(-n_blocks // LANES) * LANES
    kern = functools.partial(_dispatch_kernel, blk=blk)
    return pl.pallas_call(
        kern,
        grid=(2, n // tn),
        in_specs=[pl.BlockSpec((8, tn), lambda ph, i: (0, i))],
        out_specs=[pl.BlockSpec((1, 8, tn), lambda ph, i: (i * ph, 0, 0)),
                   pl.BlockSpec((1, nbp), lambda ph, i: (0, 0)),
                   pl.BlockSpec((8, LANES), lambda ph, i: (0, 0))],
        out_shape=[jax.ShapeDtypeStruct((n // tn, 8, tn), jnp.int32),
                   jax.ShapeDtypeStruct((1, nbp), jnp.int32),
                   jax.ShapeDtypeStruct((8, LANES), jnp.int32)],
        scratch_shapes=[pltpu.VMEM((N_EXPERTS, LANES), F32)] * 3,
        compiler_params=_cparams(("arbitrary", "arbitrary")),
    )(top_e)


def _scatter_rows_kernel(dest_ref, x_ref, zero_ref, o_ref, sem):
    del zero_ref
    tm = x_ref.shape[0]

    def row_copy(t, k):
        return pltpu.make_async_copy(x_ref.at[pl.ds(t, 1)], o_ref.at[pl.ds(dest_ref[0, k, t], 1)], sem)

    def issue(t, carry):
        for k in range(TOP_K):
            row_copy(t, k).start()
        return carry

    def drain(t, carry):
        for k in range(TOP_K):
            row_copy(t, k).wait()
        return carry

    lax.fori_loop(0, tm, issue, 0)
    lax.fori_loop(0, tm, drain, 0)


def _scatter_rows(dest, x, n_rows):
    n, dh = x.shape
    tm = LANES
    zeros = jnp.zeros((n_rows, dh), x.dtype)
    return pl.pallas_call(
        _scatter_rows_kernel,
        grid=(n // tm,),
        in_specs=[pl.BlockSpec((1, 8, tm), lambda i: (i, 0, 0), memory_space=pltpu.SMEM),
                  pl.BlockSpec((tm, dh), lambda i: (i, 0)),
                  pl.BlockSpec(memory_space=pl.ANY)],
        out_specs=pl.BlockSpec(memory_space=pl.ANY),
        out_shape=jax.ShapeDtypeStruct((n_rows, dh), x.dtype),
        scratch_shapes=[pltpu.SemaphoreType.DMA(())],
        input_output_aliases={2: 0},
        compiler_params=_cparams(("arbitrary",)),
    )(dest, x, zeros)


def _combine_kernel(dest0_ref, dest1_ref, y_ref, gate_ref, f_ref, xa_ref, sg_ref, su_ref, sd_ref, gp_ref,
                    g2_ref, o_ref, buf, sem):
    halves = (dest0_ref, dest1_ref)

    def row_copy(half, t, k):
        return pltpu.make_async_copy(y_ref.at[pl.ds(halves[half][0, k, t], 1)],
                                     buf.at[k, pl.ds(half * LANES + t, 1)], sem)

    def issue(t, carry):
        for half in range(2):
            for k in range(TOP_K):
                row_copy(half, t, k).start()
        return carry

    def drain(t, carry):
        for half in range(2):
            for k in range(TOP_K):
                row_copy(half, t, k).wait()
        return carry

    lax.fori_loop(0, LANES, issue, 0)
    f = _swiglu(_unpack_bf16_pairs(f_ref[...]), sg_ref[...], su_ref[...], sd_ref[...])
    lax.fori_loop(0, LANES, drain, 0)
    gate = gate_ref[...]
    for k in range(TOP_K):
        f = f + gate[:, k:k + 1] * _unpack_bf16_pairs(buf[k]).astype(F32)
    fn = f * lax.rsqrt(jnp.mean(f * f, axis=-1, keepdims=True) + NORM_EPS) * gp_ref[...]
    o_ref[0] = xa_ref[0] + g2_ref[0] * fn


def _combine(dest, y, gate, f_pk, xa, sg, su, sd, g_post, g2, *, row_tile0, seg_tiles):
    b, rows, d = xa.shape
    tm = 2 * LANES
    assert tm == ROW_TILE
    nt = rows // tm
    mod_map = lambda bi, i: (bi * 2 + jnp.where(i + row_tile0 >= seg_tiles, 1, 0), 0, 0)
    const = lambda shape: pl.BlockSpec(shape, lambda bi, i: (0,) * len(shape))
    dest_spec = lambda half: pl.BlockSpec((1, 8, LANES), lambda bi, i: (2 * (bi * nt + i) + half, 0, 0),
                                          memory_space=pltpu.SMEM)
    return pl.pallas_call(
        _combine_kernel,
        grid=(b, nt),
        in_specs=[
            dest_spec(0), dest_spec(1),
            pl.BlockSpec(memory_space=pl.ANY),
            pl.BlockSpec((tm, 8), lambda bi, i: (bi * nt + i, 0)),
            pl.BlockSpec((tm, d // 2), lambda bi, i: (bi * nt + i, 0)),
            pl.BlockSpec((1, tm, d), lambda bi, i: (bi, i, 0)),
            const(sg.shape), const(su.shape), const(sd.shape), const((1, d)),
            pl.BlockSpec((1, 1, d), mod_map),
        ],
        out_specs=pl.BlockSpec((1, tm, d), lambda bi, i: (bi, i, 0)),
        out_shape=jax.ShapeDtypeStruct((b, rows, d), F32),
        scratch_shapes=[pltpu.VMEM((TOP_K, tm, d // 2), jnp.uint32), pltpu.SemaphoreType.DMA(())],
        compiler_params=_cparams(("arbitrary", "arbitrary")),
    )(dest, dest, y, gate, f_pk, xa, sg, su, sd, g_post.reshape(1, d).astype(F32), g2)


def _rope_tables(s, ctx_len, dim):
    rows = s // GRID_W
    row = jnp.repeat(jnp.arange(rows), GRID_W)
    col = jnp.tile(jnp.arange(GRID_W), rows)
    quarter = dim // 4
    inv = ROPE_BASE ** (-jnp.arange(quarter, dtype=F32) / quarter)
    ang = jnp.concatenate([row.astype(F32)[:, None] * inv, col.astype(F32)[:, None] * inv], axis=-1)
    cos = jnp.repeat(jnp.cos(ang), 2, axis=-1)
    sin = jnp.repeat(jnp.sin(ang), 2, axis=-1) * jnp.tile(jnp.array([-1.0, 1.0], F32), dim // 2)
    cos = jnp.concatenate([jnp.ones((ctx_len, dim), F32), cos], axis=0)
    sin = jnp.concatenate([jnp.zeros((ctx_len, dim), F32), sin], axis=0)
    return cos, sin


def _log_sigmoid(x):
    return jnp.minimum(x, 0.0) - jnp.log1p(jnp.exp(-jnp.abs(x)))


def _mlstm_kernel(q_ref, k_ref, kt_ref, v_ref, g_ref, gt_ref, gb_ref, gbt_ref, hf_ref, mo_ref, mn_ref,
                  o_ref, s_scr, m_scr, *, reverse, final):
    L = MLSTM_CHUNK
    hv = MLSTM_V
    io, fo = (2 * MLSTM_HEADS, 3 * MLSTM_HEADS) if reverse else (0, MLSTM_HEADS)

    @pl.when(pl.program_id(1) == 0)
    def _():
        s_scr[...] = jnp.zeros(s_scr.shape, F32)
        m_scr[...] = jnp.zeros(m_scr.shape, F32)

    g = g_ref[0] + gb_ref[...]
    gt = gt_ref[0] + gbt_ref[...]
    r_io = lax.broadcasted_iota(jnp.int32, (L, L), 0)
    c_io = lax.broadcasted_iota(jnp.int32, (L, L), 1)
    seen = (c_io >= r_io) if reverse else (c_io <= r_io)
    tri = jnp.where(seen, 1.0, 0.0)
    hi = lax.Precision.HIGHEST
    bc_col = jnp.dot(tri, _log_sigmoid(g), precision=hi, preferred_element_type=F32)
    lf_row = _log_sigmoid(gt)
    bc_row = lax.dot_general(lf_row, tri, (((1,), (1,)), ((), ())), precision=hi,
                             preferred_element_type=F32)
    lane = lax.broadcasted_iota(jnp.int32, (L, LANES), 1)
    sub = lax.broadcasted_iota(jnp.int32, (LANES, L), 0)

    for h in range(MLSTM_HEADS):
        pair, odd = h // 2, h % 2
        lo = odd * MLSTM_QK
        a_col = bc_col[:, fo + h:fo + h + 1]
        i_col = g[:, io + h:io + h + 1]
        b_row = bc_row[fo + h:fo + h + 1, :]
        i_row = gt[io + h:io + h + 1, :]
        btot = jnp.sum(lf_row[fo + h:fo + h + 1, :], axis=1, keepdims=True)
        m_st = m_scr[h:h + 1, 0:1]

        w_end = btot - a_col + i_col
        m_new = jnp.maximum(btot + m_st, jnp.max(w_end, axis=0, keepdims=True))
        decay = jnp.exp(btot + m_st - m_new)
        w_k = jnp.exp(w_end - m_new)

        log_d = jnp.where(seen, a_col - b_row + i_row, -jnp.inf)
        log_inter = a_col + m_st
        m_row = jnp.maximum(log_inter, jnp.max(log_d, axis=1, keepdims=True))
        w_intra = jnp.exp(log_d - m_row)
        w_inter = jnp.exp(log_inter - m_row)

        in_head = (lane >= lo) & (lane < lo + MLSTM_QK)
        qm = jnp.where(in_head, q_ref[0, :, pair * LANES:(pair + 1) * LANES], 0).astype(BF16)
        kp = k_ref[0, :, pair * LANES:(pair + 1) * LANES]
        v = v_ref[0, :, h * hv:(h + 1) * hv]
        state = s_scr[h]

        qk = lax.dot_general(qm, kp, (((1,), (1,)), ((), ())), preferred_element_type=F32)
        qk = qk * (MLSTM_QK ** -0.5) * w_intra
        inter = jnp.dot(qm, state.astype(BF16), preferred_element_type=F32)
        num = w_inter * inter[:, :hv] + jnp.dot(qk.astype(BF16), v, preferred_element_type=F32)
        den = w_inter * inter[:, hv:hv + 1] + jnp.sum(qk, axis=1, keepdims=True)
        out = num / jnp.maximum(jnp.abs(den), jnp.exp(-m_row))

        wv = jnp.concatenate([(w_k * v.astype(F32)).astype(BF16),
                              jnp.where(lane == 0, w_k, 0.0).astype(BF16)], axis=1)
        kt = kt_ref[0, pair * LANES:(pair + 1) * LANES, :]
        in_rows = (sub >= lo) & (sub < lo + MLSTM_QK)
        ktm = (jnp.where(in_rows, kt, 0).astype(F32) * (MLSTM_QK ** -0.5)).astype(BF16)
        s_scr[h] = decay * state + jnp.dot(ktm, wv, preferred_element_type=F32)
        m_scr[h:h + 1, :] = jnp.broadcast_to(m_new, (1, LANES))

        if final:
            tot = out + hf_ref[0, :, h * hv:(h + 1) * hv]
            nrm = tot * lax.rsqrt(jnp.mean(tot * tot, axis=-1, keepdims=True) + NORM_EPS) * mn_ref[...]
            gate = jax.nn.sigmoid(mo_ref[0, :, h * hv:(h + 1) * hv].astype(F32))
            o_ref[0, :, h * hv:(h + 1) * hv] = (nrm * gate).astype(o_ref.dtype)
        else:
            o_ref[0, :, h * hv:(h + 1) * hv] = out.astype(o_ref.dtype)


def _mlstm(pv, kt, gates, gates_t, gate_b, m_norm, hf, *, ctx_chunks, reverse):
    b, t, _ = pv.shape
    L = MLSTM_CHUNK
    nc = t // L
    final = hf is not None
    nq = MLSTM_HEADS * MLSTM_QK
    nv = MLSTM_HEADS * MLSTM_V

    def chunk(j):
        if not reverse:
            return j
        return jnp.where(j < ctx_chunks, ctx_chunks - 1 - j, nc - 1 - (j - ctx_chunks))

    gb = jnp.pad(gate_b.astype(F32).reshape(-1), (0, LANES - gate_b.size))
    if hf is None:
        hf = jnp.zeros((1, L, nv), F32)
        hf_spec = pl.BlockSpec((1, L, nv), lambda bi, j: (0, 0, 0))
    else:
        hf_spec = pl.BlockSpec((1, L, nv), lambda bi, j: (bi, chunk(j), 0))
    kern = functools.partial(_mlstm_kernel, reverse=reverse, final=final)
    return pl.pallas_call(
        kern,
        grid=(b, nc),
        in_specs=[
            pl.BlockSpec((1, L, nq), lambda bi, j: (bi, chunk(j), nv // nq)),
            pl.BlockSpec((1, L, nq), lambda bi, j: (bi, chunk(j), nv // nq + 1)),
            pl.BlockSpec((1, nq, L), lambda bi, j: (bi, 0, chunk(j))),
            pl.BlockSpec((1, L, nv), lambda bi, j: (bi, chunk(j), 2)),
            pl.BlockSpec((1, L, LANES), lambda bi, j: (bi, chunk(j), 0)),
            pl.BlockSpec((1, LANES, L), lambda bi, j: (bi, 0, chunk(j))),
            pl.BlockSpec((1, LANES), lambda bi, j: (0, 0)),
            pl.BlockSpec((LANES, 1), lambda bi, j: (0, 0)),
            hf_spec,
            pl.BlockSpec((1, L, nv), lambda bi, j: (bi, chunk(j), 3)),
            pl.BlockSpec((1, MLSTM_V), lambda bi, j: (0, 0)),
        ],
        out_specs=pl.BlockSpec((1, L, nv), lambda bi, j: (bi, chunk(j), 0)),
        out_shape=jax.ShapeDtypeStruct((b, t, nv), BF16 if final else F32),
        scratch_shapes=[pltpu.VMEM((MLSTM_HEADS, LANES, 2 * LANES), F32), pltpu.VMEM((MLSTM_HEADS, LANES), F32)],
        compiler_params=_cparams(("parallel", "arbitrary")),
    )(pv, pv, kt, pv, gates, gates_t, gb.reshape(1, LANES), gb.reshape(LANES, 1), hf, pv,
      m_norm.reshape(1, MLSTM_V).astype(F32))


def _swa_kernel(sink_ref, q_ref, kp_ref, kc_ref, kn_ref, kx_ref, vp_ref, vc_ref, vn_ref, vx_ref, o_ref, *,
                n_blocks):
    i = pl.program_id(1)
    bq = BLOCK_Q
    hd = SWA_HD
    rep = SWA_HEADS // SWA_KV_HEADS
    n_ctx = kx_ref.shape[1]
    rows = rep * bq
    scale = hd ** -0.5 * LOG2E
    row = lax.broadcasted_iota(jnp.int32, (rows, 3 * bq), 0)
    col = lax.broadcasted_iota(jnp.int32, (rows, 3 * bq), 1)
    dt = (col - bq) - (row % bq)
    blk = col // bq
    ok = (jnp.abs(dt) <= WINDOW) & ((blk != 0) | (i > 0)) & ((blk != 2) | (i < n_blocks - 1))
    head_of_row = lax.broadcasted_iota(jnp.int32, (rows, 1), 0) // bq
    for g in range(SWA_KV_HEADS):
        q = jnp.concatenate([q_ref[0, :, (g * rep + r) * hd:(g * rep + r + 1) * hd] for r in range(rep)], axis=0)
        q = (q.astype(F32) * scale).astype(BF16)
        cs = slice(g * hd, (g + 1) * hd)
        k_loc = jnp.concatenate([kp_ref[0, :, cs], kc_ref[0, :, cs], kn_ref[0, :, cs]], axis=0)
        v_loc = jnp.concatenate([vp_ref[0, :, cs], vc_ref[0, :, cs], vn_ref[0, :, cs]], axis=0)
        nt = (((1,), (1,)), ((), ()))
        s_loc = jnp.where(ok, lax.dot_general(q, k_loc, nt, preferred_element_type=F32), -jnp.inf)
        s_ctx = lax.dot_general(q, kx_ref[0, :, cs], nt, preferred_element_type=F32)
        sink = jnp.zeros((rows, 1), F32)
        for r in range(rep):
            sink = jnp.where(head_of_row == r, sink_ref[g * rep + r] * LOG2E, sink)
        m = jnp.maximum(jnp.maximum(jnp.max(s_loc, axis=-1, keepdims=True),
                                    jnp.max(s_ctx, axis=-1, keepdims=True)), sink)
        p_loc = jnp.exp2(s_loc - m)
        p_ctx = jnp.exp2(s_ctx - m)
        den = (jnp.sum(p_loc, axis=-1, keepdims=True) + jnp.sum(p_ctx, axis=-1, keepdims=True)
               + jnp.exp2(sink - m))
        out = (jnp.dot(p_loc.astype(BF16), v_loc, preferred_element_type=F32)
               + jnp.dot(p_ctx.astype(BF16), vx_ref[0, :, cs], preferred_element_type=F32)) / den
        for r in range(rep):
            o_ref[0, :, (g * rep + r) * hd:(g * rep + r + 1) * hd] = out[r * bq:(r + 1) * bq].astype(o_ref.dtype)


def _swa(qk, v, sink, *, n_ctx, k_col0):
    b, t, _ = qk.shape
    bq = BLOCK_Q
    cb = n_ctx // bq
    nb = (t - n_ctx) // bq
    nq = SWA_HEADS * SWA_HD
    nk = SWA_KV_HEADS * SWA_HD
    kcol = k_col0
    prev = lambda bi, i, c: (bi, i + cb - 1, c)
    cur = lambda bi, i, c: (bi, i + cb, c)
    nxt = lambda bi, i, c: (bi, jnp.minimum(i + cb + 1, nb + cb - 1), c)
    kern = functools.partial(_swa_kernel, n_blocks=nb)

    def spec(rows, fn, c):
        return pl.BlockSpec((1, rows, nk), lambda bi, i: fn(bi, i, c))

    ctx = lambda bi, i, c: (bi, 0, c)
    return pl.pallas_call(
        kern,
        grid=(b, nb),
        in_specs=[
            pl.BlockSpec(memory_space=pltpu.SMEM),
            pl.BlockSpec((1, bq, nq), lambda bi, i: (bi, i + cb, 0)),
            spec(bq, prev, kcol), spec(bq, cur, kcol), spec(bq, nxt, kcol), spec(n_ctx, ctx, kcol),
            spec(bq, prev, 0), spec(bq, cur, 0), spec(bq, nxt, 0), spec(n_ctx, ctx, 0),
        ],
        out_specs=pl.BlockSpec((1, bq, nq), lambda bi, i: (bi, i, 0)),
        out_shape=jax.ShapeDtypeStruct((b, nb * bq, nq), BF16),
        compiler_params=_cparams(("parallel", "arbitrary")),
    )(sink.astype(F32), qk, qk, qk, qk, qk, v, v, v, v)


def _mix_out_and_moe(mix, xs, w_out, g_post, g_ffn_pre, g_ffn_post, mods, router_w, router_b,
                     wg, wu, wd, sg, su, sd, *, layer, row_tile0, seg_tiles):
    sh1, sc1, g1, sh2, sc2, g2 = mods
    b, rows, _ = mix[0].shape
    n = b * rows
    seg = dict(row_tile0=row_tile0, seg_tiles=seg_tiles)
    xa, f_pk, logits_t = _post_mix(mix, xs, w_out.astype(BF16), g_post, g_ffn_pre, g1, sh2, sc2,
                                   router_w.T.astype(BF16), **seg)
    top_e, gate = _route(logits_t, router_b, tn=512)
    blk = EXPERT_ROWS
    n_blocks = -(-n * TOP_K // blk) + N_EXPERTS
    dest, block_e, n_used = _dispatch(top_e, blk=blk, n_blocks=n_blocks)
    f_pk = f_pk.reshape(n, -1)
    x_sorted = _scatter_rows(dest, f_pk, n_blocks * blk)
    y = _expert_ffn(x_sorted, block_e[0, :n_blocks], n_used[0, :1], wg, wu, wd, layer=layer, tm=blk)
    return _combine(dest, y, gate.T, f_pk, xa, sg.astype(BF16), su.astype(BF16), sd.astype(BF16),
                    g_ffn_post, g2, **seg)


def kernel(x, c, ctx, c_ctx, mod_w, mod_b, norm_mix_pre, norm_mix_post, norm_ffn_pre, norm_ffn_post, ab_w_in, ab_w_out, diff_lambda, diff_norm, mlstm_gate_b, mlstm_norm, cd_w_in, cd_w_out, mla_q_norm, mla_w_uq, mla_kv_norm, mla_w_ukv, swa_sink, router_w, router_b, exp_w_gate, exp_w_up, exp_w_down, sh_w_gate, sh_w_up, sh_w_down):
    b, s, d = x.shape
    L = ctx.shape[1]
    t = L + s
    assert L == ROW_TILE and s % ROW_TILE == 0
    depth = mod_w.shape[0]
    ctx_tiles = L // ROW_TILE
    tk = next(c for c in (2816, 768, ROW_TILE) if t % c == 0)

    cos64, sin64 = _rope_tables(s, L, DIFF_HD)
    cos64 = jnp.tile(cos64, (1, LANES // DIFF_HD))
    sin64 = jnp.tile(sin64, (1, LANES // DIFF_HD))
    cos128, sin128 = _rope_tables(s, L, SWA_HD)
    cos_kr = jnp.concatenate([cos64[:, :MLA_ROPE], jnp.ones((t, LANES - MLA_ROPE), F32)], axis=1)
    sin_kr = jnp.concatenate([sin64[:, :MLA_ROPE], jnp.zeros((t, LANES - MLA_ROPE), F32)], axis=1)

    xs = jnp.concatenate([ctx, x], axis=1)

    for layer in range(depth):
        with_ctx = layer < depth - 1
        j = layer // 2
        mod_l = jax.nn.silu(c) @ mod_w[layer] + mod_b[layer]
        mod_c = jax.nn.silu(c_ctx) @ mod_w[layer] + mod_b[layer]
        mods = jnp.stack([jnp.broadcast_to(mod_c, (b, 6 * d)), mod_l], axis=1).reshape(b * 2, 1, 6, d)
        sh1, sc1, g1, sh2, sc2, g2 = (mods[:, :, m] for m in range(6))

        proj = functools.partial(_norm_proj, xs, norm_mix_pre[layer], sh1, sc1, seg_tiles=ctx_tiles)
        if layer % 2 == 0:
            w_in = ab_w_in[j].astype(BF16)
            n_qk = 2 * DIFF_HEADS * 2 * DIFF_HD
            qk = proj(w_in[:, :n_qk], tn=2048, rope=(cos64[None], sin64[None]), pattern=(0,) * 16)
            pv = proj(w_in[:, n_qk:n_qk + AB_PLAIN], tn=2048)
            w_gates = jnp.pad(w_in[:, n_qk + AB_PLAIN:], ((0, 0), (0, LANES - 4 * MLSTM_HEADS)))
            gates = proj(w_gates, tn=LANES, out_dtype=F32)

            lam_init = 0.8 - 0.6 * math.exp(-0.3 * layer)
            a = _attention(qk if with_ctx else qk[:, L:], qk, pv, heads=DIFF_HEADS, dq=2 * DIFF_HD,
                           dv=DIFF_VD, q_col0=0, k_col0=DIFF_HEADS, v_col0=0, tq=ROW_TILE,
                           scale=DIFF_HD ** -0.5, tk=tk, diff=True, lam_vec=diff_lambda[j],
                           d_norm=diff_norm[j], lam_init=lam_init,
                           ctx_tiles=ctx_tiles if with_ctx else 0, ctx_len=L)
            o_mk = DIFF_HEADS * DIFF_VD + MLSTM_HEADS * MLSTM_QK
            kt = jnp.swapaxes(pv[..., o_mk:o_mk + MLSTM_HEADS * MLSTM_QK], 1, 2)
            scan = functools.partial(_mlstm, pv, kt, gates, jnp.swapaxes(gates, 1, 2), mlstm_gate_b[j],
                                     mlstm_norm[j], ctx_chunks=L // MLSTM_CHUNK)
            m = scan(scan(None, reverse=False), reverse=True)
            mix = (a, m if with_ctx else m[:, L:])
            w_out = ab_w_out[j]
        else:
            assert not with_ctx
            w_in = cd_w_in[j].astype(BF16)
            c0 = Q_LORA
            c1 = c0 + KV_LORA
            c2 = c1 + MLA_ROPE
            c3 = c2 + SWA_HEADS * SWA_HD
            c4 = c3 + SWA_KV_HEADS * SWA_HD
            w_rope = jnp.concatenate([w_in[:, c2:c4], w_in[:, c1:c2],
                                      jnp.zeros((d, LANES - MLA_ROPE), BF16)], axis=1)
            n_rope = w_rope.shape[1]
            rp = proj(w_rope, tn=n_rope, rope=(jnp.stack([cos128, cos_kr]), jnp.stack([sin128, sin_kr])),
                      pattern=(0,) * (n_rope // LANES - 1) + (1,))
            q_pad = 512 - Q_LORA
            w_plain = jnp.concatenate([w_in[:, c4:], w_in[:, :c0], jnp.zeros((d, q_pad), BF16),
                                       w_in[:, c0:c1]], axis=1)
            pp = proj(w_plain, tn=w_plain.shape[1])
            n_sv = SWA_KV_HEADS * SWA_HD
            cq = pp[..., n_sv:n_sv + 512]
            ckv = pp[..., n_sv + 512:]

            hq = MLA_NOPE + MLA_ROPE
            w_uq = mla_w_uq[j].astype(BF16).reshape(Q_LORA, MLA_HEADS, hq)
            w_uq = jnp.pad(w_uq, ((0, q_pad), (0, 0), (0, 256 - hq))).reshape(512, MLA_HEADS * 256)
            no_mod = jnp.zeros((b * 2, 1, 512), F32)
            qn = jnp.pad(mla_q_norm[j], (0, q_pad))
            q_mla = _norm_proj(cq[:, L:], qn, no_mod, no_mod, w_uq, seg_tiles=0, tn=2048, n_valid=Q_LORA,
                               rope=(cos_kr[None, L:], sin_kr[None, L:]), pattern=(-1, 0) * 8)
            kv = _norm_proj(ckv, mla_kv_norm[j], no_mod, no_mod, mla_w_ukv[j].astype(BF16),
                            seg_tiles=ctx_tiles, tn=2048).reshape(b, t, MLA_HEADS, MLA_NOPE + MLA_V)
            kr = rp[..., n_rope - LANES:]
            k_mla = jnp.concatenate([kv[..., :MLA_NOPE],
                                     jnp.broadcast_to(kr[:, :, None, :], (b, t, MLA_HEADS, LANES))],
                                    axis=-1).reshape(b, t, MLA_HEADS * 256)
            v_mla = kv[..., MLA_NOPE:].reshape(b, t, MLA_HEADS * MLA_V)
            a = _attention(q_mla, k_mla, v_mla, heads=MLA_HEADS, dq=256, dv=MLA_V, q_col0=0, k_col0=0,
                           v_col0=0, tq=512, scale=MLA_SCALE, tk=tk)

            w = _swa(rp, pp, swa_sink[j], n_ctx=L, k_col0=SWA_HEADS * SWA_HD // n_sv)
            mix = (a, w)
            w_out = cd_w_out[j]

        xa = _mix_out_and_moe(mix, xs, w_out, norm_mix_post[layer], norm_ffn_pre[layer], norm_ffn_post[layer],
                              (sh1, sc1, g1, sh2, sc2, g2), router_w[layer], router_b[layer],
                              exp_w_gate, exp_w_up, exp_w_down,
                              sh_w_gate[layer], sh_w_up[layer], sh_w_down[layer],
                              layer=layer, row_tile0=0 if with_ctx else ctx_tiles, seg_tiles=ctx_tiles)
        xs = xa if with_ctx else jnp.concatenate([xs[:, :L], xa], axis=1)
    return xs[:, L:]
```

```python
import functools
import math

import jax
import jax.numpy as jnp
import numpy as np
from jax import lax
from jax.experimental import pallas as pl
from jax.experimental.pallas import tpu as pltpu

F32 = jnp.float32
BF16 = jnp.bfloat16

GRID_W = 64
ROPE_BASE = 10000.0
NORM_EPS = 1e-6

DIFF_HEADS = 8
DIFF_HD = 64
DIFF_VD = 2 * DIFF_HD
MLSTM_HEADS = 8
MLSTM_QK = 64
MLSTM_V = 128
MLSTM_CHUNK = 128
MLA_HEADS = 8
MLA_NOPE = 128
MLA_ROPE = 64
MLA_V = 128
Q_LORA = 448
KV_LORA = 512
MLA_SCALE = (MLA_NOPE + MLA_ROPE) ** -0.5
SWA_HEADS = 8
SWA_KV_HEADS = 2
SWA_HD = 128
WINDOW = 128
BLOCK_Q = 128
N_EXPERTS = 64
TOP_K = 6
N_GROUPS = 8
TOPK_GROUPS = 4
ROUTED_SCALE = 2.5
AB_PLAIN = DIFF_HEADS * DIFF_VD + 2 * MLSTM_HEADS * MLSTM_QK + 2 * MLSTM_HEADS * MLSTM_V

LANES = 128
LOG2E = 1.4426950408889634
VMEM_LIMIT = 56 * 1024 * 1024

ROW_TILE = 256
EXPERT_ROWS = 512


def _cparams(sem):
    return pltpu.CompilerParams(dimension_semantics=sem, vmem_limit_bytes=VMEM_LIMIT)


def _pair_swap(a):
    lane = lax.broadcasted_iota(jnp.int32, a.shape, 1)
    return jnp.where(lane % 2 == 0, pltpu.roll(a, LANES - 1, 1), pltpu.roll(a, 1, 1))


def _norm_proj_kernel(x_ref, g_ref, sh_ref, sc_ref, w_ref, cos_ref, sin_ref, o_ref, *,
                      n_valid, pattern):
    x = x_ref[0].astype(F32)
    ms = jnp.sum(x * x, axis=-1, keepdims=True) * (1.0 / n_valid)
    y = x * lax.rsqrt(ms + NORM_EPS) * g_ref[...]
    h = (y * (1.0 + sc_ref[0]) + sh_ref[0]).astype(BF16)
    acc = jnp.dot(h, w_ref[...], preferred_element_type=F32)
    if pattern is None:
        o_ref[0] = acc.astype(o_ref.dtype)
    else:
        for c, tbl in enumerate(pattern):
            a = acc[:, c * LANES:(c + 1) * LANES]
            if tbl >= 0:
                a = a * cos_ref[tbl] + _pair_swap(a) * sin_ref[tbl]
            o_ref[0, :, c * LANES:(c + 1) * LANES] = a.astype(o_ref.dtype)


def _norm_proj(x, g, shift, scale, w, *, seg_tiles, tn, out_dtype=BF16, n_valid=None,
               rope=None, pattern=None):
    b, t, k = x.shape
    n = w.shape[1]
    tm = ROW_TILE
    assert t % tm == 0 and n % tn == 0 and tn % LANES == 0
    if rope is None:
        cos = sin = jnp.zeros((1, tm, LANES), F32)
        tbl_map = lambda j, bi, i: (0, 0, 0)
    else:
        cos, sin = rope
        tbl_map = lambda j, bi, i: (0, i, 0)
    ntab = cos.shape[0]
    kern = functools.partial(_norm_proj_kernel, n_valid=float(n_valid or k), pattern=pattern)
    mod_map = lambda j, bi, i: (bi * 2 + jnp.where(i >= seg_tiles, 1, 0), 0, 0)
    return pl.pallas_call(
        kern,
        grid=(n // tn, b, t // tm),
        in_specs=[
            pl.BlockSpec((1, tm, k), lambda j, bi, i: (bi, i, 0)),
            pl.BlockSpec((1, k), lambda j, bi, i: (0, 0)),
            pl.BlockSpec((1, 1, k), mod_map),
            pl.BlockSpec((1, 1, k), mod_map),
            pl.BlockSpec((k, tn), lambda j, bi, i: (0, j)),
            pl.BlockSpec((ntab, tm, LANES), tbl_map),
            pl.BlockSpec((ntab, tm, LANES), tbl_map),
        ],
        out_specs=pl.BlockSpec((1, tm, tn), lambda j, bi, i: (bi, i, j)),
        out_shape=jax.ShapeDtypeStruct((b, t, n), out_dtype),
        compiler_params=_cparams(("parallel", "parallel", "parallel")),
    )(x, g.reshape(1, k).astype(F32), shift, scale, w, cos, sin)


def _attn_kernel(q_ref, k_ref, v_ref, lam_ref, dn_ref, o_ref, q_scr, s_scr, m_scr, l_scr, acc_scr, *,
                 scale, diff, lam_init, ctx_tiles, ctx_len, tq, tk, t):
    i = pl.program_id(2)
    q = q_ref[0].astype(F32) * (scale * LOG2E)
    if diff:
        lane = lax.broadcasted_iota(jnp.int32, q.shape, 1)
        half = q.shape[1] // 2
        q_scr[0:tq] = jnp.where(lane < half, q, 0.0).astype(BF16)
        q_scr[tq:2 * tq] = jnp.where(lane >= half, q, 0.0).astype(BF16)
    else:
        q_scr[...] = q.astype(BF16)

    def attend(kv_len, chunk):
        n_chunks = kv_len // chunk
        groups = chunk // LANES

        def pass1(c, carry):
            k = k_ref[0, pl.ds(pl.multiple_of(c * chunk, chunk), chunk), :]
            s = lax.dot_general(q_scr[...], k, (((1,), (1,)), ((), ())), preferred_element_type=F32)
            s_scr[c, :, 0:chunk] = s
            m = m_scr[...]
            for g in range(groups):
                m = jnp.maximum(m, s[:, g * LANES:(g + 1) * LANES])
            m_scr[...] = m
            return carry

        def pass2(c, carry):
            s = s_scr[c, :, 0:chunk]
            p = jnp.exp2(s - jnp.concatenate([m_scr[...]] * groups, axis=1))
            l = l_scr[...]
            for g in range(groups):
                l = l + p[:, g * LANES:(g + 1) * LANES]
            l_scr[...] = l
            v = v_ref[0, pl.ds(pl.multiple_of(c * chunk, chunk), chunk), :]
            acc_scr[...] += jnp.dot(p.astype(BF16), v, preferred_element_type=F32)
            return carry

        m_scr[...] = jnp.full(m_scr.shape, -jnp.inf, F32)
        if n_chunks == 1:
            pass1(0, 0)
        else:
            lax.fori_loop(0, n_chunks, pass1, 0)
        m_scr[...] = jnp.broadcast_to(jnp.max(m_scr[...], axis=-1, keepdims=True), m_scr.shape)
        l_scr[...] = jnp.zeros(l_scr.shape, F32)
        acc_scr[...] = jnp.zeros(acc_scr.shape, F32)
        if n_chunks == 1:
            pass2(0, 0)
        else:
            lax.fori_loop(0, n_chunks, pass2, 0)

        o = acc_scr[...] / jnp.sum(l_scr[...], axis=-1, keepdims=True)
        if diff:
            lv = lam_ref[...]
            lam = (jnp.exp(jnp.sum(lv[0:1] * lv[1:2], axis=-1, keepdims=True))
                   - jnp.exp(jnp.sum(lv[2:3] * lv[3:4], axis=-1, keepdims=True)) + lam_init)
            a = o[0:tq] - lam * o[tq:2 * tq]
            ms = jnp.mean(a * a, axis=-1, keepdims=True)
            a = a * lax.rsqrt(ms + NORM_EPS) * dn_ref[...] * (1.0 - lam_init)
            o_ref[0] = a.astype(o_ref.dtype)
        else:
            o_ref[0] = o.astype(o_ref.dtype)

    if ctx_tiles:
        pl.when(i < ctx_tiles)(lambda: attend(ctx_len, ctx_len))
        pl.when(i >= ctx_tiles)(lambda: attend(t, tk))
    else:
        attend(t, tk)


def _attention(q, k, v, *, heads, dq, dv, q_col0, k_col0, v_col0, tq, scale, tk, diff=False,
               lam_vec=None, d_norm=None, lam_init=0.0, ctx_tiles=0, ctx_len=0):
    b, t, _ = k.shape
    sq = q.shape[1]
    assert t % tk == 0 and sq % tq == 0 and tk % LANES == 0 and ctx_len % LANES == 0
    assert ctx_len <= tk
    rows = 2 * tq if diff else tq
    if lam_vec is None:
        lam_vec = jnp.zeros((4, DIFF_HD), F32)
        d_norm = jnp.zeros((dv,), F32)
    kern = functools.partial(_attn_kernel, scale=scale, diff=diff, lam_init=lam_init,
                             ctx_tiles=ctx_tiles, ctx_len=ctx_len, tq=tq, tk=tk, t=t)
    return pl.pallas_call(
        kern,
        grid=(b, heads, sq // tq),
        in_specs=[
            pl.BlockSpec((1, tq, dq), lambda bi, h, i: (bi, i, q_col0 + h)),
            pl.BlockSpec((1, t, dq), lambda bi, h, i: (bi, 0, k_col0 + h)),
            pl.BlockSpec((1, t, dv), lambda bi, h, i: (bi, 0, v_col0 + h)),
            pl.BlockSpec((4, DIFF_HD), lambda bi, h, i: (0, 0)),
            pl.BlockSpec((1, dv), lambda bi, h, i: (0, 0)),
        ],
        out_specs=pl.BlockSpec((1, tq, dv), lambda bi, h, i: (bi, i, h)),
        out_shape=jax.ShapeDtypeStruct((b, sq, heads * dv), BF16),
        scratch_shapes=[pltpu.VMEM((rows, dq), BF16), pltpu.VMEM((t // tk, rows, tk), F32),
                        pltpu.VMEM((rows, LANES), F32), pltpu.VMEM((rows, LANES), F32),
                        pltpu.VMEM((rows, dv), F32)],
        compiler_params=_cparams(("parallel", "parallel", "arbitrary")),
    )(q, k, v, lam_vec.astype(F32), d_norm.reshape(1, dv).astype(F32))


def _pack_bf16_pairs(x):
    h = x.shape[1] // 2
    lo = lax.bitcast_convert_type(x[:, :h].astype(F32), jnp.uint32) >> 16
    hi = lax.bitcast_convert_type(x[:, h:].astype(F32), jnp.uint32) & jnp.uint32(0xFFFF0000)
    return hi | lo


def _unpack_bf16_pairs(w):
    lo = lax.bitcast_convert_type(w << 16, F32)
    hi = lax.bitcast_convert_type(w & jnp.uint32(0xFFFF0000), F32)
    return jnp.concatenate([lo.astype(BF16), hi.astype(BF16)], axis=1)


def _swiglu(x, wg, wu, wd):
    g = jnp.dot(x, wg, preferred_element_type=F32)
    u = jnp.dot(x, wu, preferred_element_type=F32)
    a = (g * jax.nn.sigmoid(g) * u).astype(BF16)
    return jnp.dot(a, wd, preferred_element_type=F32)


def _expert_kernel(be_ref, nu_ref, x_ref, wg_ref, wu_ref, wd_ref, o_ref, wg_s, wu_s, wd_s):
    i = pl.program_id(0)

    @pl.when(i < nu_ref[0])
    def _():
        @pl.when((i == 0) | (be_ref[i] != be_ref[jnp.maximum(i - 1, 0)]))
        def _():
            wg_s[...] = wg_ref[0, 0].astype(BF16)
            wu_s[...] = wu_ref[0, 0].astype(BF16)
            wd_s[...] = wd_ref[0, 0].astype(BF16)

        y = _swiglu(_unpack_bf16_pairs(x_ref[...]), wg_s[...], wu_s[...], wd_s[...])
        o_ref[...] = _pack_bf16_pairs(y.astype(BF16))

    @pl.when(i >= nu_ref[0])
    def _():
        o_ref[...] = jnp.zeros(o_ref.shape, o_ref.dtype)


def _expert_ffn(x, block_e, n_used, wg, wu, wd, *, layer, tm):
    n, dh = x.shape
    d = 2 * dh
    ff = wg.shape[3]
    assert n % tm == 0

    def blk(i, be, nu):
        return jnp.minimum(i, nu[0] - 1)

    grid_spec = pltpu.PrefetchScalarGridSpec(
        num_scalar_prefetch=2,
        grid=(n // tm,),
        in_specs=[
            pl.BlockSpec((tm, dh), lambda i, be, nu: (blk(i, be, nu), 0)),
            pl.BlockSpec((1, 1, d, ff), lambda i, be, nu: (layer, be[blk(i, be, nu)], 0, 0)),
            pl.BlockSpec((1, 1, d, ff), lambda i, be, nu: (layer, be[blk(i, be, nu)], 0, 0)),
            pl.BlockSpec((1, 1, ff, d), lambda i, be, nu: (layer, be[blk(i, be, nu)], 0, 0)),
        ],
        out_specs=pl.BlockSpec((tm, dh), lambda i, be, nu: (i, 0)),
        scratch_shapes=[pltpu.VMEM((d, ff), BF16), pltpu.VMEM((d, ff), BF16), pltpu.VMEM((ff, d), BF16)],
    )
    return pl.pallas_call(
        _expert_kernel,
        grid_spec=grid_spec,
        out_shape=jax.ShapeDtypeStruct((n, dh), jnp.uint32),
        compiler_params=_cparams(("arbitrary",)),
    )(block_e, n_used, x, wg, wu, wd)


def _post_mix_kernel(a_ref, m_ref, x_ref, w_ref, gp_ref, gf_ref, g1_ref, sh_ref, sc_ref, rw_ref,
                     xa_ref, f_ref, lg_ref):
    ka = a_ref.shape[2]
    y = (jnp.dot(a_ref[0], w_ref[0:ka, :], preferred_element_type=F32)
         + jnp.dot(m_ref[0], w_ref[ka:, :], preferred_element_type=F32))
    yn = y * lax.rsqrt(jnp.mean(y * y, axis=-1, keepdims=True) + NORM_EPS) * gp_ref[...]
    xa = x_ref[0] + g1_ref[0] * yn
    xa_ref[0] = xa
    fn = xa * lax.rsqrt(jnp.mean(xa * xa, axis=-1, keepdims=True) + NORM_EPS) * gf_ref[...]
    f = (fn * (1.0 + sc_ref[0]) + sh_ref[0]).astype(BF16)
    f_ref[0] = _pack_bf16_pairs(f)
    lg_ref[...] = lax.dot_general(rw_ref[...], f, (((1,), (1,)), ((), ())), preferred_element_type=F32)


def _post_mix(mix, xs, w_out, g_post, g_ffn, g1, sh2, sc2, router_wt, *, row_tile0, seg_tiles):
    mix_a, mix_m = mix
    b, rows, ka = mix_a.shape
    km = mix_m.shape[2]
    k = ka + km
    d = w_out.shape[1]
    e = router_wt.shape[0]
    tm = ROW_TILE
    nt = rows // tm
    mod_map = lambda bi, i: (bi * 2 + jnp.where(i + row_tile0 >= seg_tiles, 1, 0), 0, 0)
    vec = lambda: pl.BlockSpec((1, d), lambda bi, i: (0, 0))
    return pl.pallas_call(
        _post_mix_kernel,
        grid=(b, nt),
        in_specs=[
            pl.BlockSpec((1, tm, ka), lambda bi, i: (bi, i, 0)),
            pl.BlockSpec((1, tm, km), lambda bi, i: (bi, i, 0)),
            pl.BlockSpec((1, tm, d), lambda bi, i: (bi, i + row_tile0, 0)),
            pl.BlockSpec((k, d), lambda bi, i: (0, 0)),
            vec(), vec(),
            pl.BlockSpec((1, 1, d), mod_map), pl.BlockSpec((1, 1, d), mod_map),
            pl.BlockSpec((1, 1, d), mod_map),
            pl.BlockSpec((e, d), lambda bi, i: (0, 0)),
        ],
        out_specs=[
            pl.BlockSpec((1, tm, d), lambda bi, i: (bi, i, 0)),
            pl.BlockSpec((1, tm, d // 2), lambda bi, i: (bi, i, 0)),
            pl.BlockSpec((e, tm), lambda bi, i: (0, bi * nt + i)),
        ],
        out_shape=[jax.ShapeDtypeStruct((b, rows, d), F32),
                   jax.ShapeDtypeStruct((b, rows, d // 2), jnp.uint32),
                   jax.ShapeDtypeStruct((e, b * rows), F32)],
        compiler_params=_cparams(("parallel", "arbitrary")),
    )(mix_a, mix_m, xs, w_out, g_post.reshape(1, d).astype(F32), g_ffn.reshape(1, d).astype(F32),
      g1, sh2, sc2, router_wt)


def _route_kernel(lg_ref, rb_ref, e_ref, g_ref):
    per = N_EXPERTS // N_GROUPS
    tn = lg_ref.shape[1]
    neg = -jnp.inf
    r_io = lax.broadcasted_iota(jnp.int32, (per, tn), 0)
    scores, choice, gs = [], [], []
    for g in range(N_GROUPS):
        sg = jax.nn.sigmoid(lg_ref[g * per:(g + 1) * per, :])
        cg = sg + rb_ref[g * per:(g + 1) * per, :]
        m1 = jnp.max(cg, axis=0, keepdims=True)
        i1 = jnp.min(jnp.where(cg == m1, r_io, per), axis=0, keepdims=True)
        m2 = jnp.max(jnp.where(r_io == i1, neg, cg), axis=0, keepdims=True)
        scores.append(sg)
        choice.append(cg)
        gs.append(m1 + m2)
    masked = []
    for g in range(N_GROUPS):
        ahead = jnp.zeros((1, tn), jnp.int32)
        for o in range(N_GROUPS):
            if o < g:
                ahead = ahead + jnp.where(gs[o] >= gs[g], 1, 0)
            elif o > g:
                ahead = ahead + jnp.where(gs[o] > gs[g], 1, 0)
        masked.append(jnp.where(ahead < TOPK_GROUPS, choice[g], neg))
    ids, gates = [], []
    for _ in range(TOP_K):
        best = masked[0]
        for g in range(1, N_GROUPS):
            best = jnp.maximum(best, masked[g])
        best = jnp.max(best, axis=0, keepdims=True)
        cand = jnp.where(masked[0] == best, r_io, N_EXPERTS)
        for g in range(1, N_GROUPS):
            cand = jnp.minimum(cand, jnp.where(masked[g] == best, r_io + g * per, N_EXPERTS))
        idx = jnp.min(cand, axis=0, keepdims=True)
        gk = jnp.zeros((per, tn), F32)
        for g in range(N_GROUPS):
            hit = (r_io + g * per) == idx
            gk = gk + jnp.where(hit, scores[g], 0.0)
            masked[g] = jnp.where(hit, neg, masked[g])
        ids.append(idx)
        gates.append(jnp.sum(gk, axis=0, keepdims=True))
    total = gates[0]
    for k in range(1, TOP_K):
        total = total + gates[k]
    pad = 8 - TOP_K
    e_ref[...] = jnp.concatenate(ids + [jnp.zeros((pad, tn), jnp.int32)], axis=0)
    g_ref[...] = jnp.concatenate([gk / total * ROUTED_SCALE for gk in gates]
                                 + [jnp.zeros((pad, tn), F32)], axis=0)


def _route(logits_t, router_b, *, tn):
    e, n = logits_t.shape
    assert n % tn == 0
    return pl.pallas_call(
        _route_kernel,
        grid=(n // tn,),
        in_specs=[pl.BlockSpec((e, tn), lambda i: (0, i)),
                  pl.BlockSpec((e, 1), lambda i: (0, 0))],
        out_specs=[pl.BlockSpec((8, tn), lambda i: (0, i)), pl.BlockSpec((8, tn), lambda i: (0, i))],
        out_shape=[jax.ShapeDtypeStruct((8, n), jnp.int32), jax.ShapeDtypeStruct((8, n), F32)],
        compiler_params=_cparams(("parallel",)),
    )(logits_t, router_b.reshape(e, 1).astype(F32))


def _dispatch_kernel(e_ref, dest_ref, be_ref, nu_ref, cnt_scr, start_scr, run_scr, *, blk):
    ph = pl.program_id(0)
    i = pl.program_id(1)
    tn = e_ref.shape[1]
    e_io = lax.broadcasted_iota(jnp.int32, (N_EXPERTS, tn), 0)
    hot = jnp.zeros((N_EXPERTS, tn), F32)
    for k in range(TOP_K):
        hot = hot + jnp.where(e_io == e_ref[k:k + 1, :], 1.0, 0.0)
    tile_cnt = jnp.sum(hot, axis=1, keepdims=True)

    @pl.when((ph == 0) & (i == 0))
    def _():
        cnt_scr[...] = jnp.zeros(cnt_scr.shape, F32)

    @pl.when(ph == 0)
    def _():
        cnt_scr[...] += jnp.broadcast_to(tile_cnt, cnt_scr.shape)

    @pl.when((ph == 1) & (i == 0))
    def _():
        bpe = jnp.floor((cnt_scr[...] + (blk - 1.0)) * (1.0 / blk))
        r = lax.broadcasted_iota(jnp.int32, (N_EXPERTS, N_EXPERTS), 0)
        c = lax.broadcasted_iota(jnp.int32, (N_EXPERTS, N_EXPERTS), 1)
        lower = jnp.where(c < r, 1.0, 0.0)
        before = jnp.dot(lower, bpe, precision=lax.Precision.HIGHEST, preferred_element_type=F32)
        start_scr[...] = before * blk
        run_scr[...] = jnp.zeros(run_scr.shape, F32)
        ends = (before + bpe)[:, 0:1]
        nb = be_ref.shape[1]
        bid = lax.broadcasted_iota(jnp.int32, (N_EXPERTS, nb), 1).astype(F32)
        be = jnp.sum(jnp.where(ends <= bid, 1, 0), axis=0, keepdims=True)
        be_ref[...] = jnp.minimum(be, N_EXPERTS - 1).astype(jnp.int32)
        n_used = jnp.broadcast_to(jnp.max(ends, axis=0, keepdims=True), (1, LANES))
        last = jnp.where(bpe > 0.0, (before + bpe - 1.0) * blk, -1.0)
        er = lax.broadcasted_iota(jnp.int32, (N_EXPERTS, LANES), 0)
        ec = lax.broadcasted_iota(jnp.int32, (N_EXPERTS, LANES), 1)
        last_row = jnp.sum(jnp.where(er == ec, last, 0.0), axis=0, keepdims=True)
        last_row = jnp.where(ec[0:1] < N_EXPERTS, last_row, -1.0)
        nu_ref[...] = jnp.concatenate([n_used, last_row, jnp.zeros((6, LANES), F32)], axis=0).astype(jnp.int32)

    @pl.when(ph == 1)
    def _():
        rr = lax.broadcasted_iota(jnp.int32, (tn, tn), 0)
        cc = lax.broadcasted_iota(jnp.int32, (tn, tn), 1)
        upper = jnp.where(rr < cc, 1.0, 0.0).astype(BF16)
        prior = jnp.dot(hot.astype(BF16), upper, preferred_element_type=F32)
        pos = prior + jnp.concatenate([start_scr[...] + run_scr[...]] * (tn // LANES), axis=1)
        rows = []
        for k in range(TOP_K):
            rows.append(jnp.sum(jnp.where(e_io == e_ref[k:k + 1, :], pos, 0.0), axis=0, keepdims=True))
        rows.append(jnp.zeros((8 - TOP_K, tn), F32))
        dest_ref[0] = jnp.concatenate(rows, axis=0).astype(jnp.int32)
        run_scr[...] += jnp.broadcast_to(tile_cnt, run_scr.shape)


def _dispatch(top_e, *, blk, n_blocks):
    n = top_e.shape[1]
    tn = LANES
    nbp = -(-n_blocks // LANES) * LANES
    kern = functools.partial(_dispatch_kernel, blk=blk)
    return pl.pallas_call(
        kern,
        grid=(2, n // tn),
        in_specs=[pl.BlockSpec((8, tn), lambda ph, i: (0, i))],
        out_specs=[pl.BlockSpec((1, 8, tn), lambda ph, i: (i * ph, 0, 0)),
                   pl.BlockSpec((1, nbp), lambda ph, i: (0, 0)),
                   pl.BlockSpec((8, LANES), lambda ph, i: (0, 0))],
        out_shape=[jax.ShapeDtypeStruct((n // tn, 8, tn), jnp.int32),
                   jax.ShapeDtypeStruct((1, nbp), jnp.int32),
                   jax.ShapeDtypeStruct((8, LANES), jnp.int32)],
        scratch_shapes=[pltpu.VMEM((N_EXPERTS, LANES), F32)] * 3,
        compiler_params=_cparams(("arbitrary", "arbitrary")),
    )(top_e)


def _scatter_rows_kernel(dest_ref, meta_ref, x_ref, o_ref, zbuf, sem, zsem, *, blk, n_blocks):
    tm = x_ref.shape[0]

    @pl.when(pl.program_id(0) == 0)
    def _():
        zbuf[...] = jnp.zeros(zbuf.shape, zbuf.dtype)

        def zero_copy(row):
            return pltpu.make_async_copy(zbuf, o_ref.at[pl.ds(pl.multiple_of(row, blk), blk)], zsem)

        def each_block(fn):
            def expert(e, carry):
                row = meta_ref[1, e]

                @pl.when(row >= 0)
                def _():
                    fn(zero_copy(row))
                return carry

            def tail(i, carry):
                fn(zero_copy(i * blk))
                return carry

            lax.fori_loop(0, N_EXPERTS, expert, 0)
            lax.fori_loop(meta_ref[0, 0], n_blocks, tail, 0)

        each_block(lambda cp: cp.start())
        each_block(lambda cp: cp.wait())

    def row_copy(t, k):
        return pltpu.make_async_copy(x_ref.at[pl.ds(t, 1)], o_ref.at[pl.ds(dest_ref[0, k, t], 1)], sem)

    def issue(t, carry):
        for k in range(TOP_K):
            row_copy(t, k).start()
        return carry

    def drain(t, carry):
        for k in range(TOP_K):
            row_copy(t, k).wait()
        return carry

    lax.fori_loop(0, tm, issue, 0)
    lax.fori_loop(0, tm, drain, 0)


def _scatter_rows(dest, meta, x, *, blk, n_blocks):
    n, dh = x.shape
    tm = LANES
    kern = functools.partial(_scatter_rows_kernel, blk=blk, n_blocks=n_blocks)
    return pl.pallas_call(
        kern,
        grid=(n // tm,),
        in_specs=[pl.BlockSpec((1, 8, tm), lambda i: (i, 0, 0), memory_space=pltpu.SMEM),
                  pl.BlockSpec(memory_space=pltpu.SMEM),
                  pl.BlockSpec((tm, dh), lambda i: (i, 0))],
        out_specs=pl.BlockSpec(memory_space=pl.ANY),
        out_shape=jax.ShapeDtypeStruct((n_blocks * blk, dh), x.dtype),
        scratch_shapes=[pltpu.VMEM((blk, dh), x.dtype), pltpu.SemaphoreType.DMA(()),
                        pltpu.SemaphoreType.DMA(())],
        compiler_params=_cparams(("arbitrary",)),
    )(dest, meta, x)


def _combine_kernel(dest0_ref, dest1_ref, y_ref, gate_ref, f_ref, xa_ref, sg_ref, su_ref, sd_ref, gp_ref,
                    g2_ref, o_ref, buf, sem):
    halves = (dest0_ref, dest1_ref)

    def row_copy(half, t, k):
        return pltpu.make_async_copy(y_ref.at[pl.ds(halves[half][0, k, t], 1)],
                                     buf.at[k, pl.ds(half * LANES + t, 1)], sem)

    def issue(t, carry):
        for half in range(2):
            for k in range(TOP_K):
                row_copy(half, t, k).start()
        return carry

    def drain(t, carry):
        for half in range(2):
            for k in range(TOP_K):
                row_copy(half, t, k).wait()
        return carry

    lax.fori_loop(0, LANES, issue, 0)
    f = _swiglu(_unpack_bf16_pairs(f_ref[...]), sg_ref[...], su_ref[...], sd_ref[...])
    lax.fori_loop(0, LANES, drain, 0)
    gate = gate_ref[...]
    for k in range(TOP_K):
        f = f + gate[:, k:k + 1] * _unpack_bf16_pairs(buf[k]).astype(F32)
    fn = f * lax.rsqrt(jnp.mean(f * f, axis=-1, keepdims=True) + NORM_EPS) * gp_ref[...]
    o_ref[0] = xa_ref[0] + g2_ref[0] * fn


def _combine(dest, y, gate, f_pk, xa, sg, su, sd, g_post, g2, *, row_tile0, seg_tiles):
    b, rows, d = xa.shape
    tm = 2 * LANES
    assert tm == ROW_TILE
    nt = rows // tm
    mod_map = lambda bi, i: (bi * 2 + jnp.where(i + row_tile0 >= seg_tiles, 1, 0), 0, 0)
    const = lambda shape: pl.BlockSpec(shape, lambda bi, i: (0,) * len(shape))
    dest_spec = lambda half: pl.BlockSpec((1, 8, LANES), lambda bi, i: (2 * (bi * nt + i) + half, 0, 0),
                                          memory_space=pltpu.SMEM)
    return pl.pallas_call(
        _combine_kernel,
        grid=(b, nt),
        in_specs=[
            dest_spec(0), dest_spec(1),
            pl.BlockSpec(memory_space=pl.ANY),
            pl.BlockSpec((tm, 8), lambda bi, i: (bi * nt + i, 0)),
            pl.BlockSpec((tm, d // 2), lambda bi, i: (bi * nt + i, 0)),
            pl.BlockSpec((1, tm, d), lambda bi, i: (bi, i, 0)),
            const(sg.shape), const(su.shape), const(sd.shape), const((1, d)),
            pl.BlockSpec((1, 1, d), mod_map),
        ],
        out_specs=pl.BlockSpec((1, tm, d), lambda bi, i: (bi, i, 0)),
        out_shape=jax.ShapeDtypeStruct((b, rows, d), F32),
        scratch_shapes=[pltpu.VMEM((TOP_K, tm, d // 2), jnp.uint32), pltpu.SemaphoreType.DMA(())],
        compiler_params=_cparams(("arbitrary", "arbitrary")),
    )(dest, dest, y, gate, f_pk, xa, sg, su, sd, g_post.reshape(1, d).astype(F32), g2)


def _rope_tables(s, ctx_len, dim):
    rows = s // GRID_W
    row = jnp.repeat(jnp.arange(rows), GRID_W)
    col = jnp.tile(jnp.arange(GRID_W), rows)
    quarter = dim // 4
    inv = ROPE_BASE ** (-jnp.arange(quarter, dtype=F32) / quarter)
    ang = jnp.concatenate([row.astype(F32)[:, None] * inv, col.astype(F32)[:, None] * inv], axis=-1)
    cos = jnp.repeat(jnp.cos(ang), 2, axis=-1)
    sin = jnp.repeat(jnp.sin(ang), 2, axis=-1) * jnp.tile(jnp.array([-1.0, 1.0], F32), dim // 2)
    cos = jnp.concatenate([jnp.ones((ctx_len, dim), F32), cos], axis=0)
    sin = jnp.concatenate([jnp.zeros((ctx_len, dim), F32), sin], axis=0)
    return cos, sin


def _log_sigmoid(x):
    return jnp.minimum(x, 0.0) - jnp.log1p(jnp.exp(-jnp.abs(x)))


def _mlstm_kernel(q_ref, k_ref, kt_ref, v_ref, g_ref, gt_ref, gb_ref, gbt_ref, hf_ref, mo_ref, mn_ref,
                  o_ref, s_scr, m_scr, *, reverse, final):
    L = MLSTM_CHUNK
    hv = MLSTM_V
    io, fo = (2 * MLSTM_HEADS, 3 * MLSTM_HEADS) if reverse else (0, MLSTM_HEADS)

    @pl.when(pl.program_id(1) == 0)
    def _():
        s_scr[...] = jnp.zeros(s_scr.shape, F32)
        m_scr[...] = jnp.zeros(m_scr.shape, F32)

    g = g_ref[0] + gb_ref[...]
    gt = gt_ref[0] + gbt_ref[...]
    r_io = lax.broadcasted_iota(jnp.int32, (L, L), 0)
    c_io = lax.broadcasted_iota(jnp.int32, (L, L), 1)
    seen = (c_io >= r_io) if reverse else (c_io <= r_io)
    tri = jnp.where(seen, 1.0, 0.0)
    hi = lax.Precision.HIGHEST
    bc_col = jnp.dot(tri, _log_sigmoid(g), precision=hi, preferred_element_type=F32)
    lf_row = _log_sigmoid(gt)
    bc_row = lax.dot_general(lf_row, tri, (((1,), (1,)), ((), ())), precision=hi,
                             preferred_element_type=F32)
    lane = lax.broadcasted_iota(jnp.int32, (L, LANES), 1)
    sub = lax.broadcasted_iota(jnp.int32, (LANES, L), 0)

    for h in range(MLSTM_HEADS):
        pair, odd = h // 2, h % 2
        lo = odd * MLSTM_QK
        a_col = bc_col[:, fo + h:fo + h + 1]
        i_col = g[:, io + h:io + h + 1]
        b_row = bc_row[fo + h:fo + h + 1, :]
        i_row = gt[io + h:io + h + 1, :]
        btot = jnp.sum(lf_row[fo + h:fo + h + 1, :], axis=1, keepdims=True)
        m_st = m_scr[h:h + 1, 0:1]

        w_end = btot - a_col + i_col
        m_new = jnp.maximum(btot + m_st, jnp.max(w_end, axis=0, keepdims=True))
        decay = jnp.exp(btot + m_st - m_new)
        w_k = jnp.exp(w_end - m_new)

        log_d = jnp.where(seen, a_col - b_row + i_row, -jnp.inf)
        log_inter = a_col + m_st
        m_row = jnp.maximum(log_inter, jnp.max(log_d, axis=1, keepdims=True))
        w_intra = jnp.exp(log_d - m_row)
        w_inter = jnp.exp(log_inter - m_row)

        in_head = (lane >= lo) & (lane < lo + MLSTM_QK)
        qm = jnp.where(in_head, q_ref[0, :, pair * LANES:(pair + 1) * LANES], 0).astype(BF16)
        kp = k_ref[0, :, pair * LANES:(pair + 1) * LANES]
        v = v_ref[0, :, h * hv:(h + 1) * hv]
        state = s_scr[h]

        qk = lax.dot_general(qm, kp, (((1,), (1,)), ((), ())), preferred_element_type=F32)
        qk = qk * (MLSTM_QK ** -0.5) * w_intra
        inter = jnp.dot(qm, state.astype(BF16), preferred_element_type=F32)
        num = w_inter * inter[:, :hv] + jnp.dot(qk.astype(BF16), v, preferred_element_type=F32)
        den = w_inter * inter[:, hv:hv + 1] + jnp.sum(qk, axis=1, keepdims=True)
        out = num / jnp.maximum(jnp.abs(den), jnp.exp(-m_row))

        wv = jnp.concatenate([(w_k * v.astype(F32)).astype(BF16),
                              jnp.where(lane == 0, w_k, 0.0).astype(BF16)], axis=1)
        kt = kt_ref[0, pair * LANES:(pair + 1) * LANES, :]
        in_rows = (sub >= lo) & (sub < lo + MLSTM_QK)
        ktm = (jnp.where(in_rows, kt, 0).astype(F32) * (MLSTM_QK ** -0.5)).astype(BF16)
        s_scr[h] = decay * state + jnp.dot(ktm, wv, preferred_element_type=F32)
        m_scr[h:h + 1, :] = jnp.broadcast_to(m_new, (1, LANES))

        if final:
            tot = out + hf_ref[0, :, h * hv:(h + 1) * hv]
            nrm = tot * lax.rsqrt(jnp.mean(tot * tot, axis=-1, keepdims=True) + NORM_EPS) * mn_ref[...]
            gate = jax.nn.sigmoid(mo_ref[0, :, h * hv:(h + 1) * hv].astype(F32))
            o_ref[0, :, h * hv:(h + 1) * hv] = (nrm * gate).astype(o_ref.dtype)
        else:
            o_ref[0, :, h * hv:(h + 1) * hv] = out.astype(o_ref.dtype)


def _mlstm(pv, kt, gates, gates_t, gate_b, m_norm, hf, *, ctx_chunks, reverse):
    b, t, _ = pv.shape
    L = MLSTM_CHUNK
    nc = t // L
    final = hf is not None
    nq = MLSTM_HEADS * MLSTM_QK
    nv = MLSTM_HEADS * MLSTM_V

    def chunk(j):
        if not reverse:
            return j
        return jnp.where(j < ctx_chunks, ctx_chunks - 1 - j, nc - 1 - (j - ctx_chunks))

    gb = jnp.pad(gate_b.astype(F32).reshape(-1), (0, LANES - gate_b.size))
    if hf is None:
        hf = jnp.zeros((1, L, nv), F32)
        hf_spec = pl.BlockSpec((1, L, nv), lambda bi, j: (0, 0, 0))
    else:
        hf_spec = pl.BlockSpec((1, L, nv), lambda bi, j: (bi, chunk(j), 0))
    kern = functools.partial(_mlstm_kernel, reverse=reverse, final=final)
    return pl.pallas_call(
        kern,
        grid=(b, nc),
        in_specs=[
            pl.BlockSpec((1, L, nq), lambda bi, j: (bi, chunk(j), nv // nq)),
            pl.BlockSpec((1, L, nq), lambda bi, j: (bi, chunk(j), nv // nq + 1)),
            pl.BlockSpec((1, nq, L), lambda bi, j: (bi, 0, chunk(j))),
            pl.BlockSpec((1, L, nv), lambda bi, j: (bi, chunk(j), 2)),
            pl.BlockSpec((1, L, LANES), lambda bi, j: (bi, chunk(j), 0)),
            pl.BlockSpec((1, LANES, L), lambda bi, j: (bi, 0, chunk(j))),
            pl.BlockSpec((1, LANES), lambda bi, j: (0, 0)),
            pl.BlockSpec((LANES, 1), lambda bi, j: (0, 0)),
            hf_spec,
            pl.BlockSpec((1, L, nv), lambda bi, j: (bi, chunk(j), 3)),
            pl.BlockSpec((1, MLSTM_V), lambda bi, j: (0, 0)),
        ],
        out_specs=pl.BlockSpec((1, L, nv), lambda bi, j: (bi, chunk(j), 0)),
        out_shape=jax.ShapeDtypeStruct((b, t, nv), BF16 if final else F32),
        scratch_shapes=[pltpu.VMEM((MLSTM_HEADS, LANES, 2 * LANES), F32), pltpu.VMEM((MLSTM_HEADS, LANES), F32)],
        compiler_params=_cparams(("parallel", "arbitrary")),
    )(pv, pv, kt, pv, gates, gates_t, gb.reshape(1, LANES), gb.reshape(LANES, 1), hf, pv,
      m_norm.reshape(1, MLSTM_V).astype(F32))


def _swa_kernel(sink_ref, q_ref, kp_ref, kc_ref, kn_ref, kx_ref, vp_ref, vc_ref, vn_ref, vx_ref, o_ref, *,
                n_blocks):
    i = pl.program_id(1)
    bq = BLOCK_Q
    hd = SWA_HD
    rep = SWA_HEADS // SWA_KV_HEADS
    n_ctx = kx_ref.shape[1]
    rows = rep * bq
    scale = hd ** -0.5 * LOG2E
    row = lax.broadcasted_iota(jnp.int32, (rows, 3 * bq), 0)
    col = lax.broadcasted_iota(jnp.int32, (rows, 3 * bq), 1)
    dt = (col - bq) - (row % bq)
    blk = col // bq
    ok = (jnp.abs(dt) <= WINDOW) & ((blk != 0) | (i > 0)) & ((blk != 2) | (i < n_blocks - 1))
    head_of_row = lax.broadcasted_iota(jnp.int32, (rows, 1), 0) // bq
    for g in range(SWA_KV_HEADS):
        q = jnp.concatenate([q_ref[0, :, (g * rep + r) * hd:(g * rep + r + 1) * hd] for r in range(rep)], axis=0)
        q = (q.astype(F32) * scale).astype(BF16)
        cs = slice(g * hd, (g + 1) * hd)
        k_loc = jnp.concatenate([kp_ref[0, :, cs], kc_ref[0, :, cs], kn_ref[0, :, cs]], axis=0)
        v_loc = jnp.concatenate([vp_ref[0, :, cs], vc_ref[0, :, cs], vn_ref[0, :, cs]], axis=0)
        nt = (((1,), (1,)), ((), ()))
        s_loc = jnp.where(ok, lax.dot_general(q, k_loc, nt, preferred_element_type=F32), -jnp.inf)
        s_ctx = lax.dot_general(q, kx_ref[0, :, cs], nt, preferred_element_type=F32)
        sink = jnp.zeros((rows, 1), F32)
        for r in range(rep):
            sink = jnp.where(head_of_row == r, sink_ref[g * rep + r] * LOG2E, sink)
        m = jnp.maximum(jnp.maximum(jnp.max(s_loc, axis=-1, keepdims=True),
                                    jnp.max(s_ctx, axis=-1, keepdims=True)), sink)
        p_loc = jnp.exp2(s_loc - m)
        p_ctx = jnp.exp2(s_ctx - m)
        den = (jnp.sum(p_loc, axis=-1, keepdims=True) + jnp.sum(p_ctx, axis=-1, keepdims=True)
               + jnp.exp2(sink - m))
        out = (jnp.dot(p_loc.astype(BF16), v_loc, preferred_element_type=F32)
               + jnp.dot(p_ctx.astype(BF16), vx_ref[0, :, cs], preferred_element_type=F32)) / den
        for r in range(rep):
            o_ref[0, :, (g * rep + r) * hd:(g * rep + r + 1) * hd] = out[r * bq:(r + 1) * bq].astype(o_ref.dtype)


def _swa(qk, v, sink, *, n_ctx, k_col0):
    b, t, _ = qk.shape
    bq = BLOCK_Q
    cb = n_ctx // bq
    nb = (t - n_ctx) // bq
    nq = SWA_HEADS * SWA_HD
    nk = SWA_KV_HEADS * SWA_HD
    kcol = k_col0
    prev = lambda bi, i, c: (bi, i + cb - 1, c)
    cur = lambda bi, i, c: (bi, i + cb, c)
    nxt = lambda bi, i, c: (bi, jnp.minimum(i + cb + 1, nb + cb - 1), c)
    kern = functools.partial(_swa_kernel, n_blocks=nb)

    def spec(rows, fn, c):
        return pl.BlockSpec((1, rows, nk), lambda bi, i: fn(bi, i, c))

    ctx = lambda bi, i, c: (bi, 0, c)
    return pl.pallas_call(
        kern,
        grid=(b, nb),
        in_specs=[
            pl.BlockSpec(memory_space=pltpu.SMEM),
            pl.BlockSpec((1, bq, nq), lambda bi, i: (bi, i + cb, 0)),
            spec(bq, prev, kcol), spec(bq, cur, kcol), spec(bq, nxt, kcol), spec(n_ctx, ctx, kcol),
            spec(bq, prev, 0), spec(bq, cur, 0), spec(bq, nxt, 0), spec(n_ctx, ctx, 0),
        ],
        out_specs=pl.BlockSpec((1, bq, nq), lambda bi, i: (bi, i, 0)),
        out_shape=jax.ShapeDtypeStruct((b, nb * bq, nq), BF16),
        compiler_params=_cparams(("parallel", "arbitrary")),
    )(sink.astype(F32), qk, qk, qk, qk, qk, v, v, v, v)


def _mix_out_and_moe(mix, xs, w_out, g_post, g_ffn_pre, g_ffn_post, mods, router_w, router_b,
                     wg, wu, wd, sg, su, sd, *, layer, row_tile0, seg_tiles):
    sh1, sc1, g1, sh2, sc2, g2 = mods
    b, rows, _ = mix[0].shape
    n = b * rows
    seg = dict(row_tile0=row_tile0, seg_tiles=seg_tiles)
    xa, f_pk, logits_t = _post_mix(mix, xs, w_out.astype(BF16), g_post, g_ffn_pre, g1, sh2, sc2,
                                   router_w.T.astype(BF16), **seg)
    top_e, gate = _route(logits_t, router_b, tn=512)
    blk = EXPERT_ROWS
    n_blocks = -(-n * TOP_K // blk) + N_EXPERTS
    dest, block_e, n_used = _dispatch(top_e, blk=blk, n_blocks=n_blocks)
    f_pk = f_pk.reshape(n, -1)
    x_sorted = _scatter_rows(dest, n_used, f_pk, blk=blk, n_blocks=n_blocks)
    y = _expert_ffn(x_sorted, block_e[0, :n_blocks], n_used[0, :1], wg, wu, wd, layer=layer, tm=blk)
    return _combine(dest, y, gate.T, f_pk, xa, sg.astype(BF16), su.astype(BF16), sd.astype(BF16),
                    g_ffn_post, g2, **seg)


def kernel(x, c, ctx, c_ctx, mod_w, mod_b, norm_mix_pre, norm_mix_post, norm_ffn_pre, norm_ffn_post, ab_w_in, ab_w_out, diff_lambda, diff_norm, mlstm_gate_b, mlstm_norm, cd_w_in, cd_w_out, mla_q_norm, mla_w_uq, mla_kv_norm, mla_w_ukv, swa_sink, router_w, router_b, exp_w_gate, exp_w_up, exp_w_down, sh_w_gate, sh_w_up, sh_w_down):
    b, s, d = x.shape
    L = ctx.shape[1]
    t = L + s
    assert L == ROW_TILE and s % ROW_TILE == 0
    depth = mod_w.shape[0]
    ctx_tiles = L // ROW_TILE
    tk = next(c for c in (2816, 768, ROW_TILE) if t % c == 0)

    cos64, sin64 = _rope_tables(s, L, DIFF_HD)
    cos64 = jnp.tile(cos64, (1, LANES // DIFF_HD))
    sin64 = jnp.tile(sin64, (1, LANES // DIFF_HD))
    cos128, sin128 = _rope_tables(s, L, SWA_HD)
    cos_kr = jnp.concatenate([cos64[:, :MLA_ROPE], jnp.ones((t, LANES - MLA_ROPE), F32)], axis=1)
    sin_kr = jnp.concatenate([sin64[:, :MLA_ROPE], jnp.zeros((t, LANES - MLA_ROPE), F32)], axis=1)

    xs = jnp.concatenate([ctx, x], axis=1)

    for layer in range(depth):
        with_ctx = layer < depth - 1
        j = layer // 2
        mod_l = jax.nn.silu(c) @ mod_w[layer] + mod_b[layer]
        mod_c = jax.nn.silu(c_ctx) @ mod_w[layer] + mod_b[layer]
        mods = jnp.stack([jnp.broadcast_to(mod_c, (b, 6 * d)), mod_l], axis=1).reshape(b * 2, 1, 6, d)
        sh1, sc1, g1, sh2, sc2, g2 = (mods[:, :, m] for m in range(6))

        proj = functools.partial(_norm_proj, xs, norm_mix_pre[layer], sh1, sc1, seg_tiles=ctx_tiles)
        if layer % 2 == 0:
            w_in = ab_w_in[j].astype(BF16)
            n_qk = 2 * DIFF_HEADS * 2 * DIFF_HD
            qk = proj(w_in[:, :n_qk], tn=2048, rope=(cos64[None], sin64[None]), pattern=(0,) * 16)
            pv = proj(w_in[:, n_qk:n_qk + AB_PLAIN], tn=2048)
            w_gates = jnp.pad(w_in[:, n_qk + AB_PLAIN:], ((0, 0), (0, LANES - 4 * MLSTM_HEADS)))
            gates = proj(w_gates, tn=LANES, out_dtype=F32)

            lam_init = 0.8 - 0.6 * math.exp(-0.3 * layer)
            a = _attention(qk if with_ctx else qk[:, L:], qk, pv, heads=DIFF_HEADS, dq=2 * DIFF_HD,
                           dv=DIFF_VD, q_col0=0, k_col0=DIFF_HEADS, v_col0=0, tq=ROW_TILE,
                           scale=DIFF_HD ** -0.5, tk=tk, diff=True, lam_vec=diff_lambda[j],
                           d_norm=diff_norm[j], lam_init=lam_init,
                           ctx_tiles=ctx_tiles if with_ctx else 0, ctx_len=L)
            o_mk = DIFF_HEADS * DIFF_VD + MLSTM_HEADS * MLSTM_QK
            kt = jnp.swapaxes(pv[..., o_mk:o_mk + MLSTM_HEADS * MLSTM_QK], 1, 2)
            scan = functools.partial(_mlstm, pv, kt, gates, jnp.swapaxes(gates, 1, 2), mlstm_gate_b[j],
                                     mlstm_norm[j], ctx_chunks=L // MLSTM_CHUNK)
            m = scan(scan(None, reverse=False), reverse=True)
            mix = (a, m if with_ctx else m[:, L:])
            w_out = ab_w_out[j]
        else:
            assert not with_ctx
            w_in = cd_w_in[j].astype(BF16)
            c0 = Q_LORA
            c1 = c0 + KV_LORA
            c2 = c1 + MLA_ROPE
            c3 = c2 + SWA_HEADS * SWA_HD
            c4 = c3 + SWA_KV_HEADS * SWA_HD
            w_rope = jnp.concatenate([w_in[:, c2:c4], w_in[:, c1:c2],
                                      jnp.zeros((d, LANES - MLA_ROPE), BF16)], axis=1)
            n_rope = w_rope.shape[1]
            rp = proj(w_rope, tn=n_rope, rope=(jnp.stack([cos128, cos_kr]), jnp.stack([sin128, sin_kr])),
                      pattern=(0,) * (n_rope // LANES - 1) + (1,))
            q_pad = 512 - Q_LORA
            w_plain = jnp.concatenate([w_in[:, c4:], w_in[:, :c0], jnp.zeros((d, q_pad), BF16),
                                       w_in[:, c0:c1]], axis=1)
            pp = proj(w_plain, tn=w_plain.shape[1])
            n_sv = SWA_KV_HEADS * SWA_HD
            cq = pp[..., n_sv:n_sv + 512]
            ckv = pp[..., n_sv + 512:]

            hq = MLA_NOPE + MLA_ROPE
            w_uq = mla_w_uq[j].astype(BF16).reshape(Q_LORA, MLA_HEADS, hq)
            w_uq = jnp.pad(w_uq, ((0, q_pad), (0, 0), (0, 256 - hq))).reshape(512, MLA_HEADS * 256)
            no_mod = jnp.zeros((b * 2, 1, 512), F32)
            qn = jnp.pad(mla_q_norm[j], (0, q_pad))
            q_mla = _norm_proj(cq[:, L:], qn, no_mod, no_mod, w_uq, seg_tiles=0, tn=2048, n_valid=Q_LORA,
                               rope=(cos_kr[None, L:], sin_kr[None, L:]), pattern=(-1, 0) * 8)
            kv = _norm_proj(ckv, mla_kv_norm[j], no_mod, no_mod, mla_w_ukv[j].astype(BF16),
                            seg_tiles=ctx_tiles, tn=2048).reshape(b, t, MLA_HEADS, MLA_NOPE + MLA_V)
            kr = rp[..., n_rope - LANES:]
            k_mla = jnp.concatenate([kv[..., :MLA_NOPE],
                                     jnp.broadcast_to(kr[:, :, None, :], (b, t, MLA_HEADS, LANES))],
                                    axis=-1).reshape(b, t, MLA_HEADS * 256)
            v_mla = kv[..., MLA_NOPE:].reshape(b, t, MLA_HEADS * MLA_V)
            a = _attention(q_mla, k_mla, v_mla, heads=MLA_HEADS, dq=256, dv=MLA_V, q_col0=0, k_col0=0,
                           v_col0=0, tq=512, scale=MLA_SCALE, tk=tk)

            w = _swa(rp, pp, swa_sink[j], n_ctx=L, k_col0=SWA_HEADS * SWA_HD // n_sv)
            mix = (a, w)
            w_out = cd_w_out[j]

        xa = _mix_out_and_moe(mix, xs, w_out, norm_mix_post[layer], norm_ffn_pre[layer], norm_ffn_post[layer],
                              (sh1, sc1, g1, sh2, sc2, g2), router_w[layer], router_b[layer],
                              exp_w_gate, exp_w_up, exp_w_down,
                              sh_w_gate[layer], sh_w_up[layer], sh_w_down[layer],
                              layer=layer, row_tile0=0 if with_ctx else ctx_tiles, seg_tiles=ctx_tiles)
        xs = xa if with_ctx else jnp.concatenate([xs[:, :L], xa], axis=1)
    return xs[:, L:]
```

```python
import functools
import math

import jax
import jax.numpy as jnp
import numpy as np
from jax import lax
from jax.experimental import pallas as pl
from jax.experimental.pallas import tpu as pltpu

F32 = jnp.float32
BF16 = jnp.bfloat16

GRID_W = 64
ROPE_BASE = 10000.0
NORM_EPS = 1e-6

DIFF_HEADS = 8
DIFF_HD = 64
DIFF_VD = 2 * DIFF_HD
MLSTM_HEADS = 8
MLSTM_QK = 64
MLSTM_V = 128
MLSTM_CHUNK = 128
MLA_HEADS = 8
MLA_NOPE = 128
MLA_ROPE = 64
MLA_V = 128
Q_LORA = 448
KV_LORA = 512
MLA_SCALE = (MLA_NOPE + MLA_ROPE) ** -0.5
SWA_HEADS = 8
SWA_KV_HEADS = 2
SWA_HD = 128
WINDOW = 128
BLOCK_Q = 128
N_EXPERTS = 64
TOP_K = 6
N_GROUPS = 8
TOPK_GROUPS = 4
ROUTED_SCALE = 2.5
AB_PLAIN = DIFF_HEADS * DIFF_VD + 2 * MLSTM_HEADS * MLSTM_QK + 2 * MLSTM_HEADS * MLSTM_V

LANES = 128
LOG2E = 1.4426950408889634
VMEM_LIMIT = 56 * 1024 * 1024

ROW_TILE = 256
EXPERT_ROWS = 512


def _cparams(sem):
    return pltpu.CompilerParams(dimension_semantics=sem, vmem_limit_bytes=VMEM_LIMIT)


def _pair_swap(a):
    lane = lax.broadcasted_iota(jnp.int32, a.shape, 1)
    return jnp.where(lane % 2 == 0, pltpu.roll(a, LANES - 1, 1), pltpu.roll(a, 1, 1))


FILL = -2


def _norm_proj_kernel(x_ref, g_ref, sh_ref, sc_ref, w_ref, cos_ref, sin_ref, fill_ref, o_ref, *,
                      n_valid, pattern):
    x = x_ref[0].astype(F32)
    ms = jnp.sum(x * x, axis=-1, keepdims=True) * (1.0 / n_valid)
    y = x * lax.rsqrt(ms + NORM_EPS) * g_ref[...]
    h = (y * (1.0 + sc_ref[0]) + sh_ref[0]).astype(BF16)
    acc = jnp.dot(h, w_ref[...], preferred_element_type=F32)
    if pattern is None:
        o_ref[0] = acc.astype(o_ref.dtype)
    else:
        for c, tbl in enumerate(pattern):
            if tbl == FILL:
                o_ref[0, :, c * LANES:(c + 1) * LANES] = fill_ref[0].astype(o_ref.dtype)
                continue
            a = acc[:, c * LANES:(c + 1) * LANES]
            if tbl >= 0:
                a = a * cos_ref[tbl] + _pair_swap(a) * sin_ref[tbl]
            o_ref[0, :, c * LANES:(c + 1) * LANES] = a.astype(o_ref.dtype)


def _norm_proj(x, g, shift, scale, w, *, seg_tiles, tn, out_dtype=BF16, n_valid=None,
               rope=None, pattern=None, fill=None):
    b, t, k = x.shape
    if fill is None:
        fill_arr, fill_map = jnp.zeros((1, ROW_TILE, LANES), BF16), (lambda j, bi, i: (0, 0, 0))
    else:
        fill_arr, fill_map = fill[0], (lambda j, bi, i: (bi, i, fill[1]))
    n = w.shape[1]
    tm = ROW_TILE
    assert t % tm == 0 and n % tn == 0 and tn % LANES == 0
    if rope is None:
        cos = sin = jnp.zeros((1, tm, LANES), F32)
        tbl_map = lambda j, bi, i: (0, 0, 0)
    else:
        cos, sin = rope
        tbl_map = lambda j, bi, i: (0, i, 0)
    ntab = cos.shape[0]
    kern = functools.partial(_norm_proj_kernel, n_valid=float(n_valid or k), pattern=pattern)
    mod_map = lambda j, bi, i: (bi * 2 + jnp.where(i >= seg_tiles, 1, 0), 0, 0)
    return pl.pallas_call(
        kern,
        grid=(n // tn, b, t // tm),
        in_specs=[
            pl.BlockSpec((1, tm, k), lambda j, bi, i: (bi, i, 0)),
            pl.BlockSpec((1, k), lambda j, bi, i: (0, 0)),
            pl.BlockSpec((1, 1, k), mod_map),
            pl.BlockSpec((1, 1, k), mod_map),
            pl.BlockSpec((k, tn), lambda j, bi, i: (0, j)),
            pl.BlockSpec((ntab, tm, LANES), tbl_map),
            pl.BlockSpec((ntab, tm, LANES), tbl_map),
            pl.BlockSpec((1, tm, LANES), fill_map),
        ],
        out_specs=pl.BlockSpec((1, tm, tn), lambda j, bi, i: (bi, i, j)),
        out_shape=jax.ShapeDtypeStruct((b, t, n), out_dtype),
        compiler_params=_cparams(("parallel", "parallel", "parallel")),
    )(x, g.reshape(1, k).astype(F32), shift, scale, w, cos, sin, fill_arr)


def _attn_kernel(q_ref, k_ref, v_ref, lam_ref, dn_ref, o_ref, q_scr, s_scr, m_scr, l_scr, acc_scr, *,
                 scale, diff, lam_init, ctx_tiles, ctx_len, tq, tk, t):
    i = pl.program_id(2)
    q = q_ref[0].astype(F32) * (scale * LOG2E)
    if diff:
        lane = lax.broadcasted_iota(jnp.int32, q.shape, 1)
        half = q.shape[1] // 2
        q_scr[0:tq] = jnp.where(lane < half, q, 0.0).astype(BF16)
        q_scr[tq:2 * tq] = jnp.where(lane >= half, q, 0.0).astype(BF16)
    else:
        q_scr[...] = q.astype(BF16)

    def attend(kv_len, chunk):
        n_chunks = kv_len // chunk
        groups = chunk // LANES

        def pass1(c, carry):
            k = k_ref[0, pl.ds(pl.multiple_of(c * chunk, chunk), chunk), :]
            s = lax.dot_general(q_scr[...], k, (((1,), (1,)), ((), ())), preferred_element_type=F32)
            s_scr[c, :, 0:chunk] = s
            m = m_scr[...]
            for g in range(groups):
                m = jnp.maximum(m, s[:, g * LANES:(g + 1) * LANES])
            m_scr[...] = m
            return carry

        def pass2(c, carry):
            s = s_scr[c, :, 0:chunk]
            p = jnp.exp2(s - jnp.concatenate([m_scr[...]] * groups, axis=1))
            l = l_scr[...]
            for g in range(groups):
                l = l + p[:, g * LANES:(g + 1) * LANES]
            l_scr[...] = l
            v = v_ref[0, pl.ds(pl.multiple_of(c * chunk, chunk), chunk), :]
            acc_scr[...] += jnp.dot(p.astype(BF16), v, preferred_element_type=F32)
            return carry

        m_scr[...] = jnp.full(m_scr.shape, -jnp.inf, F32)
        if n_chunks == 1:
            pass1(0, 0)
        else:
            lax.fori_loop(0, n_chunks, pass1, 0)
        m_scr[...] = jnp.broadcast_to(jnp.max(m_scr[...], axis=-1, keepdims=True), m_scr.shape)
        l_scr[...] = jnp.zeros(l_scr.shape, F32)
        acc_scr[...] = jnp.zeros(acc_scr.shape, F32)
        if n_chunks == 1:
            pass2(0, 0)
        else:
            lax.fori_loop(0, n_chunks, pass2, 0)

        o = acc_scr[...] / jnp.sum(l_scr[...], axis=-1, keepdims=True)
        if diff:
            lv = lam_ref[...]
            lam = (jnp.exp(jnp.sum(lv[0:1] * lv[1:2], axis=-1, keepdims=True))
                   - jnp.exp(jnp.sum(lv[2:3] * lv[3:4], axis=-1, keepdims=True)) + lam_init)
            a = o[0:tq] - lam * o[tq:2 * tq]
            ms = jnp.mean(a * a, axis=-1, keepdims=True)
            a = a * lax.rsqrt(ms + NORM_EPS) * dn_ref[...] * (1.0 - lam_init)
            o_ref[0] = a.astype(o_ref.dtype)
        else:
            o_ref[0] = o.astype(o_ref.dtype)

    if ctx_tiles:
        pl.when(i < ctx_tiles)(lambda: attend(ctx_len, ctx_len))
        pl.when(i >= ctx_tiles)(lambda: attend(t, tk))
    else:
        attend(t, tk)


def _attention(q, k, v, *, heads, dq, dv, q_col0, k_col0, v_col0, tq, scale, tk, diff=False,
               lam_vec=None, d_norm=None, lam_init=0.0, ctx_tiles=0, ctx_len=0):
    b, t, _ = k.shape
    sq = q.shape[1]
    assert t % tk == 0 and sq % tq == 0 and tk % LANES == 0 and ctx_len % LANES == 0
    assert ctx_len <= tk
    rows = 2 * tq if diff else tq
    if lam_vec is None:
        lam_vec = jnp.zeros((4, DIFF_HD), F32)
        d_norm = jnp.zeros((dv,), F32)
    kern = functools.partial(_attn_kernel, scale=scale, diff=diff, lam_init=lam_init,
                             ctx_tiles=ctx_tiles, ctx_len=ctx_len, tq=tq, tk=tk, t=t)
    return pl.pallas_call(
        kern,
        grid=(b, heads, sq // tq),
        in_specs=[
            pl.BlockSpec((1, tq, dq), lambda bi, h, i: (bi, i, q_col0 + h)),
            pl.BlockSpec((1, t, dq), lambda bi, h, i: (bi, 0, k_col0 + h)),
            pl.BlockSpec((1, t, dv), lambda bi, h, i: (bi, 0, v_col0 + h)),
            pl.BlockSpec((4, DIFF_HD), lambda bi, h, i: (0, 0)),
            pl.BlockSpec((1, dv), lambda bi, h, i: (0, 0)),
        ],
        out_specs=pl.BlockSpec((1, tq, dv), lambda bi, h, i: (bi, i, h)),
        out_shape=jax.ShapeDtypeStruct((b, sq, heads * dv), BF16),
        scratch_shapes=[pltpu.VMEM((rows, dq), BF16), pltpu.VMEM((t // tk, rows, tk), F32),
                        pltpu.VMEM((rows, LANES), F32), pltpu.VMEM((rows, LANES), F32),
                        pltpu.VMEM((rows, dv), F32)],
        compiler_params=_cparams(("parallel", "parallel", "arbitrary")),
    )(q, k, v, lam_vec.astype(F32), d_norm.reshape(1, dv).astype(F32))


def _pack_bf16_pairs(x):
    h = x.shape[1] // 2
    lo = lax.bitcast_convert_type(x[:, :h].astype(F32), jnp.uint32) >> 16
    hi = lax.bitcast_convert_type(x[:, h:].astype(F32), jnp.uint32) & jnp.uint32(0xFFFF0000)
    return hi | lo


def _unpack_bf16_pairs(w):
    lo = lax.bitcast_convert_type(w << 16, F32)
    hi = lax.bitcast_convert_type(w & jnp.uint32(0xFFFF0000), F32)
    return jnp.concatenate([lo.astype(BF16), hi.astype(BF16)], axis=1)


def _swiglu(x, wg, wu, wd):
    g = jnp.dot(x, wg, preferred_element_type=F32)
    u = jnp.dot(x, wu, preferred_element_type=F32)
    a = (g * jax.nn.sigmoid(g) * u).astype(BF16)
    return jnp.dot(a, wd, preferred_element_type=F32)


def _expert_kernel(be_ref, nu_ref, x_ref, wg_ref, wu_ref, wd_ref, o_ref, wg_s, wu_s, wd_s):
    i = pl.program_id(0)

    @pl.when(i < nu_ref[0])
    def _():
        @pl.when((i == 0) | (be_ref[i] != be_ref[jnp.maximum(i - 1, 0)]))
        def _():
            wg_s[...] = wg_ref[0, 0].astype(BF16)
            wu_s[...] = wu_ref[0, 0].astype(BF16)
            wd_s[...] = wd_ref[0, 0].astype(BF16)

        y = _swiglu(_unpack_bf16_pairs(x_ref[...]), wg_s[...], wu_s[...], wd_s[...])
        o_ref[...] = _pack_bf16_pairs(y.astype(BF16))

    @pl.when(i >= nu_ref[0])
    def _():
        o_ref[...] = jnp.zeros(o_ref.shape, o_ref.dtype)


def _expert_ffn(x, block_e, n_used, wg, wu, wd, *, layer, tm):
    n, dh = x.shape
    d = 2 * dh
    ff = wg.shape[3]
    assert n % tm == 0

    def blk(i, be, nu):
        return jnp.minimum(i, nu[0] - 1)

    grid_spec = pltpu.PrefetchScalarGridSpec(
        num_scalar_prefetch=2,
        grid=(n // tm,),
        in_specs=[
            pl.BlockSpec((tm, dh), lambda i, be, nu: (blk(i, be, nu), 0)),
            pl.BlockSpec((1, 1, d, ff), lambda i, be, nu: (layer, be[blk(i, be, nu)], 0, 0)),
            pl.BlockSpec((1, 1, d, ff), lambda i, be, nu: (layer, be[blk(i, be, nu)], 0, 0)),
            pl.BlockSpec((1, 1, ff, d), lambda i, be, nu: (layer, be[blk(i, be, nu)], 0, 0)),
        ],
        out_specs=pl.BlockSpec((tm, dh), lambda i, be, nu: (i, 0)),
        scratch_shapes=[pltpu.VMEM((d, ff), BF16), pltpu.VMEM((d, ff), BF16), pltpu.VMEM((ff, d), BF16)],
    )
    return pl.pallas_call(
        _expert_kernel,
        grid_spec=grid_spec,
        out_shape=jax.ShapeDtypeStruct((n, dh), jnp.uint32),
        compiler_params=_cparams(("arbitrary",)),
    )(block_e, n_used, x, wg, wu, wd)


def _post_mix_kernel(a_ref, m_ref, x_ref, w_ref, gp_ref, gf_ref, g1_ref, sh_ref, sc_ref, rw_ref,
                     xa_ref, f_ref, lg_ref):
    ka = a_ref.shape[2]
    y = (jnp.dot(a_ref[0], w_ref[0:ka, :], preferred_element_type=F32)
         + jnp.dot(m_ref[0], w_ref[ka:, :], preferred_element_type=F32))
    yn = y * lax.rsqrt(jnp.mean(y * y, axis=-1, keepdims=True) + NORM_EPS) * gp_ref[...]
    xa = x_ref[0] + g1_ref[0] * yn
    xa_ref[0] = xa
    fn = xa * lax.rsqrt(jnp.mean(xa * xa, axis=-1, keepdims=True) + NORM_EPS) * gf_ref[...]
    f = (fn * (1.0 + sc_ref[0]) + sh_ref[0]).astype(BF16)
    f_ref[0] = _pack_bf16_pairs(f)
    lg_ref[...] = lax.dot_general(rw_ref[...], f, (((1,), (1,)), ((), ())), preferred_element_type=F32)


def _post_mix(mix, xs, w_out, g_post, g_ffn, g1, sh2, sc2, router_wt, *, row_tile0, seg_tiles):
    mix_a, mix_m = mix
    b, rows, ka = mix_a.shape
    km = mix_m.shape[2]
    k = ka + km
    d = w_out.shape[1]
    e = router_wt.shape[0]
    tm = ROW_TILE
    nt = rows // tm
    mod_map = lambda bi, i: (bi * 2 + jnp.where(i + row_tile0 >= seg_tiles, 1, 0), 0, 0)
    vec = lambda: pl.BlockSpec((1, d), lambda bi, i: (0, 0))
    return pl.pallas_call(
        _post_mix_kernel,
        grid=(b, nt),
        in_specs=[
            pl.BlockSpec((1, tm, ka), lambda bi, i: (bi, i, 0)),
            pl.BlockSpec((1, tm, km), lambda bi, i: (bi, i, 0)),
            pl.BlockSpec((1, tm, d), lambda bi, i: (bi, i + row_tile0, 0)),
            pl.BlockSpec((k, d), lambda bi, i: (0, 0)),
            vec(), vec(),
            pl.BlockSpec((1, 1, d), mod_map), pl.BlockSpec((1, 1, d), mod_map),
            pl.BlockSpec((1, 1, d), mod_map),
            pl.BlockSpec((e, d), lambda bi, i: (0, 0)),
        ],
        out_specs=[
            pl.BlockSpec((1, tm, d), lambda bi, i: (bi, i, 0)),
            pl.BlockSpec((1, tm, d // 2), lambda bi, i: (bi, i, 0)),
            pl.BlockSpec((e, tm), lambda bi, i: (0, bi * nt + i)),
        ],
        out_shape=[jax.ShapeDtypeStruct((b, rows, d), F32),
                   jax.ShapeDtypeStruct((b, rows, d // 2), jnp.uint32),
                   jax.ShapeDtypeStruct((e, b * rows), F32)],
        compiler_params=_cparams(("parallel", "arbitrary")),
    )(mix_a, mix_m, xs, w_out, g_post.reshape(1, d).astype(F32), g_ffn.reshape(1, d).astype(F32),
      g1, sh2, sc2, router_wt)


def _route_kernel(lg_ref, rb_ref, e_ref, g_ref):
    per = N_EXPERTS // N_GROUPS
    tn = lg_ref.shape[1]
    neg = -jnp.inf
    r_io = lax.broadcasted_iota(jnp.int32, (per, tn), 0)
    scores, choice, gs = [], [], []
    for g in range(N_GROUPS):
        sg = jax.nn.sigmoid(lg_ref[g * per:(g + 1) * per, :])
        cg = sg + rb_ref[g * per:(g + 1) * per, :]
        m1 = jnp.max(cg, axis=0, keepdims=True)
        i1 = jnp.min(jnp.where(cg == m1, r_io, per), axis=0, keepdims=True)
        m2 = jnp.max(jnp.where(r_io == i1, neg, cg), axis=0, keepdims=True)
        scores.append(sg)
        choice.append(cg)
        gs.append(m1 + m2)
    masked = []
    for g in range(N_GROUPS):
        ahead = jnp.zeros((1, tn), jnp.int32)
        for o in range(N_GROUPS):
            if o < g:
                ahead = ahead + jnp.where(gs[o] >= gs[g], 1, 0)
            elif o > g:
                ahead = ahead + jnp.where(gs[o] > gs[g], 1, 0)
        masked.append(jnp.where(ahead < TOPK_GROUPS, choice[g], neg))
    ids, gates = [], []
    for _ in range(TOP_K):
        best = masked[0]
        for g in range(1, N_GROUPS):
            best = jnp.maximum(best, masked[g])
        best = jnp.max(best, axis=0, keepdims=True)
        cand = jnp.where(masked[0] == best, r_io, N_EXPERTS)
        for g in range(1, N_GROUPS):
            cand = jnp.minimum(cand, jnp.where(masked[g] == best, r_io + g * per, N_EXPERTS))
        idx = jnp.min(cand, axis=0, keepdims=True)
        gk = jnp.zeros((per, tn), F32)
        for g in range(N_GROUPS):
            hit = (r_io + g * per) == idx
            gk = gk + jnp.where(hit, scores[g], 0.0)
            masked[g] = jnp.where(hit, neg, masked[g])
        ids.append(idx)
        gates.append(jnp.sum(gk, axis=0, keepdims=True))
    total = gates[0]
    for k in range(1, TOP_K):
        total = total + gates[k]
    pad = 8 - TOP_K
    e_ref[...] = jnp.concatenate(ids + [jnp.zeros((pad, tn), jnp.int32)], axis=0)
    g_ref[...] = jnp.concatenate([gk / total * ROUTED_SCALE for gk in gates]
                                 + [jnp.zeros((pad, tn), F32)], axis=0)


def _route(logits_t, router_b, *, tn):
    e, n = logits_t.shape
    assert n % tn == 0
    return pl.pallas_call(
        _route_kernel,
        grid=(n // tn,),
        in_specs=[pl.BlockSpec((e, tn), lambda i: (0, i)),
                  pl.BlockSpec((e, 1), lambda i: (0, 0))],
        out_specs=[pl.BlockSpec((8, tn), lambda i: (0, i)), pl.BlockSpec((8, tn), lambda i: (0, i))],
        out_shape=[jax.ShapeDtypeStruct((8, n), jnp.int32), jax.ShapeDtypeStruct((8, n), F32)],
        compiler_params=_cparams(("parallel",)),
    )(logits_t, router_b.reshape(e, 1).astype(F32))


def _dispatch_kernel(e_ref, dest_ref, be_ref, nu_ref, cnt_scr, start_scr, run_scr, *, blk):
    ph = pl.program_id(0)
    i = pl.program_id(1)
    tn = e_ref.shape[1]
    e_io = lax.broadcasted_iota(jnp.int32, (N_EXPERTS, tn), 0)
    hot = jnp.zeros((N_EXPERTS, tn), F32)
    for k in range(TOP_K):
        hot = hot + jnp.where(e_io == e_ref[k:k + 1, :], 1.0, 0.0)
    tile_cnt = jnp.sum(hot, axis=1, keepdims=True)

    @pl.when((ph == 0) & (i == 0))
    def _():
        cnt_scr[...] = jnp.zeros(cnt_scr.shape, F32)

    @pl.when(ph == 0)
    def _():
        cnt_scr[...] += jnp.broadcast_to(tile_cnt, cnt_scr.shape)

    @pl.when((ph == 1) & (i == 0))
    def _():
        bpe = jnp.floor((cnt_scr[...] + (blk - 1.0)) * (1.0 / blk))
        r = lax.broadcasted_iota(jnp.int32, (N_EXPERTS, N_EXPERTS), 0)
        c = lax.broadcasted_iota(jnp.int32, (N_EXPERTS, N_EXPERTS), 1)
        lower = jnp.where(c < r, 1.0, 0.0)
        before = jnp.dot(lower, bpe, precision=lax.Precision.HIGHEST, preferred_element_type=F32)
        start_scr[...] = before * blk
        run_scr[...] = jnp.zeros(run_scr.shape, F32)
        ends = (before + bpe)[:, 0:1]
        nb = be_ref.shape[1]
        bid = lax.broadcasted_iota(jnp.int32, (N_EXPERTS, nb), 1).astype(F32)
        be = jnp.sum(jnp.where(ends <= bid, 1, 0), axis=0, keepdims=True)
        be_ref[...] = jnp.minimum(be, N_EXPERTS - 1).astype(jnp.int32)
        n_used = jnp.broadcast_to(jnp.max(ends, axis=0, keepdims=True), (1, LANES))
        last = jnp.where(bpe > 0.0, (before + bpe - 1.0) * blk, -1.0)
        er = lax.broadcasted_iota(jnp.int32, (N_EXPERTS, LANES), 0)
        ec = lax.broadcasted_iota(jnp.int32, (N_EXPERTS, LANES), 1)
        last_row = jnp.sum(jnp.where(er == ec, last, 0.0), axis=0, keepdims=True)
        last_row = jnp.where(ec[0:1] < N_EXPERTS, last_row, -1.0)
        nu_ref[...] = jnp.concatenate([n_used, last_row, jnp.zeros((6, LANES), F32)], axis=0).astype(jnp.int32)

    @pl.when(ph == 1)
    def _():
        rr = lax.broadcasted_iota(jnp.int32, (tn, tn), 0)
        cc = lax.broadcasted_iota(jnp.int32, (tn, tn), 1)
        upper = jnp.where(rr < cc, 1.0, 0.0).astype(BF16)
        prior = jnp.dot(hot.astype(BF16), upper, preferred_element_type=F32)
        pos = prior + jnp.concatenate([start_scr[...] + run_scr[...]] * (tn // LANES), axis=1)
        rows = []
        for k in range(TOP_K):
            rows.append(jnp.sum(jnp.where(e_io == e_ref[k:k + 1, :], pos, 0.0), axis=0, keepdims=True))
        rows.append(jnp.zeros((8 - TOP_K, tn), F32))
        dest_ref[0] = jnp.concatenate(rows, axis=0).astype(jnp.int32)
        run_scr[...] += jnp.broadcast_to(tile_cnt, run_scr.shape)


def _dispatch(top_e, *, blk, n_blocks):
    n = top_e.shape[1]
    tn = LANES
    nbp = -(-n_blocks // LANES) * LANES
    kern = functools.partial(_dispatch_kernel, blk=blk)
    return pl.pallas_call(
        kern,
        grid=(2, n // tn),
        in_specs=[pl.BlockSpec((8, tn), lambda ph, i: (0, i))],
        out_specs=[pl.BlockSpec((1, 8, tn), lambda ph, i: (i * ph, 0, 0)),
                   pl.BlockSpec((1, nbp), lambda ph, i: (0, 0)),
                   pl.BlockSpec((8, LANES), lambda ph, i: (0, 0))],
        out_shape=[jax.ShapeDtypeStruct((n // tn, 8, tn), jnp.int32),
                   jax.ShapeDtypeStruct((1, nbp), jnp.int32),
                   jax.ShapeDtypeStruct((8, LANES), jnp.int32)],
        scratch_shapes=[pltpu.VMEM((N_EXPERTS, LANES), F32)] * 3,
        compiler_params=_cparams(("arbitrary", "arbitrary")),
    )(top_e)


def _scatter_rows_kernel(dest_ref, meta_ref, x_ref, o_ref, zbuf, sem, zsem, *, blk, n_blocks):
    tm = x_ref.shape[0]

    @pl.when(pl.program_id(0) == 0)
    def _():
        zbuf[...] = jnp.zeros(zbuf.shape, zbuf.dtype)

        def zero_copy(row):
            return pltpu.make_async_copy(zbuf, o_ref.at[pl.ds(pl.multiple_of(row, blk), blk)], zsem)

        def each_block(fn):
            def expert(e, carry):
                row = meta_ref[1, e]

                @pl.when(row >= 0)
                def _():
                    fn(zero_copy(row))
                return carry

            def tail(i, carry):
                fn(zero_copy(i * blk))
                return carry

            lax.fori_loop(0, N_EXPERTS, expert, 0)
            lax.fori_loop(meta_ref[0, 0], n_blocks, tail, 0)

        each_block(lambda cp: cp.start())
        each_block(lambda cp: cp.wait())

    def row_copy(t, k):
        return pltpu.make_async_copy(x_ref.at[pl.ds(t, 1)], o_ref.at[pl.ds(dest_ref[0, k, t], 1)], sem)

    def issue(t, carry):
        for k in range(TOP_K):
            row_copy(t, k).start()
        return carry

    def drain(t, carry):
        for k in range(TOP_K):
            row_copy(t, k).wait()
        return carry

    lax.fori_loop(0, tm, issue, 0)
    lax.fori_loop(0, tm, drain, 0)


def _scatter_rows(dest, meta, x, *, blk, n_blocks):
    n, dh = x.shape
    tm = LANES
    kern = functools.partial(_scatter_rows_kernel, blk=blk, n_blocks=n_blocks)
    return pl.pallas_call(
        kern,
        grid=(n // tm,),
        in_specs=[pl.BlockSpec((1, 8, tm), lambda i: (i, 0, 0), memory_space=pltpu.SMEM),
                  pl.BlockSpec(memory_space=pltpu.SMEM),
                  pl.BlockSpec((tm, dh), lambda i: (i, 0))],
        out_specs=pl.BlockSpec(memory_space=pl.ANY),
        out_shape=jax.ShapeDtypeStruct((n_blocks * blk, dh), x.dtype),
        scratch_shapes=[pltpu.VMEM((blk, dh), x.dtype), pltpu.SemaphoreType.DMA(()),
                        pltpu.SemaphoreType.DMA(())],
        compiler_params=_cparams(("arbitrary",)),
    )(dest, meta, x)


def _combine_kernel(dest0_ref, dest1_ref, y_ref, gate_ref, f_ref, xa_ref, sg_ref, su_ref, sd_ref, gp_ref,
                    g2_ref, o_ref, buf, sem):
    halves = (dest0_ref, dest1_ref)

    def row_copy(half, t, k):
        return pltpu.make_async_copy(y_ref.at[pl.ds(halves[half][0, k, t], 1)],
                                     buf.at[k, pl.ds(half * LANES + t, 1)], sem)

    def issue(t, carry):
        for half in range(2):
            for k in range(TOP_K):
                row_copy(half, t, k).start()
        return carry

    def drain(t, carry):
        for half in range(2):
            for k in range(TOP_K):
                row_copy(half, t, k).wait()
        return carry

    lax.fori_loop(0, LANES, issue, 0)
    f = _swiglu(_unpack_bf16_pairs(f_ref[...]), sg_ref[...], su_ref[...], sd_ref[...])
    lax.fori_loop(0, LANES, drain, 0)
    gate = gate_ref[...]
    for k in range(TOP_K):
        f = f + gate[:, k:k + 1] * _unpack_bf16_pairs(buf[k]).astype(F32)
    fn = f * lax.rsqrt(jnp.mean(f * f, axis=-1, keepdims=True) + NORM_EPS) * gp_ref[...]
    o_ref[0] = xa_ref[0] + g2_ref[0] * fn


def _combine(dest, y, gate, f_pk, xa, sg, su, sd, g_post, g2, *, row_tile0, seg_tiles):
    b, rows, d = xa.shape
    tm = 2 * LANES
    assert tm == ROW_TILE
    nt = rows // tm
    mod_map = lambda bi, i: (bi * 2 + jnp.where(i + row_tile0 >= seg_tiles, 1, 0), 0, 0)
    const = lambda shape: pl.BlockSpec(shape, lambda bi, i: (0,) * len(shape))
    dest_spec = lambda half: pl.BlockSpec((1, 8, LANES), lambda bi, i: (2 * (bi * nt + i) + half, 0, 0),
                                          memory_space=pltpu.SMEM)
    return pl.pallas_call(
        _combine_kernel,
        grid=(b, nt),
        in_specs=[
            dest_spec(0), dest_spec(1),
            pl.BlockSpec(memory_space=pl.ANY),
            pl.BlockSpec((tm, 8), lambda bi, i: (bi * nt + i, 0)),
            pl.BlockSpec((tm, d // 2), lambda bi, i: (bi * nt + i, 0)),
            pl.BlockSpec((1, tm, d), lambda bi, i: (bi, i, 0)),
            const(sg.shape), const(su.shape), const(sd.shape), const((1, d)),
            pl.BlockSpec((1, 1, d), mod_map),
        ],
        out_specs=pl.BlockSpec((1, tm, d), lambda bi, i: (bi, i, 0)),
        out_shape=jax.ShapeDtypeStruct((b, rows, d), F32),
        scratch_shapes=[pltpu.VMEM((TOP_K, tm, d // 2), jnp.uint32), pltpu.SemaphoreType.DMA(())],
        compiler_params=_cparams(("arbitrary", "arbitrary")),
    )(dest, dest, y, gate, f_pk, xa, sg, su, sd, g_post.reshape(1, d).astype(F32), g2)


def _rope_tables(s, ctx_len, dim):
    rows = s // GRID_W
    row = jnp.repeat(jnp.arange(rows), GRID_W)
    col = jnp.tile(jnp.arange(GRID_W), rows)
    quarter = dim // 4
    inv = ROPE_BASE ** (-jnp.arange(quarter, dtype=F32) / quarter)
    ang = jnp.concatenate([row.astype(F32)[:, None] * inv, col.astype(F32)[:, None] * inv], axis=-1)
    cos = jnp.repeat(jnp.cos(ang), 2, axis=-1)
    sin = jnp.repeat(jnp.sin(ang), 2, axis=-1) * jnp.tile(jnp.array([-1.0, 1.0], F32), dim // 2)
    cos = jnp.concatenate([jnp.ones((ctx_len, dim), F32), cos], axis=0)
    sin = jnp.concatenate([jnp.zeros((ctx_len, dim), F32), sin], axis=0)
    return cos, sin


def _log_sigmoid(x):
    return jnp.minimum(x, 0.0) - jnp.log1p(jnp.exp(-jnp.abs(x)))


def _mlstm_kernel(q_ref, k_ref, kt_ref, v_ref, g_ref, gt_ref, gb_ref, gbt_ref, hf_ref, mo_ref, mn_ref,
                  o_ref, s_scr, m_scr, *, reverse, final):
    L = MLSTM_CHUNK
    hv = MLSTM_V
    io, fo = (2 * MLSTM_HEADS, 3 * MLSTM_HEADS) if reverse else (0, MLSTM_HEADS)

    @pl.when(pl.program_id(1) == 0)
    def _():
        s_scr[...] = jnp.zeros(s_scr.shape, F32)
        m_scr[...] = jnp.zeros(m_scr.shape, F32)

    g = g_ref[0] + gb_ref[...]
    gt = gt_ref[0] + gbt_ref[...]
    r_io = lax.broadcasted_iota(jnp.int32, (L, L), 0)
    c_io = lax.broadcasted_iota(jnp.int32, (L, L), 1)
    seen = (c_io >= r_io) if reverse else (c_io <= r_io)
    tri = jnp.where(seen, 1.0, 0.0)
    hi = lax.Precision.HIGHEST
    bc_col = jnp.dot(tri, _log_sigmoid(g), precision=hi, preferred_element_type=F32)
    lf_row = _log_sigmoid(gt)
    bc_row = lax.dot_general(lf_row, tri, (((1,), (1,)), ((), ())), precision=hi,
                             preferred_element_type=F32)
    lane = lax.broadcasted_iota(jnp.int32, (L, LANES), 1)
    sub = lax.broadcasted_iota(jnp.int32, (LANES, L), 0)

    for h in range(MLSTM_HEADS):
        pair, odd = h // 2, h % 2
        lo = odd * MLSTM_QK
        a_col = bc_col[:, fo + h:fo + h + 1]
        i_col = g[:, io + h:io + h + 1]
        b_row = bc_row[fo + h:fo + h + 1, :]
        i_row = gt[io + h:io + h + 1, :]
        btot = jnp.sum(lf_row[fo + h:fo + h + 1, :], axis=1, keepdims=True)
        m_st = m_scr[h:h + 1, 0:1]

        w_end = btot - a_col + i_col
        m_new = jnp.maximum(btot + m_st, jnp.max(w_end, axis=0, keepdims=True))
        decay = jnp.exp(btot + m_st - m_new)
        w_k = jnp.exp(w_end - m_new)

        log_d = jnp.where(seen, a_col - b_row + i_row, -jnp.inf)
        log_inter = a_col + m_st
        m_row = jnp.maximum(log_inter, jnp.max(log_d, axis=1, keepdims=True))
        w_intra = jnp.exp(log_d - m_row)
        w_inter = jnp.exp(log_inter - m_row)

        in_head = (lane >= lo) & (lane < lo + MLSTM_QK)
        qm = jnp.where(in_head, q_ref[0, :, pair * LANES:(pair + 1) * LANES], 0).astype(BF16)
        kp = k_ref[0, :, pair * LANES:(pair + 1) * LANES]
        v = v_ref[0, :, h * hv:(h + 1) * hv]
        state = s_scr[h]

        qk = lax.dot_general(qm, kp, (((1,), (1,)), ((), ())), preferred_element_type=F32)
        qk = qk * (MLSTM_QK ** -0.5) * w_intra
        inter = jnp.dot(qm, state.astype(BF16), preferred_element_type=F32)
        num = w_inter * inter[:, :hv] + jnp.dot(qk.astype(BF16), v, preferred_element_type=F32)
        den = w_inter * inter[:, hv:hv + 1] + jnp.sum(qk, axis=1, keepdims=True)
        out = num / jnp.maximum(jnp.abs(den), jnp.exp(-m_row))

        wv = jnp.concatenate([(w_k * v.astype(F32)).astype(BF16),
                              jnp.where(lane == 0, w_k, 0.0).astype(BF16)], axis=1)
        kt = kt_ref[0, pair * LANES:(pair + 1) * LANES, :]
        in_rows = (sub >= lo) & (sub < lo + MLSTM_QK)
        ktm = (jnp.where(in_rows, kt, 0).astype(F32) * (MLSTM_QK ** -0.5)).astype(BF16)
        s_scr[h] = decay * state + jnp.dot(ktm, wv, preferred_element_type=F32)
        m_scr[h:h + 1, :] = jnp.broadcast_to(m_new, (1, LANES))

        if final:
            tot = out + hf_ref[0, :, h * hv:(h + 1) * hv]
            nrm = tot * lax.rsqrt(jnp.mean(tot * tot, axis=-1, keepdims=True) + NORM_EPS) * mn_ref[...]
            gate = jax.nn.sigmoid(mo_ref[0, :, h * hv:(h + 1) * hv].astype(F32))
            o_ref[0, :, h * hv:(h + 1) * hv] = (nrm * gate).astype(o_ref.dtype)
        else:
            o_ref[0, :, h * hv:(h + 1) * hv] = out.astype(o_ref.dtype)


def _mlstm(pv, kt, gates, gates_t, gate_b, m_norm, hf, *, ctx_chunks, reverse):
    b, t, _ = pv.shape
    L = MLSTM_CHUNK
    nc = t // L
    final = hf is not None
    nq = MLSTM_HEADS * MLSTM_QK
    nv = MLSTM_HEADS * MLSTM_V

    def chunk(j):
        if not reverse:
            return j
        return jnp.where(j < ctx_chunks, ctx_chunks - 1 - j, nc - 1 - (j - ctx_chunks))

    gb = jnp.pad(gate_b.astype(F32).reshape(-1), (0, LANES - gate_b.size))
    if hf is None:
        hf = jnp.zeros((1, L, nv), F32)
        hf_spec = pl.BlockSpec((1, L, nv), lambda bi, j: (0, 0, 0))
    else:
        hf_spec = pl.BlockSpec((1, L, nv), lambda bi, j: (bi, chunk(j), 0))
    kern = functools.partial(_mlstm_kernel, reverse=reverse, final=final)
    return pl.pallas_call(
        kern,
        grid=(b, nc),
        in_specs=[
            pl.BlockSpec((1, L, nq), lambda bi, j: (bi, chunk(j), nv // nq)),
            pl.BlockSpec((1, L, nq), lambda bi, j: (bi, chunk(j), nv // nq + 1)),
            pl.BlockSpec((1, nq, L), lambda bi, j: (bi, 0, chunk(j))),
            pl.BlockSpec((1, L, nv), lambda bi, j: (bi, chunk(j), 2)),
            pl.BlockSpec((1, L, LANES), lambda bi, j: (bi, chunk(j), 0)),
            pl.BlockSpec((1, LANES, L), lambda bi, j: (bi, 0, chunk(j))),
            pl.BlockSpec((1, LANES), lambda bi, j: (0, 0)),
            pl.BlockSpec((LANES, 1), lambda bi, j: (0, 0)),
            hf_spec,
            pl.BlockSpec((1, L, nv), lambda bi, j: (bi, chunk(j), 3)),
            pl.BlockSpec((1, MLSTM_V), lambda bi, j: (0, 0)),
        ],
        out_specs=pl.BlockSpec((1, L, nv), lambda bi, j: (bi, chunk(j), 0)),
        out_shape=jax.ShapeDtypeStruct((b, t, nv), BF16 if final else F32),
        scratch_shapes=[pltpu.VMEM((MLSTM_HEADS, LANES, 2 * LANES), F32), pltpu.VMEM((MLSTM_HEADS, LANES), F32)],
        compiler_params=_cparams(("parallel", "arbitrary")),
    )(pv, pv, kt, pv, gates, gates_t, gb.reshape(1, LANES), gb.reshape(LANES, 1), hf, pv,
      m_norm.reshape(1, MLSTM_V).astype(F32))


def _swa_kernel(sink_ref, q_ref, kp_ref, kc_ref, kn_ref, kx_ref, vp_ref, vc_ref, vn_ref, vx_ref, o_ref, *,
                n_blocks):
    i = pl.program_id(1)
    bq = BLOCK_Q
    hd = SWA_HD
    rep = SWA_HEADS // SWA_KV_HEADS
    n_ctx = kx_ref.shape[1]
    rows = rep * bq
    scale = hd ** -0.5 * LOG2E
    row = lax.broadcasted_iota(jnp.int32, (rows, 3 * bq), 0)
    col = lax.broadcasted_iota(jnp.int32, (rows, 3 * bq), 1)
    dt = (col - bq) - (row % bq)
    blk = col // bq
    ok = (jnp.abs(dt) <= WINDOW) & ((blk != 0) | (i > 0)) & ((blk != 2) | (i < n_blocks - 1))
    head_of_row = lax.broadcasted_iota(jnp.int32, (rows, 1), 0) // bq
    for g in range(SWA_KV_HEADS):
        q = jnp.concatenate([q_ref[0, :, (g * rep + r) * hd:(g * rep + r + 1) * hd] for r in range(rep)], axis=0)
        q = (q.astype(F32) * scale).astype(BF16)
        cs = slice(g * hd, (g + 1) * hd)
        k_loc = jnp.concatenate([kp_ref[0, :, cs], kc_ref[0, :, cs], kn_ref[0, :, cs]], axis=0)
        v_loc = jnp.concatenate([vp_ref[0, :, cs], vc_ref[0, :, cs], vn_ref[0, :, cs]], axis=0)
        nt = (((1,), (1,)), ((), ()))
        s_loc = jnp.where(ok, lax.dot_general(q, k_loc, nt, preferred_element_type=F32), -jnp.inf)
        s_ctx = lax.dot_general(q, kx_ref[0, :, cs], nt, preferred_element_type=F32)
        sink = jnp.zeros((rows, 1), F32)
        for r in range(rep):
            sink = jnp.where(head_of_row == r, sink_ref[g * rep + r] * LOG2E, sink)
        m = jnp.maximum(jnp.maximum(jnp.max(s_loc, axis=-1, keepdims=True),
                                    jnp.max(s_ctx, axis=-1, keepdims=True)), sink)
        p_loc = jnp.exp2(s_loc - m)
        p_ctx = jnp.exp2(s_ctx - m)
        den = (jnp.sum(p_loc, axis=-1, keepdims=True) + jnp.sum(p_ctx, axis=-1, keepdims=True)
               + jnp.exp2(sink - m))
        out = (jnp.dot(p_loc.astype(BF16), v_loc, preferred_element_type=F32)
               + jnp.dot(p_ctx.astype(BF16), vx_ref[0, :, cs], preferred_element_type=F32)) / den
        for r in range(rep):
            o_ref[0, :, (g * rep + r) * hd:(g * rep + r + 1) * hd] = out[r * bq:(r + 1) * bq].astype(o_ref.dtype)


def _swa(qk, v, sink, *, n_ctx, k_col0):
    b, t, _ = qk.shape
    bq = BLOCK_Q
    cb = n_ctx // bq
    nb = (t - n_ctx) // bq
    nq = SWA_HEADS * SWA_HD
    nk = SWA_KV_HEADS * SWA_HD
    kcol = k_col0
    prev = lambda bi, i, c: (bi, i + cb - 1, c)
    cur = lambda bi, i, c: (bi, i + cb, c)
    nxt = lambda bi, i, c: (bi, jnp.minimum(i + cb + 1, nb + cb - 1), c)
    kern = functools.partial(_swa_kernel, n_blocks=nb)

    def spec(rows, fn, c):
        return pl.BlockSpec((1, rows, nk), lambda bi, i: fn(bi, i, c))

    ctx = lambda bi, i, c: (bi, 0, c)
    return pl.pallas_call(
        kern,
        grid=(b, nb),
        in_specs=[
            pl.BlockSpec(memory_space=pltpu.SMEM),
            pl.BlockSpec((1, bq, nq), lambda bi, i: (bi, i + cb, 0)),
            spec(bq, prev, kcol), spec(bq, cur, kcol), spec(bq, nxt, kcol), spec(n_ctx, ctx, kcol),
            spec(bq, prev, 0), spec(bq, cur, 0), spec(bq, nxt, 0), spec(n_ctx, ctx, 0),
        ],
        out_specs=pl.BlockSpec((1, bq, nq), lambda bi, i: (bi, i, 0)),
        out_shape=jax.ShapeDtypeStruct((b, nb * bq, nq), BF16),
        compiler_params=_cparams(("parallel", "arbitrary")),
    )(sink.astype(F32), qk, qk, qk, qk, qk, v, v, v, v)


def _mix_out_and_moe(mix, xs, w_out, g_post, g_ffn_pre, g_ffn_post, mods, router_w, router_b,
                     wg, wu, wd, sg, su, sd, *, layer, row_tile0, seg_tiles):
    sh1, sc1, g1, sh2, sc2, g2 = mods
    b, rows, _ = mix[0].shape
    n = b * rows
    seg = dict(row_tile0=row_tile0, seg_tiles=seg_tiles)
    xa, f_pk, logits_t = _post_mix(mix, xs, w_out.astype(BF16), g_post, g_ffn_pre, g1, sh2, sc2,
                                   router_w.T.astype(BF16), **seg)
    top_e, gate = _route(logits_t, router_b, tn=512)
    blk = EXPERT_ROWS
    n_blocks = -(-n * TOP_K // blk) + N_EXPERTS
    dest, block_e, n_used = _dispatch(top_e, blk=blk, n_blocks=n_blocks)
    f_pk = f_pk.reshape(n, -1)
    x_sorted = _scatter_rows(dest, n_used, f_pk, blk=blk, n_blocks=n_blocks)
    y = _expert_ffn(x_sorted, block_e[0, :n_blocks], n_used[0, :1], wg, wu, wd, layer=layer, tm=blk)
    return _combine(dest, y, gate.T, f_pk, xa, sg.astype(BF16), su.astype(BF16), sd.astype(BF16),
                    g_ffn_post, g2, **seg)


def kernel(x, c, ctx, c_ctx, mod_w, mod_b, norm_mix_pre, norm_mix_post, norm_ffn_pre, norm_ffn_post, ab_w_in, ab_w_out, diff_lambda, diff_norm, mlstm_gate_b, mlstm_norm, cd_w_in, cd_w_out, mla_q_norm, mla_w_uq, mla_kv_norm, mla_w_ukv, swa_sink, router_w, router_b, exp_w_gate, exp_w_up, exp_w_down, sh_w_gate, sh_w_up, sh_w_down):
    b, s, d = x.shape
    L = ctx.shape[1]
    t = L + s
    assert L == ROW_TILE and s % ROW_TILE == 0
    depth = mod_w.shape[0]
    ctx_tiles = L // ROW_TILE
    tk = next(c for c in (2816, 768, ROW_TILE) if t % c == 0)

    cos64, sin64 = _rope_tables(s, L, DIFF_HD)
    cos64 = jnp.tile(cos64, (1, LANES // DIFF_HD))
    sin64 = jnp.tile(sin64, (1, LANES // DIFF_HD))
    cos128, sin128 = _rope_tables(s, L, SWA_HD)
    cos_kr = jnp.concatenate([cos64[:, :MLA_ROPE], jnp.ones((t, LANES - MLA_ROPE), F32)], axis=1)
    sin_kr = jnp.concatenate([sin64[:, :MLA_ROPE], jnp.zeros((t, LANES - MLA_ROPE), F32)], axis=1)

    xs = jnp.concatenate([ctx, x], axis=1)

    for layer in range(depth):
        with_ctx = layer < depth - 1
        j = layer // 2
        mod_l = jax.nn.silu(c) @ mod_w[layer] + mod_b[layer]
        mod_c = jax.nn.silu(c_ctx) @ mod_w[layer] + mod_b[layer]
        mods = jnp.stack([jnp.broadcast_to(mod_c, (b, 6 * d)), mod_l], axis=1).reshape(b * 2, 1, 6, d)
        sh1, sc1, g1, sh2, sc2, g2 = (mods[:, :, m] for m in range(6))

        proj = functools.partial(_norm_proj, xs, norm_mix_pre[layer], sh1, sc1, seg_tiles=ctx_tiles)
        if layer % 2 == 0:
            w_in = ab_w_in[j].astype(BF16)
            n_qk = 2 * DIFF_HEADS * 2 * DIFF_HD
            qk = proj(w_in[:, :n_qk], tn=2048, rope=(cos64[None], sin64[None]), pattern=(0,) * 16)
            pv = proj(w_in[:, n_qk:n_qk + AB_PLAIN], tn=2048)
            w_gates = jnp.pad(w_in[:, n_qk + AB_PLAIN:], ((0, 0), (0, LANES - 4 * MLSTM_HEADS)))
            gates = proj(w_gates, tn=LANES, out_dtype=F32)

            lam_init = 0.8 - 0.6 * math.exp(-0.3 * layer)
            a = _attention(qk if with_ctx else qk[:, L:], qk, pv, heads=DIFF_HEADS, dq=2 * DIFF_HD,
                           dv=DIFF_VD, q_col0=0, k_col0=DIFF_HEADS, v_col0=0, tq=ROW_TILE,
                           scale=DIFF_HD ** -0.5, tk=tk, diff=True, lam_vec=diff_lambda[j],
                           d_norm=diff_norm[j], lam_init=lam_init,
                           ctx_tiles=ctx_tiles if with_ctx else 0, ctx_len=L)
            o_mk = DIFF_HEADS * DIFF_VD + MLSTM_HEADS * MLSTM_QK
            kt = jnp.swapaxes(pv[..., o_mk:o_mk + MLSTM_HEADS * MLSTM_QK], 1, 2)
            scan = functools.partial(_mlstm, pv, kt, gates, jnp.swapaxes(gates, 1, 2), mlstm_gate_b[j],
                                     mlstm_norm[j], ctx_chunks=L // MLSTM_CHUNK)
            m = scan(scan(None, reverse=False), reverse=True)
            mix = (a, m if with_ctx else m[:, L:])
            w_out = ab_w_out[j]
        else:
            assert not with_ctx
            w_in = cd_w_in[j].astype(BF16)
            c0 = Q_LORA
            c1 = c0 + KV_LORA
            c2 = c1 + MLA_ROPE
            c3 = c2 + SWA_HEADS * SWA_HD
            c4 = c3 + SWA_KV_HEADS * SWA_HD
            w_rope = jnp.concatenate([w_in[:, c2:c4], w_in[:, c1:c2],
                                      jnp.zeros((d, LANES - MLA_ROPE), BF16)], axis=1)
            n_rope = w_rope.shape[1]
            rp = proj(w_rope, tn=n_rope, rope=(jnp.stack([cos128, cos_kr]), jnp.stack([sin128, sin_kr])),
                      pattern=(0,) * (n_rope // LANES - 1) + (1,))
            q_pad = 512 - Q_LORA
            w_plain = jnp.concatenate([w_in[:, c4:], w_in[:, :c0], jnp.zeros((d, q_pad), BF16),
                                       w_in[:, c0:c1]], axis=1)
            pp = proj(w_plain, tn=w_plain.shape[1])
            n_sv = SWA_KV_HEADS * SWA_HD
            cq = pp[..., n_sv:n_sv + 512]
            ckv = pp[..., n_sv + 512:]

            hq = MLA_NOPE + MLA_ROPE
            w_uq = mla_w_uq[j].astype(BF16).reshape(Q_LORA, MLA_HEADS, hq)
            w_uq = jnp.pad(w_uq, ((0, q_pad), (0, 0), (0, 256 - hq))).reshape(512, MLA_HEADS * 256)
            no_mod = jnp.zeros((b * 2, 1, 512), F32)
            qn = jnp.pad(mla_q_norm[j], (0, q_pad))
            q_mla = _norm_proj(cq[:, L:], qn, no_mod, no_mod, w_uq, seg_tiles=0, tn=2048, n_valid=Q_LORA,
                               rope=(cos_kr[None, L:], sin_kr[None, L:]), pattern=(-1, 0) * 8)
            w_ukv = mla_w_ukv[j].astype(BF16).reshape(KV_LORA, MLA_HEADS, MLA_NOPE + MLA_V)
            w_k = jnp.pad(w_ukv[..., :MLA_NOPE], ((0, 0), (0, 0), (0, LANES))).reshape(KV_LORA, MLA_HEADS * 256)
            w_v = w_ukv[..., MLA_NOPE:].reshape(KV_LORA, MLA_HEADS * MLA_V)
            w_kv = jnp.concatenate([w_k, w_v], axis=1)
            kv = _norm_proj(ckv, mla_kv_norm[j], no_mod, no_mod, w_kv, seg_tiles=ctx_tiles, tn=w_kv.shape[1],
                            pattern=(-1, FILL) * MLA_HEADS + (-1,) * (MLA_HEADS * MLA_V // LANES),
                            fill=(rp, n_rope // LANES - 1))
            a = _attention(q_mla, kv, kv, heads=MLA_HEADS, dq=256, dv=MLA_V, q_col0=0, k_col0=0,
                           v_col0=MLA_HEADS * 256 // MLA_V, tq=512, scale=MLA_SCALE, tk=tk)

            w = _swa(rp, pp, swa_sink[j], n_ctx=L, k_col0=SWA_HEADS * SWA_HD // n_sv)
            mix = (a, w)
            w_out = cd_w_out[j]

        xa = _mix_out_and_moe(mix, xs, w_out, norm_mix_post[layer], norm_ffn_pre[layer], norm_ffn_post[layer],
                              (sh1, sc1, g1, sh2, sc2, g2), router_w[layer], router_b[layer],
                              exp_w_gate, exp_w_up, exp_w_down,
                              sh_w_gate[layer], sh_w_up[layer], sh_w_down[layer],
                              layer=layer, row_tile0=0 if with_ctx else ctx_tiles, seg_tiles=ctx_tiles)
        xs = xa if with_ctx else jnp.concatenate([xs[:, :L], xa], axis=1)
    return xs[:, L:]
```

```python
import functools
import math

import jax
import jax.numpy as jnp
import numpy as np
from jax import lax
from jax.experimental import pallas as pl
from jax.experimental.pallas import tpu as pltpu

F32 = jnp.float32
BF16 = jnp.bfloat16

GRID_W = 64
ROPE_BASE = 10000.0
NORM_EPS = 1e-6

DIFF_HEADS = 8
DIFF_HD = 64
DIFF_VD = 2 * DIFF_HD
MLSTM_HEADS = 8
MLSTM_QK = 64
MLSTM_V = 128
MLSTM_CHUNK = 128
MLA_HEADS = 8
MLA_NOPE = 128
MLA_ROPE = 64
MLA_V = 128
Q_LORA = 448
KV_LORA = 512
MLA_SCALE = (MLA_NOPE + MLA_ROPE) ** -0.5
SWA_HEADS = 8
SWA_KV_HEADS = 2
SWA_HD = 128
WINDOW = 128
BLOCK_Q = 128
N_EXPERTS = 64
TOP_K = 6
N_GROUPS = 8
TOPK_GROUPS = 4
ROUTED_SCALE = 2.5
AB_PLAIN = DIFF_HEADS * DIFF_VD + 2 * MLSTM_HEADS * MLSTM_QK + 2 * MLSTM_HEADS * MLSTM_V

LANES = 128
LOG2E = 1.4426950408889634
VMEM_LIMIT = 56 * 1024 * 1024

ROW_TILE = 256
EXPERT_ROWS = 512


def _cparams(sem):
    return pltpu.CompilerParams(dimension_semantics=sem, vmem_limit_bytes=VMEM_LIMIT)


def _modulation_kernel(c_ref, w_ref, b_ref, o_ref):
    c = c_ref[...]
    h = (c * jax.nn.sigmoid(c)).astype(BF16)
    o_ref[...] = jnp.dot(h, w_ref[0].astype(BF16), preferred_element_type=F32) + b_ref[0]


def _modulation(c_rows, mod_w, mod_b, *, layer):
    r, d = c_rows.shape
    n = mod_w.shape[2]
    tn = d // 2
    assert n % tn == 0 and tn % LANES == 0
    return pl.pallas_call(
        _modulation_kernel,
        grid=(n // tn,),
        in_specs=[pl.BlockSpec((r, d), lambda j: (0, 0)),
                  pl.BlockSpec((1, d, tn), lambda j: (layer, 0, j)),
                  pl.BlockSpec((1, 1, tn), lambda j: (layer, 0, j))],
        out_specs=pl.BlockSpec((r, tn), lambda j: (0, j)),
        out_shape=jax.ShapeDtypeStruct((r, n), F32),
        compiler_params=_cparams(("parallel",)),
    )(c_rows, mod_w, mod_b.reshape(mod_b.shape[0], 1, n))


def _pair_swap(a):
    lane = lax.broadcasted_iota(jnp.int32, a.shape, 1)
    return jnp.where(lane % 2 == 0, pltpu.roll(a, LANES - 1, 1), pltpu.roll(a, 1, 1))


FILL = -2


def _norm_proj_kernel(x_ref, g_ref, sh_ref, sc_ref, w_ref, cos_ref, sin_ref, fill_ref, o_ref, *,
                      n_valid, pattern):
    x = x_ref[0].astype(F32)
    ms = jnp.sum(x * x, axis=-1, keepdims=True) * (1.0 / n_valid)
    y = x * lax.rsqrt(ms + NORM_EPS) * g_ref[...]
    h = (y * (1.0 + sc_ref[0]) + sh_ref[0]).astype(BF16)
    acc = jnp.dot(h, w_ref[...], preferred_element_type=F32)
    if pattern is None:
        o_ref[0] = acc.astype(o_ref.dtype)
    else:
        for c, tbl in enumerate(pattern):
            if tbl == FILL:
                o_ref[0, :, c * LANES:(c + 1) * LANES] = fill_ref[0].astype(o_ref.dtype)
                continue
            a = acc[:, c * LANES:(c + 1) * LANES]
            if tbl >= 0:
                a = a * cos_ref[tbl] + _pair_swap(a) * sin_ref[tbl]
            o_ref[0, :, c * LANES:(c + 1) * LANES] = a.astype(o_ref.dtype)


def _norm_proj(x, g, shift, scale, w, *, seg_tiles, tn, out_dtype=BF16, n_valid=None,
               rope=None, pattern=None, fill=None):
    b, t, k = x.shape
    if fill is None:
        fill_arr, fill_map = jnp.zeros((1, ROW_TILE, LANES), BF16), (lambda j, bi, i: (0, 0, 0))
    else:
        fill_arr, fill_map = fill[0], (lambda j, bi, i: (bi, i, fill[1]))
    n = w.shape[1]
    tm = ROW_TILE
    assert t % tm == 0 and n % tn == 0 and tn % LANES == 0
    if rope is None:
        cos = sin = jnp.zeros((1, tm, LANES), F32)
        tbl_map = lambda j, bi, i: (0, 0, 0)
    else:
        cos, sin = rope
        tbl_map = lambda j, bi, i: (0, i, 0)
    ntab = cos.shape[0]
    kern = functools.partial(_norm_proj_kernel, n_valid=float(n_valid or k), pattern=pattern)
    mod_map = lambda j, bi, i: (bi * 2 + jnp.where(i >= seg_tiles, 1, 0), 0, 0)
    return pl.pallas_call(
        kern,
        grid=(n // tn, b, t // tm),
        in_specs=[
            pl.BlockSpec((1, tm, k), lambda j, bi, i: (bi, i, 0)),
            pl.BlockSpec((1, k), lambda j, bi, i: (0, 0)),
            pl.BlockSpec((1, 1, k), mod_map),
            pl.BlockSpec((1, 1, k), mod_map),
            pl.BlockSpec((k, tn), lambda j, bi, i: (0, j)),
            pl.BlockSpec((ntab, tm, LANES), tbl_map),
            pl.BlockSpec((ntab, tm, LANES), tbl_map),
            pl.BlockSpec((1, tm, LANES), fill_map),
        ],
        out_specs=pl.BlockSpec((1, tm, tn), lambda j, bi, i: (bi, i, j)),
        out_shape=jax.ShapeDtypeStruct((b, t, n), out_dtype),
        compiler_params=_cparams(("parallel", "parallel", "parallel")),
    )(x, g.reshape(1, k).astype(F32), shift, scale, w, cos, sin, fill_arr)


def _attn_kernel(q_ref, k_ref, v_ref, lam_ref, dn_ref, o_ref, q_scr, s_scr, m_scr, l_scr, acc_scr, *,
                 scale, diff, lam_init, ctx_tiles, ctx_len, tq, tk, t):
    i = pl.program_id(2)
    q = q_ref[0].astype(F32) * (scale * LOG2E)
    if diff:
        lane = lax.broadcasted_iota(jnp.int32, q.shape, 1)
        half = q.shape[1] // 2
        q_scr[0:tq] = jnp.where(lane < half, q, 0.0).astype(BF16)
        q_scr[tq:2 * tq] = jnp.where(lane >= half, q, 0.0).astype(BF16)
    else:
        q_scr[...] = q.astype(BF16)

    def attend(kv_len, chunk):
        n_chunks = kv_len // chunk
        groups = chunk // LANES

        def pass1(c, carry):
            k = k_ref[0, pl.ds(pl.multiple_of(c * chunk, chunk), chunk), :]
            s = lax.dot_general(q_scr[...], k, (((1,), (1,)), ((), ())), preferred_element_type=F32)
            s_scr[c, :, 0:chunk] = s
            m = m_scr[...]
            for g in range(groups):
                m = jnp.maximum(m, s[:, g * LANES:(g + 1) * LANES])
            m_scr[...] = m
            return carry

        def pass2(c, carry):
            s = s_scr[c, :, 0:chunk]
            p = jnp.exp2(s - jnp.concatenate([m_scr[...]] * groups, axis=1))
            l = l_scr[...]
            for g in range(groups):
                l = l + p[:, g * LANES:(g + 1) * LANES]
            l_scr[...] = l
            v = v_ref[0, pl.ds(pl.multiple_of(c * chunk, chunk), chunk), :]
            acc_scr[...] += jnp.dot(p.astype(BF16), v, preferred_element_type=F32)
            return carry

        m_scr[...] = jnp.full(m_scr.shape, -jnp.inf, F32)
        if n_chunks == 1:
            pass1(0, 0)
        else:
            lax.fori_loop(0, n_chunks, pass1, 0)
        m_scr[...] = jnp.broadcast_to(jnp.max(m_scr[...], axis=-1, keepdims=True), m_scr.shape)
        l_scr[...] = jnp.zeros(l_scr.shape, F32)
        acc_scr[...] = jnp.zeros(acc_scr.shape, F32)
        if n_chunks == 1:
            pass2(0, 0)
        else:
            lax.fori_loop(0, n_chunks, pass2, 0)

        o = acc_scr[...] / jnp.sum(l_scr[...], axis=-1, keepdims=True)
        if diff:
            lv = lam_ref[...]
            lam = (jnp.exp(jnp.sum(lv[0:1] * lv[1:2], axis=-1, keepdims=True))
                   - jnp.exp(jnp.sum(lv[2:3] * lv[3:4], axis=-1, keepdims=True)) + lam_init)
            a = o[0:tq] - lam * o[tq:2 * tq]
            ms = jnp.mean(a * a, axis=-1, keepdims=True)
            a = a * lax.rsqrt(ms + NORM_EPS) * dn_ref[...] * (1.0 - lam_init)
            o_ref[0] = a.astype(o_ref.dtype)
        else:
            o_ref[0] = o.astype(o_ref.dtype)

    if ctx_tiles:
        pl.when(i < ctx_tiles)(lambda: attend(ctx_len, ctx_len))
        pl.when(i >= ctx_tiles)(lambda: attend(t, tk))
    else:
        attend(t, tk)


def _attention(q, k, v, *, heads, dq, dv, q_col0, k_col0, v_col0, tq, scale, tk, diff=False,
               lam_vec=None, d_norm=None, lam_init=0.0, ctx_tiles=0, ctx_len=0):
    b, t, _ = k.shape
    sq = q.shape[1]
    assert t % tk == 0 and sq % tq == 0 and tk % LANES == 0 and ctx_len % LANES == 0
    assert ctx_len <= tk
    rows = 2 * tq if diff else tq
    if lam_vec is None:
        lam_vec = jnp.zeros((4, DIFF_HD), F32)
        d_norm = jnp.zeros((dv,), F32)
    kern = functools.partial(_attn_kernel, scale=scale, diff=diff, lam_init=lam_init,
                             ctx_tiles=ctx_tiles, ctx_len=ctx_len, tq=tq, tk=tk, t=t)
    return pl.pallas_call(
        kern,
        grid=(b, heads, sq // tq),
        in_specs=[
            pl.BlockSpec((1, tq, dq), lambda bi, h, i: (bi, i, q_col0 + h)),
            pl.BlockSpec((1, t, dq), lambda bi, h, i: (bi, 0, k_col0 + h)),
            pl.BlockSpec((1, t, dv), lambda bi, h, i: (bi, 0, v_col0 + h)),
            pl.BlockSpec((4, DIFF_HD), lambda bi, h, i: (0, 0)),
            pl.BlockSpec((1, dv), lambda bi, h, i: (0, 0)),
        ],
        out_specs=pl.BlockSpec((1, tq, dv), lambda bi, h, i: (bi, i, h)),
        out_shape=jax.ShapeDtypeStruct((b, sq, heads * dv), BF16),
        scratch_shapes=[pltpu.VMEM((rows, dq), BF16), pltpu.VMEM((t // tk, rows, tk), F32),
                        pltpu.VMEM((rows, LANES), F32), pltpu.VMEM((rows, LANES), F32),
                        pltpu.VMEM((rows, dv), F32)],
        compiler_params=_cparams(("parallel", "parallel", "arbitrary")),
    )(q, k, v, lam_vec.astype(F32), d_norm.reshape(1, dv).astype(F32))


def _pack_bf16_pairs(x):
    h = x.shape[1] // 2
    lo = lax.bitcast_convert_type(x[:, :h].astype(F32), jnp.uint32) >> 16
    hi = lax.bitcast_convert_type(x[:, h:].astype(F32), jnp.uint32) & jnp.uint32(0xFFFF0000)
    return hi | lo


def _unpack_bf16_pairs(w):
    lo = lax.bitcast_convert_type(w << 16, F32)
    hi = lax.bitcast_convert_type(w & jnp.uint32(0xFFFF0000), F32)
    return jnp.concatenate([lo.astype(BF16), hi.astype(BF16)], axis=1)


def _swiglu(x, wg, wu, wd):
    g = jnp.dot(x, wg, preferred_element_type=F32)
    u = jnp.dot(x, wu, preferred_element_type=F32)
    a = (g * jax.nn.sigmoid(g) * u).astype(BF16)
    return jnp.dot(a, wd, preferred_element_type=F32)


def _expert_kernel(be_ref, nu_ref, x_ref, wg_ref, wu_ref, wd_ref, o_ref, wg_s, wu_s, wd_s):
    i = pl.program_id(0)

    @pl.when(i < nu_ref[0])
    def _():
        @pl.when((i == 0) | (be_ref[i] != be_ref[jnp.maximum(i - 1, 0)]))
        def _():
            wg_s[...] = wg_ref[0, 0].astype(BF16)
            wu_s[...] = wu_ref[0, 0].astype(BF16)
            wd_s[...] = wd_ref[0, 0].astype(BF16)

        y = _swiglu(_unpack_bf16_pairs(x_ref[...]), wg_s[...], wu_s[...], wd_s[...])
        o_ref[...] = _pack_bf16_pairs(y.astype(BF16))

    @pl.when(i >= nu_ref[0])
    def _():
        o_ref[...] = jnp.zeros(o_ref.shape, o_ref.dtype)


def _expert_ffn(x, block_e, n_used, wg, wu, wd, *, layer, tm):
    n, dh = x.shape
    d = 2 * dh
    ff = wg.shape[3]
    assert n % tm == 0

    def blk(i, be, nu):
        return jnp.minimum(i, nu[0] - 1)

    grid_spec = pltpu.PrefetchScalarGridSpec(
        num_scalar_prefetch=2,
        grid=(n // tm,),
        in_specs=[
            pl.BlockSpec((tm, dh), lambda i, be, nu: (blk(i, be, nu), 0)),
            pl.BlockSpec((1, 1, d, ff), lambda i, be, nu: (layer, be[blk(i, be, nu)], 0, 0)),
            pl.BlockSpec((1, 1, d, ff), lambda i, be, nu: (layer, be[blk(i, be, nu)], 0, 0)),
            pl.BlockSpec((1, 1, ff, d), lambda i, be, nu: (layer, be[blk(i, be, nu)], 0, 0)),
        ],
        out_specs=pl.BlockSpec((tm, dh), lambda i, be, nu: (i, 0)),
        scratch_shapes=[pltpu.VMEM((d, ff), BF16), pltpu.VMEM((d, ff), BF16), pltpu.VMEM((ff, d), BF16)],
    )
    return pl.pallas_call(
        _expert_kernel,
        grid_spec=grid_spec,
        out_shape=jax.ShapeDtypeStruct((n, dh), jnp.uint32),
        compiler_params=_cparams(("arbitrary",)),
    )(block_e, n_used, x, wg, wu, wd)


def _post_mix_kernel(a_ref, m_ref, x_ref, w_ref, gp_ref, gf_ref, g1_ref, sh_ref, sc_ref, rw_ref,
                     xa_ref, f_ref, lg_ref):
    ka = a_ref.shape[2]
    y = (jnp.dot(a_ref[0], w_ref[0:ka, :], preferred_element_type=F32)
         + jnp.dot(m_ref[0], w_ref[ka:, :], preferred_element_type=F32))
    yn = y * lax.rsqrt(jnp.mean(y * y, axis=-1, keepdims=True) + NORM_EPS) * gp_ref[...]
    xa = x_ref[0] + g1_ref[0] * yn
    xa_ref[0] = xa
    fn = xa * lax.rsqrt(jnp.mean(xa * xa, axis=-1, keepdims=True) + NORM_EPS) * gf_ref[...]
    f = (fn * (1.0 + sc_ref[0]) + sh_ref[0]).astype(BF16)
    f_ref[0] = _pack_bf16_pairs(f)
    lg_ref[...] = lax.dot_general(rw_ref[...], f, (((1,), (1,)), ((), ())), preferred_element_type=F32)


def _post_mix(mix, xs, w_out, g_post, g_ffn, g1, sh2, sc2, router_wt, *, row_tile0, seg_tiles):
    mix_a, mix_m = mix
    b, rows, ka = mix_a.shape
    km = mix_m.shape[2]
    k = ka + km
    d = w_out.shape[1]
    e = router_wt.shape[0]
    tm = ROW_TILE
    nt = rows // tm
    mod_map = lambda bi, i: (bi * 2 + jnp.where(i + row_tile0 >= seg_tiles, 1, 0), 0, 0)
    vec = lambda: pl.BlockSpec((1, d), lambda bi, i: (0, 0))
    return pl.pallas_call(
        _post_mix_kernel,
        grid=(b, nt),
        in_specs=[
            pl.BlockSpec((1, tm, ka), lambda bi, i: (bi, i, 0)),
            pl.BlockSpec((1, tm, km), lambda bi, i: (bi, i, 0)),
            pl.BlockSpec((1, tm, d), lambda bi, i: (bi, i + row_tile0, 0)),
            pl.BlockSpec((k, d), lambda bi, i: (0, 0)),
            vec(), vec(),
            pl.BlockSpec((1, 1, d), mod_map), pl.BlockSpec((1, 1, d), mod_map),
            pl.BlockSpec((1, 1, d), mod_map),
            pl.BlockSpec((e, d), lambda bi, i: (0, 0)),
        ],
        out_specs=[
            pl.BlockSpec((1, tm, d), lambda bi, i: (bi, i, 0)),
            pl.BlockSpec((1, tm, d // 2), lambda bi, i: (bi, i, 0)),
            pl.BlockSpec((e, tm), lambda bi, i: (0, bi * nt + i)),
        ],
        out_shape=[jax.ShapeDtypeStruct((b, rows, d), F32),
                   jax.ShapeDtypeStruct((b, rows, d // 2), jnp.uint32),
                   jax.ShapeDtypeStruct((e, b * rows), F32)],
        compiler_params=_cparams(("parallel", "arbitrary")),
    )(mix_a, mix_m, xs, w_out, g_post.reshape(1, d).astype(F32), g_ffn.reshape(1, d).astype(F32),
      g1, sh2, sc2, router_wt)


def _route_kernel(lg_ref, rb_ref, e_ref, g_ref):
    per = N_EXPERTS // N_GROUPS
    tn = lg_ref.shape[1]
    neg = -jnp.inf
    r_io = lax.broadcasted_iota(jnp.int32, (per, tn), 0)
    scores, choice, gs = [], [], []
    for g in range(N_GROUPS):
        sg = jax.nn.sigmoid(lg_ref[g * per:(g + 1) * per, :])
        cg = sg + rb_ref[g * per:(g + 1) * per, :]
        m1 = jnp.max(cg, axis=0, keepdims=True)
        i1 = jnp.min(jnp.where(cg == m1, r_io, per), axis=0, keepdims=True)
        m2 = jnp.max(jnp.where(r_io == i1, neg, cg), axis=0, keepdims=True)
        scores.append(sg)
        choice.append(cg)
        gs.append(m1 + m2)
    masked = []
    for g in range(N_GROUPS):
        ahead = jnp.zeros((1, tn), jnp.int32)
        for o in range(N_GROUPS):
            if o < g:
                ahead = ahead + jnp.where(gs[o] >= gs[g], 1, 0)
            elif o > g:
                ahead = ahead + jnp.where(gs[o] > gs[g], 1, 0)
        masked.append(jnp.where(ahead < TOPK_GROUPS, choice[g], neg))
    ids, gates = [], []
    for _ in range(TOP_K):
        best = masked[0]
        for g in range(1, N_GROUPS):
            best = jnp.maximum(best, masked[g])
        best = jnp.max(best, axis=0, keepdims=True)
        cand = jnp.where(masked[0] == best, r_io, N_EXPERTS)
        for g in range(1, N_GROUPS):
            cand = jnp.minimum(cand, jnp.where(masked[g] == best, r_io + g * per, N_EXPERTS))
        idx = jnp.min(cand, axis=0, keepdims=True)
        gk = jnp.zeros((per, tn), F32)
        for g in range(N_GROUPS):
            hit = (r_io + g * per) == idx
            gk = gk + jnp.where(hit, scores[g], 0.0)
            masked[g] = jnp.where(hit, neg, masked[g])
        ids.append(idx)
        gates.append(jnp.sum(gk, axis=0, keepdims=True))
    total = gates[0]
    for k in range(1, TOP_K):
        total = total + gates[k]
    pad = 8 - TOP_K
    e_ref[...] = jnp.concatenate(ids + [jnp.zeros((pad, tn), jnp.int32)], axis=0)
    g_ref[...] = jnp.concatenate([gk / total * ROUTED_SCALE for gk in gates]
                                 + [jnp.zeros((pad, tn), F32)], axis=0)


def _route(logits_t, router_b, *, tn):
    e, n = logits_t.shape
    assert n % tn == 0
    return pl.pallas_call(
        _route_kernel,
        grid=(n // tn,),
        in_specs=[pl.BlockSpec((e, tn), lambda i: (0, i)),
                  pl.BlockSpec((e, 1), lambda i: (0, 0))],
        out_specs=[pl.BlockSpec((8, tn), lambda i: (0, i)), pl.BlockSpec((8, tn), lambda i: (0, i))],
        out_shape=[jax.ShapeDtypeStruct((8, n), jnp.int32), jax.ShapeDtypeStruct((8, n), F32)],
        compiler_params=_cparams(("parallel",)),
    )(logits_t, router_b.reshape(e, 1).astype(F32))


def _dispatch_kernel(e_ref, dest_ref, be_ref, nu_ref, cnt_scr, start_scr, run_scr, *, blk):
    ph = pl.program_id(0)
    i = pl.program_id(1)
    tn = e_ref.shape[1]
    e_io = lax.broadcasted_iota(jnp.int32, (N_EXPERTS, tn), 0)
    hot = jnp.zeros((N_EXPERTS, tn), F32)
    for k in range(TOP_K):
        hot = hot + jnp.where(e_io == e_ref[k:k + 1, :], 1.0, 0.0)
    tile_cnt = jnp.sum(hot, axis=1, keepdims=True)

    @pl.when((ph == 0) & (i == 0))
    def _():
        cnt_scr[...] = jnp.zeros(cnt_scr.shape, F32)

    @pl.when(ph == 0)
    def _():
        cnt_scr[...] += jnp.broadcast_to(tile_cnt, cnt_scr.shape)

    @pl.when((ph == 1) & (i == 0))
    def _():
        bpe = jnp.floor((cnt_scr[...] + (blk - 1.0)) * (1.0 / blk))
        r = lax.broadcasted_iota(jnp.int32, (N_EXPERTS, N_EXPERTS), 0)
        c = lax.broadcasted_iota(jnp.int32, (N_EXPERTS, N_EXPERTS), 1)
        lower = jnp.where(c < r, 1.0, 0.0)
        before = jnp.dot(lower, bpe, precision=lax.Precision.HIGHEST, preferred_element_type=F32)
        start_scr[...] = before * blk
        run_scr[...] = jnp.zeros(run_scr.shape, F32)
        ends = (before + bpe)[:, 0:1]
        nb = be_ref.shape[1]
        bid = lax.broadcasted_iota(jnp.int32, (N_EXPERTS, nb), 1).astype(F32)
        be = jnp.sum(jnp.where(ends <= bid, 1, 0), axis=0, keepdims=True)
        be_ref[...] = jnp.minimum(be, N_EXPERTS - 1).astype(jnp.int32)
        n_used = jnp.broadcast_to(jnp.max(ends, axis=0, keepdims=True), (1, LANES))
        last = jnp.where(bpe > 0.0, (before + bpe - 1.0) * blk, -1.0)
        er = lax.broadcasted_iota(jnp.int32, (N_EXPERTS, LANES), 0)
        ec = lax.broadcasted_iota(jnp.int32, (N_EXPERTS, LANES), 1)
        last_row = jnp.sum(jnp.where(er == ec, last, 0.0), axis=0, keepdims=True)
        last_row = jnp.where(ec[0:1] < N_EXPERTS, last_row, -1.0)
        nu_ref[...] = jnp.concatenate([n_used, last_row, jnp.zeros((6, LANES), F32)], axis=0).astype(jnp.int32)

    @pl.when(ph == 1)
    def _():
        rr = lax.broadcasted_iota(jnp.int32, (tn, tn), 0)
        cc = lax.broadcasted_iota(jnp.int32, (tn, tn), 1)
        upper = jnp.where(rr < cc, 1.0, 0.0).astype(BF16)
        prior = jnp.dot(hot.astype(BF16), upper, preferred_element_type=F32)
        pos = prior + jnp.concatenate([start_scr[...] + run_scr[...]] * (tn // LANES), axis=1)
        rows = []
        for k in range(TOP_K):
            rows.append(jnp.sum(jnp.where(e_io == e_ref[k:k + 1, :], pos, 0.0), axis=0, keepdims=True))
        rows.append(jnp.zeros((8 - TOP_K, tn), F32))
        dest_ref[0] = jnp.concatenate(rows, axis=0).astype(jnp.int32)
        run_scr[...] += jnp.broadcast_to(tile_cnt, run_scr.shape)


def _dispatch(top_e, *, blk, n_blocks):
    n = top_e.shape[1]
    tn = LANES
    nbp = -(-n_blocks // LANES) * LANES
    kern = functools.partial(_dispatch_kernel, blk=blk)
    return pl.pallas_call(
        kern,
        grid=(2, n // tn),
        in_specs=[pl.BlockSpec((8, tn), lambda ph, i: (0, i))],
        out_specs=[pl.BlockSpec((1, 8, tn), lambda ph, i: (i * ph, 0, 0)),
                   pl.BlockSpec((1, nbp), lambda ph, i: (0, 0)),
                   pl.BlockSpec((8, LANES), lambda ph, i: (0, 0))],
        out_shape=[jax.ShapeDtypeStruct((n // tn, 8, tn), jnp.int32),
                   jax.ShapeDtypeStruct((1, nbp), jnp.int32),
                   jax.ShapeDtypeStruct((8, LANES), jnp.int32)],
        scratch_shapes=[pltpu.VMEM((N_EXPERTS, LANES), F32)] * 3,
        compiler_params=_cparams(("arbitrary", "arbitrary")),
    )(top_e)


def _scatter_rows_kernel(dest_ref, meta_ref, x_ref, o_ref, zbuf, sem, zsem, *, blk, n_blocks):
    tm = x_ref.shape[0]

    @pl.when(pl.program_id(0) == 0)
    def _():
        zbuf[...] = jnp.zeros(zbuf.shape, zbuf.dtype)

        def zero_copy(row):
            return pltpu.make_async_copy(zbuf, o_ref.at[pl.ds(pl.multiple_of(row, blk), blk)], zsem)

        def each_block(fn):
            def expert(e, carry):
                row = meta_ref[1, e]

                @pl.when(row >= 0)
                def _():
                    fn(zero_copy(row))
                return carry

            def tail(i, carry):
                fn(zero_copy(i * blk))
                return carry

            lax.fori_loop(0, N_EXPERTS, expert, 0)
            lax.fori_loop(meta_ref[0, 0], n_blocks, tail, 0)

        each_block(lambda cp: cp.start())
        each_block(lambda cp: cp.wait())

    def row_copy(t, k):
        return pltpu.make_async_copy(x_ref.at[pl.ds(t, 1)], o_ref.at[pl.ds(dest_ref[0, k, t], 1)], sem)

    def issue(t, carry):
        for k in range(TOP_K):
            row_copy(t, k).start()
        return carry

    def drain(t, carry):
        for k in range(TOP_K):
            row_copy(t, k).wait()
        return carry

    lax.fori_loop(0, tm, issue, 0)
    lax.fori_loop(0, tm, drain, 0)


def _scatter_rows(dest, meta, x, *, blk, n_blocks):
    n, dh = x.shape
    tm = LANES
    kern = functools.partial(_scatter_rows_kernel, blk=blk, n_blocks=n_blocks)
    return pl.pallas_call(
        kern,
        grid=(n // tm,),
        in_specs=[pl.BlockSpec((1, 8, tm), lambda i: (i, 0, 0), memory_space=pltpu.SMEM),
                  pl.BlockSpec(memory_space=pltpu.SMEM),
                  pl.BlockSpec((tm, dh), lambda i: (i, 0))],
        out_specs=pl.BlockSpec(memory_space=pl.ANY),
        out_shape=jax.ShapeDtypeStruct((n_blocks * blk, dh), x.dtype),
        scratch_shapes=[pltpu.VMEM((blk, dh), x.dtype), pltpu.SemaphoreType.DMA(()),
                        pltpu.SemaphoreType.DMA(())],
        compiler_params=_cparams(("arbitrary",)),
    )(dest, meta, x)


def _combine_kernel(dest0_ref, dest1_ref, y_ref, gate_ref, f_ref, xa_ref, sg_ref, su_ref, sd_ref, gp_ref,
                    g2_ref, o_ref, buf, sem):
    halves = (dest0_ref, dest1_ref)

    def row_copy(half, t, k):
        return pltpu.make_async_copy(y_ref.at[pl.ds(halves[half][0, k, t], 1)],
                                     buf.at[k, pl.ds(half * LANES + t, 1)], sem)

    def issue(t, carry):
        for half in range(2):
            for k in range(TOP_K):
                row_copy(half, t, k).start()
        return carry

    def drain(t, carry):
        for half in range(2):
            for k in range(TOP_K):
                row_copy(half, t, k).wait()
        return carry

    lax.fori_loop(0, LANES, issue, 0)
    f = _swiglu(_unpack_bf16_pairs(f_ref[...]), sg_ref[...], su_ref[...], sd_ref[...])
    lax.fori_loop(0, LANES, drain, 0)
    gate = gate_ref[...]
    for k in range(TOP_K):
        f = f + gate[:, k:k + 1] * _unpack_bf16_pairs(buf[k]).astype(F32)
    fn = f * lax.rsqrt(jnp.mean(f * f, axis=-1, keepdims=True) + NORM_EPS) * gp_ref[...]
    o_ref[0] = xa_ref[0] + g2_ref[0] * fn


def _combine(dest, y, gate, f_pk, xa, sg, su, sd, g_post, g2, *, row_tile0, seg_tiles):
    b, rows, d = xa.shape
    tm = 2 * LANES
    assert tm == ROW_TILE
    nt = rows // tm
    mod_map = lambda bi, i: (bi * 2 + jnp.where(i + row_tile0 >= seg_tiles, 1, 0), 0, 0)
    const = lambda shape: pl.BlockSpec(shape, lambda bi, i: (0,) * len(shape))
    dest_spec = lambda half: pl.BlockSpec((1, 8, LANES), lambda bi, i: (2 * (bi * nt + i) + half, 0, 0),
                                          memory_space=pltpu.SMEM)
    return pl.pallas_call(
        _combine_kernel,
        grid=(b, nt),
        in_specs=[
            dest_spec(0), dest_spec(1),
            pl.BlockSpec(memory_space=pl.ANY),
            pl.BlockSpec((tm, 8), lambda bi, i: (bi * nt + i, 0)),
            pl.BlockSpec((tm, d // 2), lambda bi, i: (bi * nt + i, 0)),
            pl.BlockSpec((1, tm, d), lambda bi, i: (bi, i, 0)),
            const(sg.shape), const(su.shape), const(sd.shape), const((1, d)),
            pl.BlockSpec((1, 1, d), mod_map),
        ],
        out_specs=pl.BlockSpec((1, tm, d), lambda bi, i: (bi, i, 0)),
        out_shape=jax.ShapeDtypeStruct((b, rows, d), F32),
        scratch_shapes=[pltpu.VMEM((TOP_K, tm, d // 2), jnp.uint32), pltpu.SemaphoreType.DMA(())],
        compiler_params=_cparams(("arbitrary", "arbitrary")),
    )(dest, dest, y, gate, f_pk, xa, sg, su, sd, g_post.reshape(1, d).astype(F32), g2)


def _rope_tables(s, ctx_len, dim):
    rows = s // GRID_W
    row = jnp.repeat(jnp.arange(rows), GRID_W)
    col = jnp.tile(jnp.arange(GRID_W), rows)
    quarter = dim // 4
    inv = ROPE_BASE ** (-jnp.arange(quarter, dtype=F32) / quarter)
    ang = jnp.concatenate([row.astype(F32)[:, None] * inv, col.astype(F32)[:, None] * inv], axis=-1)
    cos = jnp.repeat(jnp.cos(ang), 2, axis=-1)
    sin = jnp.repeat(jnp.sin(ang), 2, axis=-1) * jnp.tile(jnp.array([-1.0, 1.0], F32), dim // 2)
    cos = jnp.concatenate([jnp.ones((ctx_len, dim), F32), cos], axis=0)
    sin = jnp.concatenate([jnp.zeros((ctx_len, dim), F32), sin], axis=0)
    return cos, sin


def _log_sigmoid(x):
    return jnp.minimum(x, 0.0) - jnp.log1p(jnp.exp(-jnp.abs(x)))


def _mlstm_kernel(q_ref, k_ref, kt_ref, v_ref, g_ref, gt_ref, gb_ref, gbt_ref, hf_ref, mo_ref, mn_ref,
                  o_ref, s_scr, m_scr, *, reverse, final):
    L = MLSTM_CHUNK
    hv = MLSTM_V
    io, fo = (2 * MLSTM_HEADS, 3 * MLSTM_HEADS) if reverse else (0, MLSTM_HEADS)

    @pl.when(pl.program_id(1) == 0)
    def _():
        s_scr[...] = jnp.zeros(s_scr.shape, F32)
        m_scr[...] = jnp.zeros(m_scr.shape, F32)

    g = g_ref[0] + gb_ref[...]
    gt = gt_ref[0] + gbt_ref[...]
    r_io = lax.broadcasted_iota(jnp.int32, (L, L), 0)
    c_io = lax.broadcasted_iota(jnp.int32, (L, L), 1)
    seen = (c_io >= r_io) if reverse else (c_io <= r_io)
    tri = jnp.where(seen, 1.0, 0.0)
    hi = lax.Precision.HIGHEST
    bc_col = jnp.dot(tri, _log_sigmoid(g), precision=hi, preferred_element_type=F32)
    lf_row = _log_sigmoid(gt)
    bc_row = lax.dot_general(lf_row, tri, (((1,), (1,)), ((), ())), precision=hi,
                             preferred_element_type=F32)
    lane = lax.broadcasted_iota(jnp.int32, (L, LANES), 1)
    sub = lax.broadcasted_iota(jnp.int32, (LANES, L), 0)

    for h in range(MLSTM_HEADS):
        pair, odd = h // 2, h % 2
        lo = odd * MLSTM_QK
        a_col = bc_col[:, fo + h:fo + h + 1]
        i_col = g[:, io + h:io + h + 1]
        b_row = bc_row[fo + h:fo + h + 1, :]
        i_row = gt[io + h:io + h + 1, :]
        btot = jnp.sum(lf_row[fo + h:fo + h + 1, :], axis=1, keepdims=True)
        m_st = m_scr[h:h + 1, 0:1]

        w_end = btot - a_col + i_col
        m_new = jnp.maximum(btot + m_st, jnp.max(w_end, axis=0, keepdims=True))
        decay = jnp.exp(btot + m_st - m_new)
        w_k = jnp.exp(w_end - m_new)

        log_d = jnp.where(seen, a_col - b_row + i_row, -jnp.inf)
        log_inter = a_col + m_st
        m_row = jnp.maximum(log_inter, jnp.max(log_d, axis=1, keepdims=True))
        w_intra = jnp.exp(log_d - m_row)
        w_inter = jnp.exp(log_inter - m_row)

        in_head = (lane >= lo) & (lane < lo + MLSTM_QK)
        qm = jnp.where(in_head, q_ref[0, :, pair * LANES:(pair + 1) * LANES], 0).astype(BF16)
        kp = k_ref[0, :, pair * LANES:(pair + 1) * LANES]
        v = v_ref[0, :, h * hv:(h + 1) * hv]
        state = s_scr[h]

        qk = lax.dot_general(qm, kp, (((1,), (1,)), ((), ())), preferred_element_type=F32)
        qk = qk * (MLSTM_QK ** -0.5) * w_intra
        inter = jnp.dot(qm, state.astype(BF16), preferred_element_type=F32)
        num = w_inter * inter[:, :hv] + jnp.dot(qk.astype(BF16), v, preferred_element_type=F32)
        den = w_inter * inter[:, hv:hv + 1] + jnp.sum(qk, axis=1, keepdims=True)
        out = num / jnp.maximum(jnp.abs(den), jnp.exp(-m_row))

        wv = jnp.concatenate([(w_k * v.astype(F32)).astype(BF16),
                              jnp.where(lane == 0, w_k, 0.0).astype(BF16)], axis=1)
        kt = kt_ref[0, pair * LANES:(pair + 1) * LANES, :]
        in_rows = (sub >= lo) & (sub < lo + MLSTM_QK)
        ktm = (jnp.where(in_rows, kt, 0).astype(F32) * (MLSTM_QK ** -0.5)).astype(BF16)
        s_scr[h] = decay * state + jnp.dot(ktm, wv, preferred_element_type=F32)
        m_scr[h:h + 1, :] = jnp.broadcast_to(m_new, (1, LANES))

        if final:
            tot = out + hf_ref[0, :, h * hv:(h + 1) * hv]
            nrm = tot * lax.rsqrt(jnp.mean(tot * tot, axis=-1, keepdims=True) + NORM_EPS) * mn_ref[...]
            gate = jax.nn.sigmoid(mo_ref[0, :, h * hv:(h + 1) * hv].astype(F32))
            o_ref[0, :, h * hv:(h + 1) * hv] = (nrm * gate).astype(o_ref.dtype)
        else:
            o_ref[0, :, h * hv:(h + 1) * hv] = out.astype(o_ref.dtype)


def _mlstm(pv, kt, gates, gates_t, gate_b, m_norm, hf, *, ctx_chunks, reverse):
    b, t, _ = pv.shape
    L = MLSTM_CHUNK
    nc = t // L
    final = hf is not None
    nq = MLSTM_HEADS * MLSTM_QK
    nv = MLSTM_HEADS * MLSTM_V

    def chunk(j):
        if not reverse:
            return j
        return jnp.where(j < ctx_chunks, ctx_chunks - 1 - j, nc - 1 - (j - ctx_chunks))

    gb = jnp.pad(gate_b.astype(F32).reshape(-1), (0, LANES - gate_b.size))
    if hf is None:
        hf = jnp.zeros((1, L, nv), F32)
        hf_spec = pl.BlockSpec((1, L, nv), lambda bi, j: (0, 0, 0))
    else:
        hf_spec = pl.BlockSpec((1, L, nv), lambda bi, j: (bi, chunk(j), 0))
    kern = functools.partial(_mlstm_kernel, reverse=reverse, final=final)
    return pl.pallas_call(
        kern,
        grid=(b, nc),
        in_specs=[
            pl.BlockSpec((1, L, nq), lambda bi, j: (bi, chunk(j), nv // nq)),
            pl.BlockSpec((1, L, nq), lambda bi, j: (bi, chunk(j), nv // nq + 1)),
            pl.BlockSpec((1, nq, L), lambda bi, j: (bi, 0, chunk(j))),
            pl.BlockSpec((1, L, nv), lambda bi, j: (bi, chunk(j), 2)),
            pl.BlockSpec((1, L, LANES), lambda bi, j: (bi, chunk(j), 0)),
            pl.BlockSpec((1, LANES, L), lambda bi, j: (bi, 0, chunk(j))),
            pl.BlockSpec((1, LANES), lambda bi, j: (0, 0)),
            pl.BlockSpec((LANES, 1), lambda bi, j: (0, 0)),
            hf_spec,
            pl.BlockSpec((1, L, nv), lambda bi, j: (bi, chunk(j), 3)),
            pl.BlockSpec((1, MLSTM_V), lambda bi, j: (0, 0)),
        ],
        out_specs=pl.BlockSpec((1, L, nv), lambda bi, j: (bi, chunk(j), 0)),
        out_shape=jax.ShapeDtypeStruct((b, t, nv), BF16 if final else F32),
        scratch_shapes=[pltpu.VMEM((MLSTM_HEADS, LANES, 2 * LANES), F32), pltpu.VMEM((MLSTM_HEADS, LANES), F32)],
        compiler_params=_cparams(("parallel", "arbitrary")),
    )(pv, pv, kt, pv, gates, gates_t, gb.reshape(1, LANES), gb.reshape(LANES, 1), hf, pv,
      m_norm.reshape(1, MLSTM_V).astype(F32))


def _swa_kernel(sink_ref, q_ref, kp_ref, kc_ref, kn_ref, kx_ref, vp_ref, vc_ref, vn_ref, vx_ref, o_ref, *,
                n_blocks):
    i = pl.program_id(1)
    bq = BLOCK_Q
    hd = SWA_HD
    rep = SWA_HEADS // SWA_KV_HEADS
    n_ctx = kx_ref.shape[1]
    rows = rep * bq
    scale = hd ** -0.5 * LOG2E
    row = lax.broadcasted_iota(jnp.int32, (rows, 3 * bq), 0)
    col = lax.broadcasted_iota(jnp.int32, (rows, 3 * bq), 1)
    dt = (col - bq) - (row % bq)
    blk = col // bq
    ok = (jnp.abs(dt) <= WINDOW) & ((blk != 0) | (i > 0)) & ((blk != 2) | (i < n_blocks - 1))
    head_of_row = lax.broadcasted_iota(jnp.int32, (rows, 1), 0) // bq
    for g in range(SWA_KV_HEADS):
        q = jnp.concatenate([q_ref[0, :, (g * rep + r) * hd:(g * rep + r + 1) * hd] for r in range(rep)], axis=0)
        q = (q.astype(F32) * scale).astype(BF16)
        cs = slice(g * hd, (g + 1) * hd)
        k_loc = jnp.concatenate([kp_ref[0, :, cs], kc_ref[0, :, cs], kn_ref[0, :, cs]], axis=0)
        v_loc = jnp.concatenate([vp_ref[0, :, cs], vc_ref[0, :, cs], vn_ref[0, :, cs]], axis=0)
        nt = (((1,), (1,)), ((), ()))
        s_loc = jnp.where(ok, lax.dot_general(q, k_loc, nt, preferred_element_type=F32), -jnp.inf)
        s_ctx = lax.dot_general(q, kx_ref[0, :, cs], nt, preferred_element_type=F32)
        sink = jnp.zeros((rows, 1), F32)
        for r in range(rep):
            sink = jnp.where(head_of_row == r, sink_ref[g * rep + r] * LOG2E, sink)
        m = jnp.maximum(jnp.maximum(jnp.max(s_loc, axis=-1, keepdims=True),
                                    jnp.max(s_ctx, axis=-1, keepdims=True)), sink)
        p_loc = jnp.exp2(s_loc - m)
        p_ctx = jnp.exp2(s_ctx - m)
        den = (jnp.sum(p_loc, axis=-1, keepdims=True) + jnp.sum(p_ctx, axis=-1, keepdims=True)
               + jnp.exp2(sink - m))
        out = (jnp.dot(p_loc.astype(BF16), v_loc, preferred_element_type=F32)
               + jnp.dot(p_ctx.astype(BF16), vx_ref[0, :, cs], preferred_element_type=F32)) / den
        for r in range(rep):
            o_ref[0, :, (g * rep + r) * hd:(g * rep + r + 1) * hd] = out[r * bq:(r + 1) * bq].astype(o_ref.dtype)


def _swa(qk, v, sink, *, n_ctx, k_col0):
    b, t, _ = qk.shape
    bq = BLOCK_Q
    cb = n_ctx // bq
    nb = (t - n_ctx) // bq
    nq = SWA_HEADS * SWA_HD
    nk = SWA_KV_HEADS * SWA_HD
    kcol = k_col0
    prev = lambda bi, i, c: (bi, i + cb - 1, c)
    cur = lambda bi, i, c: (bi, i + cb, c)
    nxt = lambda bi, i, c: (bi, jnp.minimum(i + cb + 1, nb + cb - 1), c)
    kern = functools.partial(_swa_kernel, n_blocks=nb)

    def spec(rows, fn, c):
        return pl.BlockSpec((1, rows, nk), lambda bi, i: fn(bi, i, c))

    ctx = lambda bi, i, c: (bi, 0, c)
    return pl.pallas_call(
        kern,
        grid=(b, nb),
        in_specs=[
            pl.BlockSpec(memory_space=pltpu.SMEM),
            pl.BlockSpec((1, bq, nq), lambda bi, i: (bi, i + cb, 0)),
            spec(bq, prev, kcol), spec(bq, cur, kcol), spec(bq, nxt, kcol), spec(n_ctx, ctx, kcol),
            spec(bq, prev, 0), spec(bq, cur, 0), spec(bq, nxt, 0), spec(n_ctx, ctx, 0),
        ],
        out_specs=pl.BlockSpec((1, bq, nq), lambda bi, i: (bi, i, 0)),
        out_shape=jax.ShapeDtypeStruct((b, nb * bq, nq), BF16),
        compiler_params=_cparams(("parallel", "arbitrary")),
    )(sink.astype(F32), qk, qk, qk, qk, qk, v, v, v, v)


def _mix_out_and_moe(mix, xs, w_out, g_post, g_ffn_pre, g_ffn_post, mods, router_w, router_b,
                     wg, wu, wd, sg, su, sd, *, layer, row_tile0, seg_tiles):
    sh1, sc1, g1, sh2, sc2, g2 = mods
    b, rows, _ = mix[0].shape
    n = b * rows
    seg = dict(row_tile0=row_tile0, seg_tiles=seg_tiles)
    xa, f_pk, logits_t = _post_mix(mix, xs, w_out.astype(BF16), g_post, g_ffn_pre, g1, sh2, sc2,
                                   router_w.T.astype(BF16), **seg)
    top_e, gate = _route(logits_t, router_b, tn=512)
    blk = EXPERT_ROWS
    n_blocks = -(-n * TOP_K // blk) + N_EXPERTS
    dest, block_e, n_used = _dispatch(top_e, blk=blk, n_blocks=n_blocks)
    f_pk = f_pk.reshape(n, -1)
    x_sorted = _scatter_rows(dest, n_used, f_pk, blk=blk, n_blocks=n_blocks)
    y = _expert_ffn(x_sorted, block_e[0, :n_blocks], n_used[0, :1], wg, wu, wd, layer=layer, tm=blk)
    return _combine(dest, y, gate.T, f_pk, xa, sg.astype(BF16), su.astype(BF16), sd.astype(BF16),
                    g_ffn_post, g2, **seg)


def kernel(x, c, ctx, c_ctx, mod_w, mod_b, norm_mix_pre, norm_mix_post, norm_ffn_pre, norm_ffn_post, ab_w_in, ab_w_out, diff_lambda, diff_norm, mlstm_gate_b, mlstm_norm, cd_w_in, cd_w_out, mla_q_norm, mla_w_uq, mla_kv_norm, mla_w_ukv, swa_sink, router_w, router_b, exp_w_gate, exp_w_up, exp_w_down, sh_w_gate, sh_w_up, sh_w_down):
    b, s, d = x.shape
    L = ctx.shape[1]
    t = L + s
    assert L == ROW_TILE and s % ROW_TILE == 0
    depth = mod_w.shape[0]
    ctx_tiles = L // ROW_TILE
    tk = next(c for c in (2816, 768, ROW_TILE) if t % c == 0)

    cos64, sin64 = _rope_tables(s, L, DIFF_HD)
    cos64 = jnp.tile(cos64, (1, LANES // DIFF_HD))
    sin64 = jnp.tile(sin64, (1, LANES // DIFF_HD))
    cos128, sin128 = _rope_tables(s, L, SWA_HD)
    cos_kr = jnp.concatenate([cos64[:, :MLA_ROPE], jnp.ones((t, LANES - MLA_ROPE), F32)], axis=1)
    sin_kr = jnp.concatenate([sin64[:, :MLA_ROPE], jnp.zeros((t, LANES - MLA_ROPE), F32)], axis=1)

    xs = jnp.concatenate([ctx, x], axis=1)
    c_rows = jnp.concatenate([c, c_ctx[None], jnp.zeros((-(b + 1) % 8, d), F32)], axis=0)

    for layer in range(depth):
        with_ctx = layer < depth - 1
        j = layer // 2
        mod_all = _modulation(c_rows, mod_w, mod_b, layer=layer)
        mod_l, mod_c = mod_all[:b], mod_all[b]
        mods = jnp.stack([jnp.broadcast_to(mod_c, (b, 6 * d)), mod_l], axis=1).reshape(b * 2, 1, 6, d)
        sh1, sc1, g1, sh2, sc2, g2 = (mods[:, :, m] for m in range(6))

        proj = functools.partial(_norm_proj, xs, norm_mix_pre[layer], sh1, sc1, seg_tiles=ctx_tiles)
        if layer % 2 == 0:
            w_in = ab_w_in[j].astype(BF16)
            n_qk = 2 * DIFF_HEADS * 2 * DIFF_HD
            qk = proj(w_in[:, :n_qk], tn=2048, rope=(cos64[None], sin64[None]), pattern=(0,) * 16)
            pv = proj(w_in[:, n_qk:n_qk + AB_PLAIN], tn=2048)
            w_gates = jnp.pad(w_in[:, n_qk + AB_PLAIN:], ((0, 0), (0, LANES - 4 * MLSTM_HEADS)))
            gates = proj(w_gates, tn=LANES, out_dtype=F32)

            lam_init = 0.8 - 0.6 * math.exp(-0.3 * layer)
            a = _attention(qk if with_ctx else qk[:, L:], qk, pv, heads=DIFF_HEADS, dq=2 * DIFF_HD,
                           dv=DIFF_VD, q_col0=0, k_col0=DIFF_HEADS, v_col0=0, tq=ROW_TILE,
                           scale=DIFF_HD ** -0.5, tk=tk, diff=True, lam_vec=diff_lambda[j],
                           d_norm=diff_norm[j], lam_init=lam_init,
                           ctx_tiles=ctx_tiles if with_ctx else 0, ctx_len=L)
            o_mk = DIFF_HEADS * DIFF_VD + MLSTM_HEADS * MLSTM_QK
            kt = jnp.swapaxes(pv[..., o_mk:o_mk + MLSTM_HEADS * MLSTM_QK], 1, 2)
            scan = functools.partial(_mlstm, pv, kt, gates, jnp.swapaxes(gates, 1, 2), mlstm_gate_b[j],
                                     mlstm_norm[j], ctx_chunks=L // MLSTM_CHUNK)
            m = scan(scan(None, reverse=False), reverse=True)
            mix = (a, m if with_ctx else m[:, L:])
            w_out = ab_w_out[j]
        else:
            assert not with_ctx
            w_in = cd_w_in[j].astype(BF16)
            c0 = Q_LORA
            c1 = c0 + KV_LORA
            c2 = c1 + MLA_ROPE
            c3 = c2 + SWA_HEADS * SWA_HD
            c4 = c3 + SWA_KV_HEADS * SWA_HD
            w_rope = jnp.concatenate([w_in[:, c2:c4], w_in[:, c1:c2],
                                      jnp.zeros((d, LANES - MLA_ROPE), BF16)], axis=1)
            n_rope = w_rope.shape[1]
            rp = proj(w_rope, tn=n_rope, rope=(jnp.stack([cos128, cos_kr]), jnp.stack([sin128, sin_kr])),
                      pattern=(0,) * (n_rope // LANES - 1) + (1,))
            q_pad = 512 - Q_LORA
            w_plain = jnp.concatenate([w_in[:, c4:], w_in[:, :c0], jnp.zeros((d, q_pad), BF16),
                                       w_in[:, c0:c1]], axis=1)
            pp = proj(w_plain, tn=w_plain.shape[1])
            n_sv = SWA_KV_HEADS * SWA_HD
            cq = pp[..., n_sv:n_sv + 512]
            ckv = pp[..., n_sv + 512:]

            hq = MLA_NOPE + MLA_ROPE
            w_uq = mla_w_uq[j].astype(BF16).reshape(Q_LORA, MLA_HEADS, hq)
            w_uq = jnp.pad(w_uq, ((0, q_pad), (0, 0), (0, 256 - hq))).reshape(512, MLA_HEADS * 256)
            no_mod = jnp.zeros((b * 2, 1, 512), F32)
            qn = jnp.pad(mla_q_norm[j], (0, q_pad))
            q_mla = _norm_proj(cq[:, L:], qn, no_mod, no_mod, w_uq, seg_tiles=0, tn=2048, n_valid=Q_LORA,
                               rope=(cos_kr[None, L:], sin_kr[None, L:]), pattern=(-1, 0) * 8)
            w_ukv = mla_w_ukv[j].astype(BF16).reshape(KV_LORA, MLA_HEADS, MLA_NOPE + MLA_V)
            w_k = jnp.pad(w_ukv[..., :MLA_NOPE], ((0, 0), (0, 0), (0, LANES))).reshape(KV_LORA, MLA_HEADS * 256)
            w_v = w_ukv[..., MLA_NOPE:].reshape(KV_LORA, MLA_HEADS * MLA_V)
            w_kv = jnp.concatenate([w_k, w_v], axis=1)
            kv = _norm_proj(ckv, mla_kv_norm[j], no_mod, no_mod, w_kv, seg_tiles=ctx_tiles, tn=w_kv.shape[1],
                            pattern=(-1, FILL) * MLA_HEADS + (-1,) * (MLA_HEADS * MLA_V // LANES),
                            fill=(rp, n_rope // LANES - 1))
            a = _attention(q_mla, kv, kv, heads=MLA_HEADS, dq=256, dv=MLA_V, q_col0=0, k_col0=0,
                           v_col0=MLA_HEADS * 256 // MLA_V, tq=512, scale=MLA_SCALE, tk=tk)

            w = _swa(rp, pp, swa_sink[j], n_ctx=L, k_col0=SWA_HEADS * SWA_HD // n_sv)
            mix = (a, w)
            w_out = cd_w_out[j]

        xa = _mix_out_and_moe(mix, xs, w_out, norm_mix_post[layer], norm_ffn_pre[layer], norm_ffn_post[layer],
                              (sh1, sc1, g1, sh2, sc2, g2), router_w[layer], router_b[layer],
                              exp_w_gate, exp_w_up, exp_w_down,
                              sh_w_gate[layer], sh_w_up[layer], sh_w_down[layer],
                              layer=layer, row_tile0=0 if with_ctx else ctx_tiles, seg_tiles=ctx_tiles)
        xs = xa if with_ctx else jnp.concatenate([xs[:, :L], xa], axis=1)
    return xs[:, L:]
```

```python
import functools
import math

import jax
import jax.numpy as jnp
from jax import lax
from jax.experimental import pallas as pl
from jax.experimental.pallas import tpu as pltpu

F32 = jnp.float32
BF16 = jnp.bfloat16

GRID_W = 64
ROPE_BASE = 10000.0
NORM_EPS = 1e-6

DIFF_HEADS = 8
DIFF_HD = 64
DIFF_VD = 2 * DIFF_HD
MLSTM_HEADS = 8
MLSTM_QK = 64
MLSTM_V = 128
MLSTM_CHUNK = 128
MLA_HEADS = 8
MLA_NOPE = 128
MLA_ROPE = 64
MLA_V = 128
Q_LORA = 448
KV_LORA = 512
MLA_SCALE = (MLA_NOPE + MLA_ROPE) ** -0.5
SWA_HEADS = 8
SWA_KV_HEADS = 2
SWA_HD = 128
WINDOW = 128
BLOCK_Q = 128
N_EXPERTS = 64
TOP_K = 6
N_GROUPS = 8
TOPK_GROUPS = 4
ROUTED_SCALE = 2.5
AB_PLAIN = DIFF_HEADS * DIFF_VD + 2 * MLSTM_HEADS * MLSTM_QK + 2 * MLSTM_HEADS * MLSTM_V

LANES = 128
LOG2E = 1.4426950408889634
VMEM_LIMIT = 56 * 1024 * 1024

ROW_TILE = 256
EXPERT_ROWS = 512
PROJ_COLS = 2048
ROUTE_TOKENS = 512
MLA_Q_TILE = 512
ATTN_CHUNKS = (2816, 768, ROW_TILE)
Q_LORA_PAD = 512
MLA_QK_PAD = 256


def _cparams(sem):
    return pltpu.CompilerParams(dimension_semantics=sem, vmem_limit_bytes=VMEM_LIMIT)


def _modulation_kernel(c_ref, w_ref, b_ref, o_ref):
    c = c_ref[...]
    h = (c * jax.nn.sigmoid(c)).astype(BF16)
    o_ref[...] = jnp.dot(h, w_ref[0].astype(BF16), preferred_element_type=F32) + b_ref[0]


def _modulation(c_rows, mod_w, mod_b, *, layer):
    r, d = c_rows.shape
    n = mod_w.shape[2]
    tn = d // 2
    assert n % tn == 0 and tn % LANES == 0
    return pl.pallas_call(
        _modulation_kernel,
        grid=(n // tn,),
        in_specs=[pl.BlockSpec((r, d), lambda j: (0, 0)),
                  pl.BlockSpec((1, d, tn), lambda j: (layer, 0, j)),
                  pl.BlockSpec((1, 1, tn), lambda j: (layer, 0, j))],
        out_specs=pl.BlockSpec((r, tn), lambda j: (0, j)),
        out_shape=jax.ShapeDtypeStruct((r, n), F32),
        compiler_params=_cparams(("parallel",)),
    )(c_rows, mod_w, mod_b.reshape(mod_b.shape[0], 1, n))


def _pair_swap(a):
    lane = lax.broadcasted_iota(jnp.int32, a.shape, 1)
    return jnp.where(lane % 2 == 0, pltpu.roll(a, LANES - 1, 1), pltpu.roll(a, 1, 1))


FILL = -2


def _norm_proj_kernel(x_ref, g_ref, sh_ref, sc_ref, w_ref, cos_ref, sin_ref, fill_ref, o_ref, *,
                      n_valid, pattern):
    x = x_ref[0].astype(F32)
    ms = jnp.sum(x * x, axis=-1, keepdims=True) * (1.0 / n_valid)
    y = x * lax.rsqrt(ms + NORM_EPS) * g_ref[...]
    h = (y * (1.0 + sc_ref[0]) + sh_ref[0]).astype(BF16)
    acc = jnp.dot(h, w_ref[...], preferred_element_type=F32)
    if pattern is None:
        o_ref[0] = acc.astype(o_ref.dtype)
    else:
        for c, tbl in enumerate(pattern):
            if tbl == FILL:
                o_ref[0, :, c * LANES:(c + 1) * LANES] = fill_ref[0].astype(o_ref.dtype)
                continue
            a = acc[:, c * LANES:(c + 1) * LANES]
            if tbl >= 0:
                a = a * cos_ref[tbl] + _pair_swap(a) * sin_ref[tbl]
            o_ref[0, :, c * LANES:(c + 1) * LANES] = a.astype(o_ref.dtype)


def _norm_proj(x, g, shift, scale, w, *, seg_tiles, tn, out_dtype=BF16, n_valid=None,
               rope=None, pattern=None, fill=None):
    b, t, k = x.shape
    if fill is None:
        fill_arr, fill_map = jnp.zeros((1, ROW_TILE, LANES), BF16), (lambda j, bi, i: (0, 0, 0))
    else:
        fill_arr, fill_map = fill[0], (lambda j, bi, i: (bi, i, fill[1]))
    n = w.shape[1]
    tm = ROW_TILE
    assert t % tm == 0 and n % tn == 0 and tn % LANES == 0
    if rope is None:
        cos = sin = jnp.zeros((1, tm, LANES), F32)
        tbl_map = lambda j, bi, i: (0, 0, 0)
    else:
        cos, sin = rope
        tbl_map = lambda j, bi, i: (0, i, 0)
    ntab = cos.shape[0]
    kern = functools.partial(_norm_proj_kernel, n_valid=float(n_valid or k), pattern=pattern)
    mod_map = lambda j, bi, i: (bi * 2 + jnp.where(i >= seg_tiles, 1, 0), 0, 0)
    return pl.pallas_call(
        kern,
        grid=(n // tn, b, t // tm),
        in_specs=[
            pl.BlockSpec((1, tm, k), lambda j, bi, i: (bi, i, 0)),
            pl.BlockSpec((1, k), lambda j, bi, i: (0, 0)),
            pl.BlockSpec((1, 1, k), mod_map),
            pl.BlockSpec((1, 1, k), mod_map),
            pl.BlockSpec((k, tn), lambda j, bi, i: (0, j)),
            pl.BlockSpec((ntab, tm, LANES), tbl_map),
            pl.BlockSpec((ntab, tm, LANES), tbl_map),
            pl.BlockSpec((1, tm, LANES), fill_map),
        ],
        out_specs=pl.BlockSpec((1, tm, tn), lambda j, bi, i: (bi, i, j)),
        out_shape=jax.ShapeDtypeStruct((b, t, n), out_dtype),
        compiler_params=_cparams(("parallel", "parallel", "parallel")),
    )(x, g.reshape(1, k).astype(F32), shift, scale, w, cos, sin, fill_arr)


def _attn_kernel(q_ref, k_ref, v_ref, lam_ref, dn_ref, o_ref, q_scr, s_scr, m_scr, l_scr, acc_scr, *,
                 scale, diff, lam_init, ctx_tiles, ctx_len, tq, tk, t):
    i = pl.program_id(2)
    q = q_ref[0].astype(F32) * (scale * LOG2E)
    if diff:
        lane = lax.broadcasted_iota(jnp.int32, q.shape, 1)
        half = q.shape[1] // 2
        q_scr[0:tq] = jnp.where(lane < half, q, 0.0).astype(BF16)
        q_scr[tq:2 * tq] = jnp.where(lane >= half, q, 0.0).astype(BF16)
    else:
        q_scr[...] = q.astype(BF16)

    def attend(kv_len, chunk):
        n_chunks = kv_len // chunk
        groups = chunk // LANES

        def pass1(c, carry):
            k = k_ref[0, pl.ds(pl.multiple_of(c * chunk, chunk), chunk), :]
            s = lax.dot_general(q_scr[...], k, (((1,), (1,)), ((), ())), preferred_element_type=F32)
            s_scr[c, :, 0:chunk] = s
            m = m_scr[...]
            for g in range(groups):
                m = jnp.maximum(m, s[:, g * LANES:(g + 1) * LANES])
            m_scr[...] = m
            return carry

        def pass2(c, carry):
            s = s_scr[c, :, 0:chunk]
            p = jnp.exp2(s - jnp.concatenate([m_scr[...]] * groups, axis=1))
            l = l_scr[...]
            for g in range(groups):
                l = l + p[:, g * LANES:(g + 1) * LANES]
            l_scr[...] = l
            v = v_ref[0, pl.ds(pl.multiple_of(c * chunk, chunk), chunk), :]
            acc_scr[...] += jnp.dot(p.astype(BF16), v, preferred_element_type=F32)
            return carry

        m_scr[...] = jnp.full(m_scr.shape, -jnp.inf, F32)
        if n_chunks == 1:
            pass1(0, 0)
        else:
            lax.fori_loop(0, n_chunks, pass1, 0)
        m_scr[...] = jnp.broadcast_to(jnp.max(m_scr[...], axis=-1, keepdims=True), m_scr.shape)
        l_scr[...] = jnp.zeros(l_scr.shape, F32)
        acc_scr[...] = jnp.zeros(acc_scr.shape, F32)
        if n_chunks == 1:
            pass2(0, 0)
        else:
            lax.fori_loop(0, n_chunks, pass2, 0)

        o = acc_scr[...] / jnp.sum(l_scr[...], axis=-1, keepdims=True)
        if diff:
            lv = lam_ref[...]
            lam = (jnp.exp(jnp.sum(lv[0:1] * lv[1:2], axis=-1, keepdims=True))
                   - jnp.exp(jnp.sum(lv[2:3] * lv[3:4], axis=-1, keepdims=True)) + lam_init)
            a = o[0:tq] - lam * o[tq:2 * tq]
            ms = jnp.mean(a * a, axis=-1, keepdims=True)
            a = a * lax.rsqrt(ms + NORM_EPS) * dn_ref[...] * (1.0 - lam_init)
            o_ref[0] = a.astype(o_ref.dtype)
        else:
            o_ref[0] = o.astype(o_ref.dtype)

    if ctx_tiles:
        pl.when(i < ctx_tiles)(lambda: attend(ctx_len, ctx_len))
        pl.when(i >= ctx_tiles)(lambda: attend(t, tk))
    else:
        attend(t, tk)


def _attention(q, k, v, *, heads, dq, dv, q_col0, k_col0, v_col0, tq, scale, tk, diff=False,
               lam_vec=None, d_norm=None, lam_init=0.0, ctx_tiles=0, ctx_len=0):
    b, t, _ = k.shape
    sq = q.shape[1]
    assert t % tk == 0 and sq % tq == 0 and tk % LANES == 0 and ctx_len % LANES == 0
    assert ctx_len <= tk
    rows = 2 * tq if diff else tq
    if lam_vec is None:
        lam_vec = jnp.zeros((4, DIFF_HD), F32)
        d_norm = jnp.zeros((dv,), F32)
    kern = functools.partial(_attn_kernel, scale=scale, diff=diff, lam_init=lam_init,
                             ctx_tiles=ctx_tiles, ctx_len=ctx_len, tq=tq, tk=tk, t=t)
    return pl.pallas_call(
        kern,
        grid=(b, heads, sq // tq),
        in_specs=[
            pl.BlockSpec((1, tq, dq), lambda bi, h, i: (bi, i, q_col0 + h)),
            pl.BlockSpec((1, t, dq), lambda bi, h, i: (bi, 0, k_col0 + h)),
            pl.BlockSpec((1, t, dv), lambda bi, h, i: (bi, 0, v_col0 + h)),
            pl.BlockSpec((4, DIFF_HD), lambda bi, h, i: (0, 0)),
            pl.BlockSpec((1, dv), lambda bi, h, i: (0, 0)),
        ],
        out_specs=pl.BlockSpec((1, tq, dv), lambda bi, h, i: (bi, i, h)),
        out_shape=jax.ShapeDtypeStruct((b, sq, heads * dv), BF16),
        scratch_shapes=[pltpu.VMEM((rows, dq), BF16), pltpu.VMEM((t // tk, rows, tk), F32),
                        pltpu.VMEM((rows, LANES), F32), pltpu.VMEM((rows, LANES), F32),
                        pltpu.VMEM((rows, dv), F32)],
        compiler_params=_cparams(("parallel", "parallel", "arbitrary")),
    )(q, k, v, lam_vec.astype(F32), d_norm.reshape(1, dv).astype(F32))


def _pack_bf16_pairs(x):
    h = x.shape[1] // 2
    lo = lax.bitcast_convert_type(x[:, :h].astype(F32), jnp.uint32) >> 16
    hi = lax.bitcast_convert_type(x[:, h:].astype(F32), jnp.uint32) & jnp.uint32(0xFFFF0000)
    return hi | lo


def _unpack_bf16_pairs(w):
    lo = lax.bitcast_convert_type(w << 16, F32)
    hi = lax.bitcast_convert_type(w & jnp.uint32(0xFFFF0000), F32)
    return jnp.concatenate([lo.astype(BF16), hi.astype(BF16)], axis=1)


def _swiglu(x, wg, wu, wd):
    g = jnp.dot(x, wg, preferred_element_type=F32)
    u = jnp.dot(x, wu, preferred_element_type=F32)
    a = (g * jax.nn.sigmoid(g) * u).astype(BF16)
    return jnp.dot(a, wd, preferred_element_type=F32)


def _expert_kernel(be_ref, nu_ref, x_ref, wg_ref, wu_ref, wd_ref, o_ref, wg_s, wu_s, wd_s):
    i = pl.program_id(0)

    @pl.when(i < nu_ref[0])
    def _():
        @pl.when((i == 0) | (be_ref[i] != be_ref[jnp.maximum(i - 1, 0)]))
        def _():
            wg_s[...] = wg_ref[0, 0].astype(BF16)
            wu_s[...] = wu_ref[0, 0].astype(BF16)
            wd_s[...] = wd_ref[0, 0].astype(BF16)

        y = _swiglu(_unpack_bf16_pairs(x_ref[...]), wg_s[...], wu_s[...], wd_s[...])
        o_ref[...] = _pack_bf16_pairs(y.astype(BF16))

    @pl.when(i >= nu_ref[0])
    def _():
        o_ref[...] = jnp.zeros(o_ref.shape, o_ref.dtype)


def _expert_ffn(x, block_e, n_used, wg, wu, wd, *, layer, tm):
    n, dh = x.shape
    d = 2 * dh
    ff = wg.shape[3]
    assert n % tm == 0

    def blk(i, be, nu):
        return jnp.minimum(i, nu[0] - 1)

    grid_spec = pltpu.PrefetchScalarGridSpec(
        num_scalar_prefetch=2,
        grid=(n // tm,),
        in_specs=[
            pl.BlockSpec((tm, dh), lambda i, be, nu: (blk(i, be, nu), 0)),
            pl.BlockSpec((1, 1, d, ff), lambda i, be, nu: (layer, be[blk(i, be, nu)], 0, 0)),
            pl.BlockSpec((1, 1, d, ff), lambda i, be, nu: (layer, be[blk(i, be, nu)], 0, 0)),
            pl.BlockSpec((1, 1, ff, d), lambda i, be, nu: (layer, be[blk(i, be, nu)], 0, 0)),
        ],
        out_specs=pl.BlockSpec((tm, dh), lambda i, be, nu: (i, 0)),
        scratch_shapes=[pltpu.VMEM((d, ff), BF16), pltpu.VMEM((d, ff), BF16), pltpu.VMEM((ff, d), BF16)],
    )
    return pl.pallas_call(
        _expert_kernel,
        grid_spec=grid_spec,
        out_shape=jax.ShapeDtypeStruct((n, dh), jnp.uint32),
        compiler_params=_cparams(("arbitrary",)),
    )(block_e, n_used, x, wg, wu, wd)


def _post_mix_kernel(a_ref, m_ref, x_ref, w_ref, gp_ref, gf_ref, g1_ref, sh_ref, sc_ref, rw_ref,
                     xa_ref, f_ref, lg_ref):
    ka = a_ref.shape[2]
    y = (jnp.dot(a_ref[0], w_ref[0:ka, :], preferred_element_type=F32)
         + jnp.dot(m_ref[0], w_ref[ka:, :], preferred_element_type=F32))
    yn = y * lax.rsqrt(jnp.mean(y * y, axis=-1, keepdims=True) + NORM_EPS) * gp_ref[...]
    xa = x_ref[0] + g1_ref[0] * yn
    xa_ref[0] = xa
    fn = xa * lax.rsqrt(jnp.mean(xa * xa, axis=-1, keepdims=True) + NORM_EPS) * gf_ref[...]
    f = (fn * (1.0 + sc_ref[0]) + sh_ref[0]).astype(BF16)
    f_ref[0] = _pack_bf16_pairs(f)
    lg_ref[...] = lax.dot_general(rw_ref[...], f, (((1,), (1,)), ((), ())), preferred_element_type=F32)


def _post_mix(mix, xs, w_out, g_post, g_ffn, g1, sh2, sc2, router_wt, *, row_tile0, seg_tiles):
    mix_a, mix_m = mix
    b, rows, ka = mix_a.shape
    km = mix_m.shape[2]
    k = ka + km
    d = w_out.shape[1]
    e = router_wt.shape[0]
    tm = ROW_TILE
    nt = rows // tm
    mod_map = lambda bi, i: (bi * 2 + jnp.where(i + row_tile0 >= seg_tiles, 1, 0), 0, 0)
    vec = lambda: pl.BlockSpec((1, d), lambda bi, i: (0, 0))
    return pl.pallas_call(
        _post_mix_kernel,
        grid=(b, nt),
        in_specs=[
            pl.BlockSpec((1, tm, ka), lambda bi, i: (bi, i, 0)),
            pl.BlockSpec((1, tm, km), lambda bi, i: (bi, i, 0)),
            pl.BlockSpec((1, tm, d), lambda bi, i: (bi, i + row_tile0, 0)),
            pl.BlockSpec((k, d), lambda bi, i: (0, 0)),
            vec(), vec(),
            pl.BlockSpec((1, 1, d), mod_map), pl.BlockSpec((1, 1, d), mod_map),
            pl.BlockSpec((1, 1, d), mod_map),
            pl.BlockSpec((e, d), lambda bi, i: (0, 0)),
        ],
        out_specs=[
            pl.BlockSpec((1, tm, d), lambda bi, i: (bi, i, 0)),
            pl.BlockSpec((1, tm, d // 2), lambda bi, i: (bi, i, 0)),
            pl.BlockSpec((e, tm), lambda bi, i: (0, bi * nt + i)),
        ],
        out_shape=[jax.ShapeDtypeStruct((b, rows, d), F32),
                   jax.ShapeDtypeStruct((b, rows, d // 2), jnp.uint32),
                   jax.ShapeDtypeStruct((e, b * rows), F32)],
        compiler_params=_cparams(("parallel", "arbitrary")),
    )(mix_a, mix_m, xs, w_out, g_post.reshape(1, d).astype(F32), g_ffn.reshape(1, d).astype(F32),
      g1, sh2, sc2, router_wt)


def _route_kernel(lg_ref, rb_ref, e_ref, g_ref):
    per = N_EXPERTS // N_GROUPS
    tn = lg_ref.shape[1]
    neg = -jnp.inf
    r_io = lax.broadcasted_iota(jnp.int32, (per, tn), 0)
    scores, choice, gs = [], [], []
    for g in range(N_GROUPS):
        sg = jax.nn.sigmoid(lg_ref[g * per:(g + 1) * per, :])
        cg = sg + rb_ref[g * per:(g + 1) * per, :]
        m1 = jnp.max(cg, axis=0, keepdims=True)
        i1 = jnp.min(jnp.where(cg == m1, r_io, per), axis=0, keepdims=True)
        m2 = jnp.max(jnp.where(r_io == i1, neg, cg), axis=0, keepdims=True)
        scores.append(sg)
        choice.append(cg)
        gs.append(m1 + m2)
    masked = []
    for g in range(N_GROUPS):
        ahead = jnp.zeros((1, tn), jnp.int32)
        for o in range(N_GROUPS):
            if o < g:
                ahead = ahead + jnp.where(gs[o] >= gs[g], 1, 0)
            elif o > g:
                ahead = ahead + jnp.where(gs[o] > gs[g], 1, 0)
        masked.append(jnp.where(ahead < TOPK_GROUPS, choice[g], neg))
    ids, gates = [], []
    for _ in range(TOP_K):
        best = masked[0]
        for g in range(1, N_GROUPS):
            best = jnp.maximum(best, masked[g])
        best = jnp.max(best, axis=0, keepdims=True)
        cand = jnp.where(masked[0] == best, r_io, N_EXPERTS)
        for g in range(1, N_GROUPS):
            cand = jnp.minimum(cand, jnp.where(masked[g] == best, r_io + g * per, N_EXPERTS))
        idx = jnp.min(cand, axis=0, keepdims=True)
        gk = jnp.zeros((per, tn), F32)
        for g in range(N_GROUPS):
            hit = (r_io + g * per) == idx
            gk = gk + jnp.where(hit, scores[g], 0.0)
            masked[g] = jnp.where(hit, neg, masked[g])
        ids.append(idx)
        gates.append(jnp.sum(gk, axis=0, keepdims=True))
    total = gates[0]
    for k in range(1, TOP_K):
        total = total + gates[k]
    pad = 8 - TOP_K
    e_ref[...] = jnp.concatenate(ids + [jnp.zeros((pad, tn), jnp.int32)], axis=0)
    g_ref[...] = jnp.concatenate([gk / total * ROUTED_SCALE for gk in gates]
                                 + [jnp.zeros((pad, tn), F32)], axis=0)


def _route(logits_t, router_b, *, tn):
    e, n = logits_t.shape
    assert n % tn == 0
    return pl.pallas_call(
        _route_kernel,
        grid=(n // tn,),
        in_specs=[pl.BlockSpec((e, tn), lambda i: (0, i)),
                  pl.BlockSpec((e, 1), lambda i: (0, 0))],
        out_specs=[pl.BlockSpec((8, tn), lambda i: (0, i)), pl.BlockSpec((8, tn), lambda i: (0, i))],
        out_shape=[jax.ShapeDtypeStruct((8, n), jnp.int32), jax.ShapeDtypeStruct((8, n), F32)],
        compiler_params=_cparams(("parallel",)),
    )(logits_t, router_b.reshape(e, 1).astype(F32))


def _dispatch_kernel(e_ref, dest_ref, be_ref, nu_ref, cnt_scr, start_scr, run_scr, *, blk):
    ph = pl.program_id(0)
    i = pl.program_id(1)
    tn = e_ref.shape[1]
    e_io = lax.broadcasted_iota(jnp.int32, (N_EXPERTS, tn), 0)
    hot = jnp.zeros((N_EXPERTS, tn), F32)
    for k in range(TOP_K):
        hot = hot + jnp.where(e_io == e_ref[k:k + 1, :], 1.0, 0.0)
    tile_cnt = jnp.sum(hot, axis=1, keepdims=True)

    @pl.when((ph == 0) & (i == 0))
    def _():
        cnt_scr[...] = jnp.zeros(cnt_scr.shape, F32)

    @pl.when(ph == 0)
    def _():
        cnt_scr[...] += jnp.broadcast_to(tile_cnt, cnt_scr.shape)

    @pl.when((ph == 1) & (i == 0))
    def _():
        bpe = jnp.floor((cnt_scr[...] + (blk - 1.0)) * (1.0 / blk))
        r = lax.broadcasted_iota(jnp.int32, (N_EXPERTS, N_EXPERTS), 0)
        c = lax.broadcasted_iota(jnp.int32, (N_EXPERTS, N_EXPERTS), 1)
        lower = jnp.where(c < r, 1.0, 0.0)
        before = jnp.dot(lower, bpe, precision=lax.Precision.HIGHEST, preferred_element_type=F32)
        start_scr[...] = before * blk
        run_scr[...] = jnp.zeros(run_scr.shape, F32)
        ends = (before + bpe)[:, 0:1]
        nb = be_ref.shape[1]
        bid = lax.broadcasted_iota(jnp.int32, (N_EXPERTS, nb), 1).astype(F32)
        be = jnp.sum(jnp.where(ends <= bid, 1, 0), axis=0, keepdims=True)
        be_ref[...] = jnp.minimum(be, N_EXPERTS - 1).astype(jnp.int32)
        n_used = jnp.broadcast_to(jnp.max(ends, axis=0, keepdims=True), (1, LANES))
        last = jnp.where(bpe > 0.0, (before + bpe - 1.0) * blk, -1.0)
        er = lax.broadcasted_iota(jnp.int32, (N_EXPERTS, LANES), 0)
        ec = lax.broadcasted_iota(jnp.int32, (N_EXPERTS, LANES), 1)
        last_row = jnp.sum(jnp.where(er == ec, last, 0.0), axis=0, keepdims=True)
        last_row = jnp.where(ec[0:1] < N_EXPERTS, last_row, -1.0)
        nu_ref[...] = jnp.concatenate([n_used, last_row, jnp.zeros((6, LANES), F32)], axis=0).astype(jnp.int32)

    @pl.when(ph == 1)
    def _():
        rr = lax.broadcasted_iota(jnp.int32, (tn, tn), 0)
        cc = lax.broadcasted_iota(jnp.int32, (tn, tn), 1)
        upper = jnp.where(rr < cc, 1.0, 0.0).astype(BF16)
        prior = jnp.dot(hot.astype(BF16), upper, preferred_element_type=F32)
        pos = prior + jnp.concatenate([start_scr[...] + run_scr[...]] * (tn // LANES), axis=1)
        rows = []
        for k in range(TOP_K):
            rows.append(jnp.sum(jnp.where(e_io == e_ref[k:k + 1, :], pos, 0.0), axis=0, keepdims=True))
        rows.append(jnp.zeros((8 - TOP_K, tn), F32))
        dest_ref[0] = jnp.concatenate(rows, axis=0).astype(jnp.int32)
        run_scr[...] += jnp.broadcast_to(tile_cnt, run_scr.shape)


def _dispatch(top_e, *, blk, n_blocks):
    n = top_e.shape[1]
    tn = LANES
    nbp = -(-n_blocks // LANES) * LANES
    kern = functools.partial(_dispatch_kernel, blk=blk)
    return pl.pallas_call(
        kern,
        grid=(2, n // tn),
        in_specs=[pl.BlockSpec((8, tn), lambda ph, i: (0, i))],
        out_specs=[pl.BlockSpec((1, 8, tn), lambda ph, i: (i * ph, 0, 0)),
                   pl.BlockSpec((1, nbp), lambda ph, i: (0, 0)),
                   pl.BlockSpec((8, LANES), lambda ph, i: (0, 0))],
        out_shape=[jax.ShapeDtypeStruct((n // tn, 8, tn), jnp.int32),
                   jax.ShapeDtypeStruct((1, nbp), jnp.int32),
                   jax.ShapeDtypeStruct((8, LANES), jnp.int32)],
        scratch_shapes=[pltpu.VMEM((N_EXPERTS, LANES), F32)] * 3,
        compiler_params=_cparams(("arbitrary", "arbitrary")),
    )(top_e)


def _scatter_rows_kernel(dest_ref, meta_ref, x_ref, o_ref, zbuf, sem, zsem, *, blk, n_blocks):
    tm = x_ref.shape[0]

    @pl.when(pl.program_id(0) == 0)
    def _():
        zbuf[...] = jnp.zeros(zbuf.shape, zbuf.dtype)

        def zero_copy(row):
            return pltpu.make_async_copy(zbuf, o_ref.at[pl.ds(pl.multiple_of(row, blk), blk)], zsem)

        def each_block(fn):
            def expert(e, carry):
                row = meta_ref[1, e]

                @pl.when(row >= 0)
                def _():
                    fn(zero_copy(row))
                return carry

            def tail(i, carry):
                fn(zero_copy(i * blk))
                return carry

            lax.fori_loop(0, N_EXPERTS, expert, 0)
            lax.fori_loop(meta_ref[0, 0], n_blocks, tail, 0)

        each_block(lambda cp: cp.start())
        each_block(lambda cp: cp.wait())

    def row_copy(t, k):
        return pltpu.make_async_copy(x_ref.at[pl.ds(t, 1)], o_ref.at[pl.ds(dest_ref[0, k, t], 1)], sem)

    def issue(t, carry):
        for k in range(TOP_K):
            row_copy(t, k).start()
        return carry

    def drain(t, carry):
        for k in range(TOP_K):
            row_copy(t, k).wait()
        return carry

    lax.fori_loop(0, tm, issue, 0)
    lax.fori_loop(0, tm, drain, 0)


def _scatter_rows(dest, meta, x, *, blk, n_blocks):
    n, dh = x.shape
    tm = LANES
    kern = functools.partial(_scatter_rows_kernel, blk=blk, n_blocks=n_blocks)
    return pl.pallas_call(
        kern,
        grid=(n // tm,),
        in_specs=[pl.BlockSpec((1, 8, tm), lambda i: (i, 0, 0), memory_space=pltpu.SMEM),
                  pl.BlockSpec(memory_space=pltpu.SMEM),
                  pl.BlockSpec((tm, dh), lambda i: (i, 0))],
        out_specs=pl.BlockSpec(memory_space=pl.ANY),
        out_shape=jax.ShapeDtypeStruct((n_blocks * blk, dh), x.dtype),
        scratch_shapes=[pltpu.VMEM((blk, dh), x.dtype), pltpu.SemaphoreType.DMA(()),
                        pltpu.SemaphoreType.DMA(())],
        compiler_params=_cparams(("arbitrary",)),
    )(dest, meta, x)


def _combine_kernel(dest0_ref, dest1_ref, y_ref, gate_ref, f_ref, xa_ref, sg_ref, su_ref, sd_ref, gp_ref,
                    g2_ref, o_ref, buf, sem):
    halves = (dest0_ref, dest1_ref)

    def row_copy(half, t, k):
        return pltpu.make_async_copy(y_ref.at[pl.ds(halves[half][0, k, t], 1)],
                                     buf.at[k, pl.ds(half * LANES + t, 1)], sem)

    def issue(t, carry):
        for half in range(2):
            for k in range(TOP_K):
                row_copy(half, t, k).start()
        return carry

    def drain(t, carry):
        for half in range(2):
            for k in range(TOP_K):
                row_copy(half, t, k).wait()
        return carry

    lax.fori_loop(0, LANES, issue, 0)
    f = _swiglu(_unpack_bf16_pairs(f_ref[...]), sg_ref[...], su_ref[...], sd_ref[...])
    lax.fori_loop(0, LANES, drain, 0)
    gate = gate_ref[...]
    for k in range(TOP_K):
        f = f + gate[:, k:k + 1] * _unpack_bf16_pairs(buf[k]).astype(F32)
    fn = f * lax.rsqrt(jnp.mean(f * f, axis=-1, keepdims=True) + NORM_EPS) * gp_ref[...]
    o_ref[0] = xa_ref[0] + g2_ref[0] * fn


def _combine(dest, y, gate, f_pk, xa, sg, su, sd, g_post, g2, *, row_tile0, seg_tiles):
    b, rows, d = xa.shape
    tm = 2 * LANES
    assert tm == ROW_TILE
    nt = rows // tm
    mod_map = lambda bi, i: (bi * 2 + jnp.where(i + row_tile0 >= seg_tiles, 1, 0), 0, 0)
    const = lambda shape: pl.BlockSpec(shape, lambda bi, i: (0,) * len(shape))
    dest_spec = lambda half: pl.BlockSpec((1, 8, LANES), lambda bi, i: (2 * (bi * nt + i) + half, 0, 0),
                                          memory_space=pltpu.SMEM)
    return pl.pallas_call(
        _combine_kernel,
        grid=(b, nt),
        in_specs=[
            dest_spec(0), dest_spec(1),
            pl.BlockSpec(memory_space=pl.ANY),
            pl.BlockSpec((tm, 8), lambda bi, i: (bi * nt + i, 0)),
            pl.BlockSpec((tm, d // 2), lambda bi, i: (bi * nt + i, 0)),
            pl.BlockSpec((1, tm, d), lambda bi, i: (bi, i, 0)),
            const(sg.shape), const(su.shape), const(sd.shape), const((1, d)),
            pl.BlockSpec((1, 1, d), mod_map),
        ],
        out_specs=pl.BlockSpec((1, tm, d), lambda bi, i: (bi, i, 0)),
        out_shape=jax.ShapeDtypeStruct((b, rows, d), F32),
        scratch_shapes=[pltpu.VMEM((TOP_K, tm, d // 2), jnp.uint32), pltpu.SemaphoreType.DMA(())],
        compiler_params=_cparams(("arbitrary", "arbitrary")),
    )(dest, dest, y, gate, f_pk, xa, sg, su, sd, g_post.reshape(1, d).astype(F32), g2)


def _rope_tables(s, ctx_len, dim):
    rows = s // GRID_W
    row = jnp.repeat(jnp.arange(rows), GRID_W)
    col = jnp.tile(jnp.arange(GRID_W), rows)
    quarter = dim // 4
    inv = ROPE_BASE ** (-jnp.arange(quarter, dtype=F32) / quarter)
    ang = jnp.concatenate([row.astype(F32)[:, None] * inv, col.astype(F32)[:, None] * inv], axis=-1)
    cos = jnp.repeat(jnp.cos(ang), 2, axis=-1)
    sin = jnp.repeat(jnp.sin(ang), 2, axis=-1) * jnp.tile(jnp.array([-1.0, 1.0], F32), dim // 2)
    cos = jnp.concatenate([jnp.ones((ctx_len, dim), F32), cos], axis=0)
    sin = jnp.concatenate([jnp.zeros((ctx_len, dim), F32), sin], axis=0)
    return cos, sin


def _log_sigmoid(x):
    return jnp.minimum(x, 0.0) - jnp.log1p(jnp.exp(-jnp.abs(x)))


def _mlstm_kernel(q_ref, k_ref, kt_ref, v_ref, g_ref, gt_ref, gb_ref, gbt_ref, hf_ref, mo_ref, mn_ref,
                  o_ref, s_scr, m_scr, *, reverse, final):
    L = MLSTM_CHUNK
    hv = MLSTM_V
    io, fo = (2 * MLSTM_HEADS, 3 * MLSTM_HEADS) if reverse else (0, MLSTM_HEADS)

    @pl.when(pl.program_id(1) == 0)
    def _():
        s_scr[...] = jnp.zeros(s_scr.shape, F32)
        m_scr[...] = jnp.zeros(m_scr.shape, F32)

    g = g_ref[0] + gb_ref[...]
    gt = gt_ref[0] + gbt_ref[...]
    r_io = lax.broadcasted_iota(jnp.int32, (L, L), 0)
    c_io = lax.broadcasted_iota(jnp.int32, (L, L), 1)
    seen = (c_io >= r_io) if reverse else (c_io <= r_io)
    tri = jnp.where(seen, 1.0, 0.0)
    hi = lax.Precision.HIGHEST
    bc_col = jnp.dot(tri, _log_sigmoid(g), precision=hi, preferred_element_type=F32)
    lf_row = _log_sigmoid(gt)
    bc_row = lax.dot_general(lf_row, tri, (((1,), (1,)), ((), ())), precision=hi,
                             preferred_element_type=F32)
    lane = lax.broadcasted_iota(jnp.int32, (L, LANES), 1)
    sub = lax.broadcasted_iota(jnp.int32, (LANES, L), 0)

    for h in range(MLSTM_HEADS):
        pair, odd = h // 2, h % 2
        lo = odd * MLSTM_QK
        a_col = bc_col[:, fo + h:fo + h + 1]
        i_col = g[:, io + h:io + h + 1]
        b_row = bc_row[fo + h:fo + h + 1, :]
        i_row = gt[io + h:io + h + 1, :]
        btot = jnp.sum(lf_row[fo + h:fo + h + 1, :], axis=1, keepdims=True)
        m_st = m_scr[h:h + 1, 0:1]

        w_end = btot - a_col + i_col
        m_new = jnp.maximum(btot + m_st, jnp.max(w_end, axis=0, keepdims=True))
        decay = jnp.exp(btot + m_st - m_new)
        w_k = jnp.exp(w_end - m_new)

        log_d = jnp.where(seen, a_col - b_row + i_row, -jnp.inf)
        log_inter = a_col + m_st
        m_row = jnp.maximum(log_inter, jnp.max(log_d, axis=1, keepdims=True))
        w_intra = jnp.exp(log_d - m_row)
        w_inter = jnp.exp(log_inter - m_row)

        in_head = (lane >= lo) & (lane < lo + MLSTM_QK)
        qm = jnp.where(in_head, q_ref[0, :, pair * LANES:(pair + 1) * LANES], 0).astype(BF16)
        kp = k_ref[0, :, pair * LANES:(pair + 1) * LANES]
        v = v_ref[0, :, h * hv:(h + 1) * hv]
        state = s_scr[h]

        qk = lax.dot_general(qm, kp, (((1,), (1,)), ((), ())), preferred_element_type=F32)
        qk = qk * (MLSTM_QK ** -0.5) * w_intra
        inter = jnp.dot(qm, state.astype(BF16), preferred_element_type=F32)
        num = w_inter * inter[:, :hv] + jnp.dot(qk.astype(BF16), v, preferred_element_type=F32)
        den = w_inter * inter[:, hv:hv + 1] + jnp.sum(qk, axis=1, keepdims=True)
        out = num / jnp.maximum(jnp.abs(den), jnp.exp(-m_row))

        wv = jnp.concatenate([(w_k * v.astype(F32)).astype(BF16),
                              jnp.where(lane == 0, w_k, 0.0).astype(BF16)], axis=1)
        kt = kt_ref[0, pair * LANES:(pair + 1) * LANES, :]
        in_rows = (sub >= lo) & (sub < lo + MLSTM_QK)
        ktm = (jnp.where(in_rows, kt, 0).astype(F32) * (MLSTM_QK ** -0.5)).astype(BF16)
        s_scr[h] = decay * state + jnp.dot(ktm, wv, preferred_element_type=F32)
        m_scr[h:h + 1, :] = jnp.broadcast_to(m_new, (1, LANES))

        if final:
            tot = out + hf_ref[0, :, h * hv:(h + 1) * hv]
            nrm = tot * lax.rsqrt(jnp.mean(tot * tot, axis=-1, keepdims=True) + NORM_EPS) * mn_ref[...]
            gate = jax.nn.sigmoid(mo_ref[0, :, h * hv:(h + 1) * hv].astype(F32))
            o_ref[0, :, h * hv:(h + 1) * hv] = (nrm * gate).astype(o_ref.dtype)
        else:
            o_ref[0, :, h * hv:(h + 1) * hv] = out.astype(o_ref.dtype)


def _mlstm(pv, kt, gates, gates_t, gate_b, m_norm, hf, *, ctx_chunks, reverse):
    b, t, _ = pv.shape
    L = MLSTM_CHUNK
    nc = t // L
    final = hf is not None
    nq = MLSTM_HEADS * MLSTM_QK
    nv = MLSTM_HEADS * MLSTM_V

    def chunk(j):
        if not reverse:
            return j
        return jnp.where(j < ctx_chunks, ctx_chunks - 1 - j, nc - 1 - (j - ctx_chunks))

    gb = jnp.pad(gate_b.astype(F32).reshape(-1), (0, LANES - gate_b.size))
    if hf is None:
        hf = jnp.zeros((1, L, nv), F32)
        hf_spec = pl.BlockSpec((1, L, nv), lambda bi, j: (0, 0, 0))
    else:
        hf_spec = pl.BlockSpec((1, L, nv), lambda bi, j: (bi, chunk(j), 0))
    kern = functools.partial(_mlstm_kernel, reverse=reverse, final=final)
    return pl.pallas_call(
        kern,
        grid=(b, nc),
        in_specs=[
            pl.BlockSpec((1, L, nq), lambda bi, j: (bi, chunk(j), nv // nq)),
            pl.BlockSpec((1, L, nq), lambda bi, j: (bi, chunk(j), nv // nq + 1)),
            pl.BlockSpec((1, nq, L), lambda bi, j: (bi, 0, chunk(j))),
            pl.BlockSpec((1, L, nv), lambda bi, j: (bi, chunk(j), 2)),
            pl.BlockSpec((1, L, LANES), lambda bi, j: (bi, chunk(j), 0)),
            pl.BlockSpec((1, LANES, L), lambda bi, j: (bi, 0, chunk(j))),
            pl.BlockSpec((1, LANES), lambda bi, j: (0, 0)),
            pl.BlockSpec((LANES, 1), lambda bi, j: (0, 0)),
            hf_spec,
            pl.BlockSpec((1, L, nv), lambda bi, j: (bi, chunk(j), 3)),
            pl.BlockSpec((1, MLSTM_V), lambda bi, j: (0, 0)),
        ],
        out_specs=pl.BlockSpec((1, L, nv), lambda bi, j: (bi, chunk(j), 0)),
        out_shape=jax.ShapeDtypeStruct((b, t, nv), BF16 if final else F32),
        scratch_shapes=[pltpu.VMEM((MLSTM_HEADS, LANES, 2 * LANES), F32), pltpu.VMEM((MLSTM_HEADS, LANES), F32)],
        compiler_params=_cparams(("parallel", "arbitrary")),
    )(pv, pv, kt, pv, gates, gates_t, gb.reshape(1, LANES), gb.reshape(LANES, 1), hf, pv,
      m_norm.reshape(1, MLSTM_V).astype(F32))


def _swa_kernel(sink_ref, q_ref, kp_ref, kc_ref, kn_ref, kx_ref, vp_ref, vc_ref, vn_ref, vx_ref, o_ref, *,
                n_blocks):
    i = pl.program_id(1)
    bq = BLOCK_Q
    hd = SWA_HD
    rep = SWA_HEADS // SWA_KV_HEADS
    n_ctx = kx_ref.shape[1]
    rows = rep * bq
    scale = hd ** -0.5 * LOG2E
    row = lax.broadcasted_iota(jnp.int32, (rows, 3 * bq), 0)
    col = lax.broadcasted_iota(jnp.int32, (rows, 3 * bq), 1)
    dt = (col - bq) - (row % bq)
    blk = col // bq
    ok = (jnp.abs(dt) <= WINDOW) & ((blk != 0) | (i > 0)) & ((blk != 2) | (i < n_blocks - 1))
    head_of_row = lax.broadcasted_iota(jnp.int32, (rows, 1), 0) // bq
    for g in range(SWA_KV_HEADS):
        q = jnp.concatenate([q_ref[0, :, (g * rep + r) * hd:(g * rep + r + 1) * hd] for r in range(rep)], axis=0)
        q = (q.astype(F32) * scale).astype(BF16)
        cs = slice(g * hd, (g + 1) * hd)
        k_loc = jnp.concatenate([kp_ref[0, :, cs], kc_ref[0, :, cs], kn_ref[0, :, cs]], axis=0)
        v_loc = jnp.concatenate([vp_ref[0, :, cs], vc_ref[0, :, cs], vn_ref[0, :, cs]], axis=0)
        nt = (((1,), (1,)), ((), ()))
        s_loc = jnp.where(ok, lax.dot_general(q, k_loc, nt, preferred_element_type=F32), -jnp.inf)
        s_ctx = lax.dot_general(q, kx_ref[0, :, cs], nt, preferred_element_type=F32)
        sink = jnp.zeros((rows, 1), F32)
        for r in range(rep):
            sink = jnp.where(head_of_row == r, sink_ref[g * rep + r] * LOG2E, sink)
        m = jnp.maximum(jnp.maximum(jnp.max(s_loc, axis=-1, keepdims=True),
                                    jnp.max(s_ctx, axis=-1, keepdims=True)), sink)
        p_loc = jnp.exp2(s_loc - m)
        p_ctx = jnp.exp2(s_ctx - m)
        den = (jnp.sum(p_loc, axis=-1, keepdims=True) + jnp.sum(p_ctx, axis=-1, keepdims=True)
               + jnp.exp2(sink - m))
        out = (jnp.dot(p_loc.astype(BF16), v_loc, preferred_element_type=F32)
               + jnp.dot(p_ctx.astype(BF16), vx_ref[0, :, cs], preferred_element_type=F32)) / den
        for r in range(rep):
            o_ref[0, :, (g * rep + r) * hd:(g * rep + r + 1) * hd] = out[r * bq:(r + 1) * bq].astype(o_ref.dtype)


def _swa(qk, v, sink, *, n_ctx, k_col0):
    b, t, _ = qk.shape
    bq = BLOCK_Q
    cb = n_ctx // bq
    nb = (t - n_ctx) // bq
    nq = SWA_HEADS * SWA_HD
    nk = SWA_KV_HEADS * SWA_HD
    kcol = k_col0
    prev = lambda bi, i, c: (bi, i + cb - 1, c)
    cur = lambda bi, i, c: (bi, i + cb, c)
    nxt = lambda bi, i, c: (bi, jnp.minimum(i + cb + 1, nb + cb - 1), c)
    kern = functools.partial(_swa_kernel, n_blocks=nb)

    def spec(rows, fn, c):
        return pl.BlockSpec((1, rows, nk), lambda bi, i: fn(bi, i, c))

    ctx = lambda bi, i, c: (bi, 0, c)
    return pl.pallas_call(
        kern,
        grid=(b, nb),
        in_specs=[
            pl.BlockSpec(memory_space=pltpu.SMEM),
            pl.BlockSpec((1, bq, nq), lambda bi, i: (bi, i + cb, 0)),
            spec(bq, prev, kcol), spec(bq, cur, kcol), spec(bq, nxt, kcol), spec(n_ctx, ctx, kcol),
            spec(bq, prev, 0), spec(bq, cur, 0), spec(bq, nxt, 0), spec(n_ctx, ctx, 0),
        ],
        out_specs=pl.BlockSpec((1, bq, nq), lambda bi, i: (bi, i, 0)),
        out_shape=jax.ShapeDtypeStruct((b, nb * bq, nq), BF16),
        compiler_params=_cparams(("parallel", "arbitrary")),
    )(sink.astype(F32), qk, qk, qk, qk, qk, v, v, v, v)


def _mix_out_and_moe(mix, xs, w_out, g_post, g_ffn_pre, g_ffn_post, mods, router_w, router_b,
                     wg, wu, wd, sg, su, sd, *, layer, row_tile0, seg_tiles):
    sh1, sc1, g1, sh2, sc2, g2 = mods
    b, rows, _ = mix[0].shape
    n = b * rows
    seg = dict(row_tile0=row_tile0, seg_tiles=seg_tiles)
    xa, f_pk, logits_t = _post_mix(mix, xs, w_out.astype(BF16), g_post, g_ffn_pre, g1, sh2, sc2,
                                   router_w.T.astype(BF16), **seg)
    top_e, gate = _route(logits_t, router_b, tn=ROUTE_TOKENS)
    blk = EXPERT_ROWS
    n_blocks = -(-n * TOP_K // blk) + N_EXPERTS
    dest, block_e, n_used = _dispatch(top_e, blk=blk, n_blocks=n_blocks)
    f_pk = f_pk.reshape(n, -1)
    x_sorted = _scatter_rows(dest, n_used, f_pk, blk=blk, n_blocks=n_blocks)
    y = _expert_ffn(x_sorted, block_e[0, :n_blocks], n_used[0, :1], wg, wu, wd, layer=layer, tm=blk)
    return _combine(dest, y, gate.T, f_pk, xa, sg.astype(BF16), su.astype(BF16), sd.astype(BF16),
                    g_ffn_post, g2, **seg)


def kernel(x, c, ctx, c_ctx, mod_w, mod_b, norm_mix_pre, norm_mix_post, norm_ffn_pre, norm_ffn_post, ab_w_in, ab_w_out, diff_lambda, diff_norm, mlstm_gate_b, mlstm_norm, cd_w_in, cd_w_out, mla_q_norm, mla_w_uq, mla_kv_norm, mla_w_ukv, swa_sink, router_w, router_b, exp_w_gate, exp_w_up, exp_w_down, sh_w_gate, sh_w_up, sh_w_down):
    b, s, d = x.shape
    L = ctx.shape[1]
    t = L + s
    assert L == ROW_TILE and s % ROW_TILE == 0
    depth = mod_w.shape[0]
    ctx_tiles = L // ROW_TILE
    tk = next(c for c in ATTN_CHUNKS if t % c == 0)

    cos64, sin64 = _rope_tables(s, L, DIFF_HD)
    cos64 = jnp.tile(cos64, (1, LANES // DIFF_HD))
    sin64 = jnp.tile(sin64, (1, LANES // DIFF_HD))
    cos128, sin128 = _rope_tables(s, L, SWA_HD)
    cos_kr = jnp.concatenate([cos64[:, :MLA_ROPE], jnp.ones((t, LANES - MLA_ROPE), F32)], axis=1)
    sin_kr = jnp.concatenate([sin64[:, :MLA_ROPE], jnp.zeros((t, LANES - MLA_ROPE), F32)], axis=1)

    xs = jnp.concatenate([ctx, x], axis=1)
    c_rows = jnp.concatenate([c, c_ctx[None], jnp.zeros((-(b + 1) % 8, d), F32)], axis=0)

    for layer in range(depth):
        with_ctx = layer < depth - 1
        j = layer // 2
        mod_all = _modulation(c_rows, mod_w, mod_b, layer=layer)
        mod_l, mod_c = mod_all[:b], mod_all[b]
        mods = jnp.stack([jnp.broadcast_to(mod_c, (b, 6 * d)), mod_l], axis=1).reshape(b * 2, 1, 6, d)
        sh1, sc1, g1, sh2, sc2, g2 = (mods[:, :, m] for m in range(6))

        proj = functools.partial(_norm_proj, xs, norm_mix_pre[layer], sh1, sc1, seg_tiles=ctx_tiles)
        if layer % 2 == 0:
            w_in = ab_w_in[j].astype(BF16)
            n_qk = 2 * DIFF_HEADS * 2 * DIFF_HD
            qk = proj(w_in[:, :n_qk], tn=PROJ_COLS, rope=(cos64[None], sin64[None]),
                      pattern=(0,) * (PROJ_COLS // LANES))
            pv = proj(w_in[:, n_qk:n_qk + AB_PLAIN], tn=PROJ_COLS)
            w_gates = jnp.pad(w_in[:, n_qk + AB_PLAIN:], ((0, 0), (0, LANES - 4 * MLSTM_HEADS)))
            gates = proj(w_gates, tn=LANES, out_dtype=F32)

            lam_init = 0.8 - 0.6 * math.exp(-0.3 * layer)
            a = _attention(qk if with_ctx else qk[:, L:], qk, pv, heads=DIFF_HEADS, dq=2 * DIFF_HD,
                           dv=DIFF_VD, q_col0=0, k_col0=DIFF_HEADS, v_col0=0, tq=ROW_TILE,
                           scale=DIFF_HD ** -0.5, tk=tk, diff=True, lam_vec=diff_lambda[j],
                           d_norm=diff_norm[j], lam_init=lam_init,
                           ctx_tiles=ctx_tiles if with_ctx else 0, ctx_len=L)
            o_mk = DIFF_HEADS * DIFF_VD + MLSTM_HEADS * MLSTM_QK
            kt = jnp.swapaxes(pv[..., o_mk:o_mk + MLSTM_HEADS * MLSTM_QK], 1, 2)
            scan = functools.partial(_mlstm, pv, kt, gates, jnp.swapaxes(gates, 1, 2), mlstm_gate_b[j],
                                     mlstm_norm[j], ctx_chunks=L // MLSTM_CHUNK)
            m = scan(scan(None, reverse=False), reverse=True)
            mix = (a, m if with_ctx else m[:, L:])
            w_out = ab_w_out[j]
        else:
            assert not with_ctx
            w_in = cd_w_in[j].astype(BF16)
            c0 = Q_LORA
            c1 = c0 + KV_LORA
            c2 = c1 + MLA_ROPE
            c3 = c2 + SWA_HEADS * SWA_HD
            c4 = c3 + SWA_KV_HEADS * SWA_HD
            w_rope = jnp.concatenate([w_in[:, c2:c4], w_in[:, c1:c2],
                                      jnp.zeros((d, LANES - MLA_ROPE), BF16)], axis=1)
            n_rope = w_rope.shape[1]
            rp = proj(w_rope, tn=n_rope, rope=(jnp.stack([cos128, cos_kr]), jnp.stack([sin128, sin_kr])),
                      pattern=(0,) * (n_rope // LANES - 1) + (1,))
            assert KV_LORA == Q_LORA_PAD
            q_pad = Q_LORA_PAD - Q_LORA
            w_plain = jnp.concatenate([w_in[:, c4:], w_in[:, :c0], jnp.zeros((d, q_pad), BF16),
                                       w_in[:, c0:c1]], axis=1)
            pp = proj(w_plain, tn=w_plain.shape[1])
            n_sv = SWA_KV_HEADS * SWA_HD
            cq = pp[..., n_sv:n_sv + Q_LORA_PAD]
            ckv = pp[..., n_sv + Q_LORA_PAD:]

            hq = MLA_NOPE + MLA_ROPE
            w_uq = mla_w_uq[j].astype(BF16).reshape(Q_LORA, MLA_HEADS, hq)
            w_uq = jnp.pad(w_uq, ((0, q_pad), (0, 0), (0, MLA_QK_PAD - hq))).reshape(Q_LORA_PAD, MLA_HEADS * MLA_QK_PAD)
            no_mod = jnp.zeros((b * 2, 1, Q_LORA_PAD), F32)
            qn = jnp.pad(mla_q_norm[j], (0, q_pad))
            q_mla = _norm_proj(cq[:, L:], qn, no_mod, no_mod, w_uq, seg_tiles=0, tn=MLA_HEADS * MLA_QK_PAD,
                               n_valid=Q_LORA, rope=(cos_kr[None, L:], sin_kr[None, L:]),
                               pattern=(-1, 0) * MLA_HEADS)
            w_ukv = mla_w_ukv[j].astype(BF16).reshape(KV_LORA, MLA_HEADS, MLA_NOPE + MLA_V)
            w_k = jnp.pad(w_ukv[..., :MLA_NOPE], ((0, 0), (0, 0), (0, MLA_QK_PAD - MLA_NOPE)))
            w_k = w_k.reshape(KV_LORA, MLA_HEADS * MLA_QK_PAD)
            w_v = w_ukv[..., MLA_NOPE:].reshape(KV_LORA, MLA_HEADS * MLA_V)
            w_kv = jnp.concatenate([w_k, w_v], axis=1)
            kv = _norm_proj(ckv, mla_kv_norm[j], no_mod, no_mod, w_kv, seg_tiles=ctx_tiles, tn=w_kv.shape[1],
                            pattern=(-1, FILL) * MLA_HEADS + (-1,) * (MLA_HEADS * MLA_V // LANES),
                            fill=(rp, n_rope // LANES - 1))
            a = _attention(q_mla, kv, kv, heads=MLA_HEADS, dq=MLA_QK_PAD, dv=MLA_V, q_col0=0, k_col0=0,
                           v_col0=MLA_HEADS * MLA_QK_PAD // MLA_V, tq=MLA_Q_TILE, scale=MLA_SCALE, tk=tk)

            w = _swa(rp, pp, swa_sink[j], n_ctx=L, k_col0=SWA_HEADS * SWA_HD // n_sv)
            mix = (a, w)
            w_out = cd_w_out[j]

        xa = _mix_out_and_moe(mix, xs, w_out, norm_mix_post[layer], norm_ffn_pre[layer], norm_ffn_post[layer],
                              (sh1, sc1, g1, sh2, sc2, g2), router_w[layer], router_b[layer],
                              exp_w_gate, exp_w_up, exp_w_down,
                              sh_w_gate[layer], sh_w_up[layer], sh_w_down[layer],
                              layer=layer, row_tile0=0 if with_ctx else ctx_tiles, seg_tiles=ctx_tiles)
        xs = xa if with_ctx else jnp.concatenate([xs[:, :L], xa], axis=1)
    return xs[:, L:]
```
